```python
import math
import jax
import jax.numpy as jnp
from jax import lax
import numpy as np

D_MODEL = 1024
BATCH = 8
SEQ = 2048
DEPTH = 1
DEC_BATCH = 128
DEC_SEQ = 1
PAST_LEN = 16384
PAGE_SIZE = 128

N_META = 16
D_S5 = D_MODEL // 2
S5_GROUP = 16
N_S5_GROUPS = D_S5 // S5_GROUP
S5_STATE = 64
D_RET = D_MODEL - D_S5
N_RET_HEADS = 8
RET_HEAD_DIM = D_RET // N_RET_HEADS
RET_CHUNK = 128
ROPE_BASE = 10000.0
D_IN = D_S5 + 4 * D_RET
N_EXPERTS = 64
TOP_K = 8
N_EXPERT_GROUPS = 8
TOPK_GROUPS = 4
D_EXPERT = D_MODEL // 4
ROUTED_SCALE = 2.5
EPS = 1e-6

kernel_name = 'hymba_s5_retnet_moe_step'


def rmsnorm(x, g):
    xf = x.astype(jnp.float32)
    y = xf * lax.rsqrt(jnp.mean(xf * xf, axis=-1, keepdims=True) + EPS)
    return (y * g.astype(jnp.float32)).astype(x.dtype)


def rotary(x, pos):
    half = x.shape[-1] // 2
    inv_freq = ROPE_BASE ** (-jnp.arange(half, dtype=jnp.float32) / half)
    ang = pos.astype(jnp.float32)[:, None] * inv_freq[None, :]
    cos = jnp.cos(ang)[None, :, None, :]
    sin = jnp.sin(ang)[None, :, None, :]
    xf = x.astype(jnp.float32)
    x1, x2 = xf[..., :half], xf[..., half:]
    return jnp.concatenate([x1 * cos - x2 * sin, x1 * sin + x2 * cos], axis=-1)


def retention_log_decay():
    return jnp.log1p(-jnp.exp2(-5.0 - jnp.arange(N_RET_HEADS, dtype=jnp.float32)))


def retention_chunk(q, k, v, state, log_g):
    c = q.shape[2]
    n = jnp.arange(c, dtype=jnp.float32)
    diff = n[:, None] - n[None, :]
    decay = jnp.where(diff[None] >= 0.0,
                      jnp.exp(log_g[:, None, None] * jnp.maximum(diff, 0.0)[None]), 0.0)
    scores = jnp.einsum('bhnd,bhmd->bhnm', q, k) * decay[None]
    inner = jnp.einsum('bhnm,bhmv->bhnv', scores, v)
    q_dec = q * jnp.exp(log_g[:, None] * (n + 1.0)[None])[None, :, :, None]
    cross = jnp.einsum('bhnd,bhdv->bhnv', q_dec, state)
    k_dec = k * jnp.exp(log_g[:, None] * (c - 1.0 - n)[None])[None, :, :, None]
    new_state = (jnp.exp(log_g * c)[None, :, None, None] * state
                 + jnp.einsum('bhmd,bhmv->bhdv', k_dec, v))
    return inner + cross, new_state


def retention(q, k, v, state, lead):
    log_g = retention_log_decay()
    o0, state = retention_chunk(q[:, :, :lead], k[:, :, :lead], v[:, :, :lead], state, log_g)
    b, h, length, dk = q.shape
    if length == lead:
        return o0, state
    dv = v.shape[-1]
    nc = (length - lead) // RET_CHUNK

    def to_chunks(t):
        return jnp.moveaxis(t[:, :, lead:].reshape(b, h, nc, RET_CHUNK, t.shape[-1]), 2, 0)

    def step(s, qkv):
        qc, kc, vc = qkv
        o, s = retention_chunk(qc, kc, vc, s, log_g)
        return s, o

    state, oc = lax.scan(step, state, (to_chunks(q), to_chunks(k), to_chunks(v)))
    oc = jnp.moveaxis(oc, 0, 2).reshape(b, h, nc * RET_CHUNK, dv)
    return jnp.concatenate([o0, oc], axis=2), state


def _complex_affine_combine(e1, e2):
    a1r, a1i, b1r, b1i = e1
    a2r, a2i, b2r, b2i = e2
    return (a2r * a1r - a2i * a1i,
            a2r * a1i + a2i * a1r,
            a2r * b1r - a2i * b1i + b2r,
            a2r * b1i + a2i * b1r + b2i)


def s5_scan(u, lam_re, lam_im, log_dt, b_re, b_im, c_re, c_im, d_skip, x0_re, x0_im):
    f32 = jnp.float32
    lam_re, lam_im = lam_re.astype(f32), lam_im.astype(f32)
    dt = jnp.exp(log_dt.astype(f32))[:, None]
    mag = jnp.exp(lam_re * dt)
    abar_re, abar_im = mag * jnp.cos(lam_im * dt), mag * jnp.sin(lam_im * dt)
    num_re, num_im = abar_re - 1.0, abar_im
    den = lam_re * lam_re + lam_im * lam_im
    f_re = (num_re * lam_re + num_im * lam_im) / den
    f_im = (num_im * lam_re - num_re * lam_im) / den
    b_re, b_im = b_re.astype(f32), b_im.astype(f32)
    bbar_re = f_re[..., None] * b_re - f_im[..., None] * b_im
    bbar_im = f_re[..., None] * b_im + f_im[..., None] * b_re
    bu_re = jnp.einsum('blgh,gph->blgp', u, bbar_re)
    bu_im = jnp.einsum('blgh,gph->blgp', u, bbar_im)
    a_re = jnp.broadcast_to(abar_re, bu_re.shape)
    a_im = jnp.broadcast_to(abar_im, bu_re.shape)
    acum_re, acum_im, h_re, h_im = lax.associative_scan(
        _complex_affine_combine, (a_re, a_im, bu_re, bu_im), axis=1)
    x0_re, x0_im = x0_re.astype(f32)[:, None], x0_im.astype(f32)[:, None]
    xs_re = acum_re * x0_re - acum_im * x0_im + h_re
    xs_im = acum_re * x0_im + acum_im * x0_re + h_im
    y = (jnp.einsum('gnp,blgp->blgn', c_re.astype(f32), xs_re)
         - jnp.einsum('gnp,blgp->blgn', c_im.astype(f32), xs_im)
         + d_skip.astype(f32)[None, None] * u)
    return y, xs_re[:, -1], xs_im[:, -1]


def moe(h, w_router, router_bias, w_gate_e, w_up_e, w_down_e, w_gate_sh, w_up_sh, w_down_sh):
    t = h.shape[0]
    scores = jax.nn.sigmoid((h @ w_router).astype(jnp.float32))
    sel = scores + router_bias.astype(jnp.float32)[None]
    grp = sel.reshape(t, N_EXPERT_GROUPS, N_EXPERTS // N_EXPERT_GROUPS)
    group_score = jnp.sum(lax.top_k(grp, 2)[0], axis=-1)
    _, gidx = lax.top_k(group_score, TOPK_GROUPS)
    gmask = jnp.sum(jax.nn.one_hot(gidx, N_EXPERT_GROUPS, dtype=jnp.float32), axis=1) > 0.0
    emask = jnp.repeat(gmask, N_EXPERTS // N_EXPERT_GROUPS, axis=1)
    _, eidx = lax.top_k(jnp.where(emask, sel, -jnp.inf), TOP_K)
    w = jnp.take_along_axis(scores, eidx, axis=1)
    w = w / jnp.sum(w, axis=-1, keepdims=True) * ROUTED_SCALE
    gates = jnp.einsum('tk,tke->te', w, jax.nn.one_hot(eidx, N_EXPERTS, dtype=jnp.float32)).astype(h.dtype)
    hg = jnp.einsum('td,edf->tef', h, w_gate_e)
    hu = jnp.einsum('td,edf->tef', h, w_up_e)
    act = jax.nn.silu(hg) * hu * gates[..., None]
    routed = jnp.einsum('tef,efd->td', act, w_down_e)
    shared = (jax.nn.silu(h @ w_gate_sh) * (h @ w_up_sh)) @ w_down_sh
    return routed + shared


def decoder_layer(x, pos, lead, s5_x0_re, s5_x0_im, ret_s0, map_rows,
                  norm1, w_in, s5_lam_re, s5_lam_im, s5_log_dt, s5_b_re, s5_b_im,
                  s5_c_re, s5_c_im, s5_d, s5_w_glu, s5_norm, ret_norm, w_out, norm2,
                  w_router, router_bias, w_gate_e, w_up_e, w_down_e, w_gate_sh, w_up_sh, w_down_sh):
    b, length, _ = x.shape
    h = rmsnorm(x, norm1)
    proj = h @ w_in
    u, q, k, v, g = jnp.split(proj, [D_S5, D_S5 + D_RET, D_S5 + 2 * D_RET, D_S5 + 3 * D_RET], axis=-1)
    y_s5, s5_re, s5_im = s5_scan(u.reshape(b, length, N_S5_GROUPS, S5_GROUP).astype(jnp.float32),
                                 s5_lam_re, s5_lam_im, s5_log_dt, s5_b_re, s5_b_im,
                                 s5_c_re, s5_c_im, s5_d, s5_x0_re, s5_x0_im)
    y_s5 = jax.nn.gelu(y_s5.reshape(b, length, D_S5))
    y_s5 = y_s5 * jax.nn.sigmoid(y_s5 @ s5_w_glu.astype(jnp.float32))
    y_s5 = rmsnorm(y_s5, s5_norm)
    qh = rotary(q.reshape(b, length, N_RET_HEADS, RET_HEAD_DIM), pos)
    kh = rotary(k.reshape(b, length, N_RET_HEADS, RET_HEAD_DIM), pos) * (RET_HEAD_DIM ** -0.5)
    vh = v.reshape(b, length, N_RET_HEADS, RET_HEAD_DIM).astype(jnp.float32)
    o, ret_state = retention(qh.transpose(0, 2, 1, 3), kh.transpose(0, 2, 1, 3),
                             vh.transpose(0, 2, 1, 3), ret_s0.astype(jnp.float32), lead)
    o = o.transpose(0, 2, 1, 3)
    mu = jnp.mean(o, axis=-1, keepdims=True)
    var = jnp.mean(jnp.square(o - mu), axis=-1, keepdims=True)
    o = ((o - mu) * lax.rsqrt(var + EPS)).reshape(b, length, D_RET) * ret_norm.astype(jnp.float32)
    y_ret = jax.nn.silu(g.astype(jnp.float32)) * o
    x = x + jnp.concatenate([y_s5, y_ret], axis=-1).astype(x.dtype) @ w_out
    h2 = rmsnorm(x, norm2)

    def moe_fn(tok):
        return moe(tok, w_router, router_bias, w_gate_e, w_up_e, w_down_e, w_gate_sh, w_up_sh, w_down_sh)

    if map_rows:
        f = lax.map(moe_fn, h2)
    else:
        f = moe_fn(h2.reshape(b * length, D_MODEL)).reshape(b, length, D_MODEL)
    return x + f, s5_re, s5_im, ret_state


def setup_inputs(seed: int = 0) -> dict:
    key = jax.random.key(seed)
    ks = jax.random.split(key, 32)
    nrm = jax.random.normal
    f32 = jnp.float32
    lam_im_base = math.pi * jnp.arange(S5_STATE, dtype=f32)
    return {
        'x_prompt': nrm(ks[0], (BATCH, SEQ, D_MODEL), f32),
        'x_sample': nrm(ks[1], (DEC_BATCH, DEC_SEQ, D_MODEL), f32),
        'state_s5_re': 0.1 * nrm(ks[2], (DEPTH, DEC_BATCH, N_S5_GROUPS, S5_STATE), f32),
        'state_s5_im': 0.1 * nrm(ks[3], (DEPTH, DEC_BATCH, N_S5_GROUPS, S5_STATE), f32),
        'state_ret': nrm(ks[4], (DEPTH, DEC_BATCH, N_RET_HEADS, RET_HEAD_DIM, RET_HEAD_DIM), f32),
        'meta_tokens': nrm(ks[5], (N_META, D_MODEL), f32),
        'norm1': 1.0 + 0.02 * nrm(ks[6], (DEPTH, D_MODEL), f32),
        'w_in': nrm(ks[7], (DEPTH, D_MODEL, D_IN), f32) * D_MODEL ** -0.5,
        's5_lam_re': -0.5 + 0.01 * nrm(ks[8], (DEPTH, N_S5_GROUPS, S5_STATE), f32),
        's5_lam_im': lam_im_base + 0.01 * nrm(ks[9], (DEPTH, N_S5_GROUPS, S5_STATE), f32),
        's5_log_dt': jax.random.uniform(ks[10], (DEPTH, N_S5_GROUPS), f32, math.log(1e-3), math.log(1e-1)),
        's5_b_re': nrm(ks[11], (DEPTH, N_S5_GROUPS, S5_STATE, S5_GROUP), f32) * (2 * S5_GROUP) ** -0.5,
        's5_b_im': nrm(ks[12], (DEPTH, N_S5_GROUPS, S5_STATE, S5_GROUP), f32) * (2 * S5_GROUP) ** -0.5,
        's5_c_re': nrm(ks[13], (DEPTH, N_S5_GROUPS, S5_GROUP, S5_STATE), f32) * S5_STATE ** -0.5,
        's5_c_im': nrm(ks[14], (DEPTH, N_S5_GROUPS, S5_GROUP, S5_STATE), f32) * S5_STATE ** -0.5,
        's5_d': nrm(ks[15], (DEPTH, N_S5_GROUPS, S5_GROUP), f32),
        's5_w_glu': nrm(ks[16], (DEPTH, D_S5, D_S5), f32) * D_S5 ** -0.5,
        's5_norm': 1.0 + 0.02 * nrm(ks[17], (DEPTH, D_S5), f32),
        'ret_norm': 1.0 + 0.02 * nrm(ks[18], (DEPTH, D_RET), f32),
        'w_out': nrm(ks[19], (DEPTH, D_MODEL, D_MODEL), f32) * D_MODEL ** -0.5,
        'norm2': 1.0 + 0.02 * nrm(ks[20], (DEPTH, D_MODEL), f32),
        'w_router': nrm(ks[21], (DEPTH, D_MODEL, N_EXPERTS), f32) * D_MODEL ** -0.5,
        'router_bias': 0.01 * nrm(ks[22], (DEPTH, N_EXPERTS), f32),
        'w_gate_e': nrm(ks[23], (DEPTH, N_EXPERTS, D_MODEL, D_EXPERT), f32) * D_MODEL ** -0.5,
        'w_up_e': nrm(ks[24], (DEPTH, N_EXPERTS, D_MODEL, D_EXPERT), f32) * D_MODEL ** -0.5,
        'w_down_e': nrm(ks[25], (DEPTH, N_EXPERTS, D_EXPERT, D_MODEL), f32) * D_EXPERT ** -0.5,
        'w_gate_sh': nrm(ks[26], (DEPTH, D_MODEL, D_EXPERT), f32) * D_MODEL ** -0.5,
        'w_up_sh': nrm(ks[27], (DEPTH, D_MODEL, D_EXPERT), f32) * D_MODEL ** -0.5,
        'w_down_sh': nrm(ks[28], (DEPTH, D_EXPERT, D_MODEL), f32) * D_EXPERT ** -0.5,
        'final_norm': 1.0 + 0.02 * nrm(ks[29], (D_MODEL,), f32),
    }


def reference(x_prompt, x_sample, state_s5_re, state_s5_im, state_ret, meta_tokens, norm1, w_in,
              s5_lam_re, s5_lam_im, s5_log_dt, s5_b_re, s5_b_im, s5_c_re, s5_c_im, s5_d, s5_w_glu,
              s5_norm, ret_norm, w_out, norm2, w_router, router_bias, w_gate_e, w_up_e, w_down_e,
              w_gate_sh, w_up_sh, w_down_sh, final_norm):
    bp = x_prompt.shape[0]
    meta = jnp.broadcast_to(meta_tokens.astype(x_prompt.dtype)[None], (bp, N_META, D_MODEL))
    xp = jnp.concatenate([meta, x_prompt], axis=1)
    xs = x_sample
    pos_p = jnp.arange(xp.shape[1], dtype=jnp.int32)
    pos_s = PAST_LEN + jnp.arange(xs.shape[1], dtype=jnp.int32)
    zero_s5 = jnp.zeros((bp, N_S5_GROUPS, S5_STATE), jnp.float32)
    zero_ret = jnp.zeros((bp, N_RET_HEADS, RET_HEAD_DIM, RET_HEAD_DIM), jnp.float32)
    p_re, p_im, p_ret, s_re, s_im, s_ret = [], [], [], [], [], []
    for l in range(DEPTH):
        lp = (norm1[l], w_in[l], s5_lam_re[l], s5_lam_im[l], s5_log_dt[l], s5_b_re[l], s5_b_im[l],
              s5_c_re[l], s5_c_im[l], s5_d[l], s5_w_glu[l], s5_norm[l], ret_norm[l], w_out[l], norm2[l],
              w_router[l], router_bias[l], w_gate_e[l], w_up_e[l], w_down_e[l],
              w_gate_sh[l], w_up_sh[l], w_down_sh[l])
        xp, a_re, a_im, a_ret = decoder_layer(xp, pos_p, N_META, zero_s5, zero_s5, zero_ret, True, *lp)
        xs, c_re, c_im, c_ret = decoder_layer(xs, pos_s, xs.shape[1], state_s5_re[l], state_s5_im[l],
                                              state_ret[l], False, *lp)
        p_re.append(a_re)
        p_im.append(a_im)
        p_ret.append(a_ret)
        s_re.append(c_re)
        s_im.append(c_im)
        s_ret.append(c_ret)
    y_prompt = rmsnorm(xp, final_norm)[:, N_META:]
    y_sample = rmsnorm(xs, final_norm)
    s5_re_prompt = jnp.stack(p_re)
    s5_im_prompt = jnp.stack(p_im)
    ret_prompt = jnp.stack(p_ret)
    s5_re_sample = jnp.stack(s_re)
    s5_im_sample = jnp.stack(s_im)
    ret_sample = jnp.stack(s_ret)
    return (y_prompt, y_sample, s5_re_prompt, s5_im_prompt, ret_prompt, s5_re_sample, s5_im_sample, ret_sample)
```

```python
import functools
import math

import jax
import jax.numpy as jnp
from jax import lax
from jax.experimental import pallas as pl
from jax.experimental.pallas import tpu as pltpu

F32 = jnp.float32
BF16 = jnp.bfloat16

D_MODEL = 1024
N_META = 16
PAST_LEN = 16384
D_S5 = 512
S5_GROUP = 16
N_S5_GROUPS = 32
S5_STATE = 64
S5_LANES = N_S5_GROUPS * S5_STATE
D_RET = 512
N_RET_HEADS = 8
RET_HEAD_DIM = 64
N_HEAD_PAIRS = 4
ROPE_BASE = 10000.0
D_IN = D_S5 + 4 * D_RET
N_EXPERTS = 64
TOP_K = 8
N_EXPERT_GROUPS = 8
GROUP_SIZE = 8
TOPK_GROUPS = 4
D_EXPERT = 256
ROUTED_SCALE = 2.5
EPS = 1e-6

LANES = 128
SUBLANES = 8
VMEM_LIMIT = 56 * 1024 * 1024


def _cparams(*sem):
    return pltpu.CompilerParams(dimension_semantics=sem, vmem_limit_bytes=VMEM_LIMIT)


def _const_spec(shape):
    nd = len(shape)
    return pl.BlockSpec(shape, lambda *_: (0,) * nd)


def _rms(x, gain):
    return x * lax.rsqrt(jnp.mean(x * x, axis=-1, keepdims=True) + EPS) * gain


def _proj_kernel(x_ref, n1_ref, w_ref, cos_ref, sin_ref, u_ref, q_ref, k_ref, v_ref, g_ref):
    h = _rms(x_ref[...], n1_ref[...]).astype(BF16)
    proj = jnp.dot(h, w_ref[...], preferred_element_type=F32)
    cos = cos_ref[...]
    sin = sin_ref[...]
    lane = lax.broadcasted_iota(jnp.int32, cos.shape, 1)
    first_half = (lane % RET_HEAD_DIM) < (RET_HEAD_DIM // 2)

    def rotary(t):
        partner = jnp.where(first_half,
                            pltpu.roll(t, LANES - RET_HEAD_DIM // 2, 1),
                            pltpu.roll(t, RET_HEAD_DIM // 2, 1))
        return t * cos + partner * sin

    u_ref[...] = proj[:, :D_S5]
    for j in range(N_HEAD_PAIRS):
        lo = D_S5 + j * LANES
        q_ref[:, j * LANES:(j + 1) * LANES] = rotary(proj[:, lo:lo + LANES]).astype(BF16)
        lo += D_RET
        k_ref[:, j * LANES:(j + 1) * LANES] = (
            rotary(proj[:, lo:lo + LANES]) * (RET_HEAD_DIM ** -0.5)).astype(BF16)
    v_ref[...] = proj[:, D_S5 + 2 * D_RET:D_S5 + 3 * D_RET].astype(BF16)
    g_ref[...] = proj[:, D_S5 + 3 * D_RET:]


def _project(x, norm1, w_in_b, cos_t, sin_t, tm, table_blocks):
    rows = x.shape[0]
    row = lambda i: (i, 0)
    tab = lambda i: (i % table_blocks, 0)
    return pl.pallas_call(
        _proj_kernel,
        grid=(rows // tm,),
        in_specs=[pl.BlockSpec((tm, D_MODEL), row),
                  _const_spec((1, D_MODEL)),
                  _const_spec((D_MODEL, D_IN)),
                  pl.BlockSpec((tm, LANES), tab),
                  pl.BlockSpec((tm, LANES), tab)],
        out_specs=[pl.BlockSpec((tm, D_S5), row),
                   pl.BlockSpec((tm, D_RET), row),
                   pl.BlockSpec((tm, D_RET), row),
                   pl.BlockSpec((tm, D_RET), row),
                   pl.BlockSpec((tm, D_RET), row)],
        out_shape=[jax.ShapeDtypeStruct((rows, D_S5), F32),
                   jax.ShapeDtypeStruct((rows, D_RET), BF16),
                   jax.ShapeDtypeStruct((rows, D_RET), BF16),
                   jax.ShapeDtypeStruct((rows, D_RET), BF16),
                   jax.ShapeDtypeStruct((rows, D_RET), F32)],
        compiler_params=_cparams("parallel"),
        name="in_proj",
    )(x, norm1, w_in_b, cos_t, sin_t)


def _s5_kernel(u_ref, x0re_ref, x0im_ref, are_ref, aim_ref, b_ref, c_ref, d_ref, wglu_ref,
               nrm_ref, y_ref, sre_ref, sim_ref, st, sre, sim, utm, ytm, *, nb, tt, bt_major):
    i = pl.program_id(0)
    rows = nb * tt

    @pl.when(i == 0)
    def _():
        sre[...] = x0re_ref[...]
        sim[...] = x0im_ref[...]

    if bt_major:
        for t in range(tt):
            utm[t * nb:(t + 1) * nb, :] = u_ref[:, t, :]
    else:
        utm[...] = u_ref[...]

    ub = utm[...].astype(BF16)
    half = S5_LANES // 2
    kh = D_S5 // 2
    for part in range(2):
        for hf in range(2):
            st[:, part * S5_LANES + hf * half:part * S5_LANES + (hf + 1) * half] = jnp.dot(
                ub[:, hf * kh:(hf + 1) * kh], b_ref[part, hf], preferred_element_type=F32)

    lw = 512 if nb == SUBLANES else LANES
    for lg in range(S5_LANES // lw):
        re = slice(lg * lw, (lg + 1) * lw)
        im = slice(S5_LANES + lg * lw, S5_LANES + (lg + 1) * lw)
        a_re = are_ref[:, re]
        a_im = aim_ref[:, re]

        def step(t, carry):
            s_re, s_im = carry
            r0 = pl.multiple_of(t * nb, nb)
            n_re = a_re * s_re - a_im * s_im + st[pl.ds(r0, nb), re]
            n_im = a_re * s_im + a_im * s_re + st[pl.ds(r0, nb), im]
            st[pl.ds(r0, nb), re] = n_re
            st[pl.ds(r0, nb), im] = n_im
            return n_re, n_im

        f_re, f_im = lax.fori_loop(0, tt, step, (sre[:, re], sim[:, re]))
        sre[:, re] = f_re
        sim[:, re] = f_im

    sre_ref[...] = sre[...]
    sim_ref[...] = sim[...]

    ys = []
    for hf in range(2):
        xr = st[:, hf * half:(hf + 1) * half].astype(BF16)
        xi = st[:, S5_LANES + hf * half:S5_LANES + (hf + 1) * half].astype(BF16)
        ys.append(jnp.dot(xr, c_ref[0, hf], preferred_element_type=F32)
                  - jnp.dot(xi, c_ref[1, hf], preferred_element_type=F32))
    y = jnp.concatenate(ys, axis=1) + d_ref[...] * utm[...]
    y = jax.nn.gelu(y)
    y = y * jax.nn.sigmoid(jnp.dot(y.astype(BF16), wglu_ref[...], preferred_element_type=F32))
    y = _rms(y, nrm_ref[...]).astype(BF16)
    if bt_major:
        yf = y.astype(F32)
        for j in range(D_S5 // LANES):
            ytm[j] = yf[:, j * LANES:(j + 1) * LANES]
        for b in range(nb):
            for j in range(D_S5 // LANES):
                y_ref[b, :, j * LANES:(j + 1) * LANES] = (
                    ytm[j, pl.ds(b, tt, stride=nb), :].astype(BF16))
    else:
        y_ref[...] = y


def _s5(u, x0re, x0im, s5p, *, nb, tt, nblk, bt_major):
    rows = nb * tt
    if bt_major:
        u_spec = pl.BlockSpec((nb, tt, D_S5), lambda i: (0, i, 0))
        y_shape = jax.ShapeDtypeStruct((nb, tt * nblk, D_S5), BF16)
    else:
        u_spec = pl.BlockSpec((rows, D_S5), lambda i: (i, 0))
        y_shape = jax.ShapeDtypeStruct((rows * nblk, D_S5), BF16)
    state_spec = _const_spec((nb, S5_LANES))
    are = jnp.broadcast_to(s5p["are"], (nb, S5_LANES))
    aim = jnp.broadcast_to(s5p["aim"], (nb, S5_LANES))
    return pl.pallas_call(
        functools.partial(_s5_kernel, nb=nb, tt=tt, bt_major=bt_major),
        grid=(nblk,),
        in_specs=[u_spec, state_spec, state_spec, state_spec, state_spec,
                  _const_spec((2, 2, D_S5 // 2, S5_LANES // 2)),
                  _const_spec((2, 2, S5_LANES // 2, D_S5 // 2)),
                  _const_spec((1, D_S5)),
                  _const_spec((D_S5, D_S5)),
                  _const_spec((1, D_S5))],
        out_specs=[u_spec, state_spec, state_spec],
        out_shape=[y_shape,
                   jax.ShapeDtypeStruct((nb, S5_LANES), F32),
                   jax.ShapeDtypeStruct((nb, S5_LANES), F32)],
        scratch_shapes=[pltpu.VMEM((rows, 2 * S5_LANES), F32),
                        pltpu.VMEM((nb, S5_LANES), F32),
                        pltpu.VMEM((nb, S5_LANES), F32),
                        pltpu.VMEM((rows, D_S5), F32),
                        pltpu.VMEM((D_S5 // LANES, rows, LANES), F32)],
        compiler_params=_cparams("arbitrary"),
        name="s5_" + ("bt" if bt_major else "tm") + str(nb),
    )(u, x0re, x0im, are, aim, s5p["b"], s5p["c"], s5p["d"], s5p["wglu"], s5p["nrm"])


def _s5_params(lam_re, lam_im, log_dt, b_re, b_im, c_re, c_im, d_skip, w_glu, nrm):
    dt = jnp.exp(log_dt)[:, None]
    mag = jnp.exp(lam_re * dt)
    abar_re, abar_im = mag * jnp.cos(lam_im * dt), mag * jnp.sin(lam_im * dt)
    num_re, num_im = abar_re - 1.0, abar_im
    den = lam_re * lam_re + lam_im * lam_im
    f_re = (num_re * lam_re + num_im * lam_im) / den
    f_im = (num_im * lam_re - num_re * lam_im) / den
    bbar_re = f_re[..., None] * b_re - f_im[..., None] * b_im
    bbar_im = f_re[..., None] * b_im + f_im[..., None] * b_re
    hg = N_S5_GROUPS // 2
    eye = jnp.eye(hg, dtype=F32)

    def bdiag(bb):
        bb = bb.reshape(2, hg, S5_STATE, S5_GROUP)
        return jnp.einsum("zgph,gk->zghkp", bb, eye).reshape(2, hg * S5_GROUP, hg * S5_STATE)

    def cdiag(cc):
        cc = cc.reshape(2, hg, S5_GROUP, S5_STATE)
        return jnp.einsum("zgnp,gk->zgpkn", cc, eye).reshape(2, hg * S5_STATE, hg * S5_GROUP)

    return {
        "are": abar_re.reshape(1, S5_LANES), "aim": abar_im.reshape(1, S5_LANES),
        "b": jnp.stack([bdiag(bbar_re), bdiag(bbar_im)]).astype(BF16),
        "c": jnp.stack([cdiag(c_re), cdiag(c_im)]).astype(BF16),
        "d": d_skip.reshape(1, D_S5), "wglu": w_glu.astype(BF16), "nrm": nrm.reshape(1, D_S5),
    }


def _head_norm_gate(o, g, gain, lo):
    inv = 1.0 / RET_HEAD_DIM

    def seg_mean(t):
        s_lo = jnp.sum(jnp.where(lo, t, 0.0), axis=1, keepdims=True)
        s_hi = jnp.sum(jnp.where(lo, 0.0, t), axis=1, keepdims=True)
        return jnp.where(lo, s_lo, s_hi) * inv

    dlt = o - seg_mean(o)
    var = seg_mean(dlt * dlt)
    return jax.nn.silu(g) * (dlt * lax.rsqrt(var + EPS) * gain)


def _ret_kernel(q_ref, k_ref, v_ref, g_ref, s0_ref, dm_ref, qd_ref, kd_ref, gc_ref, bm_ref,
                nrm_ref, y_ref, so_ref, s_acc):
    c = pl.program_id(1)

    @pl.when(c == 0)
    def _():
        s_acc[...] = s0_ref[0]

    rows = q_ref.shape[0]
    lane = lax.broadcasted_iota(jnp.int32, (rows, LANES), 1)
    lo = lane < RET_HEAD_DIM
    for j in range(N_HEAD_PAIRS):
        sl = slice(j * LANES, (j + 1) * LANES)
        q2 = q_ref[:, sl].astype(F32)
        k2 = k_ref[:, sl]
        v2 = v_ref[:, sl]
        s_pair = s_acc[j]
        cross = jnp.dot((q2 * qd_ref[:, sl]).astype(BF16), s_pair.astype(BF16),
                        preferred_element_type=F32)
        k_dec = (k2.astype(F32) * kd_ref[:, sl]).astype(BF16)
        upd = lax.dot_general(k_dec, v2, (((0,), (0,)), ((), ())), preferred_element_type=F32)
        s_acc[j] = gc_ref[j] * s_pair + bm_ref[...] * upd
        inner = []
        for hh, qh in enumerate((jnp.where(lo, q2, 0.0), jnp.where(lo, 0.0, q2))):
            sc = lax.dot_general(qh.astype(BF16), k2, (((1,), (1,)), ((), ())),
                                 preferred_element_type=F32) * dm_ref[2 * j + hh]
            inner.append(jnp.dot(sc.astype(BF16), v2, preferred_element_type=F32))
        o = jnp.where(lo, inner[0], inner[1]) + cross
        y_ref[:, sl] = _head_norm_gate(o, g_ref[:, sl], nrm_ref[:, sl], lo).astype(BF16)
    so_ref[0] = s_acc[...]


def _ret_tables(chunk):
    log_g = jnp.log1p(-jnp.exp2(-5.0 - jnp.arange(N_RET_HEADS, dtype=F32)))
    n = jnp.arange(chunk, dtype=F32)
    diff = n[:, None] - n[None, :]
    dm = jnp.where(diff[None] >= 0.0,
                   jnp.exp(log_g[:, None, None] * jnp.maximum(diff, 0.0)[None]), 0.0)
    per_lane = lambda t: jnp.repeat(t.T, RET_HEAD_DIM, axis=1)
    qd = per_lane(jnp.exp(log_g[:, None] * (n + 1.0)[None]))
    kd = per_lane(jnp.exp(log_g[:, None] * (chunk - 1.0 - n)[None]))
    gch = jnp.exp(log_g * chunk)
    blk = jnp.kron(jnp.eye(2, dtype=F32), jnp.ones((RET_HEAD_DIM, RET_HEAD_DIM), F32))
    gc = jnp.repeat(gch.reshape(N_HEAD_PAIRS, 2), RET_HEAD_DIM, axis=1)[:, :, None] * blk[None]
    return dm, qd, kd, gc, blk


def _pair_state(s):
    b = s.shape[0]
    s = s.reshape(b, N_HEAD_PAIRS, 2, RET_HEAD_DIM, RET_HEAD_DIM)
    z = jnp.zeros_like(s[:, :, 0])
    top = jnp.concatenate([s[:, :, 0], z], axis=-1)
    bot = jnp.concatenate([z, s[:, :, 1]], axis=-1)
    return jnp.concatenate([top, bot], axis=-2)


def _unpair_state(s2):
    d = RET_HEAD_DIM
    return jnp.stack([s2[:, :, :d, :d], s2[:, :, d:, d:]], axis=2).reshape(
        s2.shape[0], N_RET_HEADS, d, d)


def _retention(q, k, v, g, s0_pair, ret_norm, *, nseq, chunk, nchunk):
    dm, qd, kd, gc, blk = _ret_tables(chunk)
    row = lambda b, c: (b * nchunk + c, 0)
    st_spec = pl.BlockSpec((1, N_HEAD_PAIRS, LANES, LANES), lambda b, c: (b, 0, 0, 0))
    blk_spec = pl.BlockSpec((chunk, D_RET), row)
    return pl.pallas_call(
        _ret_kernel,
        grid=(nseq, nchunk),
        in_specs=[blk_spec, blk_spec, blk_spec, blk_spec, st_spec,
                  _const_spec((N_RET_HEADS, chunk, chunk)),
                  _const_spec((chunk, D_RET)), _const_spec((chunk, D_RET)),
                  _const_spec((N_HEAD_PAIRS, LANES, LANES)), _const_spec((LANES, LANES)),
                  _const_spec((1, D_RET))],
        out_specs=[blk_spec, st_spec],
        out_shape=[jax.ShapeDtypeStruct((nseq * nchunk * chunk, D_RET), BF16),
                   jax.ShapeDtypeStruct((nseq, N_HEAD_PAIRS, LANES, LANES), F32)],
        scratch_shapes=[pltpu.VMEM((N_HEAD_PAIRS, LANES, LANES), F32)],
        compiler_params=_cparams("arbitrary", "arbitrary"),
        name="retention_c%d" % chunk,
    )(q, k, v, g, s0_pair, dm, qd, kd, gc, blk, ret_norm)


def _ret_step_kernel(q_ref, k_ref, vh_ref, gh_ref, s_ref, gamr_ref, gamh_ref, nrm_ref,
                     y_ref, so_ref, *, bb):
    head = lax.broadcasted_iota(jnp.int32, (N_RET_HEADS, D_RET), 0)
    lane = lax.broadcasted_iota(jnp.int32, (N_RET_HEADS, D_RET), 1)
    own = (lane // RET_HEAD_DIM) == head
    gam_rows = gamr_ref[...]
    gam_h = gamh_ref[...]
    qf = q_ref[...].astype(F32)
    kf = k_ref[...].astype(F32)
    hi = lax.Precision.HIGHEST
    for b in range(bb):
        qm = jnp.where(own, qf[b:b + 1, :], 0.0)
        km = jnp.where(own, kf[b:b + 1, :], 0.0)
        vh = vh_ref[b]
        s_b = s_ref[b]
        score = jnp.sum(qm * km, axis=1, keepdims=True)
        inner = score.astype(BF16).astype(F32) * vh
        q_dec = (qm * gam_h).astype(BF16).astype(F32)
        cross = jnp.dot(q_dec, s_b, precision=hi, preferred_element_type=F32)
        o = inner + cross
        dlt = o - jnp.mean(o, axis=1, keepdims=True)
        var = jnp.mean(dlt * dlt, axis=1, keepdims=True)
        y_ref[b] = jax.nn.silu(gh_ref[b]) * (dlt * lax.rsqrt(var + EPS) * nrm_ref[...])
        upd = lax.dot_general(km, vh, (((0,), (0,)), ((), ())), precision=hi,
                              preferred_element_type=F32)
        so_ref[b] = gam_rows * s_b + upd


def _retention_step(q, k, v, g, state, ret_norm, bb=16):
    n = q.shape[0]
    gam = jnp.exp(jnp.log1p(-jnp.exp2(-5.0 - jnp.arange(N_RET_HEADS, dtype=F32))))
    gam_rows = jnp.broadcast_to(jnp.repeat(gam, RET_HEAD_DIM)[:, None], (D_RET, RET_HEAD_DIM))
    gam_heads = jnp.broadcast_to(gam[:, None], (N_RET_HEADS, D_RET))
    vh = v.astype(F32).reshape(n, N_RET_HEADS, RET_HEAD_DIM)
    gh = g.reshape(n, N_RET_HEADS, RET_HEAD_DIM)
    s2 = state.reshape(n, D_RET, RET_HEAD_DIM)
    row = lambda i: (i, 0)
    row3 = lambda i: (i, 0, 0)
    y, s_new = pl.pallas_call(
        functools.partial(_ret_step_kernel, bb=bb),
        grid=(n // bb,),
        in_specs=[pl.BlockSpec((bb, D_RET), row), pl.BlockSpec((bb, D_RET), row),
                  pl.BlockSpec((bb, N_RET_HEADS, RET_HEAD_DIM), row3),
                  pl.BlockSpec((bb, N_RET_HEADS, RET_HEAD_DIM), row3),
                  pl.BlockSpec((bb, D_RET, RET_HEAD_DIM), row3),
                  _const_spec((D_RET, RET_HEAD_DIM)),
                  _const_spec((N_RET_HEADS, D_RET)),
                  _const_spec((N_RET_HEADS, RET_HEAD_DIM))],
        out_specs=[pl.BlockSpec((bb, N_RET_HEADS, RET_HEAD_DIM), row3),
                   pl.BlockSpec((bb, D_RET, RET_HEAD_DIM), row3)],
        out_shape=[jax.ShapeDtypeStruct((n, N_RET_HEADS, RET_HEAD_DIM), F32),
                   jax.ShapeDtypeStruct((n, D_RET, RET_HEAD_DIM), F32)],
        compiler_params=_cparams("parallel"),
        name="retention_step",
    )(q, k, vh, gh, s2, gam_rows, gam_heads, ret_norm.reshape(N_RET_HEADS, RET_HEAD_DIM))
    return (y.reshape(n, D_RET).astype(BF16),
            s_new.reshape(n, N_RET_HEADS, RET_HEAD_DIM, RET_HEAD_DIM))


def _first_max(vals, idxs, sentinel):
    m = jnp.max(vals, axis=0, keepdims=True)
    first = jnp.min(jnp.where(vals == m, idxs, sentinel), axis=0, keepdims=True)
    return m, idxs == first


def _route(scores, sel):
    t = sel.shape[1]
    neg = -jnp.inf
    member = lax.broadcasted_iota(jnp.int32, (GROUP_SIZE, t), 0).astype(F32)
    groups = [sel[g * GROUP_SIZE:(g + 1) * GROUP_SIZE, :] for g in range(N_EXPERT_GROUPS)]
    gscore = []
    for grp in groups:
        m1, pick = _first_max(grp, member, float(GROUP_SIZE))
        m2 = jnp.max(jnp.where(pick, neg, grp), axis=0, keepdims=True)
        gscore.append(m1 + m2)
    gs = jnp.concatenate(gscore, axis=0)
    gkeep = jnp.zeros(gs.shape, F32)
    for _ in range(TOPK_GROUPS):
        _, pick = _first_max(gs, member, float(N_EXPERT_GROUPS))
        gkeep = jnp.where(pick, 1.0, gkeep)
        gs = jnp.where(pick, neg, gs)
    cand = jnp.concatenate(
        [jnp.where(gkeep[g:g + 1, :] > 0.5, groups[g], neg) for g in range(N_EXPERT_GROUPS)],
        axis=0)
    expert = lax.broadcasted_iota(jnp.int32, (N_EXPERTS, t), 0).astype(F32)
    w = jnp.zeros(cand.shape, F32)
    for _ in range(TOP_K):
        _, pick = _first_max(cand, expert, float(N_EXPERTS))
        w = jnp.where(pick, scores, w)
        cand = jnp.where(pick, neg, cand)
    return w / jnp.sum(w, axis=0, keepdims=True) * ROUTED_SCALE


def _mix_kernel(x_ref, ys_ref, yr_ref, woa_ref, wob_ref, n2_ref, wrt_ref, rb_ref,
                wgs_ref, wus_ref, wds_ref, xs_ref, h2_ref, gt_ref):
    x2 = (x_ref[...]
          + jnp.dot(ys_ref[...], woa_ref[...], preferred_element_type=F32)
          + jnp.dot(yr_ref[...], wob_ref[...], preferred_element_type=F32))
    h2f = _rms(x2, n2_ref[...])
    h2 = h2f.astype(BF16)
    h2_ref[...] = h2
    logits = lax.dot_general(wrt_ref[...], h2f, (((1,), (1,)), ((), ())),
                             precision=lax.Precision.HIGHEST, preferred_element_type=F32)
    scores = jax.nn.sigmoid(logits)
    gt_ref[...] = _route(scores, scores + rb_ref[:, :1])
    a = jnp.dot(h2, wgs_ref[...], preferred_element_type=F32)
    b = jnp.dot(h2, wus_ref[...], preferred_element_type=F32)
    xs_ref[...] = x2 + jnp.dot((jax.nn.silu(a) * b).astype(BF16), wds_ref[...],
                               preferred_element_type=F32)


def _mix(x, ys5, yret, mp, tm):
    rows = x.shape[0]
    row = lambda i: (i, 0)
    return pl.pallas_call(
        _mix_kernel,
        grid=(rows // tm,),
        in_specs=[pl.BlockSpec((tm, D_MODEL), row),
                  pl.BlockSpec((tm, D_S5), row), pl.BlockSpec((tm, D_RET), row),
                  _const_spec((D_S5, D_MODEL)), _const_spec((D_RET, D_MODEL)),
                  _const_spec((1, D_MODEL)),
                  _const_spec((N_EXPERTS, D_MODEL)), _const_spec((N_EXPERTS, LANES)),
                  _const_spec((D_MODEL, D_EXPERT)), _const_spec((D_MODEL, D_EXPERT)),
                  _const_spec((D_EXPERT, D_MODEL))],
        out_specs=[pl.BlockSpec((tm, D_MODEL), row), pl.BlockSpec((tm, D_MODEL), row),
                   pl.BlockSpec((N_EXPERTS, tm), lambda i: (0, i))],
        out_shape=[jax.ShapeDtypeStruct((rows, D_MODEL), F32),
                   jax.ShapeDtypeStruct((rows, D_MODEL), BF16),
                   jax.ShapeDtypeStruct((N_EXPERTS, rows), F32)],
        compiler_params=_cparams("parallel"),
        name="out_proj_router",
    )(x, ys5, yret, mp["woa"], mp["wob"], mp["n2"], mp["wrt"], mp["rb"],
      mp["wgs"], mp["wus"], mp["wds"])


def _moe_kernel(h_ref, gate_ref, wg_ref, wu_ref, wd_ref, xs_ref, fn_ref, y_ref, acc):
    e = pl.program_id(1)

    @pl.when(e == 0)
    def _():
        acc[...] = jnp.zeros_like(acc)

    gates = gate_ref[...]
    lane = lax.broadcasted_iota(jnp.int32, gates.shape, 1)
    gcol = jnp.sum(jnp.where(lane == e, gates, 0.0), axis=1, keepdims=True)
    h = h_ref[...]
    a = jnp.dot(h, wg_ref[0], preferred_element_type=F32)
    b = jnp.dot(h, wu_ref[0], preferred_element_type=F32)
    act = (jax.nn.silu(a) * b * gcol).astype(BF16)
    acc[...] += jnp.dot(act, wd_ref[0], preferred_element_type=F32)

    @pl.when(e == N_EXPERTS - 1)
    def _():
        y_ref[...] = _rms(xs_ref[...] + acc[...], fn_ref[...])


def _moe(h2, gates, xs, ep, final_norm, tm):
    rows = h2.shape[0]
    row = lambda i, e: (i, 0)
    return pl.pallas_call(
        _moe_kernel,
        grid=(rows // tm, N_EXPERTS),
        in_specs=[pl.BlockSpec((tm, D_MODEL), row),
                  pl.BlockSpec((tm, N_EXPERTS), row),
                  pl.BlockSpec((1, D_MODEL, D_EXPERT), lambda i, e: (e, 0, 0)),
                  pl.BlockSpec((1, D_MODEL, D_EXPERT), lambda i, e: (e, 0, 0)),
                  pl.BlockSpec((1, D_EXPERT, D_MODEL), lambda i, e: (e, 0, 0)),
                  pl.BlockSpec((tm, D_MODEL), row),
                  _const_spec((1, D_MODEL))],
        out_specs=pl.BlockSpec((tm, D_MODEL), row),
        out_shape=jax.ShapeDtypeStruct((rows, D_MODEL), F32),
        scratch_shapes=[pltpu.VMEM((tm, D_MODEL), F32)],
        compiler_params=_cparams("parallel", "arbitrary"),
        name="moe_experts",
    )(h2, gates, ep["wg"], ep["wu"], ep["wd"], xs, final_norm)


def _rope_tables(pos):
    half = RET_HEAD_DIM // 2
    inv_freq = ROPE_BASE ** (-jnp.arange(half, dtype=F32) / half)
    ang = pos.astype(F32)[:, None] * inv_freq[None, :]
    cos, sin = jnp.cos(ang), jnp.sin(ang)
    cos_t = jnp.concatenate([cos, cos, cos, cos], axis=1)
    sin_t = jnp.concatenate([-sin, sin, -sin, sin], axis=1)
    return cos_t, sin_t


def kernel(x_prompt, x_sample, state_s5_re, state_s5_im, state_ret, meta_tokens, norm1, w_in,
           s5_lam_re, s5_lam_im, s5_log_dt, s5_b_re, s5_b_im, s5_c_re, s5_c_im, s5_d, s5_w_glu,
           s5_norm, ret_norm, w_out, norm2, w_router, router_bias, w_gate_e, w_up_e, w_down_e,
           w_gate_sh, w_up_sh, w_down_sh, final_norm):
    assert norm1.shape[0] == 1, "single-layer model"
    bp, seq, _ = x_prompt.shape
    ns = x_sample.shape[0]
    l = 0

    n1 = norm1[l].reshape(1, D_MODEL)
    w_in_b = w_in[l].astype(BF16)
    s5p = _s5_params(s5_lam_re[l], s5_lam_im[l], s5_log_dt[l], s5_b_re[l], s5_b_im[l],
                     s5_c_re[l], s5_c_im[l], s5_d[l], s5_w_glu[l], s5_norm[l])
    rnorm = ret_norm[l].reshape(1, D_RET)
    w_out_b = w_out[l].astype(BF16)
    mp = {
        "woa": w_out_b[:D_S5], "wob": w_out_b[D_S5:], "n2": norm2[l].reshape(1, D_MODEL),
        "wrt": w_router[l].T,
        "rb": jnp.broadcast_to(router_bias[l][:, None], (N_EXPERTS, LANES)),
        "wgs": w_gate_sh[l].astype(BF16), "wus": w_up_sh[l].astype(BF16),
        "wds": w_down_sh[l].astype(BF16),
    }
    ep = {"wg": w_gate_e[l].astype(BF16), "wu": w_up_e[l].astype(BF16),
          "wd": w_down_e[l].astype(BF16)}
    fnorm = final_norm.reshape(1, D_MODEL)

    xp = x_prompt.reshape(bp * seq, D_MODEL)
    cos_p, sin_p = _rope_tables(N_META + jnp.arange(seq, dtype=jnp.int32))
    tm_a = 512
    up, qp, kp, vp, gp = _project(xp, n1, w_in_b, cos_p, sin_p, tm_a, seq // tm_a)

    meta_tm = jnp.repeat(meta_tokens, SUBLANES, axis=0)
    mchunk = LANES
    meta_chunk = jnp.concatenate([jnp.zeros((mchunk - N_META, D_MODEL), F32), meta_tokens], axis=0)
    x_small = jnp.concatenate([x_sample.reshape(ns, D_MODEL), meta_tm, meta_chunk], axis=0)
    meta_pos = jnp.arange(N_META, dtype=jnp.int32)
    pos_small = jnp.concatenate([jnp.full((ns,), PAST_LEN, jnp.int32),
                                 jnp.repeat(meta_pos, SUBLANES),
                                 jnp.zeros((mchunk - N_META,), jnp.int32), meta_pos])
    cos_s, sin_s = _rope_tables(pos_small)
    n_small = x_small.shape[0]
    n_tm = ns + N_META * SUBLANES
    us, qs, ks, vs, gs = _project(x_small, n1, w_in_b, cos_s, sin_s, n_small, 1)

    zero8 = jnp.zeros((SUBLANES, S5_LANES), F32)
    _, m_re, m_im = _s5(us[ns:n_tm], zero8, zero8, s5p, nb=SUBLANES, tt=N_META, nblk=1,
                        bt_major=False)
    tt = 64
    ys5_p, p_re, p_im = _s5(up.reshape(bp, seq, D_S5), m_re, m_im, s5p, nb=bp, tt=tt,
                            nblk=seq // tt, bt_major=True)
    ys5_p = ys5_p.reshape(bp * seq, D_S5)
    ys5_s, s_re, s_im = _s5(us[:ns], state_s5_re[l].reshape(ns, S5_LANES),
                            state_s5_im[l].reshape(ns, S5_LANES), s5p, nb=ns, tt=1, nblk=1,
                            bt_major=False)

    zero_pair = jnp.zeros((1, N_HEAD_PAIRS, LANES, LANES), F32)
    _, m_pair = _retention(qs[n_tm:], ks[n_tm:], vs[n_tm:], gs[n_tm:],
                           zero_pair, rnorm, nseq=1, chunk=mchunk, nchunk=1)
    chunk = 256
    yret_p, p_pair = _retention(qp, kp, vp, gp, jnp.broadcast_to(m_pair, (bp,) + m_pair.shape[1:]),
                                rnorm, nseq=bp, chunk=chunk, nchunk=seq // chunk)
    yret_s, ret_s = _retention_step(qs[:ns], ks[:ns], vs[:ns], gs[:ns], state_ret[l], ret_norm[l])

    xs_p, h2_p, gt_p = _mix(xp, ys5_p, yret_p, mp, 512)
    xs_s, h2_s, gt_s = _mix(x_sample.reshape(ns, D_MODEL), ys5_s, yret_s, mp, ns)
    y_p = _moe(h2_p, gt_p.T, xs_p, ep, fnorm, 1024)
    y_s = _moe(h2_s, gt_s.T, xs_s, ep, fnorm, ns)

    shape5 = (1, bp, N_S5_GROUPS, S5_STATE)
    return (y_p.reshape(bp, seq, D_MODEL),
            y_s.reshape(ns, 1, D_MODEL),
            p_re.reshape(shape5), p_im.reshape(shape5),
            _unpair_state(p_pair)[None],
            s_re.reshape(1, ns, N_S5_GROUPS, S5_STATE), s_im.reshape(1, ns, N_S5_GROUPS, S5_STATE),
            ret_s[None])
```

```python
import functools
import math

import jax
import jax.numpy as jnp
from jax import lax
from jax.experimental import pallas as pl
from jax.experimental.pallas import tpu as pltpu

F32 = jnp.float32
BF16 = jnp.bfloat16

D_MODEL = 1024
N_META = 16
PAST_LEN = 16384
D_S5 = 512
S5_GROUP = 16
N_S5_GROUPS = 32
S5_STATE = 64
S5_LANES = N_S5_GROUPS * S5_STATE
D_RET = 512
N_RET_HEADS = 8
RET_HEAD_DIM = 64
N_HEAD_PAIRS = 4
ROPE_BASE = 10000.0
D_IN = D_S5 + 4 * D_RET
N_EXPERTS = 64
TOP_K = 8
N_EXPERT_GROUPS = 8
GROUP_SIZE = 8
TOPK_GROUPS = 4
D_EXPERT = 256
ROUTED_SCALE = 2.5
EPS = 1e-6

LANES = 128
SUBLANES = 8
VMEM_LIMIT = 56 * 1024 * 1024

PACK_ROWS = D_MODEL // 2 // LANES
ACC_ROWS = D_MODEL // LANES
MOE_BLOCKS = 3
MOE_BLOCK_TOKENS = 5632
MOE_TILE = 256
MOE_SUB = 4
ROW_UNROLL = 8
OT_PITCH = MOE_TILE + SUBLANES
XT_PITCH = MOE_TILE + SUBLANES
META_ROWS = 2
TOKEN_BITS = 13
ZERO_ROWS = 64


def _cparams(*sem):
    return pltpu.CompilerParams(dimension_semantics=sem, vmem_limit_bytes=VMEM_LIMIT)


def _const_spec(shape):
    nd = len(shape)
    return pl.BlockSpec(shape, lambda *_: (0,) * nd)


def _rms(x, gain):
    return x * lax.rsqrt(jnp.mean(x * x, axis=-1, keepdims=True) + EPS) * gain


def _proj_kernel(x_ref, n1_ref, w_ref, cos_ref, sin_ref, u_ref, q_ref, k_ref, v_ref, g_ref):
    h = _rms(x_ref[...], n1_ref[...]).astype(BF16)
    proj = jnp.dot(h, w_ref[...], preferred_element_type=F32)
    cos = cos_ref[...]
    sin = sin_ref[...]
    lane = lax.broadcasted_iota(jnp.int32, cos.shape, 1)
    first_half = (lane % RET_HEAD_DIM) < (RET_HEAD_DIM // 2)

    def rotary(t):
        partner = jnp.where(first_half,
                            pltpu.roll(t, LANES - RET_HEAD_DIM // 2, 1),
                            pltpu.roll(t, RET_HEAD_DIM // 2, 1))
        return t * cos + partner * sin

    u_ref[...] = proj[:, :D_S5]
    for j in range(N_HEAD_PAIRS):
        lo = D_S5 + j * LANES
        q_ref[:, j * LANES:(j + 1) * LANES] = rotary(proj[:, lo:lo + LANES]).astype(BF16)
        lo += D_RET
        k_ref[:, j * LANES:(j + 1) * LANES] = (
            rotary(proj[:, lo:lo + LANES]) * (RET_HEAD_DIM ** -0.5)).astype(BF16)
    v_ref[...] = proj[:, D_S5 + 2 * D_RET:D_S5 + 3 * D_RET].astype(BF16)
    g_ref[...] = proj[:, D_S5 + 3 * D_RET:]


def _project(x, norm1, w_in_b, cos_t, sin_t, tm, table_blocks):
    rows = x.shape[0]
    row = lambda i: (i, 0)
    tab = lambda i: (i % table_blocks, 0)
    return pl.pallas_call(
        _proj_kernel,
        grid=(rows // tm,),
        in_specs=[pl.BlockSpec((tm, D_MODEL), row),
                  _const_spec((1, D_MODEL)),
                  _const_spec((D_MODEL, D_IN)),
                  pl.BlockSpec((tm, LANES), tab),
                  pl.BlockSpec((tm, LANES), tab)],
        out_specs=[pl.BlockSpec((tm, D_S5), row),
                   pl.BlockSpec((tm, D_RET), row),
                   pl.BlockSpec((tm, D_RET), row),
                   pl.BlockSpec((tm, D_RET), row),
                   pl.BlockSpec((tm, D_RET), row)],
        out_shape=[jax.ShapeDtypeStruct((rows, D_S5), F32),
                   jax.ShapeDtypeStruct((rows, D_RET), BF16),
                   jax.ShapeDtypeStruct((rows, D_RET), BF16),
                   jax.ShapeDtypeStruct((rows, D_RET), BF16),
                   jax.ShapeDtypeStruct((rows, D_RET), F32)],
        compiler_params=_cparams("parallel"),
        name="in_proj",
    )(x, norm1, w_in_b, cos_t, sin_t)


def _s5_kernel(u_ref, x0re_ref, x0im_ref, are_ref, aim_ref, b_ref, c_ref, d_ref, wglu_ref,
               nrm_ref, y_ref, sre_ref, sim_ref, st, sre, sim, utm, ytm, *, nb, tt, bt_major):
    i = pl.program_id(0)
    rows = nb * tt

    @pl.when(i == 0)
    def _():
        sre[...] = x0re_ref[...]
        sim[...] = x0im_ref[...]

    if bt_major:
        for t in range(tt):
            utm[t * nb:(t + 1) * nb, :] = u_ref[:, t, :]
    else:
        utm[...] = u_ref[...]

    ub = utm[...].astype(BF16)
    half = S5_LANES // 2
    kh = D_S5 // 2
    for part in range(2):
        for hf in range(2):
            st[:, part * S5_LANES + hf * half:part * S5_LANES + (hf + 1) * half] = jnp.dot(
                ub[:, hf * kh:(hf + 1) * kh], b_ref[part, hf], preferred_element_type=F32)

    lw = 512 if nb == SUBLANES else LANES
    for lg in range(S5_LANES // lw):
        re = slice(lg * lw, (lg + 1) * lw)
        im = slice(S5_LANES + lg * lw, S5_LANES + (lg + 1) * lw)
        a_re = are_ref[:, re]
        a_im = aim_ref[:, re]

        def step(t, carry):
            s_re, s_im = carry
            r0 = pl.multiple_of(t * nb, nb)
            n_re = a_re * s_re - a_im * s_im + st[pl.ds(r0, nb), re]
            n_im = a_re * s_im + a_im * s_re + st[pl.ds(r0, nb), im]
            st[pl.ds(r0, nb), re] = n_re
            st[pl.ds(r0, nb), im] = n_im
            return n_re, n_im

        f_re, f_im = lax.fori_loop(0, tt, step, (sre[:, re], sim[:, re]))
        sre[:, re] = f_re
        sim[:, re] = f_im

    sre_ref[...] = sre[...]
    sim_ref[...] = sim[...]

    ys = []
    for hf in range(2):
        xr = st[:, hf * half:(hf + 1) * half].astype(BF16)
        xi = st[:, S5_LANES + hf * half:S5_LANES + (hf + 1) * half].astype(BF16)
        ys.append(jnp.dot(xr, c_ref[0, hf], preferred_element_type=F32)
                  - jnp.dot(xi, c_ref[1, hf], preferred_element_type=F32))
    y = jnp.concatenate(ys, axis=1) + d_ref[...] * utm[...]
    y = jax.nn.gelu(y)
    y = y * jax.nn.sigmoid(jnp.dot(y.astype(BF16), wglu_ref[...], preferred_element_type=F32))
    y = _rms(y, nrm_ref[...]).astype(BF16)
    if bt_major:
        yf = y.astype(F32)
        for j in range(D_S5 // LANES):
            ytm[j] = yf[:, j * LANES:(j + 1) * LANES]
        for b in range(nb):
            for j in range(D_S5 // LANES):
                y_ref[b, :, j * LANES:(j + 1) * LANES] = (
                    ytm[j, pl.ds(b, tt, stride=nb), :].astype(BF16))
    else:
        y_ref[...] = y


def _s5(u, x0re, x0im, s5p, *, nb, tt, nblk, bt_major):
    rows = nb * tt
    if bt_major:
        u_spec = pl.BlockSpec((nb, tt, D_S5), lambda i: (0, i, 0))
        y_shape = jax.ShapeDtypeStruct((nb, tt * nblk, D_S5), BF16)
    else:
        u_spec = pl.BlockSpec((rows, D_S5), lambda i: (i, 0))
        y_shape = jax.ShapeDtypeStruct((rows * nblk, D_S5), BF16)
    state_spec = _const_spec((nb, S5_LANES))
    are = jnp.broadcast_to(s5p["are"], (nb, S5_LANES))
    aim = jnp.broadcast_to(s5p["aim"], (nb, S5_LANES))
    return pl.pallas_call(
        functools.partial(_s5_kernel, nb=nb, tt=tt, bt_major=bt_major),
        grid=(nblk,),
        in_specs=[u_spec, state_spec, state_spec, state_spec, state_spec,
                  _const_spec((2, 2, D_S5 // 2, S5_LANES // 2)),
                  _const_spec((2, 2, S5_LANES // 2, D_S5 // 2)),
                  _const_spec((1, D_S5)),
                  _const_spec((D_S5, D_S5)),
                  _const_spec((1, D_S5))],
        out_specs=[u_spec, state_spec, state_spec],
        out_shape=[y_shape,
                   jax.ShapeDtypeStruct((nb, S5_LANES), F32),
                   jax.ShapeDtypeStruct((nb, S5_LANES), F32)],
        scratch_shapes=[pltpu.VMEM((rows, 2 * S5_LANES), F32),
                        pltpu.VMEM((nb, S5_LANES), F32),
                        pltpu.VMEM((nb, S5_LANES), F32),
                        pltpu.VMEM((rows, D_S5), F32),
                        pltpu.VMEM((D_S5 // LANES, rows, LANES), F32)],
        compiler_params=_cparams("arbitrary"),
        name="s5_" + ("bt" if bt_major else "tm") + str(nb),
    )(u, x0re, x0im, are, aim, s5p["b"], s5p["c"], s5p["d"], s5p["wglu"], s5p["nrm"])


def _s5_params(lam_re, lam_im, log_dt, b_re, b_im, c_re, c_im, d_skip, w_glu, nrm):
    dt = jnp.exp(log_dt)[:, None]
    mag = jnp.exp(lam_re * dt)
    abar_re, abar_im = mag * jnp.cos(lam_im * dt), mag * jnp.sin(lam_im * dt)
    num_re, num_im = abar_re - 1.0, abar_im
    den = lam_re * lam_re + lam_im * lam_im
    f_re = (num_re * lam_re + num_im * lam_im) / den
    f_im = (num_im * lam_re - num_re * lam_im) / den
    bbar_re = f_re[..., None] * b_re - f_im[..., None] * b_im
    bbar_im = f_re[..., None] * b_im + f_im[..., None] * b_re
    hg = N_S5_GROUPS // 2
    eye = jnp.eye(hg, dtype=F32)

    def bdiag(bb):
        bb = bb.reshape(2, hg, S5_STATE, S5_GROUP)
        return jnp.einsum("zgph,gk->zghkp", bb, eye).reshape(2, hg * S5_GROUP, hg * S5_STATE)

    def cdiag(cc):
        cc = cc.reshape(2, hg, S5_GROUP, S5_STATE)
        return jnp.einsum("zgnp,gk->zgpkn", cc, eye).reshape(2, hg * S5_STATE, hg * S5_GROUP)

    return {
        "are": abar_re.reshape(1, S5_LANES), "aim": abar_im.reshape(1, S5_LANES),
        "b": jnp.stack([bdiag(bbar_re), bdiag(bbar_im)]).astype(BF16),
        "c": jnp.stack([cdiag(c_re), cdiag(c_im)]).astype(BF16),
        "d": d_skip.reshape(1, D_S5), "wglu": w_glu.astype(BF16), "nrm": nrm.reshape(1, D_S5),
    }


def _head_norm_gate(o, g, gain, lo):
    inv = 1.0 / RET_HEAD_DIM

    def seg_mean(t):
        s_lo = jnp.sum(jnp.where(lo, t, 0.0), axis=1, keepdims=True)
        s_hi = jnp.sum(jnp.where(lo, 0.0, t), axis=1, keepdims=True)
        return jnp.where(lo, s_lo, s_hi) * inv

    dlt = o - seg_mean(o)
    var = seg_mean(dlt * dlt)
    return jax.nn.silu(g) * (dlt * lax.rsqrt(var + EPS) * gain)


def _ret_kernel(q_ref, k_ref, v_ref, g_ref, s0_ref, dm_ref, qd_ref, kd_ref, gc_ref, bm_ref,
                nrm_ref, y_ref, so_ref, s_acc):
    c = pl.program_id(1)

    @pl.when(c == 0)
    def _():
        s_acc[...] = s0_ref[0]

    rows = q_ref.shape[0]
    lane = lax.broadcasted_iota(jnp.int32, (rows, LANES), 1)
    lo = lane < RET_HEAD_DIM
    for j in range(N_HEAD_PAIRS):
        sl = slice(j * LANES, (j + 1) * LANES)
        q2 = q_ref[:, sl].astype(F32)
        k2 = k_ref[:, sl]
        v2 = v_ref[:, sl]
        s_pair = s_acc[j]
        cross = jnp.dot((q2 * qd_ref[:, sl]).astype(BF16), s_pair.astype(BF16),
                        preferred_element_type=F32)
        k_dec = (k2.astype(F32) * kd_ref[:, sl]).astype(BF16)
        upd = lax.dot_general(k_dec, v2, (((0,), (0,)), ((), ())), preferred_element_type=F32)
        s_acc[j] = gc_ref[j] * s_pair + bm_ref[...] * upd
        inner = []
        for hh, qh in enumerate((jnp.where(lo, q2, 0.0), jnp.where(lo, 0.0, q2))):
            sc = lax.dot_general(qh.astype(BF16), k2, (((1,), (1,)), ((), ())),
                                 preferred_element_type=F32) * dm_ref[2 * j + hh]
            inner.append(jnp.dot(sc.astype(BF16), v2, preferred_element_type=F32))
        o = jnp.where(lo, inner[0], inner[1]) + cross
        y_ref[:, sl] = _head_norm_gate(o, g_ref[:, sl], nrm_ref[:, sl], lo).astype(BF16)
    so_ref[0] = s_acc[...]


def _ret_tables(chunk):
    log_g = jnp.log1p(-jnp.exp2(-5.0 - jnp.arange(N_RET_HEADS, dtype=F32)))
    n = jnp.arange(chunk, dtype=F32)
    diff = n[:, None] - n[None, :]
    dm = jnp.where(diff[None] >= 0.0,
                   jnp.exp(log_g[:, None, None] * jnp.maximum(diff, 0.0)[None]), 0.0)
    per_lane = lambda t: jnp.repeat(t.T, RET_HEAD_DIM, axis=1)
    qd = per_lane(jnp.exp(log_g[:, None] * (n + 1.0)[None]))
    kd = per_lane(jnp.exp(log_g[:, None] * (chunk - 1.0 - n)[None]))
    gch = jnp.exp(log_g * chunk)
    blk = jnp.kron(jnp.eye(2, dtype=F32), jnp.ones((RET_HEAD_DIM, RET_HEAD_DIM), F32))
    gc = jnp.repeat(gch.reshape(N_HEAD_PAIRS, 2), RET_HEAD_DIM, axis=1)[:, :, None] * blk[None]
    return dm, qd, kd, gc, blk


def _pair_state(s):
    b = s.shape[0]
    s = s.reshape(b, N_HEAD_PAIRS, 2, RET_HEAD_DIM, RET_HEAD_DIM)
    z = jnp.zeros_like(s[:, :, 0])
    top = jnp.concatenate([s[:, :, 0], z], axis=-1)
    bot = jnp.concatenate([z, s[:, :, 1]], axis=-1)
    return jnp.concatenate([top, bot], axis=-2)


def _unpair_state(s2):
    d = RET_HEAD_DIM
    return jnp.stack([s2[:, :, :d, :d], s2[:, :, d:, d:]], axis=2).reshape(
        s2.shape[0], N_RET_HEADS, d, d)


def _retention(q, k, v, g, s0_pair, ret_norm, *, nseq, chunk, nchunk):
    dm, qd, kd, gc, blk = _ret_tables(chunk)
    row = lambda b, c: (b * nchunk + c, 0)
    st_spec = pl.BlockSpec((1, N_HEAD_PAIRS, LANES, LANES), lambda b, c: (b, 0, 0, 0))
    blk_spec = pl.BlockSpec((chunk, D_RET), row)
    return pl.pallas_call(
        _ret_kernel,
        grid=(nseq, nchunk),
        in_specs=[blk_spec, blk_spec, blk_spec, blk_spec, st_spec,
                  _const_spec((N_RET_HEADS, chunk, chunk)),
                  _const_spec((chunk, D_RET)), _const_spec((chunk, D_RET)),
                  _const_spec((N_HEAD_PAIRS, LANES, LANES)), _const_spec((LANES, LANES)),
                  _const_spec((1, D_RET))],
        out_specs=[blk_spec, st_spec],
        out_shape=[jax.ShapeDtypeStruct((nseq * nchunk * chunk, D_RET), BF16),
                   jax.ShapeDtypeStruct((nseq, N_HEAD_PAIRS, LANES, LANES), F32)],
        scratch_shapes=[pltpu.VMEM((N_HEAD_PAIRS, LANES, LANES), F32)],
        compiler_params=_cparams("arbitrary", "arbitrary"),
        name="retention_c%d" % chunk,
    )(q, k, v, g, s0_pair, dm, qd, kd, gc, blk, ret_norm)


def _ret_step_kernel(q_ref, k_ref, vh_ref, gh_ref, s_ref, gamr_ref, gamh_ref, nrm_ref,
                     y_ref, so_ref, *, bb):
    head = lax.broadcasted_iota(jnp.int32, (N_RET_HEADS, D_RET), 0)
    lane = lax.broadcasted_iota(jnp.int32, (N_RET_HEADS, D_RET), 1)
    own = (lane // RET_HEAD_DIM) == head
    gam_rows = gamr_ref[...]
    gam_h = gamh_ref[...]
    qf = q_ref[...].astype(F32)
    kf = k_ref[...].astype(F32)
    hi = lax.Precision.HIGHEST
    for b in range(bb):
        qm = jnp.where(own, qf[b:b + 1, :], 0.0)
        km = jnp.where(own, kf[b:b + 1, :], 0.0)
        vh = vh_ref[b]
        s_b = s_ref[b]
        score = jnp.sum(qm * km, axis=1, keepdims=True)
        inner = score.astype(BF16).astype(F32) * vh
        q_dec = (qm * gam_h).astype(BF16).astype(F32)
        cross = jnp.dot(q_dec, s_b, precision=hi, preferred_element_type=F32)
        o = inner + cross
        dlt = o - jnp.mean(o, axis=1, keepdims=True)
        var = jnp.mean(dlt * dlt, axis=1, keepdims=True)
        y_ref[b] = jax.nn.silu(gh_ref[b]) * (dlt * lax.rsqrt(var + EPS) * nrm_ref[...])
        upd = lax.dot_general(km, vh, (((0,), (0,)), ((), ())), precision=hi,
                              preferred_element_type=F32)
        so_ref[b] = gam_rows * s_b + upd


def _retention_step(q, k, v, g, state, ret_norm, bb=16):
    n = q.shape[0]
    gam = jnp.exp(jnp.log1p(-jnp.exp2(-5.0 - jnp.arange(N_RET_HEADS, dtype=F32))))
    gam_rows = jnp.broadcast_to(jnp.repeat(gam, RET_HEAD_DIM)[:, None], (D_RET, RET_HEAD_DIM))
    gam_heads = jnp.broadcast_to(gam[:, None], (N_RET_HEADS, D_RET))
    vh = v.astype(F32).reshape(n, N_RET_HEADS, RET_HEAD_DIM)
    gh = g.reshape(n, N_RET_HEADS, RET_HEAD_DIM)
    s2 = state.reshape(n, D_RET, RET_HEAD_DIM)
    row = lambda i: (i, 0)
    row3 = lambda i: (i, 0, 0)
    y, s_new = pl.pallas_call(
        functools.partial(_ret_step_kernel, bb=bb),
        grid=(n // bb,),
        in_specs=[pl.BlockSpec((bb, D_RET), row), pl.BlockSpec((bb, D_RET), row),
                  pl.BlockSpec((bb, N_RET_HEADS, RET_HEAD_DIM), row3),
                  pl.BlockSpec((bb, N_RET_HEADS, RET_HEAD_DIM), row3),
                  pl.BlockSpec((bb, D_RET, RET_HEAD_DIM), row3),
                  _const_spec((D_RET, RET_HEAD_DIM)),
                  _const_spec((N_RET_HEADS, D_RET)),
                  _const_spec((N_RET_HEADS, RET_HEAD_DIM))],
        out_specs=[pl.BlockSpec((bb, N_RET_HEADS, RET_HEAD_DIM), row3),
                   pl.BlockSpec((bb, D_RET, RET_HEAD_DIM), row3)],
        out_shape=[jax.ShapeDtypeStruct((n, N_RET_HEADS, RET_HEAD_DIM), F32),
                   jax.ShapeDtypeStruct((n, D_RET, RET_HEAD_DIM), F32)],
        compiler_params=_cparams("parallel"),
        name="retention_step",
    )(q, k, vh, gh, s2, gam_rows, gam_heads, ret_norm.reshape(N_RET_HEADS, RET_HEAD_DIM))
    return (y.reshape(n, D_RET).astype(BF16),
            s_new.reshape(n, N_RET_HEADS, RET_HEAD_DIM, RET_HEAD_DIM))


def _first_max(vals, idxs, sentinel):
    m = jnp.max(vals, axis=0, keepdims=True)
    first = jnp.min(jnp.where(vals == m, idxs, sentinel), axis=0, keepdims=True)
    return m, idxs == first


def _route(scores, sel):
    t = sel.shape[1]
    neg = -jnp.inf
    member = lax.broadcasted_iota(jnp.int32, (GROUP_SIZE, t), 0).astype(F32)
    groups = [sel[g * GROUP_SIZE:(g + 1) * GROUP_SIZE, :] for g in range(N_EXPERT_GROUPS)]
    gscore = []
    for grp in groups:
        m1, pick = _first_max(grp, member, float(GROUP_SIZE))
        m2 = jnp.max(jnp.where(pick, neg, grp), axis=0, keepdims=True)
        gscore.append(m1 + m2)
    gs = jnp.concatenate(gscore, axis=0)
    gkeep = jnp.zeros(gs.shape, F32)
    for _ in range(TOPK_GROUPS):
        _, pick = _first_max(gs, member, float(N_EXPERT_GROUPS))
        gkeep = jnp.where(pick, 1.0, gkeep)
        gs = jnp.where(pick, neg, gs)
    cand = jnp.concatenate(
        [jnp.where(gkeep[g:g + 1, :] > 0.5, groups[g], neg) for g in range(N_EXPERT_GROUPS)],
        axis=0)
    expert = lax.broadcasted_iota(jnp.int32, (N_EXPERTS, t), 0).astype(F32)
    ids, ws = [], []
    for _ in range(TOP_K):
        _, pick = _first_max(cand, expert, float(N_EXPERTS))
        ids.append(jnp.sum(jnp.where(pick, expert, 0.0), axis=0, keepdims=True))
        ws.append(jnp.sum(jnp.where(pick, scores, 0.0), axis=0, keepdims=True))
        cand = jnp.where(pick, neg, cand)
    w = jnp.concatenate(ws, axis=0)
    gates = w / jnp.sum(w, axis=0, keepdims=True) * ROUTED_SCALE
    return jnp.concatenate(ids, axis=0), gates


def _mix_kernel(x_ref, ys_ref, yr_ref, woa_ref, wob_ref, n2_ref, wrt_ref, rb_ref,
                wgs_ref, wus_ref, wds_ref, xs_ref, hp_ref, meta_ref, eid_ref, *, n_blocks):
    @pl.when(pl.program_id(0) >= n_blocks)
    def _():
        hp_ref[...] = jnp.zeros_like(hp_ref)
        meta_ref[...] = jnp.zeros_like(meta_ref)

    @pl.when(pl.program_id(0) < n_blocks)
    def _():
        _mix_body(x_ref, ys_ref, yr_ref, woa_ref, wob_ref, n2_ref, wrt_ref, rb_ref,
                  wgs_ref, wus_ref, wds_ref, xs_ref, hp_ref, meta_ref, eid_ref)


def _mix_body(x_ref, ys_ref, yr_ref, woa_ref, wob_ref, n2_ref, wrt_ref, rb_ref,
              wgs_ref, wus_ref, wds_ref, xs_ref, hp_ref, meta_ref, eid_ref):
    tm = x_ref.shape[0]
    x2 = (x_ref[...]
          + jnp.dot(ys_ref[...], woa_ref[...], preferred_element_type=F32)
          + jnp.dot(yr_ref[...], wob_ref[...], preferred_element_type=F32))
    h2f = _rms(x2, n2_ref[...])
    h2 = h2f.astype(BF16)
    bits = lax.bitcast_convert_type(h2.astype(F32), jnp.uint32)
    half = D_MODEL // 2
    word = (bits[:, :half] >> 16) | bits[:, half:]
    for c in range(PACK_ROWS):
        hp_ref[pl.ds(c, tm, stride=PACK_ROWS), :] = word[:, c * LANES:(c + 1) * LANES]
    logits = lax.dot_general(wrt_ref[...], h2f, (((1,), (1,)), ((), ())),
                             precision=lax.Precision.HIGHEST, preferred_element_type=F32)
    scores = jax.nn.sigmoid(logits)
    ids, gates = _route(scores, scores + rb_ref[:, :1])
    eid_ref[...] = ids.astype(jnp.int32)
    rec = jnp.concatenate([gates, ids, jnp.zeros((LANES - 2 * TOP_K, tm), F32)], axis=0).T
    meta_ref[pl.ds(0, tm, stride=META_ROWS), :] = rec
    for c in range(1, META_ROWS):
        meta_ref[pl.ds(c, tm, stride=META_ROWS), :] = jnp.zeros((tm, LANES), F32)
    a = jnp.dot(h2, wgs_ref[...], preferred_element_type=F32)
    b = jnp.dot(h2, wus_ref[...], preferred_element_type=F32)
    xs_ref[...] = x2 + jnp.dot((jax.nn.silu(a) * b).astype(BF16), wds_ref[...],
                               preferred_element_type=F32)


def _mix(x, ys5, yret, mp, tm, out_tokens):
    rows = x.shape[0]
    n_blocks = rows // tm
    assert out_tokens % tm == 0
    packed = lambda i: (i, 0)
    row = lambda i: (jnp.minimum(i, n_blocks - 1), 0)
    col = lambda i: (0, jnp.minimum(i, n_blocks - 1))
    return pl.pallas_call(
        functools.partial(_mix_kernel, n_blocks=n_blocks),
        grid=(out_tokens // tm,),
        in_specs=[pl.BlockSpec((tm, D_MODEL), row),
                  pl.BlockSpec((tm, D_S5), row), pl.BlockSpec((tm, D_RET), row),
                  _const_spec((D_S5, D_MODEL)), _const_spec((D_RET, D_MODEL)),
                  _const_spec((1, D_MODEL)),
                  _const_spec((N_EXPERTS, D_MODEL)), _const_spec((N_EXPERTS, LANES)),
                  _const_spec((D_MODEL, D_EXPERT)), _const_spec((D_MODEL, D_EXPERT)),
                  _const_spec((D_EXPERT, D_MODEL))],
        out_specs=[pl.BlockSpec((tm, D_MODEL), row),
                   pl.BlockSpec((tm * PACK_ROWS, LANES), packed),
                   pl.BlockSpec((tm * META_ROWS, LANES), packed),
                   pl.BlockSpec((TOP_K, tm), col)],
        out_shape=[jax.ShapeDtypeStruct((rows, D_MODEL), F32),
                   jax.ShapeDtypeStruct((out_tokens * PACK_ROWS, LANES), jnp.uint32),
                   jax.ShapeDtypeStruct((out_tokens * META_ROWS, LANES), F32),
                   jax.ShapeDtypeStruct((TOP_K, rows), jnp.int32)],
        compiler_params=_cparams("arbitrary"),
        name="out_proj_router",
    )(x, ys5, yret, mp["woa"], mp["wob"], mp["n2"], mp["wrt"], mp["rb"],
      mp["wgs"], mp["wus"], mp["wds"])


def _moe_kernel(t0_ref, nv_ref, ex_ref, sb_ref, first_ref, *refs):
    tok_refs = refs[:MOE_SUB]
    hp_ref, meta_ref, wg_ref, wu_ref, wd_ref, acc_ref, xt, xm, ot = refs[MOE_SUB:]
    s = pl.program_id(0)
    tm = MOE_TILE
    half = D_MODEL // 2

    @pl.when(first_ref[s] == 1)
    def _():
        zero = jnp.zeros((ZERO_ROWS, LANES), F32)

        def clear(i, carry):
            acc_ref[0, pl.ds(pl.multiple_of(i * ZERO_ROWS, ZERO_ROWS), ZERO_ROWS), :] = zero
            return carry

        lax.fori_loop(0, acc_ref.shape[1] // ZERO_ROWS, clear, 0)

    nv = nv_ref[s]
    expert = ex_ref[s].astype(F32)
    for q in range(MOE_SUB):
        @pl.when(q < nv)
        def _(q=q):
            tok = tok_refs[q]

            def fetch(it, carry):
                for u in range(ROW_UNROLL):
                    m = it * ROW_UNROLL + u
                    r = tok[0, 0, m]
                    r = jnp.where(r < MOE_BLOCK_TOKENS, r, 0)
                    xt[pl.ds(m, PACK_ROWS, stride=XT_PITCH), :] = hp_ref[
                        pl.ds(pl.multiple_of(r * PACK_ROWS, PACK_ROWS), PACK_ROWS), :]
                    xm[pl.ds(m, META_ROWS, stride=XT_PITCH), :] = meta_ref[
                        pl.ds(pl.multiple_of(r * META_ROWS, META_ROWS), META_ROWS), :]
                return carry

            lax.fori_loop(0, tm // ROW_UNROLL, fetch, 0)
            lo, hi = [], []
            for c in range(PACK_ROWS):
                wc = xt[c * XT_PITCH:c * XT_PITCH + tm, :]
                lo.append(lax.bitcast_convert_type(wc << 16, F32).astype(BF16))
                hi.append(lax.bitcast_convert_type(wc & jnp.uint32(0xFFFF0000), F32).astype(BF16))
            x_lo = jnp.concatenate(lo, axis=1)
            x_hi = jnp.concatenate(hi, axis=1)
            rec = xm[0:tm, :]
            ids = pltpu.roll(rec, LANES - TOP_K, 1)
            lane = lax.broadcasted_iota(jnp.int32, rec.shape, 1)
            gate = jnp.sum(jnp.where((lane < TOP_K) & (ids == expert), rec, 0.0),
                           axis=1, keepdims=True)
            a = (jnp.dot(x_lo, wg_ref[0, :half], preferred_element_type=F32)
                 + jnp.dot(x_hi, wg_ref[0, half:], preferred_element_type=F32))
            b = (jnp.dot(x_lo, wu_ref[0, :half], preferred_element_type=F32)
                 + jnp.dot(x_hi, wu_ref[0, half:], preferred_element_type=F32))
            out = jnp.dot((jax.nn.silu(a) * b * gate).astype(BF16), wd_ref[0],
                          preferred_element_type=F32)
            for c in range(ACC_ROWS):
                ot[c * OT_PITCH:c * OT_PITCH + tm, :] = out[:, c * LANES:(c + 1) * LANES]

            def add(it, carry):
                new = []
                for u in range(ROW_UNROLL):
                    m = it * ROW_UNROLL + u
                    base = pl.multiple_of(tok[0, 0, m] * ACC_ROWS, ACC_ROWS)
                    row = ot[pl.ds(m, ACC_ROWS, stride=OT_PITCH), :]
                    new.append((base, acc_ref[0, pl.ds(base, ACC_ROWS), :] + row))
                for base, v in new:
                    acc_ref[0, pl.ds(base, ACC_ROWS), :] = v
                return carry

            lax.fori_loop(0, tm // ROW_UNROLL, add, 0)


def _dispatch_tables(eid_t, n_tok):
    tm, ts = MOE_TILE, MOE_BLOCK_TOKENS
    ng = MOE_BLOCKS * N_EXPERTS
    assert ts < (1 << TOKEN_BITS)
    tok = jnp.arange(n_tok, dtype=jnp.int32)
    grp = (tok // ts)[None, :] * N_EXPERTS + eid_t
    gids = jnp.arange(ng, dtype=jnp.int32)
    counts = jnp.sum((grp.reshape(1, -1) == gids[:, None]).astype(jnp.int32), axis=1)
    npad = (-counts) % tm
    fill = jnp.arange(tm, dtype=jnp.int32)[None, :] < npad[:, None]
    keys = jnp.concatenate([
        ((grp << TOKEN_BITS) | (tok % ts)[None, :]).reshape(-1),
        ((jnp.where(fill, gids[:, None], ng) << TOKEN_BITS) | ts).reshape(-1)])
    toks = lax.sort(keys, dimension=0, is_stable=False) & ((1 << TOKEN_BITS) - 1)
    n_tiles = (n_tok * TOP_K + ng * tm) // tm

    ntile = (counts + npad) // tm
    tile_start = jnp.cumsum(ntile) - ntile
    nstep = (ntile + MOE_SUB - 1) // MOE_SUB
    step_end = jnp.cumsum(nstep)
    n_steps = (n_tiles + ng * (MOE_SUB - 1)) // MOE_SUB + 1
    s = jnp.arange(n_steps, dtype=jnp.int32)
    g = jnp.minimum(jnp.searchsorted(step_end, s, side="right").astype(jnp.int32), ng - 1)
    live = s < step_end[-1]
    j = s - (step_end - nstep)[g]
    t0 = jnp.where(live, tile_start[g] + MOE_SUB * j, 0)
    nv = jnp.where(live, jnp.clip(ntile[g] - MOE_SUB * j, 0, MOE_SUB), 0)
    sb = g // N_EXPERTS
    first = jnp.concatenate([jnp.ones((1,), jnp.int32), (sb[1:] != sb[:-1]).astype(jnp.int32)])
    return {"t0": t0.astype(jnp.int32), "nv": nv.astype(jnp.int32),
            "ex": (g % N_EXPERTS).astype(jnp.int32), "sb": sb.astype(jnp.int32), "first": first,
            "tok": toks.reshape(n_tiles, 1, tm)}


def _moe(hp, meta, tabs, ep):
    tm, ts = MOE_TILE, MOE_BLOCK_TOKENS
    n_tiles = tabs["tok"].shape[0]
    n_steps = tabs["t0"].shape[0]

    def tile_spec(q):
        return pl.BlockSpec((1, 1, tm),
                            lambda s, t0, nv, ex, sb, first: (jnp.minimum(t0[s] + q, n_tiles - 1), 0, 0),
                            memory_space=pltpu.SMEM)

    wspec = lambda shape: pl.BlockSpec((1,) + shape, lambda s, t0, nv, ex, sb, first: (ex[s], 0, 0))
    blk = lambda s, t0, nv, ex, sb, first: (sb[s], 0)
    acc_rows = (ts + SUBLANES) * ACC_ROWS
    assert acc_rows % ZERO_ROWS == 0
    grid_spec = pltpu.PrefetchScalarGridSpec(
        num_scalar_prefetch=5,
        grid=(n_steps,),
        in_specs=([tile_spec(q) for q in range(MOE_SUB)]
                  + [pl.BlockSpec((ts * PACK_ROWS, LANES), blk, pipeline_mode=pl.Buffered(1)),
                     pl.BlockSpec((ts * META_ROWS, LANES), blk, pipeline_mode=pl.Buffered(1)),
                     wspec((D_MODEL, D_EXPERT)), wspec((D_MODEL, D_EXPERT)),
                     wspec((D_EXPERT, D_MODEL))]),
        out_specs=pl.BlockSpec((1, acc_rows, LANES), lambda s, t0, nv, ex, sb, first: (sb[s], 0, 0),
                               pipeline_mode=pl.Buffered(1)),
        scratch_shapes=[pltpu.VMEM((PACK_ROWS * XT_PITCH, LANES), jnp.uint32),
                        pltpu.VMEM((META_ROWS * XT_PITCH, LANES), F32),
                        pltpu.VMEM((ACC_ROWS * OT_PITCH, LANES), F32)],
    )
    return pl.pallas_call(
        _moe_kernel,
        grid_spec=grid_spec,
        out_shape=jax.ShapeDtypeStruct((MOE_BLOCKS, acc_rows, LANES), F32),
        compiler_params=_cparams("arbitrary"),
        name="moe_experts",
    )(tabs["t0"], tabs["nv"], tabs["ex"], tabs["sb"], tabs["first"],
      *([tabs["tok"]] * MOE_SUB), hp, meta, ep["wg"], ep["wu"], ep["wd"])


def _final_kernel(xs_ref, r_ref, fn_ref, y_ref):
    tm = xs_ref.shape[0]
    routed = jnp.concatenate(
        [r_ref[0, pl.ds(c, tm, stride=ACC_ROWS), :] for c in range(ACC_ROWS)], axis=1)
    y_ref[...] = _rms(xs_ref[...] + routed, fn_ref[...])


def _finalize(xs, routed, final_norm, tm, tok0):
    rows = xs.shape[0]
    per_block = MOE_BLOCK_TOKENS // tm
    first = tok0 // tm
    assert tok0 % tm == 0 and MOE_BLOCK_TOKENS % tm == 0
    return pl.pallas_call(
        _final_kernel,
        grid=(rows // tm,),
        in_specs=[pl.BlockSpec((tm, D_MODEL), lambda i: (i, 0)),
                  pl.BlockSpec((1, tm * ACC_ROWS, LANES),
                               lambda i: ((first + i) // per_block, (first + i) % per_block, 0)),
                  _const_spec((1, D_MODEL))],
        out_specs=pl.BlockSpec((tm, D_MODEL), lambda i: (i, 0)),
        out_shape=jax.ShapeDtypeStruct((rows, D_MODEL), F32),
        compiler_params=_cparams("parallel"),
        name="final_norm",
    )(xs, routed, final_norm)


def _rope_tables(pos):
    half = RET_HEAD_DIM // 2
    inv_freq = ROPE_BASE ** (-jnp.arange(half, dtype=F32) / half)
    ang = pos.astype(F32)[:, None] * inv_freq[None, :]
    cos, sin = jnp.cos(ang), jnp.sin(ang)
    cos_t = jnp.concatenate([cos, cos, cos, cos], axis=1)
    sin_t = jnp.concatenate([-sin, sin, -sin, sin], axis=1)
    return cos_t, sin_t


def kernel(x_prompt, x_sample, state_s5_re, state_s5_im, state_ret, meta_tokens, norm1, w_in,
           s5_lam_re, s5_lam_im, s5_log_dt, s5_b_re, s5_b_im, s5_c_re, s5_c_im, s5_d, s5_w_glu,
           s5_norm, ret_norm, w_out, norm2, w_router, router_bias, w_gate_e, w_up_e, w_down_e,
           w_gate_sh, w_up_sh, w_down_sh, final_norm):
    assert norm1.shape[0] == 1, "single-layer model"
    bp, seq, _ = x_prompt.shape
    ns = x_sample.shape[0]
    l = 0

    n1 = norm1[l].reshape(1, D_MODEL)
    w_in_b = w_in[l].astype(BF16)
    s5p = _s5_params(s5_lam_re[l], s5_lam_im[l], s5_log_dt[l], s5_b_re[l], s5_b_im[l],
                     s5_c_re[l], s5_c_im[l], s5_d[l], s5_w_glu[l], s5_norm[l])
    rnorm = ret_norm[l].reshape(1, D_RET)
    w_out_b = w_out[l].astype(BF16)
    mp = {
        "woa": w_out_b[:D_S5], "wob": w_out_b[D_S5:], "n2": norm2[l].reshape(1, D_MODEL),
        "wrt": w_router[l].T,
        "rb": jnp.broadcast_to(router_bias[l][:, None], (N_EXPERTS, LANES)),
        "wgs": w_gate_sh[l].astype(BF16), "wus": w_up_sh[l].astype(BF16),
        "wds": w_down_sh[l].astype(BF16),
    }
    ep = {"wg": w_gate_e[l].astype(BF16), "wu": w_up_e[l].astype(BF16),
          "wd": w_down_e[l].astype(BF16)}
    fnorm = final_norm.reshape(1, D_MODEL)

    xp = x_prompt.reshape(bp * seq, D_MODEL)
    cos_p, sin_p = _rope_tables(N_META + jnp.arange(seq, dtype=jnp.int32))
    tm_a = 512
    up, qp, kp, vp, gp = _project(xp, n1, w_in_b, cos_p, sin_p, tm_a, seq // tm_a)

    meta_tm = jnp.repeat(meta_tokens, SUBLANES, axis=0)
    mchunk = LANES
    meta_chunk = jnp.concatenate([jnp.zeros((mchunk - N_META, D_MODEL), F32), meta_tokens], axis=0)
    x_small = jnp.concatenate([x_sample.reshape(ns, D_MODEL), meta_tm, meta_chunk], axis=0)
    meta_pos = jnp.arange(N_META, dtype=jnp.int32)
    pos_small = jnp.concatenate([jnp.full((ns,), PAST_LEN, jnp.int32),
                                 jnp.repeat(meta_pos, SUBLANES),
                                 jnp.zeros((mchunk - N_META,), jnp.int32), meta_pos])
    cos_s, sin_s = _rope_tables(pos_small)
    n_small = x_small.shape[0]
    n_tm = ns + N_META * SUBLANES
    us, qs, ks, vs, gs = _project(x_small, n1, w_in_b, cos_s, sin_s, n_small, 1)

    zero8 = jnp.zeros((SUBLANES, S5_LANES), F32)
    _, m_re, m_im = _s5(us[ns:n_tm], zero8, zero8, s5p, nb=SUBLANES, tt=N_META, nblk=1,
                        bt_major=False)
    tt = 64
    ys5_p, p_re, p_im = _s5(up.reshape(bp, seq, D_S5), m_re, m_im, s5p, nb=bp, tt=tt,
                            nblk=seq // tt, bt_major=True)
    ys5_p = ys5_p.reshape(bp * seq, D_S5)
    ys5_s, s_re, s_im = _s5(us[:ns], state_s5_re[l].reshape(ns, S5_LANES),
                            state_s5_im[l].reshape(ns, S5_LANES), s5p, nb=ns, tt=1, nblk=1,
                            bt_major=False)

    zero_pair = jnp.zeros((1, N_HEAD_PAIRS, LANES, LANES), F32)
    _, m_pair = _retention(qs[n_tm:], ks[n_tm:], vs[n_tm:], gs[n_tm:],
                           zero_pair, rnorm, nseq=1, chunk=mchunk, nchunk=1)
    chunk = 256
    yret_p, p_pair = _retention(qp, kp, vp, gp, jnp.broadcast_to(m_pair, (bp,) + m_pair.shape[1:]),
                                rnorm, nseq=bp, chunk=chunk, nchunk=seq // chunk)
    yret_s, ret_s = _retention_step(qs[:ns], ks[:ns], vs[:ns], gs[:ns], state_ret[l], ret_norm[l])

    n_prompt = bp * seq
    assert n_prompt + ns <= MOE_BLOCKS * MOE_BLOCK_TOKENS
    n_moe = MOE_BLOCKS * MOE_BLOCK_TOKENS
    xs_p, hp, meta, eid_p = _mix(xp, ys5_p, yret_p, mp, 512, n_moe)
    xs_s, hp_s, meta_s, eid_s = _mix(x_sample.reshape(ns, D_MODEL), ys5_s, yret_s, mp, ns, ns)
    hp = lax.dynamic_update_slice(hp, hp_s, (n_prompt * PACK_ROWS, 0))
    meta = lax.dynamic_update_slice(meta, meta_s, (n_prompt * META_ROWS, 0))
    tabs = _dispatch_tables(jnp.concatenate([eid_p, eid_s], axis=1), n_prompt + ns)
    routed = _moe(hp, meta, tabs, ep)
    y_p = _finalize(xs_p, routed, fnorm, 512, 0)
    y_s = _finalize(xs_s, routed, fnorm, ns, n_prompt)

    shape5 = (1, bp, N_S5_GROUPS, S5_STATE)
    return (y_p.reshape(bp, seq, D_MODEL),
            y_s.reshape(ns, 1, D_MODEL),
            p_re.reshape(shape5), p_im.reshape(shape5),
            _unpair_state(p_pair)[None],
            s_re.reshape(1, ns, N_S5_GROUPS, S5_STATE), s_im.reshape(1, ns, N_S5_GROUPS, S5_STATE),
            ret_s[None])
```

```python
import functools
import math

import jax
import jax.numpy as jnp
from jax import lax
from jax.experimental import pallas as pl
from jax.experimental.pallas import tpu as pltpu

F32 = jnp.float32
BF16 = jnp.bfloat16

D_MODEL = 1024
N_META = 16
PAST_LEN = 16384
D_S5 = 512
S5_GROUP = 16
N_S5_GROUPS = 32
S5_STATE = 64
S5_LANES = N_S5_GROUPS * S5_STATE
D_RET = 512
N_RET_HEADS = 8
RET_HEAD_DIM = 64
N_HEAD_PAIRS = 4
ROPE_BASE = 10000.0
D_IN = D_S5 + 4 * D_RET
N_EXPERTS = 64
TOP_K = 8
N_EXPERT_GROUPS = 8
GROUP_SIZE = 8
TOPK_GROUPS = 4
D_EXPERT = 256
ROUTED_SCALE = 2.5
EPS = 1e-6

LANES = 128
SUBLANES = 8
VMEM_LIMIT = 56 * 1024 * 1024

PACK_ROWS = D_MODEL // 2 // LANES
ACC_ROWS = D_MODEL // LANES
MOE_BLOCKS = 3
MOE_BLOCK_TOKENS = 5632
MOE_TILE = 256
MOE_SUB = 4
ROW_UNROLL = 8
ADD_UNROLL = 16
OT_PITCH = MOE_TILE + SUBLANES
XT_PITCH = MOE_TILE + SUBLANES
META_ROWS = 2
assert PACK_ROWS == 2 * META_ROWS
TOKEN_BITS = 13
ZERO_ROWS = 64


def _cparams(*sem):
    return pltpu.CompilerParams(dimension_semantics=sem, vmem_limit_bytes=VMEM_LIMIT)


def _const_spec(shape):
    nd = len(shape)
    return pl.BlockSpec(shape, lambda *_: (0,) * nd)


def _rms(x, gain):
    return x * lax.rsqrt(jnp.mean(x * x, axis=-1, keepdims=True) + EPS) * gain


def _proj_kernel(x_ref, n1_ref, w_ref, cos_ref, sin_ref, u_ref, q_ref, k_ref, v_ref, g_ref):
    h = _rms(x_ref[...], n1_ref[...]).astype(BF16)
    proj = jnp.dot(h, w_ref[...], preferred_element_type=F32)
    cos = cos_ref[...]
    sin = sin_ref[...]
    lane = lax.broadcasted_iota(jnp.int32, cos.shape, 1)
    first_half = (lane % RET_HEAD_DIM) < (RET_HEAD_DIM // 2)

    def rotary(t):
        partner = jnp.where(first_half,
                            pltpu.roll(t, LANES - RET_HEAD_DIM // 2, 1),
                            pltpu.roll(t, RET_HEAD_DIM // 2, 1))
        return t * cos + partner * sin

    u_ref[...] = proj[:, :D_S5]
    for j in range(N_HEAD_PAIRS):
        lo = D_S5 + j * LANES
        q_ref[:, j * LANES:(j + 1) * LANES] = rotary(proj[:, lo:lo + LANES]).astype(BF16)
        lo += D_RET
        k_ref[:, j * LANES:(j + 1) * LANES] = (
            rotary(proj[:, lo:lo + LANES]) * (RET_HEAD_DIM ** -0.5)).astype(BF16)
    v_ref[...] = proj[:, D_S5 + 2 * D_RET:D_S5 + 3 * D_RET].astype(BF16)
    g_ref[...] = proj[:, D_S5 + 3 * D_RET:]


def _project(x, norm1, w_in_b, cos_t, sin_t, tm, table_blocks):
    rows = x.shape[0]
    row = lambda i: (i, 0)
    tab = lambda i: (i % table_blocks, 0)
    return pl.pallas_call(
        _proj_kernel,
        grid=(rows // tm,),
        in_specs=[pl.BlockSpec((tm, D_MODEL), row),
                  _const_spec((1, D_MODEL)),
                  _const_spec((D_MODEL, D_IN)),
                  pl.BlockSpec((tm, LANES), tab),
                  pl.BlockSpec((tm, LANES), tab)],
        out_specs=[pl.BlockSpec((tm, D_S5), row),
                   pl.BlockSpec((tm, D_RET), row),
                   pl.BlockSpec((tm, D_RET), row),
                   pl.BlockSpec((tm, D_RET), row),
                   pl.BlockSpec((tm, D_RET), row)],
        out_shape=[jax.ShapeDtypeStruct((rows, D_S5), F32),
                   jax.ShapeDtypeStruct((rows, D_RET), BF16),
                   jax.ShapeDtypeStruct((rows, D_RET), BF16),
                   jax.ShapeDtypeStruct((rows, D_RET), BF16),
                   jax.ShapeDtypeStruct((rows, D_RET), F32)],
        compiler_params=_cparams("parallel"),
        name="in_proj",
    )(x, norm1, w_in_b, cos_t, sin_t)


def _s5_kernel(u_ref, x0re_ref, x0im_ref, are_ref, aim_ref, b_ref, c_ref, d_ref, wglu_ref,
               nrm_ref, y_ref, sre_ref, sim_ref, st, sre, sim, utm, ytm, *, nb, tt, bt_major):
    i = pl.program_id(0)
    rows = nb * tt

    @pl.when(i == 0)
    def _():
        sre[...] = x0re_ref[...]
        sim[...] = x0im_ref[...]

    if bt_major:
        for t in range(tt):
            utm[t * nb:(t + 1) * nb, :] = u_ref[:, t, :]
    else:
        utm[...] = u_ref[...]

    ub = utm[...].astype(BF16)
    half = S5_LANES // 2
    kh = D_S5 // 2
    for part in range(2):
        for hf in range(2):
            st[:, part * S5_LANES + hf * half:part * S5_LANES + (hf + 1) * half] = jnp.dot(
                ub[:, hf * kh:(hf + 1) * kh], b_ref[part, hf], preferred_element_type=F32)

    lw = 512 if nb == SUBLANES else LANES
    for lg in range(S5_LANES // lw):
        re = slice(lg * lw, (lg + 1) * lw)
        im = slice(S5_LANES + lg * lw, S5_LANES + (lg + 1) * lw)
        a_re = are_ref[:, re]
        a_im = aim_ref[:, re]

        def step(t, carry):
            s_re, s_im = carry
            r0 = pl.multiple_of(t * nb, nb)
            n_re = a_re * s_re - a_im * s_im + st[pl.ds(r0, nb), re]
            n_im = a_re * s_im + a_im * s_re + st[pl.ds(r0, nb), im]
            st[pl.ds(r0, nb), re] = n_re
            st[pl.ds(r0, nb), im] = n_im
            return n_re, n_im

        f_re, f_im = lax.fori_loop(0, tt, step, (sre[:, re], sim[:, re]))
        sre[:, re] = f_re
        sim[:, re] = f_im

    sre_ref[...] = sre[...]
    sim_ref[...] = sim[...]

    ys = []
    for hf in range(2):
        xr = st[:, hf * half:(hf + 1) * half].astype(BF16)
        xi = st[:, S5_LANES + hf * half:S5_LANES + (hf + 1) * half].astype(BF16)
        ys.append(jnp.dot(xr, c_ref[0, hf], preferred_element_type=F32)
                  - jnp.dot(xi, c_ref[1, hf], preferred_element_type=F32))
    y = jnp.concatenate(ys, axis=1) + d_ref[...] * utm[...]
    y = jax.nn.gelu(y)
    y = y * jax.nn.sigmoid(jnp.dot(y.astype(BF16), wglu_ref[...], preferred_element_type=F32))
    y = _rms(y, nrm_ref[...]).astype(BF16)
    if bt_major:
        yf = y.astype(F32)
        for j in range(D_S5 // LANES):
            ytm[j] = yf[:, j * LANES:(j + 1) * LANES]
        for b in range(nb):
            for j in range(D_S5 // LANES):
                y_ref[b, :, j * LANES:(j + 1) * LANES] = (
                    ytm[j, pl.ds(b, tt, stride=nb), :].astype(BF16))
    else:
        y_ref[...] = y


def _s5(u, x0re, x0im, s5p, *, nb, tt, nblk, bt_major):
    rows = nb * tt
    if bt_major:
        u_spec = pl.BlockSpec((nb, tt, D_S5), lambda i: (0, i, 0))
        y_shape = jax.ShapeDtypeStruct((nb, tt * nblk, D_S5), BF16)
    else:
        u_spec = pl.BlockSpec((rows, D_S5), lambda i: (i, 0))
        y_shape = jax.ShapeDtypeStruct((rows * nblk, D_S5), BF16)
    state_spec = _const_spec((nb, S5_LANES))
    are = jnp.broadcast_to(s5p["are"], (nb, S5_LANES))
    aim = jnp.broadcast_to(s5p["aim"], (nb, S5_LANES))
    return pl.pallas_call(
        functools.partial(_s5_kernel, nb=nb, tt=tt, bt_major=bt_major),
        grid=(nblk,),
        in_specs=[u_spec, state_spec, state_spec, state_spec, state_spec,
                  _const_spec((2, 2, D_S5 // 2, S5_LANES // 2)),
                  _const_spec((2, 2, S5_LANES // 2, D_S5 // 2)),
                  _const_spec((1, D_S5)),
                  _const_spec((D_S5, D_S5)),
                  _const_spec((1, D_S5))],
        out_specs=[u_spec, state_spec, state_spec],
        out_shape=[y_shape,
                   jax.ShapeDtypeStruct((nb, S5_LANES), F32),
                   jax.ShapeDtypeStruct((nb, S5_LANES), F32)],
        scratch_shapes=[pltpu.VMEM((rows, 2 * S5_LANES), F32),
                        pltpu.VMEM((nb, S5_LANES), F32),
                        pltpu.VMEM((nb, S5_LANES), F32),
                        pltpu.VMEM((rows, D_S5), F32),
                        pltpu.VMEM((D_S5 // LANES, rows, LANES), F32)],
        compiler_params=_cparams("arbitrary"),
        name="s5_" + ("bt" if bt_major else "tm") + str(nb),
    )(u, x0re, x0im, are, aim, s5p["b"], s5p["c"], s5p["d"], s5p["wglu"], s5p["nrm"])


def _s5_params(lam_re, lam_im, log_dt, b_re, b_im, c_re, c_im, d_skip, w_glu, nrm):
    dt = jnp.exp(log_dt)[:, None]
    mag = jnp.exp(lam_re * dt)
    abar_re, abar_im = mag * jnp.cos(lam_im * dt), mag * jnp.sin(lam_im * dt)
    num_re, num_im = abar_re - 1.0, abar_im
    den = lam_re * lam_re + lam_im * lam_im
    f_re = (num_re * lam_re + num_im * lam_im) / den
    f_im = (num_im * lam_re - num_re * lam_im) / den
    bbar_re = f_re[..., None] * b_re - f_im[..., None] * b_im
    bbar_im = f_re[..., None] * b_im + f_im[..., None] * b_re
    hg = N_S5_GROUPS // 2
    eye = jnp.eye(hg, dtype=F32)

    def bdiag(bb):
        bb = bb.reshape(2, hg, S5_STATE, S5_GROUP)
        return jnp.einsum("zgph,gk->zghkp", bb, eye).reshape(2, hg * S5_GROUP, hg * S5_STATE)

    def cdiag(cc):
        cc = cc.reshape(2, hg, S5_GROUP, S5_STATE)
        return jnp.einsum("zgnp,gk->zgpkn", cc, eye).reshape(2, hg * S5_STATE, hg * S5_GROUP)

    return {
        "are": abar_re.reshape(1, S5_LANES), "aim": abar_im.reshape(1, S5_LANES),
        "b": jnp.stack([bdiag(bbar_re), bdiag(bbar_im)]).astype(BF16),
        "c": jnp.stack([cdiag(c_re), cdiag(c_im)]).astype(BF16),
        "d": d_skip.reshape(1, D_S5), "wglu": w_glu.astype(BF16), "nrm": nrm.reshape(1, D_S5),
    }


def _head_norm_gate(o, g, gain, lo):
    inv = 1.0 / RET_HEAD_DIM

    def seg_mean(t):
        s_lo = jnp.sum(jnp.where(lo, t, 0.0), axis=1, keepdims=True)
        s_hi = jnp.sum(jnp.where(lo, 0.0, t), axis=1, keepdims=True)
        return jnp.where(lo, s_lo, s_hi) * inv

    dlt = o - seg_mean(o)
    var = seg_mean(dlt * dlt)
    return jax.nn.silu(g) * (dlt * lax.rsqrt(var + EPS) * gain)


def _ret_kernel(q_ref, k_ref, v_ref, g_ref, s0_ref, dm_ref, qd_ref, kd_ref, gc_ref, bm_ref,
                nrm_ref, y_ref, so_ref, s_acc):
    c = pl.program_id(1)

    @pl.when(c == 0)
    def _():
        s_acc[...] = s0_ref[0]

    rows = q_ref.shape[0]
    lane = lax.broadcasted_iota(jnp.int32, (rows, LANES), 1)
    lo = lane < RET_HEAD_DIM
    for j in range(N_HEAD_PAIRS):
        sl = slice(j * LANES, (j + 1) * LANES)
        q2 = q_ref[:, sl].astype(F32)
        k2 = k_ref[:, sl]
        v2 = v_ref[:, sl]
        s_pair = s_acc[j]
        cross = jnp.dot((q2 * qd_ref[:, sl]).astype(BF16), s_pair.astype(BF16),
                        preferred_element_type=F32)
        k_dec = (k2.astype(F32) * kd_ref[:, sl]).astype(BF16)
        upd = lax.dot_general(k_dec, v2, (((0,), (0,)), ((), ())), preferred_element_type=F32)
        s_acc[j] = gc_ref[j] * s_pair + bm_ref[...] * upd
        inner = []
        for hh, qh in enumerate((jnp.where(lo, q2, 0.0), jnp.where(lo, 0.0, q2))):
            sc = lax.dot_general(qh.astype(BF16), k2, (((1,), (1,)), ((), ())),
                                 preferred_element_type=F32) * dm_ref[2 * j + hh]
            inner.append(jnp.dot(sc.astype(BF16), v2, preferred_element_type=F32))
        o = jnp.where(lo, inner[0], inner[1]) + cross
        y_ref[:, sl] = _head_norm_gate(o, g_ref[:, sl], nrm_ref[:, sl], lo).astype(BF16)
    so_ref[0] = s_acc[...]


def _ret_tables(chunk):
    log_g = jnp.log1p(-jnp.exp2(-5.0 - jnp.arange(N_RET_HEADS, dtype=F32)))
    n = jnp.arange(chunk, dtype=F32)
    diff = n[:, None] - n[None, :]
    dm = jnp.where(diff[None] >= 0.0,
                   jnp.exp(log_g[:, None, None] * jnp.maximum(diff, 0.0)[None]), 0.0)
    per_lane = lambda t: jnp.repeat(t.T, RET_HEAD_DIM, axis=1)
    qd = per_lane(jnp.exp(log_g[:, None] * (n + 1.0)[None]))
    kd = per_lane(jnp.exp(log_g[:, None] * (chunk - 1.0 - n)[None]))
    gch = jnp.exp(log_g * chunk)
    blk = jnp.kron(jnp.eye(2, dtype=F32), jnp.ones((RET_HEAD_DIM, RET_HEAD_DIM), F32))
    gc = jnp.repeat(gch.reshape(N_HEAD_PAIRS, 2), RET_HEAD_DIM, axis=1)[:, :, None] * blk[None]
    return dm, qd, kd, gc, blk


def _pair_state(s):
    b = s.shape[0]
    s = s.reshape(b, N_HEAD_PAIRS, 2, RET_HEAD_DIM, RET_HEAD_DIM)
    z = jnp.zeros_like(s[:, :, 0])
    top = jnp.concatenate([s[:, :, 0], z], axis=-1)
    bot = jnp.concatenate([z, s[:, :, 1]], axis=-1)
    return jnp.concatenate([top, bot], axis=-2)


def _unpair_state(s2):
    d = RET_HEAD_DIM
    return jnp.stack([s2[:, :, :d, :d], s2[:, :, d:, d:]], axis=2).reshape(
        s2.shape[0], N_RET_HEADS, d, d)


def _retention(q, k, v, g, s0_pair, ret_norm, *, nseq, chunk, nchunk):
    dm, qd, kd, gc, blk = _ret_tables(chunk)
    row = lambda b, c: (b * nchunk + c, 0)
    st_spec = pl.BlockSpec((1, N_HEAD_PAIRS, LANES, LANES), lambda b, c: (b, 0, 0, 0))
    blk_spec = pl.BlockSpec((chunk, D_RET), row)
    return pl.pallas_call(
        _ret_kernel,
        grid=(nseq, nchunk),
        in_specs=[blk_spec, blk_spec, blk_spec, blk_spec, st_spec,
                  _const_spec((N_RET_HEADS, chunk, chunk)),
                  _const_spec((chunk, D_RET)), _const_spec((chunk, D_RET)),
                  _const_spec((N_HEAD_PAIRS, LANES, LANES)), _const_spec((LANES, LANES)),
                  _const_spec((1, D_RET))],
        out_specs=[blk_spec, st_spec],
        out_shape=[jax.ShapeDtypeStruct((nseq * nchunk * chunk, D_RET), BF16),
                   jax.ShapeDtypeStruct((nseq, N_HEAD_PAIRS, LANES, LANES), F32)],
        scratch_shapes=[pltpu.VMEM((N_HEAD_PAIRS, LANES, LANES), F32)],
        compiler_params=_cparams("arbitrary", "arbitrary"),
        name="retention_c%d" % chunk,
    )(q, k, v, g, s0_pair, dm, qd, kd, gc, blk, ret_norm)


def _ret_step_kernel(q_ref, k_ref, vh_ref, gh_ref, s_ref, gamr_ref, gamh_ref, nrm_ref,
                     y_ref, so_ref, *, bb):
    head = lax.broadcasted_iota(jnp.int32, (N_RET_HEADS, D_RET), 0)
    lane = lax.broadcasted_iota(jnp.int32, (N_RET_HEADS, D_RET), 1)
    own = (lane // RET_HEAD_DIM) == head
    gam_rows = gamr_ref[...]
    gam_h = gamh_ref[...]
    qf = q_ref[...].astype(F32)
    kf = k_ref[...].astype(F32)
    hi = lax.Precision.HIGHEST
    for b in range(bb):
        qm = jnp.where(own, qf[b:b + 1, :], 0.0)
        km = jnp.where(own, kf[b:b + 1, :], 0.0)
        vh = vh_ref[b]
        s_b = s_ref[b]
        score = jnp.sum(qm * km, axis=1, keepdims=True)
        inner = score.astype(BF16).astype(F32) * vh
        q_dec = (qm * gam_h).astype(BF16).astype(F32)
        cross = jnp.dot(q_dec, s_b, precision=hi, preferred_element_type=F32)
        o = inner + cross
        dlt = o - jnp.mean(o, axis=1, keepdims=True)
        var = jnp.mean(dlt * dlt, axis=1, keepdims=True)
        y_ref[b] = jax.nn.silu(gh_ref[b]) * (dlt * lax.rsqrt(var + EPS) * nrm_ref[...])
        upd = lax.dot_general(km, vh, (((0,), (0,)), ((), ())), precision=hi,
                              preferred_element_type=F32)
        so_ref[b] = gam_rows * s_b + upd


def _retention_step(q, k, v, g, state, ret_norm, bb=16):
    n = q.shape[0]
    gam = jnp.exp(jnp.log1p(-jnp.exp2(-5.0 - jnp.arange(N_RET_HEADS, dtype=F32))))
    gam_rows = jnp.broadcast_to(jnp.repeat(gam, RET_HEAD_DIM)[:, None], (D_RET, RET_HEAD_DIM))
    gam_heads = jnp.broadcast_to(gam[:, None], (N_RET_HEADS, D_RET))
    vh = v.astype(F32).reshape(n, N_RET_HEADS, RET_HEAD_DIM)
    gh = g.reshape(n, N_RET_HEADS, RET_HEAD_DIM)
    s2 = state.reshape(n, D_RET, RET_HEAD_DIM)
    row = lambda i: (i, 0)
    row3 = lambda i: (i, 0, 0)
    y, s_new = pl.pallas_call(
        functools.partial(_ret_step_kernel, bb=bb),
        grid=(n // bb,),
        in_specs=[pl.BlockSpec((bb, D_RET), row), pl.BlockSpec((bb, D_RET), row),
                  pl.BlockSpec((bb, N_RET_HEADS, RET_HEAD_DIM), row3),
                  pl.BlockSpec((bb, N_RET_HEADS, RET_HEAD_DIM), row3),
                  pl.BlockSpec((bb, D_RET, RET_HEAD_DIM), row3),
                  _const_spec((D_RET, RET_HEAD_DIM)),
                  _const_spec((N_RET_HEADS, D_RET)),
                  _const_spec((N_RET_HEADS, RET_HEAD_DIM))],
        out_specs=[pl.BlockSpec((bb, N_RET_HEADS, RET_HEAD_DIM), row3),
                   pl.BlockSpec((bb, D_RET, RET_HEAD_DIM), row3)],
        out_shape=[jax.ShapeDtypeStruct((n, N_RET_HEADS, RET_HEAD_DIM), F32),
                   jax.ShapeDtypeStruct((n, D_RET, RET_HEAD_DIM), F32)],
        compiler_params=_cparams("parallel"),
        name="retention_step",
    )(q, k, vh, gh, s2, gam_rows, gam_heads, ret_norm.reshape(N_RET_HEADS, RET_HEAD_DIM))
    return (y.reshape(n, D_RET).astype(BF16),
            s_new.reshape(n, N_RET_HEADS, RET_HEAD_DIM, RET_HEAD_DIM))


def _first_max(vals, idxs, sentinel):
    m = jnp.max(vals, axis=0, keepdims=True)
    first = jnp.min(jnp.where(vals == m, idxs, sentinel), axis=0, keepdims=True)
    return m, idxs == first


def _route(scores, sel):
    t = sel.shape[1]
    neg = -jnp.inf
    member = lax.broadcasted_iota(jnp.int32, (GROUP_SIZE, t), 0).astype(F32)
    groups = [sel[g * GROUP_SIZE:(g + 1) * GROUP_SIZE, :] for g in range(N_EXPERT_GROUPS)]
    gscore = []
    for grp in groups:
        m1, pick = _first_max(grp, member, float(GROUP_SIZE))
        m2 = jnp.max(jnp.where(pick, neg, grp), axis=0, keepdims=True)
        gscore.append(m1 + m2)
    gs = jnp.concatenate(gscore, axis=0)
    gkeep = jnp.zeros(gs.shape, F32)
    for _ in range(TOPK_GROUPS):
        _, pick = _first_max(gs, member, float(N_EXPERT_GROUPS))
        gkeep = jnp.where(pick, 1.0, gkeep)
        gs = jnp.where(pick, neg, gs)
    cand = jnp.concatenate(
        [jnp.where(gkeep[g:g + 1, :] > 0.5, groups[g], neg) for g in range(N_EXPERT_GROUPS)],
        axis=0)
    expert = lax.broadcasted_iota(jnp.int32, (N_EXPERTS, t), 0).astype(F32)
    ids, ws = [], []
    for _ in range(TOP_K):
        _, pick = _first_max(cand, expert, float(N_EXPERTS))
        ids.append(jnp.sum(jnp.where(pick, expert, 0.0), axis=0, keepdims=True))
        ws.append(jnp.sum(jnp.where(pick, scores, 0.0), axis=0, keepdims=True))
        cand = jnp.where(pick, neg, cand)
    w = jnp.concatenate(ws, axis=0)
    gates = w / jnp.sum(w, axis=0, keepdims=True) * ROUTED_SCALE
    return jnp.concatenate(ids, axis=0), gates


def _mix_kernel(x_ref, ys_ref, yr_ref, woa_ref, wob_ref, n2_ref, wrt_ref, rb_ref,
                wgs_ref, wus_ref, wds_ref, xs_ref, hp_ref, meta_ref, eid_ref, *, n_blocks):
    @pl.when(pl.program_id(0) >= n_blocks)
    def _():
        hp_ref[...] = jnp.zeros_like(hp_ref)
        meta_ref[...] = jnp.zeros_like(meta_ref)

    @pl.when(pl.program_id(0) < n_blocks)
    def _():
        _mix_body(x_ref, ys_ref, yr_ref, woa_ref, wob_ref, n2_ref, wrt_ref, rb_ref,
                  wgs_ref, wus_ref, wds_ref, xs_ref, hp_ref, meta_ref, eid_ref)


def _mix_body(x_ref, ys_ref, yr_ref, woa_ref, wob_ref, n2_ref, wrt_ref, rb_ref,
              wgs_ref, wus_ref, wds_ref, xs_ref, hp_ref, meta_ref, eid_ref):
    tm = x_ref.shape[0]
    x2 = (x_ref[...]
          + jnp.dot(ys_ref[...], woa_ref[...], preferred_element_type=F32)
          + jnp.dot(yr_ref[...], wob_ref[...], preferred_element_type=F32))
    h2f = _rms(x2, n2_ref[...])
    h2 = h2f.astype(BF16)
    bits = lax.bitcast_convert_type(h2.astype(F32), jnp.uint32)
    half = D_MODEL // 2
    word = (bits[:, :half] >> 16) | bits[:, half:]
    for c in range(PACK_ROWS):
        hp_ref[pl.ds(c, tm, stride=PACK_ROWS), :] = word[:, c * LANES:(c + 1) * LANES]
    logits = lax.dot_general(wrt_ref[...], h2f, (((1,), (1,)), ((), ())),
                             precision=lax.Precision.HIGHEST, preferred_element_type=F32)
    scores = jax.nn.sigmoid(logits)
    ids, gates = _route(scores, scores + rb_ref[:, :1])
    eid_ref[...] = ids.astype(jnp.int32)
    rec = jnp.concatenate([gates, ids, jnp.zeros((LANES - 2 * TOP_K, tm), F32)], axis=0).T
    meta_ref[pl.ds(0, tm, stride=META_ROWS), :] = rec
    for c in range(1, META_ROWS):
        meta_ref[pl.ds(c, tm, stride=META_ROWS), :] = jnp.zeros((tm, LANES), F32)
    a = jnp.dot(h2, wgs_ref[...], preferred_element_type=F32)
    b = jnp.dot(h2, wus_ref[...], preferred_element_type=F32)
    xs_ref[...] = x2 + jnp.dot((jax.nn.silu(a) * b).astype(BF16), wds_ref[...],
                               preferred_element_type=F32)


def _mix(x, ys5, yret, mp, tm, out_tokens):
    rows = x.shape[0]
    n_blocks = rows // tm
    assert out_tokens % tm == 0
    packed = lambda i: (i, 0)
    row = lambda i: (jnp.minimum(i, n_blocks - 1), 0)
    col = lambda i: (0, jnp.minimum(i, n_blocks - 1))
    return pl.pallas_call(
        functools.partial(_mix_kernel, n_blocks=n_blocks),
        grid=(out_tokens // tm,),
        in_specs=[pl.BlockSpec((tm, D_MODEL), row),
                  pl.BlockSpec((tm, D_S5), row), pl.BlockSpec((tm, D_RET), row),
                  _const_spec((D_S5, D_MODEL)), _const_spec((D_RET, D_MODEL)),
                  _const_spec((1, D_MODEL)),
                  _const_spec((N_EXPERTS, D_MODEL)), _const_spec((N_EXPERTS, LANES)),
                  _const_spec((D_MODEL, D_EXPERT)), _const_spec((D_MODEL, D_EXPERT)),
                  _const_spec((D_EXPERT, D_MODEL))],
        out_specs=[pl.BlockSpec((tm, D_MODEL), row),
                   pl.BlockSpec((tm * PACK_ROWS, LANES), packed),
                   pl.BlockSpec((tm * META_ROWS, LANES), packed),
                   pl.BlockSpec((TOP_K, tm), col)],
        out_shape=[jax.ShapeDtypeStruct((rows, D_MODEL), F32),
                   jax.ShapeDtypeStruct((out_tokens * PACK_ROWS, LANES), jnp.uint32),
                   jax.ShapeDtypeStruct((out_tokens * META_ROWS, LANES), F32),
                   jax.ShapeDtypeStruct((TOP_K, rows), jnp.int32)],
        compiler_params=_cparams("arbitrary"),
        name="out_proj_router",
    )(x, ys5, yret, mp["woa"], mp["wob"], mp["n2"], mp["wrt"], mp["rb"],
      mp["wgs"], mp["wus"], mp["wds"])


def _moe_kernel(t0_ref, nv_ref, ex_ref, sb_ref, first_ref, *refs):
    src_refs = refs[:MOE_SUB]
    dst_refs = refs[MOE_SUB:2 * MOE_SUB]
    hp_ref, meta_ref, wg_ref, wu_ref, wd_ref, acc_ref, xt, xm, ot = refs[2 * MOE_SUB:]
    s = pl.program_id(0)
    tm = MOE_TILE
    half = D_MODEL // 2

    @pl.when(first_ref[s] == 1)
    def _():
        zero = jnp.zeros((ZERO_ROWS, LANES), F32)

        def clear(i, carry):
            acc_ref[0, pl.ds(pl.multiple_of(i * ZERO_ROWS, ZERO_ROWS), ZERO_ROWS), :] = zero
            return carry

        lax.fori_loop(0, acc_ref.shape[1] // ZERO_ROWS, clear, 0)

    nv = nv_ref[s]
    expert = ex_ref[s].astype(F32)
    for q in range(MOE_SUB):
        @pl.when(q < nv)
        def _(q=q):
            src = src_refs[q]
            dst = dst_refs[q]

            def fetch(it, carry):
                for u in range(ROW_UNROLL):
                    m = it * ROW_UNROLL + u
                    r = pl.multiple_of(src[0, 0, m], PACK_ROWS)
                    xt[pl.ds(m, PACK_ROWS, stride=XT_PITCH), :] = hp_ref[pl.ds(r, PACK_ROWS), :]
                    xm[pl.ds(m, META_ROWS, stride=XT_PITCH), :] = meta_ref[
                        pl.ds(pl.multiple_of(r >> 1, META_ROWS), META_ROWS), :]
                return carry

            lax.fori_loop(0, tm // ROW_UNROLL, fetch, 0)
            lo, hi = [], []
            for c in range(PACK_ROWS):
                wc = xt[c * XT_PITCH:c * XT_PITCH + tm, :]
                lo.append(lax.bitcast_convert_type(wc << 16, F32).astype(BF16))
                hi.append(lax.bitcast_convert_type(wc & jnp.uint32(0xFFFF0000), F32).astype(BF16))
            x_lo = jnp.concatenate(lo, axis=1)
            x_hi = jnp.concatenate(hi, axis=1)
            rec = xm[0:tm, :]
            ids = pltpu.roll(rec, LANES - TOP_K, 1)
            lane = lax.broadcasted_iota(jnp.int32, rec.shape, 1)
            gate = jnp.sum(jnp.where((lane < TOP_K) & (ids == expert), rec, 0.0),
                           axis=1, keepdims=True)
            a = (jnp.dot(x_lo, wg_ref[0, :half], preferred_element_type=F32)
                 + jnp.dot(x_hi, wg_ref[0, half:], preferred_element_type=F32))
            b = (jnp.dot(x_lo, wu_ref[0, :half], preferred_element_type=F32)
                 + jnp.dot(x_hi, wu_ref[0, half:], preferred_element_type=F32))
            out = jnp.dot((jax.nn.silu(a) * b * gate).astype(BF16), wd_ref[0],
                          preferred_element_type=F32)
            for c in range(ACC_ROWS):
                ot[c * OT_PITCH:c * OT_PITCH + tm, :] = out[:, c * LANES:(c + 1) * LANES]

            def add(it, carry):
                new = []
                for u in range(ADD_UNROLL):
                    m = it * ADD_UNROLL + u
                    base = pl.multiple_of(dst[0, 0, m], ACC_ROWS)
                    row = ot[pl.ds(m, ACC_ROWS, stride=OT_PITCH), :]
                    new.append((base, acc_ref[0, pl.ds(base, ACC_ROWS), :] + row))
                for base, v in new:
                    acc_ref[0, pl.ds(base, ACC_ROWS), :] = v
                return carry

            lax.fori_loop(0, tm // ADD_UNROLL, add, 0)


def _dispatch_tables(eid_t, n_tok):
    tm, ts = MOE_TILE, MOE_BLOCK_TOKENS
    ng = MOE_BLOCKS * N_EXPERTS
    assert ts < (1 << TOKEN_BITS)
    tok = jnp.arange(n_tok, dtype=jnp.int32)
    grp = (tok // ts)[None, :] * N_EXPERTS + eid_t
    gids = jnp.arange(ng, dtype=jnp.int32)
    counts = jnp.sum((grp.reshape(1, -1) == gids[:, None]).astype(jnp.int32), axis=1)
    npad = (-counts) % tm
    fill = jnp.arange(tm, dtype=jnp.int32)[None, :] < npad[:, None]
    keys = jnp.concatenate([
        ((grp << TOKEN_BITS) | (tok % ts)[None, :]).reshape(-1),
        ((jnp.where(fill, gids[:, None], ng) << TOKEN_BITS) | ts).reshape(-1)])
    toks = lax.sort(keys, dimension=0, is_stable=False) & ((1 << TOKEN_BITS) - 1)
    n_tiles = (n_tok * TOP_K + ng * tm) // tm

    ntile = (counts + npad) // tm
    tile_start = jnp.cumsum(ntile) - ntile
    nstep = (ntile + MOE_SUB - 1) // MOE_SUB
    step_end = jnp.cumsum(nstep)
    n_steps = (n_tiles + ng * (MOE_SUB - 1)) // MOE_SUB + 1
    s = jnp.arange(n_steps, dtype=jnp.int32)
    g = jnp.minimum(jnp.searchsorted(step_end, s, side="right").astype(jnp.int32), ng - 1)
    live = s < step_end[-1]
    j = s - (step_end - nstep)[g]
    t0 = jnp.where(live, tile_start[g] + MOE_SUB * j, 0)
    nv = jnp.where(live, jnp.clip(ntile[g] - MOE_SUB * j, 0, MOE_SUB), 0)
    sb = g // N_EXPERTS
    first = jnp.concatenate([jnp.ones((1,), jnp.int32), (sb[1:] != sb[:-1]).astype(jnp.int32)])
    return {"t0": t0.astype(jnp.int32), "nv": nv.astype(jnp.int32),
            "ex": (g % N_EXPERTS).astype(jnp.int32), "sb": sb.astype(jnp.int32), "first": first,
            "src": (jnp.where(toks < ts, toks, 0) * PACK_ROWS).reshape(n_tiles, 1, tm),
            "dst": (toks * ACC_ROWS).reshape(n_tiles, 1, tm)}


def _moe(hp, meta, tabs, ep):
    tm, ts = MOE_TILE, MOE_BLOCK_TOKENS
    n_tiles = tabs["src"].shape[0]
    n_steps = tabs["t0"].shape[0]

    def tile_spec(q):
        return pl.BlockSpec((1, 1, tm),
                            lambda s, t0, nv, ex, sb, first: (jnp.minimum(t0[s] + q, n_tiles - 1), 0, 0),
                            memory_space=pltpu.SMEM)

    wspec = lambda shape: pl.BlockSpec((1,) + shape, lambda s, t0, nv, ex, sb, first: (ex[s], 0, 0))
    blk = lambda s, t0, nv, ex, sb, first: (sb[s], 0)
    acc_rows = (ts + SUBLANES) * ACC_ROWS
    assert acc_rows % ZERO_ROWS == 0
    grid_spec = pltpu.PrefetchScalarGridSpec(
        num_scalar_prefetch=5,
        grid=(n_steps,),
        in_specs=([tile_spec(q) for q in range(MOE_SUB)] + [tile_spec(q) for q in range(MOE_SUB)]
                  + [pl.BlockSpec((ts * PACK_ROWS, LANES), blk, pipeline_mode=pl.Buffered(1)),
                     pl.BlockSpec((ts * META_ROWS, LANES), blk, pipeline_mode=pl.Buffered(1)),
                     wspec((D_MODEL, D_EXPERT)), wspec((D_MODEL, D_EXPERT)),
                     wspec((D_EXPERT, D_MODEL))]),
        out_specs=pl.BlockSpec((1, acc_rows, LANES), lambda s, t0, nv, ex, sb, first: (sb[s], 0, 0),
                               pipeline_mode=pl.Buffered(1)),
        scratch_shapes=[pltpu.VMEM((PACK_ROWS * XT_PITCH, LANES), jnp.uint32),
                        pltpu.VMEM((META_ROWS * XT_PITCH, LANES), F32),
                        pltpu.VMEM((ACC_ROWS * OT_PITCH, LANES), F32)],
    )
    return pl.pallas_call(
        _moe_kernel,
        grid_spec=grid_spec,
        out_shape=jax.ShapeDtypeStruct((MOE_BLOCKS, acc_rows, LANES), F32),
        compiler_params=_cparams("arbitrary"),
        name="moe_experts",
    )(tabs["t0"], tabs["nv"], tabs["ex"], tabs["sb"], tabs["first"],
      *([tabs["src"]] * MOE_SUB), *([tabs["dst"]] * MOE_SUB),
      hp, meta, ep["wg"], ep["wu"], ep["wd"])


def _final_kernel(xs_ref, r_ref, fn_ref, y_ref):
    tm = xs_ref.shape[0]
    routed = jnp.concatenate(
        [r_ref[0, pl.ds(c, tm, stride=ACC_ROWS), :] for c in range(ACC_ROWS)], axis=1)
    y_ref[...] = _rms(xs_ref[...] + routed, fn_ref[...])


def _finalize(xs, routed, final_norm, tm, tok0):
    rows = xs.shape[0]
    per_block = MOE_BLOCK_TOKENS // tm
    first = tok0 // tm
    assert tok0 % tm == 0 and MOE_BLOCK_TOKENS % tm == 0
    return pl.pallas_call(
        _final_kernel,
        grid=(rows // tm,),
        in_specs=[pl.BlockSpec((tm, D_MODEL), lambda i: (i, 0)),
                  pl.BlockSpec((1, tm * ACC_ROWS, LANES),
                               lambda i: ((first + i) // per_block, (first + i) % per_block, 0)),
                  _const_spec((1, D_MODEL))],
        out_specs=pl.BlockSpec((tm, D_MODEL), lambda i: (i, 0)),
        out_shape=jax.ShapeDtypeStruct((rows, D_MODEL), F32),
        compiler_params=_cparams("parallel"),
        name="final_norm",
    )(xs, routed, final_norm)


def _rope_tables(pos):
    half = RET_HEAD_DIM // 2
    inv_freq = ROPE_BASE ** (-jnp.arange(half, dtype=F32) / half)
    ang = pos.astype(F32)[:, None] * inv_freq[None, :]
    cos, sin = jnp.cos(ang), jnp.sin(ang)
    cos_t = jnp.concatenate([cos, cos, cos, cos], axis=1)
    sin_t = jnp.concatenate([-sin, sin, -sin, sin], axis=1)
    return cos_t, sin_t


def kernel(x_prompt, x_sample, state_s5_re, state_s5_im, state_ret, meta_tokens, norm1, w_in,
           s5_lam_re, s5_lam_im, s5_log_dt, s5_b_re, s5_b_im, s5_c_re, s5_c_im, s5_d, s5_w_glu,
           s5_norm, ret_norm, w_out, norm2, w_router, router_bias, w_gate_e, w_up_e, w_down_e,
           w_gate_sh, w_up_sh, w_down_sh, final_norm):
    assert norm1.shape[0] == 1, "single-layer model"
    bp, seq, _ = x_prompt.shape
    ns = x_sample.shape[0]
    l = 0

    n1 = norm1[l].reshape(1, D_MODEL)
    w_in_b = w_in[l].astype(BF16)
    s5p = _s5_params(s5_lam_re[l], s5_lam_im[l], s5_log_dt[l], s5_b_re[l], s5_b_im[l],
                     s5_c_re[l], s5_c_im[l], s5_d[l], s5_w_glu[l], s5_norm[l])
    rnorm = ret_norm[l].reshape(1, D_RET)
    w_out_b = w_out[l].astype(BF16)
    mp = {
        "woa": w_out_b[:D_S5], "wob": w_out_b[D_S5:], "n2": norm2[l].reshape(1, D_MODEL),
        "wrt": w_router[l].T,
        "rb": jnp.broadcast_to(router_bias[l][:, None], (N_EXPERTS, LANES)),
        "wgs": w_gate_sh[l].astype(BF16), "wus": w_up_sh[l].astype(BF16),
        "wds": w_down_sh[l].astype(BF16),
    }
    ep = {"wg": w_gate_e[l].astype(BF16), "wu": w_up_e[l].astype(BF16),
          "wd": w_down_e[l].astype(BF16)}
    fnorm = final_norm.reshape(1, D_MODEL)

    xp = x_prompt.reshape(bp * seq, D_MODEL)
    cos_p, sin_p = _rope_tables(N_META + jnp.arange(seq, dtype=jnp.int32))
    tm_a = 512
    up, qp, kp, vp, gp = _project(xp, n1, w_in_b, cos_p, sin_p, tm_a, seq // tm_a)

    meta_tm = jnp.repeat(meta_tokens, SUBLANES, axis=0)
    mchunk = LANES
    meta_chunk = jnp.concatenate([jnp.zeros((mchunk - N_META, D_MODEL), F32), meta_tokens], axis=0)
    x_small = jnp.concatenate([x_sample.reshape(ns, D_MODEL), meta_tm, meta_chunk], axis=0)
    meta_pos = jnp.arange(N_META, dtype=jnp.int32)
    pos_small = jnp.concatenate([jnp.full((ns,), PAST_LEN, jnp.int32),
                                 jnp.repeat(meta_pos, SUBLANES),
                                 jnp.zeros((mchunk - N_META,), jnp.int32), meta_pos])
    cos_s, sin_s = _rope_tables(pos_small)
    n_small = x_small.shape[0]
    n_tm = ns + N_META * SUBLANES
    us, qs, ks, vs, gs = _project(x_small, n1, w_in_b, cos_s, sin_s, n_small, 1)

    zero8 = jnp.zeros((SUBLANES, S5_LANES), F32)
    _, m_re, m_im = _s5(us[ns:n_tm], zero8, zero8, s5p, nb=SUBLANES, tt=N_META, nblk=1,
                        bt_major=False)
    tt = 64
    ys5_p, p_re, p_im = _s5(up.reshape(bp, seq, D_S5), m_re, m_im, s5p, nb=bp, tt=tt,
                            nblk=seq // tt, bt_major=True)
    ys5_p = ys5_p.reshape(bp * seq, D_S5)
    ys5_s, s_re, s_im = _s5(us[:ns], state_s5_re[l].reshape(ns, S5_LANES),
                            state_s5_im[l].reshape(ns, S5_LANES), s5p, nb=ns, tt=1, nblk=1,
                            bt_major=False)

    zero_pair = jnp.zeros((1, N_HEAD_PAIRS, LANES, LANES), F32)
    _, m_pair = _retention(qs[n_tm:], ks[n_tm:], vs[n_tm:], gs[n_tm:],
                           zero_pair, rnorm, nseq=1, chunk=mchunk, nchunk=1)
    chunk = 256
    yret_p, p_pair = _retention(qp, kp, vp, gp, jnp.broadcast_to(m_pair, (bp,) + m_pair.shape[1:]),
                                rnorm, nseq=bp, chunk=chunk, nchunk=seq // chunk)
    yret_s, ret_s = _retention_step(qs[:ns], ks[:ns], vs[:ns], gs[:ns], state_ret[l], ret_norm[l])

    n_prompt = bp * seq
    assert n_prompt + ns <= MOE_BLOCKS * MOE_BLOCK_TOKENS
    n_moe = MOE_BLOCKS * MOE_BLOCK_TOKENS
    xs_p, hp, meta, eid_p = _mix(xp, ys5_p, yret_p, mp, 512, n_moe)
    xs_s, hp_s, meta_s, eid_s = _mix(x_sample.reshape(ns, D_MODEL), ys5_s, yret_s, mp, ns, ns)
    hp = lax.dynamic_update_slice(hp, hp_s, (n_prompt * PACK_ROWS, 0))
    meta = lax.dynamic_update_slice(meta, meta_s, (n_prompt * META_ROWS, 0))
    tabs = _dispatch_tables(jnp.concatenate([eid_p, eid_s], axis=1), n_prompt + ns)
    routed = _moe(hp, meta, tabs, ep)
    y_p = _finalize(xs_p, routed, fnorm, 512, 0)
    y_s = _finalize(xs_s, routed, fnorm, ns, n_prompt)

    shape5 = (1, bp, N_S5_GROUPS, S5_STATE)
    return (y_p.reshape(bp, seq, D_MODEL),
            y_s.reshape(ns, 1, D_MODEL),
            p_re.reshape(shape5), p_im.reshape(shape5),
            _unpair_state(p_pair)[None],
            s_re.reshape(1, ns, N_S5_GROUPS, S5_STATE), s_im.reshape(1, ns, N_S5_GROUPS, S5_STATE),
            ret_s[None])
```

```python
import functools
import math

import jax
import jax.numpy as jnp
from jax import lax
from jax.experimental import pallas as pl
from jax.experimental.pallas import tpu as pltpu

F32 = jnp.float32
BF16 = jnp.bfloat16

D_MODEL = 1024
N_META = 16
PAST_LEN = 16384
D_S5 = 512
S5_GROUP = 16
N_S5_GROUPS = 32
S5_STATE = 64
S5_LANES = N_S5_GROUPS * S5_STATE
D_RET = 512
N_RET_HEADS = 8
RET_HEAD_DIM = 64
N_HEAD_PAIRS = 4
ROPE_BASE = 10000.0
D_IN = D_S5 + 4 * D_RET
N_EXPERTS = 64
TOP_K = 8
N_EXPERT_GROUPS = 8
GROUP_SIZE = 8
TOPK_GROUPS = 4
D_EXPERT = 256
ROUTED_SCALE = 2.5
EPS = 1e-6

LANES = 128
SUBLANES = 8
VMEM_LIMIT = 56 * 1024 * 1024

PACK_ROWS = D_MODEL // 2 // LANES
ACC_ROWS = D_MODEL // LANES
MOE_BLOCKS = 3
MOE_BLOCK_TOKENS = 5632
MOE_TILE = 256
GROUP_TILES = MOE_BLOCK_TOKENS // MOE_TILE + 2
ADD_UNROLL = 8
OT_PITCH = MOE_TILE + SUBLANES
XT_PITCH = MOE_TILE + SUBLANES
META_ROWS = 2
assert PACK_ROWS == 2 * META_ROWS
TOKEN_BITS = 13
ZERO_ROWS = 64


def _cparams(*sem):
    return pltpu.CompilerParams(dimension_semantics=sem, vmem_limit_bytes=VMEM_LIMIT)


def _const_spec(shape):
    nd = len(shape)
    return pl.BlockSpec(shape, lambda *_: (0,) * nd)


def _rms(x, gain):
    return x * lax.rsqrt(jnp.mean(x * x, axis=-1, keepdims=True) + EPS) * gain


def _proj_kernel(x_ref, n1_ref, w_ref, cos_ref, sin_ref, u_ref, q_ref, k_ref, v_ref, g_ref):
    h = _rms(x_ref[...], n1_ref[...]).astype(BF16)
    proj = jnp.dot(h, w_ref[...], preferred_element_type=F32)
    cos = cos_ref[...]
    sin = sin_ref[...]
    lane = lax.broadcasted_iota(jnp.int32, cos.shape, 1)
    first_half = (lane % RET_HEAD_DIM) < (RET_HEAD_DIM // 2)

    def rotary(t):
        partner = jnp.where(first_half,
                            pltpu.roll(t, LANES - RET_HEAD_DIM // 2, 1),
                            pltpu.roll(t, RET_HEAD_DIM // 2, 1))
        return t * cos + partner * sin

    u_ref[...] = proj[:, :D_S5]
    for j in range(N_HEAD_PAIRS):
        lo = D_S5 + j * LANES
        q_ref[:, j * LANES:(j + 1) * LANES] = rotary(proj[:, lo:lo + LANES]).astype(BF16)
        lo += D_RET
        k_ref[:, j * LANES:(j + 1) * LANES] = (
            rotary(proj[:, lo:lo + LANES]) * (RET_HEAD_DIM ** -0.5)).astype(BF16)
    v_ref[...] = proj[:, D_S5 + 2 * D_RET:D_S5 + 3 * D_RET].astype(BF16)
    g_ref[...] = proj[:, D_S5 + 3 * D_RET:]


def _project(x, norm1, w_in_b, cos_t, sin_t, tm, table_blocks):
    rows = x.shape[0]
    row = lambda i: (i, 0)
    tab = lambda i: (i % table_blocks, 0)
    return pl.pallas_call(
        _proj_kernel,
        grid=(rows // tm,),
        in_specs=[pl.BlockSpec((tm, D_MODEL), row),
                  _const_spec((1, D_MODEL)),
                  _const_spec((D_MODEL, D_IN)),
                  pl.BlockSpec((tm, LANES), tab),
                  pl.BlockSpec((tm, LANES), tab)],
        out_specs=[pl.BlockSpec((tm, D_S5), row),
                   pl.BlockSpec((tm, D_RET), row),
                   pl.BlockSpec((tm, D_RET), row),
                   pl.BlockSpec((tm, D_RET), row),
                   pl.BlockSpec((tm, D_RET), row)],
        out_shape=[jax.ShapeDtypeStruct((rows, D_S5), F32),
                   jax.ShapeDtypeStruct((rows, D_RET), BF16),
                   jax.ShapeDtypeStruct((rows, D_RET), BF16),
                   jax.ShapeDtypeStruct((rows, D_RET), BF16),
                   jax.ShapeDtypeStruct((rows, D_RET), F32)],
        compiler_params=_cparams("parallel"),
        name="in_proj",
    )(x, norm1, w_in_b, cos_t, sin_t)


def _s5_kernel(u_ref, x0re_ref, x0im_ref, are_ref, aim_ref, b_ref, c_ref, d_ref, wglu_ref,
               nrm_ref, y_ref, sre_ref, sim_ref, st, sre, sim, utm, ytm, *, nb, tt, bt_major):
    i = pl.program_id(0)
    rows = nb * tt

    @pl.when(i == 0)
    def _():
        sre[...] = x0re_ref[...]
        sim[...] = x0im_ref[...]

    if bt_major:
        for t in range(tt):
            utm[t * nb:(t + 1) * nb, :] = u_ref[:, t, :]
    else:
        utm[...] = u_ref[...]

    ub = utm[...].astype(BF16)
    half = S5_LANES // 2
    kh = D_S5 // 2
    for part in range(2):
        for hf in range(2):
            st[:, part * S5_LANES + hf * half:part * S5_LANES + (hf + 1) * half] = jnp.dot(
                ub[:, hf * kh:(hf + 1) * kh], b_ref[part, hf], preferred_element_type=F32)

    lw = 512 if nb == SUBLANES else LANES
    for lg in range(S5_LANES // lw):
        re = slice(lg * lw, (lg + 1) * lw)
        im = slice(S5_LANES + lg * lw, S5_LANES + (lg + 1) * lw)
        a_re = are_ref[:, re]
        a_im = aim_ref[:, re]

        def step(t, carry):
            s_re, s_im = carry
            r0 = pl.multiple_of(t * nb, nb)
            n_re = a_re * s_re - a_im * s_im + st[pl.ds(r0, nb), re]
            n_im = a_re * s_im + a_im * s_re + st[pl.ds(r0, nb), im]
            st[pl.ds(r0, nb), re] = n_re
            st[pl.ds(r0, nb), im] = n_im
            return n_re, n_im

        f_re, f_im = lax.fori_loop(0, tt, step, (sre[:, re], sim[:, re]))
        sre[:, re] = f_re
        sim[:, re] = f_im

    sre_ref[...] = sre[...]
    sim_ref[...] = sim[...]

    ys = []
    for hf in range(2):
        xr = st[:, hf * half:(hf + 1) * half].astype(BF16)
        xi = st[:, S5_LANES + hf * half:S5_LANES + (hf + 1) * half].astype(BF16)
        ys.append(jnp.dot(xr, c_ref[0, hf], preferred_element_type=F32)
                  - jnp.dot(xi, c_ref[1, hf], preferred_element_type=F32))
    y = jnp.concatenate(ys, axis=1) + d_ref[...] * utm[...]
    y = jax.nn.gelu(y)
    y = y * jax.nn.sigmoid(jnp.dot(y.astype(BF16), wglu_ref[...], preferred_element_type=F32))
    y = _rms(y, nrm_ref[...]).astype(BF16)
    if bt_major:
        yf = y.astype(F32)
        for j in range(D_S5 // LANES):
            ytm[j] = yf[:, j * LANES:(j + 1) * LANES]
        for b in range(nb):
            for j in range(D_S5 // LANES):
                y_ref[b, :, j * LANES:(j + 1) * LANES] = (
                    ytm[j, pl.ds(b, tt, stride=nb), :].astype(BF16))
    else:
        y_ref[...] = y


def _s5(u, x0re, x0im, s5p, *, nb, tt, nblk, bt_major):
    rows = nb * tt
    if bt_major:
        u_spec = pl.BlockSpec((nb, tt, D_S5), lambda i: (0, i, 0))
        y_shape = jax.ShapeDtypeStruct((nb, tt * nblk, D_S5), BF16)
    else:
        u_spec = pl.BlockSpec((rows, D_S5), lambda i: (i, 0))
        y_shape = jax.ShapeDtypeStruct((rows * nblk, D_S5), BF16)
    state_spec = _const_spec((nb, S5_LANES))
    are = jnp.broadcast_to(s5p["are"], (nb, S5_LANES))
    aim = jnp.broadcast_to(s5p["aim"], (nb, S5_LANES))
    return pl.pallas_call(
        functools.partial(_s5_kernel, nb=nb, tt=tt, bt_major=bt_major),
        grid=(nblk,),
        in_specs=[u_spec, state_spec, state_spec, state_spec, state_spec,
                  _const_spec((2, 2, D_S5 // 2, S5_LANES // 2)),
                  _const_spec((2, 2, S5_LANES // 2, D_S5 // 2)),
                  _const_spec((1, D_S5)),
                  _const_spec((D_S5, D_S5)),
                  _const_spec((1, D_S5))],
        out_specs=[u_spec, state_spec, state_spec],
        out_shape=[y_shape,
                   jax.ShapeDtypeStruct((nb, S5_LANES), F32),
                   jax.ShapeDtypeStruct((nb, S5_LANES), F32)],
        scratch_shapes=[pltpu.VMEM((rows, 2 * S5_LANES), F32),
                        pltpu.VMEM((nb, S5_LANES), F32),
                        pltpu.VMEM((nb, S5_LANES), F32),
                        pltpu.VMEM((rows, D_S5), F32),
                        pltpu.VMEM((D_S5 // LANES, rows, LANES), F32)],
        compiler_params=_cparams("arbitrary"),
        name="s5_" + ("bt" if bt_major else "tm") + str(nb),
    )(u, x0re, x0im, are, aim, s5p["b"], s5p["c"], s5p["d"], s5p["wglu"], s5p["nrm"])


def _s5_params(lam_re, lam_im, log_dt, b_re, b_im, c_re, c_im, d_skip, w_glu, nrm):
    dt = jnp.exp(log_dt)[:, None]
    mag = jnp.exp(lam_re * dt)
    abar_re, abar_im = mag * jnp.cos(lam_im * dt), mag * jnp.sin(lam_im * dt)
    num_re, num_im = abar_re - 1.0, abar_im
    den = lam_re * lam_re + lam_im * lam_im
    f_re = (num_re * lam_re + num_im * lam_im) / den
    f_im = (num_im * lam_re - num_re * lam_im) / den
    bbar_re = f_re[..., None] * b_re - f_im[..., None] * b_im
    bbar_im = f_re[..., None] * b_im + f_im[..., None] * b_re
    hg = N_S5_GROUPS // 2
    eye = jnp.eye(hg, dtype=F32)

    def bdiag(bb):
        bb = bb.reshape(2, hg, S5_STATE, S5_GROUP)
        return jnp.einsum("zgph,gk->zghkp", bb, eye).reshape(2, hg * S5_GROUP, hg * S5_STATE)

    def cdiag(cc):
        cc = cc.reshape(2, hg, S5_GROUP, S5_STATE)
        return jnp.einsum("zgnp,gk->zgpkn", cc, eye).reshape(2, hg * S5_STATE, hg * S5_GROUP)

    return {
        "are": abar_re.reshape(1, S5_LANES), "aim": abar_im.reshape(1, S5_LANES),
        "b": jnp.stack([bdiag(bbar_re), bdiag(bbar_im)]).astype(BF16),
        "c": jnp.stack([cdiag(c_re), cdiag(c_im)]).astype(BF16),
        "d": d_skip.reshape(1, D_S5), "wglu": w_glu.astype(BF16), "nrm": nrm.reshape(1, D_S5),
    }


def _head_norm_gate(o, g, gain, lo):
    inv = 1.0 / RET_HEAD_DIM

    def seg_mean(t):
        s_lo = jnp.sum(jnp.where(lo, t, 0.0), axis=1, keepdims=True)
        s_hi = jnp.sum(jnp.where(lo, 0.0, t), axis=1, keepdims=True)
        return jnp.where(lo, s_lo, s_hi) * inv

    dlt = o - seg_mean(o)
    var = seg_mean(dlt * dlt)
    return jax.nn.silu(g) * (dlt * lax.rsqrt(var + EPS) * gain)


def _ret_kernel(q_ref, k_ref, v_ref, g_ref, s0_ref, dm_ref, qd_ref, kd_ref, gc_ref, bm_ref,
                nrm_ref, y_ref, so_ref, s_acc):
    c = pl.program_id(1)

    @pl.when(c == 0)
    def _():
        s_acc[...] = s0_ref[0]

    rows = q_ref.shape[0]
    lane = lax.broadcasted_iota(jnp.int32, (rows, LANES), 1)
    lo = lane < RET_HEAD_DIM
    for j in range(N_HEAD_PAIRS):
        sl = slice(j * LANES, (j + 1) * LANES)
        q2 = q_ref[:, sl].astype(F32)
        k2 = k_ref[:, sl]
        v2 = v_ref[:, sl]
        s_pair = s_acc[j]
        cross = jnp.dot((q2 * qd_ref[:, sl]).astype(BF16), s_pair.astype(BF16),
                        preferred_element_type=F32)
        k_dec = (k2.astype(F32) * kd_ref[:, sl]).astype(BF16)
        upd = lax.dot_general(k_dec, v2, (((0,), (0,)), ((), ())), preferred_element_type=F32)
        s_acc[j] = gc_ref[j] * s_pair + bm_ref[...] * upd
        inner = []
        for hh, qh in enumerate((jnp.where(lo, q2, 0.0), jnp.where(lo, 0.0, q2))):
            sc = lax.dot_general(qh.astype(BF16), k2, (((1,), (1,)), ((), ())),
                                 preferred_element_type=F32) * dm_ref[2 * j + hh]
            inner.append(jnp.dot(sc.astype(BF16), v2, preferred_element_type=F32))
        o = jnp.where(lo, inner[0], inner[1]) + cross
        y_ref[:, sl] = _head_norm_gate(o, g_ref[:, sl], nrm_ref[:, sl], lo).astype(BF16)
    so_ref[0] = s_acc[...]


def _ret_tables(chunk):
    log_g = jnp.log1p(-jnp.exp2(-5.0 - jnp.arange(N_RET_HEADS, dtype=F32)))
    n = jnp.arange(chunk, dtype=F32)
    diff = n[:, None] - n[None, :]
    dm = jnp.where(diff[None] >= 0.0,
                   jnp.exp(log_g[:, None, None] * jnp.maximum(diff, 0.0)[None]), 0.0)
    per_lane = lambda t: jnp.repeat(t.T, RET_HEAD_DIM, axis=1)
    qd = per_lane(jnp.exp(log_g[:, None] * (n + 1.0)[None]))
    kd = per_lane(jnp.exp(log_g[:, None] * (chunk - 1.0 - n)[None]))
    gch = jnp.exp(log_g * chunk)
    blk = jnp.kron(jnp.eye(2, dtype=F32), jnp.ones((RET_HEAD_DIM, RET_HEAD_DIM), F32))
    gc = jnp.repeat(gch.reshape(N_HEAD_PAIRS, 2), RET_HEAD_DIM, axis=1)[:, :, None] * blk[None]
    return dm, qd, kd, gc, blk


def _pair_state(s):
    b = s.shape[0]
    s = s.reshape(b, N_HEAD_PAIRS, 2, RET_HEAD_DIM, RET_HEAD_DIM)
    z = jnp.zeros_like(s[:, :, 0])
    top = jnp.concatenate([s[:, :, 0], z], axis=-1)
    bot = jnp.concatenate([z, s[:, :, 1]], axis=-1)
    return jnp.concatenate([top, bot], axis=-2)


def _unpair_state(s2):
    d = RET_HEAD_DIM
    return jnp.stack([s2[:, :, :d, :d], s2[:, :, d:, d:]], axis=2).reshape(
        s2.shape[0], N_RET_HEADS, d, d)


def _retention(q, k, v, g, s0_pair, ret_norm, *, nseq, chunk, nchunk):
    dm, qd, kd, gc, blk = _ret_tables(chunk)
    row = lambda b, c: (b * nchunk + c, 0)
    st_spec = pl.BlockSpec((1, N_HEAD_PAIRS, LANES, LANES), lambda b, c: (b, 0, 0, 0))
    blk_spec = pl.BlockSpec((chunk, D_RET), row)
    return pl.pallas_call(
        _ret_kernel,
        grid=(nseq, nchunk),
        in_specs=[blk_spec, blk_spec, blk_spec, blk_spec, st_spec,
                  _const_spec((N_RET_HEADS, chunk, chunk)),
                  _const_spec((chunk, D_RET)), _const_spec((chunk, D_RET)),
                  _const_spec((N_HEAD_PAIRS, LANES, LANES)), _const_spec((LANES, LANES)),
                  _const_spec((1, D_RET))],
        out_specs=[blk_spec, st_spec],
        out_shape=[jax.ShapeDtypeStruct((nseq * nchunk * chunk, D_RET), BF16),
                   jax.ShapeDtypeStruct((nseq, N_HEAD_PAIRS, LANES, LANES), F32)],
        scratch_shapes=[pltpu.VMEM((N_HEAD_PAIRS, LANES, LANES), F32)],
        compiler_params=_cparams("arbitrary", "arbitrary"),
        name="retention_c%d" % chunk,
    )(q, k, v, g, s0_pair, dm, qd, kd, gc, blk, ret_norm)


def _ret_step_kernel(q_ref, k_ref, vh_ref, gh_ref, s_ref, gamr_ref, gamh_ref, nrm_ref,
                     y_ref, so_ref, *, bb):
    head = lax.broadcasted_iota(jnp.int32, (N_RET_HEADS, D_RET), 0)
    lane = lax.broadcasted_iota(jnp.int32, (N_RET_HEADS, D_RET), 1)
    own = (lane // RET_HEAD_DIM) == head
    gam_rows = gamr_ref[...]
    gam_h = gamh_ref[...]
    qf = q_ref[...].astype(F32)
    kf = k_ref[...].astype(F32)
    hi = lax.Precision.HIGHEST
    for b in range(bb):
        qm = jnp.where(own, qf[b:b + 1, :], 0.0)
        km = jnp.where(own, kf[b:b + 1, :], 0.0)
        vh = vh_ref[b]
        s_b = s_ref[b]
        score = jnp.sum(qm * km, axis=1, keepdims=True)
        inner = score.astype(BF16).astype(F32) * vh
        q_dec = (qm * gam_h).astype(BF16).astype(F32)
        cross = jnp.dot(q_dec, s_b, precision=hi, preferred_element_type=F32)
        o = inner + cross
        dlt = o - jnp.mean(o, axis=1, keepdims=True)
        var = jnp.mean(dlt * dlt, axis=1, keepdims=True)
        y_ref[b] = jax.nn.silu(gh_ref[b]) * (dlt * lax.rsqrt(var + EPS) * nrm_ref[...])
        upd = lax.dot_general(km, vh, (((0,), (0,)), ((), ())), precision=hi,
                              preferred_element_type=F32)
        so_ref[b] = gam_rows * s_b + upd


def _retention_step(q, k, v, g, state, ret_norm, bb=16):
    n = q.shape[0]
    gam = jnp.exp(jnp.log1p(-jnp.exp2(-5.0 - jnp.arange(N_RET_HEADS, dtype=F32))))
    gam_rows = jnp.broadcast_to(jnp.repeat(gam, RET_HEAD_DIM)[:, None], (D_RET, RET_HEAD_DIM))
    gam_heads = jnp.broadcast_to(gam[:, None], (N_RET_HEADS, D_RET))
    vh = v.astype(F32).reshape(n, N_RET_HEADS, RET_HEAD_DIM)
    gh = g.reshape(n, N_RET_HEADS, RET_HEAD_DIM)
    s2 = state.reshape(n, D_RET, RET_HEAD_DIM)
    row = lambda i: (i, 0)
    row3 = lambda i: (i, 0, 0)
    y, s_new = pl.pallas_call(
        functools.partial(_ret_step_kernel, bb=bb),
        grid=(n // bb,),
        in_specs=[pl.BlockSpec((bb, D_RET), row), pl.BlockSpec((bb, D_RET), row),
                  pl.BlockSpec((bb, N_RET_HEADS, RET_HEAD_DIM), row3),
                  pl.BlockSpec((bb, N_RET_HEADS, RET_HEAD_DIM), row3),
                  pl.BlockSpec((bb, D_RET, RET_HEAD_DIM), row3),
                  _const_spec((D_RET, RET_HEAD_DIM)),
                  _const_spec((N_RET_HEADS, D_RET)),
                  _const_spec((N_RET_HEADS, RET_HEAD_DIM))],
        out_specs=[pl.BlockSpec((bb, N_RET_HEADS, RET_HEAD_DIM), row3),
                   pl.BlockSpec((bb, D_RET, RET_HEAD_DIM), row3)],
        out_shape=[jax.ShapeDtypeStruct((n, N_RET_HEADS, RET_HEAD_DIM), F32),
                   jax.ShapeDtypeStruct((n, D_RET, RET_HEAD_DIM), F32)],
        compiler_params=_cparams("parallel"),
        name="retention_step",
    )(q, k, vh, gh, s2, gam_rows, gam_heads, ret_norm.reshape(N_RET_HEADS, RET_HEAD_DIM))
    return (y.reshape(n, D_RET).astype(BF16),
            s_new.reshape(n, N_RET_HEADS, RET_HEAD_DIM, RET_HEAD_DIM))


def _first_max(vals, idxs, sentinel):
    m = jnp.max(vals, axis=0, keepdims=True)
    first = jnp.min(jnp.where(vals == m, idxs, sentinel), axis=0, keepdims=True)
    return m, idxs == first


def _route(scores, sel):
    t = sel.shape[1]
    neg = -jnp.inf
    member = lax.broadcasted_iota(jnp.int32, (GROUP_SIZE, t), 0).astype(F32)
    groups = [sel[g * GROUP_SIZE:(g + 1) * GROUP_SIZE, :] for g in range(N_EXPERT_GROUPS)]
    gscore = []
    for grp in groups:
        m1, pick = _first_max(grp, member, float(GROUP_SIZE))
        m2 = jnp.max(jnp.where(pick, neg, grp), axis=0, keepdims=True)
        gscore.append(m1 + m2)
    gs = jnp.concatenate(gscore, axis=0)
    gkeep = jnp.zeros(gs.shape, F32)
    for _ in range(TOPK_GROUPS):
        _, pick = _first_max(gs, member, float(N_EXPERT_GROUPS))
        gkeep = jnp.where(pick, 1.0, gkeep)
        gs = jnp.where(pick, neg, gs)
    cand = jnp.concatenate(
        [jnp.where(gkeep[g:g + 1, :] > 0.5, groups[g], neg) for g in range(N_EXPERT_GROUPS)],
        axis=0)
    expert = lax.broadcasted_iota(jnp.int32, (N_EXPERTS, t), 0).astype(F32)
    ids, ws = [], []
    for _ in range(TOP_K):
        _, pick = _first_max(cand, expert, float(N_EXPERTS))
        ids.append(jnp.sum(jnp.where(pick, expert, 0.0), axis=0, keepdims=True))
        ws.append(jnp.sum(jnp.where(pick, scores, 0.0), axis=0, keepdims=True))
        cand = jnp.where(pick, neg, cand)
    w = jnp.concatenate(ws, axis=0)
    gates = w / jnp.sum(w, axis=0, keepdims=True) * ROUTED_SCALE
    return jnp.concatenate(ids, axis=0), gates


def _mix_kernel(x_ref, ys_ref, yr_ref, woa_ref, wob_ref, n2_ref, wrt_ref, rb_ref,
                wgs_ref, wus_ref, wds_ref, xs_ref, hp_ref, meta_ref, eid_ref, *, n_blocks):
    @pl.when(pl.program_id(0) >= n_blocks)
    def _():
        hp_ref[...] = jnp.zeros_like(hp_ref)
        meta_ref[...] = jnp.zeros_like(meta_ref)

    @pl.when(pl.program_id(0) < n_blocks)
    def _():
        _mix_body(x_ref, ys_ref, yr_ref, woa_ref, wob_ref, n2_ref, wrt_ref, rb_ref,
                  wgs_ref, wus_ref, wds_ref, xs_ref, hp_ref, meta_ref, eid_ref)


def _mix_body(x_ref, ys_ref, yr_ref, woa_ref, wob_ref, n2_ref, wrt_ref, rb_ref,
              wgs_ref, wus_ref, wds_ref, xs_ref, hp_ref, meta_ref, eid_ref):
    tm = x_ref.shape[0]
    x2 = (x_ref[...]
          + jnp.dot(ys_ref[...], woa_ref[...], preferred_element_type=F32)
          + jnp.dot(yr_ref[...], wob_ref[...], preferred_element_type=F32))
    h2f = _rms(x2, n2_ref[...])
    h2 = h2f.astype(BF16)
    bits = lax.bitcast_convert_type(h2.astype(F32), jnp.uint32)
    half = D_MODEL // 2
    word = (bits[:, :half] >> 16) | bits[:, half:]
    for c in range(PACK_ROWS):
        hp_ref[pl.ds(c, tm, stride=PACK_ROWS), :] = word[:, c * LANES:(c + 1) * LANES]
    logits = lax.dot_general(wrt_ref[...], h2f, (((1,), (1,)), ((), ())),
                             precision=lax.Precision.HIGHEST, preferred_element_type=F32)
    scores = jax.nn.sigmoid(logits)
    ids, gates = _route(scores, scores + rb_ref[:, :1])
    eid_ref[...] = ids.astype(jnp.int32)
    rec = jnp.concatenate([gates, ids, jnp.zeros((LANES - 2 * TOP_K, tm), F32)], axis=0).T
    meta_ref[pl.ds(0, tm, stride=META_ROWS), :] = rec
    for c in range(1, META_ROWS):
        meta_ref[pl.ds(c, tm, stride=META_ROWS), :] = jnp.zeros((tm, LANES), F32)
    a = jnp.dot(h2, wgs_ref[...], preferred_element_type=F32)
    b = jnp.dot(h2, wus_ref[...], preferred_element_type=F32)
    xs_ref[...] = x2 + jnp.dot((jax.nn.silu(a) * b).astype(BF16), wds_ref[...],
                               preferred_element_type=F32)


def _mix(x, ys5, yret, mp, tm, out_tokens):
    rows = x.shape[0]
    n_blocks = rows // tm
    assert out_tokens % tm == 0
    packed = lambda i: (i, 0)
    row = lambda i: (jnp.minimum(i, n_blocks - 1), 0)
    col = lambda i: (0, jnp.minimum(i, n_blocks - 1))
    return pl.pallas_call(
        functools.partial(_mix_kernel, n_blocks=n_blocks),
        grid=(out_tokens // tm,),
        in_specs=[pl.BlockSpec((tm, D_MODEL), row),
                  pl.BlockSpec((tm, D_S5), row), pl.BlockSpec((tm, D_RET), row),
                  _const_spec((D_S5, D_MODEL)), _const_spec((D_RET, D_MODEL)),
                  _const_spec((1, D_MODEL)),
                  _const_spec((N_EXPERTS, D_MODEL)), _const_spec((N_EXPERTS, LANES)),
                  _const_spec((D_MODEL, D_EXPERT)), _const_spec((D_MODEL, D_EXPERT)),
                  _const_spec((D_EXPERT, D_MODEL))],
        out_specs=[pl.BlockSpec((tm, D_MODEL), row),
                   pl.BlockSpec((tm * PACK_ROWS, LANES), packed),
                   pl.BlockSpec((tm * META_ROWS, LANES), packed),
                   pl.BlockSpec((TOP_K, tm), col)],
        out_shape=[jax.ShapeDtypeStruct((rows, D_MODEL), F32),
                   jax.ShapeDtypeStruct((out_tokens * PACK_ROWS, LANES), jnp.uint32),
                   jax.ShapeDtypeStruct((out_tokens * META_ROWS, LANES), F32),
                   jax.ShapeDtypeStruct((TOP_K, rows), jnp.int32)],
        compiler_params=_cparams("arbitrary"),
        name="out_proj_router",
    )(x, ys5, yret, mp["woa"], mp["wob"], mp["n2"], mp["wrt"], mp["rb"],
      mp["wgs"], mp["wus"], mp["wds"])


def _moe_kernel(t0_ref, nt_ref, src_ref, dst_ref, hp_ref, meta_ref, wg_ref, wu_ref, wd_ref,
                acc_ref, xta, xma, ota, xtb, xmb, otb):
    s = pl.program_id(0)
    tm = MOE_TILE
    half = D_MODEL // 2
    n_gap = 8
    rows_per_gap = tm // n_gap

    @pl.when(s % N_EXPERTS == 0)
    def _():
        zero = jnp.zeros((ZERO_ROWS, LANES), F32)

        def clear(i, carry):
            acc_ref[0, pl.ds(pl.multiple_of(i * ZERO_ROWS, ZERO_ROWS), ZERO_ROWS), :] = zero
            return carry

        lax.fori_loop(0, acc_ref.shape[1] // ZERO_ROWS, clear, 0)

    nt = nt_ref[s]
    expert = (s % N_EXPERTS).astype(F32)

    def fetch_rows(j, xt, xm, rows):
        for m in rows:
            r = pl.multiple_of(src_ref[j, 0, m], PACK_ROWS)
            xt[pl.ds(m, PACK_ROWS, stride=XT_PITCH), :] = hp_ref[pl.ds(r, PACK_ROWS), :]
            xm[pl.ds(m, META_ROWS, stride=XT_PITCH), :] = meta_ref[
                pl.ds(pl.multiple_of(r >> 1, META_ROWS), META_ROWS), :]

    def add_rows(j, ot, rows):
        for m0 in range(rows.start, rows.stop, ADD_UNROLL):
            new = []
            for m in range(m0, m0 + ADD_UNROLL):
                base = pl.multiple_of(dst_ref[j, 0, m], ACC_ROWS)
                row = ot[pl.ds(m, ACC_ROWS, stride=OT_PITCH), :]
                new.append((base, acc_ref[0, pl.ds(base, ACC_ROWS), :] + row))
            for base, v in new:
                acc_ref[0, pl.ds(base, ACC_ROWS), :] = v

    def stage(xt, xm, ot, fetch=None, add=None):
        gap = [0]

        def row_work():
            rows = range(gap[0] * rows_per_gap, (gap[0] + 1) * rows_per_gap)
            gap[0] += 1
            if fetch is not None:
                fetch_rows(fetch[0], fetch[1], fetch[2], rows)
            if add is not None:
                add_rows(add[0], add[1], rows)

        lo, hi = [], []
        for c in range(PACK_ROWS):
            wc = xt[c * XT_PITCH:c * XT_PITCH + tm, :]
            lo.append(lax.bitcast_convert_type(wc << 16, F32).astype(BF16))
            hi.append(lax.bitcast_convert_type(wc & jnp.uint32(0xFFFF0000), F32).astype(BF16))
        x_lo = jnp.concatenate(lo, axis=1)
        x_hi = jnp.concatenate(hi, axis=1)
        rec = xm[0:tm, :]
        ids = pltpu.roll(rec, LANES - TOP_K, 1)
        lane = lax.broadcasted_iota(jnp.int32, rec.shape, 1)
        gate = jnp.sum(jnp.where((lane < TOP_K) & (ids == expert), rec, 0.0),
                       axis=1, keepdims=True)
        a = jnp.dot(x_lo, wg_ref[0, :half], preferred_element_type=F32)
        row_work()
        a = a + jnp.dot(x_hi, wg_ref[0, half:], preferred_element_type=F32)
        row_work()
        b = jnp.dot(x_lo, wu_ref[0, :half], preferred_element_type=F32)
        row_work()
        b = b + jnp.dot(x_hi, wu_ref[0, half:], preferred_element_type=F32)
        row_work()
        act = (jax.nn.silu(a) * b * gate).astype(BF16)
        n_col = D_MODEL // (n_gap - 4)
        for p in range(n_gap - 4):
            out = jnp.dot(act, wd_ref[0, :, p * n_col:(p + 1) * n_col],
                          preferred_element_type=F32)
            for cc in range(n_col // LANES):
                c = p * (n_col // LANES) + cc
                ot[c * OT_PITCH:c * OT_PITCH + tm, :] = out[:, cc * LANES:(cc + 1) * LANES]
            row_work()

    all_rows = range(tm)

    @pl.when(nt > 0)
    def _():
        otb[...] = jnp.zeros_like(otb)
        fetch_rows(0, xta, xma, all_rows)

        def pair(i, carry):
            j = 2 * i
            stage(xta, xma, ota, fetch=(j + 1, xtb, xmb), add=(jnp.maximum(j - 1, 0), otb))
            stage(xtb, xmb, otb, fetch=(j + 2, xta, xma), add=(j, ota))
            return carry

        lax.fori_loop(0, nt // 2, pair, 0)

        @pl.when(nt % 2 == 1)
        def _():
            stage(xta, xma, ota, add=(jnp.maximum(nt - 2, 0), otb))
            add_rows(nt - 1, ota, all_rows)

        @pl.when(nt % 2 == 0)
        def _():
            add_rows(nt - 1, otb, all_rows)


def _dispatch_tables(eid_t, n_tok):
    tm, ts = MOE_TILE, MOE_BLOCK_TOKENS
    ng = MOE_BLOCKS * N_EXPERTS
    assert ts < (1 << TOKEN_BITS)
    tok = jnp.arange(n_tok, dtype=jnp.int32)
    grp = (tok // ts)[None, :] * N_EXPERTS + eid_t
    gids = jnp.arange(ng, dtype=jnp.int32)
    counts = jnp.sum((grp.reshape(1, -1) == gids[:, None]).astype(jnp.int32), axis=1)
    npad = (-counts) % tm
    fill = jnp.arange(tm, dtype=jnp.int32)[None, :] < npad[:, None]
    keys = jnp.concatenate([
        ((grp << TOKEN_BITS) | (tok % ts)[None, :]).reshape(-1),
        ((jnp.where(fill, gids[:, None], ng) << TOKEN_BITS) | ts).reshape(-1)])
    toks = lax.sort(keys, dimension=0, is_stable=False) & ((1 << TOKEN_BITS) - 1)
    n_tiles = (n_tok * TOP_K + ng * tm) // tm

    ntile = (counts + npad) // tm
    tile_start = jnp.cumsum(ntile) - ntile
    src = jnp.where(toks < ts, toks, 0) * PACK_ROWS
    dst = toks * ACC_ROWS
    tail = jnp.zeros((GROUP_TILES * tm,), jnp.int32)
    return {"t0": tile_start.astype(jnp.int32), "nt": ntile.astype(jnp.int32),
            "src": jnp.concatenate([src, tail]).reshape(n_tiles + GROUP_TILES, 1, tm),
            "dst": jnp.concatenate([dst, tail]).reshape(n_tiles + GROUP_TILES, 1, tm)}


def _moe(hp, meta, tabs, ep):
    tm, ts = MOE_TILE, MOE_BLOCK_TOKENS
    assert GROUP_TILES > -(-ts // tm)
    window = pl.BlockSpec((pl.Element(GROUP_TILES), pl.Element(1), pl.Element(tm)),
                          lambda s, t0, nt: (t0[s], 0, 0), memory_space=pltpu.SMEM)
    wspec = lambda shape: pl.BlockSpec((1,) + shape, lambda s, t0, nt: (s % N_EXPERTS, 0, 0))
    blk = lambda s, t0, nt: (s // N_EXPERTS, 0)
    acc_rows = (ts + SUBLANES) * ACC_ROWS
    assert acc_rows % ZERO_ROWS == 0
    stage_bufs = [pltpu.VMEM((PACK_ROWS * XT_PITCH, LANES), jnp.uint32),
                  pltpu.VMEM((META_ROWS * XT_PITCH, LANES), F32),
                  pltpu.VMEM((ACC_ROWS * OT_PITCH, LANES), F32)]
    grid_spec = pltpu.PrefetchScalarGridSpec(
        num_scalar_prefetch=2,
        grid=(MOE_BLOCKS * N_EXPERTS,),
        in_specs=[window, window,
                  pl.BlockSpec((ts * PACK_ROWS, LANES), blk, pipeline_mode=pl.Buffered(1)),
                  pl.BlockSpec((ts * META_ROWS, LANES), blk, pipeline_mode=pl.Buffered(1)),
                  wspec((D_MODEL, D_EXPERT)), wspec((D_MODEL, D_EXPERT)),
                  wspec((D_EXPERT, D_MODEL))],
        out_specs=pl.BlockSpec((1, acc_rows, LANES), lambda s, t0, nt: (s // N_EXPERTS, 0, 0),
                               pipeline_mode=pl.Buffered(1)),
        scratch_shapes=stage_bufs + stage_bufs,
    )
    return pl.pallas_call(
        _moe_kernel,
        grid_spec=grid_spec,
        out_shape=jax.ShapeDtypeStruct((MOE_BLOCKS, acc_rows, LANES), F32),
        compiler_params=_cparams("arbitrary"),
        name="moe_experts",
    )(tabs["t0"], tabs["nt"], tabs["src"], tabs["dst"], hp, meta, ep["wg"], ep["wu"], ep["wd"])


def _final_kernel(xs_ref, r_ref, fn_ref, y_ref):
    tm = xs_ref.shape[0]
    routed = jnp.concatenate(
        [r_ref[0, pl.ds(c, tm, stride=ACC_ROWS), :] for c in range(ACC_ROWS)], axis=1)
    y_ref[...] = _rms(xs_ref[...] + routed, fn_ref[...])


def _finalize(xs, routed, final_norm, tm, tok0):
    rows = xs.shape[0]
    per_block = MOE_BLOCK_TOKENS // tm
    first = tok0 // tm
    assert tok0 % tm == 0 and MOE_BLOCK_TOKENS % tm == 0
    return pl.pallas_call(
        _final_kernel,
        grid=(rows // tm,),
        in_specs=[pl.BlockSpec((tm, D_MODEL), lambda i: (i, 0)),
                  pl.BlockSpec((1, tm * ACC_ROWS, LANES),
                               lambda i: ((first + i) // per_block, (first + i) % per_block, 0)),
                  _const_spec((1, D_MODEL))],
        out_specs=pl.BlockSpec((tm, D_MODEL), lambda i: (i, 0)),
        out_shape=jax.ShapeDtypeStruct((rows, D_MODEL), F32),
        compiler_params=_cparams("parallel"),
        name="final_norm",
    )(xs, routed, final_norm)


def _rope_tables(pos):
    half = RET_HEAD_DIM // 2
    inv_freq = ROPE_BASE ** (-jnp.arange(half, dtype=F32) / half)
    ang = pos.astype(F32)[:, None] * inv_freq[None, :]
    cos, sin = jnp.cos(ang), jnp.sin(ang)
    cos_t = jnp.concatenate([cos, cos, cos, cos], axis=1)
    sin_t = jnp.concatenate([-sin, sin, -sin, sin], axis=1)
    return cos_t, sin_t


def kernel(x_prompt, x_sample, state_s5_re, state_s5_im, state_ret, meta_tokens, norm1, w_in,
           s5_lam_re, s5_lam_im, s5_log_dt, s5_b_re, s5_b_im, s5_c_re, s5_c_im, s5_d, s5_w_glu,
           s5_norm, ret_norm, w_out, norm2, w_router, router_bias, w_gate_e, w_up_e, w_down_e,
           w_gate_sh, w_up_sh, w_down_sh, final_norm):
    assert norm1.shape[0] == 1, "single-layer model"
    bp, seq, _ = x_prompt.shape
    ns = x_sample.shape[0]
    l = 0

    n1 = norm1[l].reshape(1, D_MODEL)
    w_in_b = w_in[l].astype(BF16)
    s5p = _s5_params(s5_lam_re[l], s5_lam_im[l], s5_log_dt[l], s5_b_re[l], s5_b_im[l],
                     s5_c_re[l], s5_c_im[l], s5_d[l], s5_w_glu[l], s5_norm[l])
    rnorm = ret_norm[l].reshape(1, D_RET)
    w_out_b = w_out[l].astype(BF16)
    mp = {
        "woa": w_out_b[:D_S5], "wob": w_out_b[D_S5:], "n2": norm2[l].reshape(1, D_MODEL),
        "wrt": w_router[l].T,
        "rb": jnp.broadcast_to(router_bias[l][:, None], (N_EXPERTS, LANES)),
        "wgs": w_gate_sh[l].astype(BF16), "wus": w_up_sh[l].astype(BF16),
        "wds": w_down_sh[l].astype(BF16),
    }
    ep = {"wg": w_gate_e[l].astype(BF16), "wu": w_up_e[l].astype(BF16),
          "wd": w_down_e[l].astype(BF16)}
    fnorm = final_norm.reshape(1, D_MODEL)

    xp = x_prompt.reshape(bp * seq, D_MODEL)
    cos_p, sin_p = _rope_tables(N_META + jnp.arange(seq, dtype=jnp.int32))
    tm_a = 512
    up, qp, kp, vp, gp = _project(xp, n1, w_in_b, cos_p, sin_p, tm_a, seq // tm_a)

    meta_tm = jnp.repeat(meta_tokens, SUBLANES, axis=0)
    mchunk = LANES
    meta_chunk = jnp.concatenate([jnp.zeros((mchunk - N_META, D_MODEL), F32), meta_tokens], axis=0)
    x_small = jnp.concatenate([x_sample.reshape(ns, D_MODEL), meta_tm, meta_chunk], axis=0)
    meta_pos = jnp.arange(N_META, dtype=jnp.int32)
    pos_small = jnp.concatenate([jnp.full((ns,), PAST_LEN, jnp.int32),
                                 jnp.repeat(meta_pos, SUBLANES),
                                 jnp.zeros((mchunk - N_META,), jnp.int32), meta_pos])
    cos_s, sin_s = _rope_tables(pos_small)
    n_small = x_small.shape[0]
    n_tm = ns + N_META * SUBLANES
    us, qs, ks, vs, gs = _project(x_small, n1, w_in_b, cos_s, sin_s, n_small, 1)

    zero8 = jnp.zeros((SUBLANES, S5_LANES), F32)
    _, m_re, m_im = _s5(us[ns:n_tm], zero8, zero8, s5p, nb=SUBLANES, tt=N_META, nblk=1,
                        bt_major=False)
    tt = 64
    ys5_p, p_re, p_im = _s5(up.reshape(bp, seq, D_S5), m_re, m_im, s5p, nb=bp, tt=tt,
                            nblk=seq // tt, bt_major=True)
    ys5_p = ys5_p.reshape(bp * seq, D_S5)
    ys5_s, s_re, s_im = _s5(us[:ns], state_s5_re[l].reshape(ns, S5_LANES),
                            state_s5_im[l].reshape(ns, S5_LANES), s5p, nb=ns, tt=1, nblk=1,
                            bt_major=False)

    zero_pair = jnp.zeros((1, N_HEAD_PAIRS, LANES, LANES), F32)
    _, m_pair = _retention(qs[n_tm:], ks[n_tm:], vs[n_tm:], gs[n_tm:],
                           zero_pair, rnorm, nseq=1, chunk=mchunk, nchunk=1)
    chunk = 256
    yret_p, p_pair = _retention(qp, kp, vp, gp, jnp.broadcast_to(m_pair, (bp,) + m_pair.shape[1:]),
                                rnorm, nseq=bp, chunk=chunk, nchunk=seq // chunk)
    yret_s, ret_s = _retention_step(qs[:ns], ks[:ns], vs[:ns], gs[:ns], state_ret[l], ret_norm[l])

    n_prompt = bp * seq
    assert n_prompt + ns <= MOE_BLOCKS * MOE_BLOCK_TOKENS
    n_moe = MOE_BLOCKS * MOE_BLOCK_TOKENS
    xs_p, hp, meta, eid_p = _mix(xp, ys5_p, yret_p, mp, 512, n_moe)
    xs_s, hp_s, meta_s, eid_s = _mix(x_sample.reshape(ns, D_MODEL), ys5_s, yret_s, mp, ns, ns)
    hp = lax.dynamic_update_slice(hp, hp_s, (n_prompt * PACK_ROWS, 0))
    meta = lax.dynamic_update_slice(meta, meta_s, (n_prompt * META_ROWS, 0))
    tabs = _dispatch_tables(jnp.concatenate([eid_p, eid_s], axis=1), n_prompt + ns)
    routed = _moe(hp, meta, tabs, ep)
    y_p = _finalize(xs_p, routed, fnorm, 512, 0)
    y_s = _finalize(xs_s, routed, fnorm, ns, n_prompt)

    shape5 = (1, bp, N_S5_GROUPS, S5_STATE)
    return (y_p.reshape(bp, seq, D_MODEL),
            y_s.reshape(ns, 1, D_MODEL),
            p_re.reshape(shape5), p_im.reshape(shape5),
            _unpair_state(p_pair)[None],
            s_re.reshape(1, ns, N_S5_GROUPS, S5_STATE), s_im.reshape(1, ns, N_S5_GROUPS, S5_STATE),
            ret_s[None])
```

```python
import functools
import math

import jax
import jax.numpy as jnp
from jax import lax
from jax.experimental import pallas as pl
from jax.experimental.pallas import tpu as pltpu

F32 = jnp.float32
BF16 = jnp.bfloat16

D_MODEL = 1024
N_META = 16
PAST_LEN = 16384
D_S5 = 512
S5_GROUP = 16
N_S5_GROUPS = 32
S5_STATE = 64
S5_LANES = N_S5_GROUPS * S5_STATE
D_RET = 512
N_RET_HEADS = 8
RET_HEAD_DIM = 64
N_HEAD_PAIRS = 4
ROPE_BASE = 10000.0
D_IN = D_S5 + 4 * D_RET
N_EXPERTS = 64
TOP_K = 8
N_EXPERT_GROUPS = 8
GROUP_SIZE = 8
TOPK_GROUPS = 4
D_EXPERT = 256
ROUTED_SCALE = 2.5
EPS = 1e-6

LANES = 128
SUBLANES = 8
VMEM_LIMIT = 56 * 1024 * 1024

PACK_ROWS = D_MODEL // LANES
ACC_ROWS = D_MODEL // LANES
MOE_BLOCKS = 4
MOE_BLOCK_TOKENS = 4608
MOE_TILE = 256
GROUP_TILES = MOE_BLOCK_TOKENS // MOE_TILE + 2
ADD_UNROLL = 8
OT_PITCH = MOE_TILE + SUBLANES
XT_PITCH = MOE_TILE + SUBLANES
META_ROWS = 2
META_SHIFT = 2
assert PACK_ROWS == META_ROWS << META_SHIFT
TOKEN_BITS = 13
ZERO_ROWS = 64


def _cparams(*sem):
    return pltpu.CompilerParams(dimension_semantics=sem, vmem_limit_bytes=VMEM_LIMIT)


def _const_spec(shape):
    nd = len(shape)
    return pl.BlockSpec(shape, lambda *_: (0,) * nd)


def _rms(x, gain):
    return x * lax.rsqrt(jnp.mean(x * x, axis=-1, keepdims=True) + EPS) * gain


def _proj_kernel(x_ref, n1_ref, w_ref, cos_ref, sin_ref, u_ref, q_ref, k_ref, v_ref, g_ref):
    h = _rms(x_ref[...], n1_ref[...]).astype(BF16)
    proj = jnp.dot(h, w_ref[...], preferred_element_type=F32)
    cos = cos_ref[...]
    sin = sin_ref[...]
    lane = lax.broadcasted_iota(jnp.int32, cos.shape, 1)
    first_half = (lane % RET_HEAD_DIM) < (RET_HEAD_DIM // 2)

    def rotary(t):
        partner = jnp.where(first_half,
                            pltpu.roll(t, LANES - RET_HEAD_DIM // 2, 1),
                            pltpu.roll(t, RET_HEAD_DIM // 2, 1))
        return t * cos + partner * sin

    u_ref[...] = proj[:, :D_S5]
    for j in range(N_HEAD_PAIRS):
        lo = D_S5 + j * LANES
        q_ref[:, j * LANES:(j + 1) * LANES] = rotary(proj[:, lo:lo + LANES]).astype(BF16)
        lo += D_RET
        k_ref[:, j * LANES:(j + 1) * LANES] = (
            rotary(proj[:, lo:lo + LANES]) * (RET_HEAD_DIM ** -0.5)).astype(BF16)
    v_ref[...] = proj[:, D_S5 + 2 * D_RET:D_S5 + 3 * D_RET].astype(BF16)
    g_ref[...] = proj[:, D_S5 + 3 * D_RET:]


def _project(x, norm1, w_in_b, cos_t, sin_t, tm, table_blocks):
    rows = x.shape[0]
    row = lambda i: (i, 0)
    tab = lambda i: (i % table_blocks, 0)
    return pl.pallas_call(
        _proj_kernel,
        grid=(rows // tm,),
        in_specs=[pl.BlockSpec((tm, D_MODEL), row),
                  _const_spec((1, D_MODEL)),
                  _const_spec((D_MODEL, D_IN)),
                  pl.BlockSpec((tm, LANES), tab),
                  pl.BlockSpec((tm, LANES), tab)],
        out_specs=[pl.BlockSpec((tm, D_S5), row),
                   pl.BlockSpec((tm, D_RET), row),
                   pl.BlockSpec((tm, D_RET), row),
                   pl.BlockSpec((tm, D_RET), row),
                   pl.BlockSpec((tm, D_RET), row)],
        out_shape=[jax.ShapeDtypeStruct((rows, D_S5), F32),
                   jax.ShapeDtypeStruct((rows, D_RET), BF16),
                   jax.ShapeDtypeStruct((rows, D_RET), BF16),
                   jax.ShapeDtypeStruct((rows, D_RET), BF16),
                   jax.ShapeDtypeStruct((rows, D_RET), F32)],
        compiler_params=_cparams("parallel"),
        name="in_proj",
    )(x, norm1, w_in_b, cos_t, sin_t)


def _s5_kernel(u_ref, x0re_ref, x0im_ref, are_ref, aim_ref, b_ref, c_ref, d_ref, wglu_ref,
               nrm_ref, y_ref, sre_ref, sim_ref, st, sre, sim, utm, ytm, *, nb, tt, bt_major):
    i = pl.program_id(0)
    rows = nb * tt

    @pl.when(i == 0)
    def _():
        sre[...] = x0re_ref[...]
        sim[...] = x0im_ref[...]

    if bt_major:
        for t in range(tt):
            utm[t * nb:(t + 1) * nb, :] = u_ref[:, t, :]
    else:
        utm[...] = u_ref[...]

    ub = utm[...].astype(BF16)
    half = S5_LANES // 2
    kh = D_S5 // 2
    for part in range(2):
        for hf in range(2):
            st[:, part * S5_LANES + hf * half:part * S5_LANES + (hf + 1) * half] = jnp.dot(
                ub[:, hf * kh:(hf + 1) * kh], b_ref[part, hf], preferred_element_type=F32)

    lw = 512 if nb == SUBLANES else LANES
    for lg in range(S5_LANES // lw):
        re = slice(lg * lw, (lg + 1) * lw)
        im = slice(S5_LANES + lg * lw, S5_LANES + (lg + 1) * lw)
        a_re = are_ref[:, re]
        a_im = aim_ref[:, re]

        def step(t, carry):
            s_re, s_im = carry
            r0 = pl.multiple_of(t * nb, nb)
            n_re = a_re * s_re - a_im * s_im + st[pl.ds(r0, nb), re]
            n_im = a_re * s_im + a_im * s_re + st[pl.ds(r0, nb), im]
            st[pl.ds(r0, nb), re] = n_re
            st[pl.ds(r0, nb), im] = n_im
            return n_re, n_im

        f_re, f_im = lax.fori_loop(0, tt, step, (sre[:, re], sim[:, re]))
        sre[:, re] = f_re
        sim[:, re] = f_im

    sre_ref[...] = sre[...]
    sim_ref[...] = sim[...]

    ys = []
    for hf in range(2):
        xr = st[:, hf * half:(hf + 1) * half].astype(BF16)
        xi = st[:, S5_LANES + hf * half:S5_LANES + (hf + 1) * half].astype(BF16)
        ys.append(jnp.dot(xr, c_ref[0, hf], preferred_element_type=F32)
                  - jnp.dot(xi, c_ref[1, hf], preferred_element_type=F32))
    y = jnp.concatenate(ys, axis=1) + d_ref[...] * utm[...]
    y = jax.nn.gelu(y)
    y = y * jax.nn.sigmoid(jnp.dot(y.astype(BF16), wglu_ref[...], preferred_element_type=F32))
    y = _rms(y, nrm_ref[...]).astype(BF16)
    if bt_major:
        yf = y.astype(F32)
        for j in range(D_S5 // LANES):
            ytm[j] = yf[:, j * LANES:(j + 1) * LANES]
        for b in range(nb):
            for j in range(D_S5 // LANES):
                y_ref[b, :, j * LANES:(j + 1) * LANES] = (
                    ytm[j, pl.ds(b, tt, stride=nb), :].astype(BF16))
    else:
        y_ref[...] = y


def _s5(u, x0re, x0im, s5p, *, nb, tt, nblk, bt_major):
    rows = nb * tt
    if bt_major:
        u_spec = pl.BlockSpec((nb, tt, D_S5), lambda i: (0, i, 0))
        y_shape = jax.ShapeDtypeStruct((nb, tt * nblk, D_S5), BF16)
    else:
        u_spec = pl.BlockSpec((rows, D_S5), lambda i: (i, 0))
        y_shape = jax.ShapeDtypeStruct((rows * nblk, D_S5), BF16)
    state_spec = _const_spec((nb, S5_LANES))
    are = jnp.broadcast_to(s5p["are"], (nb, S5_LANES))
    aim = jnp.broadcast_to(s5p["aim"], (nb, S5_LANES))
    return pl.pallas_call(
        functools.partial(_s5_kernel, nb=nb, tt=tt, bt_major=bt_major),
        grid=(nblk,),
        in_specs=[u_spec, state_spec, state_spec, state_spec, state_spec,
                  _const_spec((2, 2, D_S5 // 2, S5_LANES // 2)),
                  _const_spec((2, 2, S5_LANES // 2, D_S5 // 2)),
                  _const_spec((1, D_S5)),
                  _const_spec((D_S5, D_S5)),
                  _const_spec((1, D_S5))],
        out_specs=[u_spec, state_spec, state_spec],
        out_shape=[y_shape,
                   jax.ShapeDtypeStruct((nb, S5_LANES), F32),
                   jax.ShapeDtypeStruct((nb, S5_LANES), F32)],
        scratch_shapes=[pltpu.VMEM((rows, 2 * S5_LANES), F32),
                        pltpu.VMEM((nb, S5_LANES), F32),
                        pltpu.VMEM((nb, S5_LANES), F32),
                        pltpu.VMEM((rows, D_S5), F32),
                        pltpu.VMEM((D_S5 // LANES, rows, LANES), F32)],
        compiler_params=_cparams("arbitrary"),
        name="s5_" + ("bt" if bt_major else "tm") + str(nb),
    )(u, x0re, x0im, are, aim, s5p["b"], s5p["c"], s5p["d"], s5p["wglu"], s5p["nrm"])


def _s5_params(lam_re, lam_im, log_dt, b_re, b_im, c_re, c_im, d_skip, w_glu, nrm):
    dt = jnp.exp(log_dt)[:, None]
    mag = jnp.exp(lam_re * dt)
    abar_re, abar_im = mag * jnp.cos(lam_im * dt), mag * jnp.sin(lam_im * dt)
    num_re, num_im = abar_re - 1.0, abar_im
    den = lam_re * lam_re + lam_im * lam_im
    f_re = (num_re * lam_re + num_im * lam_im) / den
    f_im = (num_im * lam_re - num_re * lam_im) / den
    bbar_re = f_re[..., None] * b_re - f_im[..., None] * b_im
    bbar_im = f_re[..., None] * b_im + f_im[..., None] * b_re
    hg = N_S5_GROUPS // 2
    eye = jnp.eye(hg, dtype=F32)

    def bdiag(bb):
        bb = bb.reshape(2, hg, S5_STATE, S5_GROUP)
        return jnp.einsum("zgph,gk->zghkp", bb, eye).reshape(2, hg * S5_GROUP, hg * S5_STATE)

    def cdiag(cc):
        cc = cc.reshape(2, hg, S5_GROUP, S5_STATE)
        return jnp.einsum("zgnp,gk->zgpkn", cc, eye).reshape(2, hg * S5_STATE, hg * S5_GROUP)

    return {
        "are": abar_re.reshape(1, S5_LANES), "aim": abar_im.reshape(1, S5_LANES),
        "b": jnp.stack([bdiag(bbar_re), bdiag(bbar_im)]).astype(BF16),
        "c": jnp.stack([cdiag(c_re), cdiag(c_im)]).astype(BF16),
        "d": d_skip.reshape(1, D_S5), "wglu": w_glu.astype(BF16), "nrm": nrm.reshape(1, D_S5),
    }


def _head_norm_gate(o, g, gain, lo):
    inv = 1.0 / RET_HEAD_DIM

    def seg_mean(t):
        s_lo = jnp.sum(jnp.where(lo, t, 0.0), axis=1, keepdims=True)
        s_hi = jnp.sum(jnp.where(lo, 0.0, t), axis=1, keepdims=True)
        return jnp.where(lo, s_lo, s_hi) * inv

    dlt = o - seg_mean(o)
    var = seg_mean(dlt * dlt)
    return jax.nn.silu(g) * (dlt * lax.rsqrt(var + EPS) * gain)


def _ret_kernel(q_ref, k_ref, v_ref, g_ref, s0_ref, dm_ref, qd_ref, kd_ref, gc_ref, bm_ref,
                nrm_ref, y_ref, so_ref, s_acc):
    c = pl.program_id(1)

    @pl.when(c == 0)
    def _():
        s_acc[...] = s0_ref[0]

    rows = q_ref.shape[0]
    lane = lax.broadcasted_iota(jnp.int32, (rows, LANES), 1)
    lo = lane < RET_HEAD_DIM
    for j in range(N_HEAD_PAIRS):
        sl = slice(j * LANES, (j + 1) * LANES)
        q2 = q_ref[:, sl].astype(F32)
        k2 = k_ref[:, sl]
        v2 = v_ref[:, sl]
        s_pair = s_acc[j]
        cross = jnp.dot((q2 * qd_ref[:, sl]).astype(BF16), s_pair.astype(BF16),
                        preferred_element_type=F32)
        k_dec = (k2.astype(F32) * kd_ref[:, sl]).astype(BF16)
        upd = lax.dot_general(k_dec, v2, (((0,), (0,)), ((), ())), preferred_element_type=F32)
        s_acc[j] = gc_ref[j] * s_pair + bm_ref[...] * upd
        inner = []
        for hh, qh in enumerate((jnp.where(lo, q2, 0.0), jnp.where(lo, 0.0, q2))):
            sc = lax.dot_general(qh.astype(BF16), k2, (((1,), (1,)), ((), ())),
                                 preferred_element_type=F32) * dm_ref[2 * j + hh]
            inner.append(jnp.dot(sc.astype(BF16), v2, preferred_element_type=F32))
        o = jnp.where(lo, inner[0], inner[1]) + cross
        y_ref[:, sl] = _head_norm_gate(o, g_ref[:, sl], nrm_ref[:, sl], lo).astype(BF16)
    so_ref[0] = s_acc[...]


def _ret_tables(chunk):
    log_g = jnp.log1p(-jnp.exp2(-5.0 - jnp.arange(N_RET_HEADS, dtype=F32)))
    n = jnp.arange(chunk, dtype=F32)
    diff = n[:, None] - n[None, :]
    dm = jnp.where(diff[None] >= 0.0,
                   jnp.exp(log_g[:, None, None] * jnp.maximum(diff, 0.0)[None]), 0.0)
    per_lane = lambda t: jnp.repeat(t.T, RET_HEAD_DIM, axis=1)
    qd = per_lane(jnp.exp(log_g[:, None] * (n + 1.0)[None]))
    kd = per_lane(jnp.exp(log_g[:, None] * (chunk - 1.0 - n)[None]))
    gch = jnp.exp(log_g * chunk)
    blk = jnp.kron(jnp.eye(2, dtype=F32), jnp.ones((RET_HEAD_DIM, RET_HEAD_DIM), F32))
    gc = jnp.repeat(gch.reshape(N_HEAD_PAIRS, 2), RET_HEAD_DIM, axis=1)[:, :, None] * blk[None]
    return dm, qd, kd, gc, blk


def _pair_state(s):
    b = s.shape[0]
    s = s.reshape(b, N_HEAD_PAIRS, 2, RET_HEAD_DIM, RET_HEAD_DIM)
    z = jnp.zeros_like(s[:, :, 0])
    top = jnp.concatenate([s[:, :, 0], z], axis=-1)
    bot = jnp.concatenate([z, s[:, :, 1]], axis=-1)
    return jnp.concatenate([top, bot], axis=-2)


def _unpair_state(s2):
    d = RET_HEAD_DIM
    return jnp.stack([s2[:, :, :d, :d], s2[:, :, d:, d:]], axis=2).reshape(
        s2.shape[0], N_RET_HEADS, d, d)


def _retention(q, k, v, g, s0_pair, ret_norm, *, nseq, chunk, nchunk):
    dm, qd, kd, gc, blk = _ret_tables(chunk)
    row = lambda b, c: (b * nchunk + c, 0)
    st_spec = pl.BlockSpec((1, N_HEAD_PAIRS, LANES, LANES), lambda b, c: (b, 0, 0, 0))
    blk_spec = pl.BlockSpec((chunk, D_RET), row)
    return pl.pallas_call(
        _ret_kernel,
        grid=(nseq, nchunk),
        in_specs=[blk_spec, blk_spec, blk_spec, blk_spec, st_spec,
                  _const_spec((N_RET_HEADS, chunk, chunk)),
                  _const_spec((chunk, D_RET)), _const_spec((chunk, D_RET)),
                  _const_spec((N_HEAD_PAIRS, LANES, LANES)), _const_spec((LANES, LANES)),
                  _const_spec((1, D_RET))],
        out_specs=[blk_spec, st_spec],
        out_shape=[jax.ShapeDtypeStruct((nseq * nchunk * chunk, D_RET), BF16),
                   jax.ShapeDtypeStruct((nseq, N_HEAD_PAIRS, LANES, LANES), F32)],
        scratch_shapes=[pltpu.VMEM((N_HEAD_PAIRS, LANES, LANES), F32)],
        compiler_params=_cparams("arbitrary", "arbitrary"),
        name="retention_c%d" % chunk,
    )(q, k, v, g, s0_pair, dm, qd, kd, gc, blk, ret_norm)


def _ret_step_kernel(q_ref, k_ref, vh_ref, gh_ref, s_ref, gamr_ref, gamh_ref, nrm_ref,
                     y_ref, so_ref, *, bb):
    head = lax.broadcasted_iota(jnp.int32, (N_RET_HEADS, D_RET), 0)
    lane = lax.broadcasted_iota(jnp.int32, (N_RET_HEADS, D_RET), 1)
    own = (lane // RET_HEAD_DIM) == head
    gam_rows = gamr_ref[...]
    gam_h = gamh_ref[...]
    qf = q_ref[...].astype(F32)
    kf = k_ref[...].astype(F32)
    hi = lax.Precision.HIGHEST
    for b in range(bb):
        qm = jnp.where(own, qf[b:b + 1, :], 0.0)
        km = jnp.where(own, kf[b:b + 1, :], 0.0)
        vh = vh_ref[b]
        s_b = s_ref[b]
        score = jnp.sum(qm * km, axis=1, keepdims=True)
        inner = score.astype(BF16).astype(F32) * vh
        q_dec = (qm * gam_h).astype(BF16).astype(F32)
        cross = jnp.dot(q_dec, s_b, precision=hi, preferred_element_type=F32)
        o = inner + cross
        dlt = o - jnp.mean(o, axis=1, keepdims=True)
        var = jnp.mean(dlt * dlt, axis=1, keepdims=True)
        y_ref[b] = jax.nn.silu(gh_ref[b]) * (dlt * lax.rsqrt(var + EPS) * nrm_ref[...])
        upd = lax.dot_general(km, vh, (((0,), (0,)), ((), ())), precision=hi,
                              preferred_element_type=F32)
        so_ref[b] = gam_rows * s_b + upd


def _retention_step(q, k, v, g, state, ret_norm, bb=16):
    n = q.shape[0]
    gam = jnp.exp(jnp.log1p(-jnp.exp2(-5.0 - jnp.arange(N_RET_HEADS, dtype=F32))))
    gam_rows = jnp.broadcast_to(jnp.repeat(gam, RET_HEAD_DIM)[:, None], (D_RET, RET_HEAD_DIM))
    gam_heads = jnp.broadcast_to(gam[:, None], (N_RET_HEADS, D_RET))
    vh = v.astype(F32).reshape(n, N_RET_HEADS, RET_HEAD_DIM)
    gh = g.reshape(n, N_RET_HEADS, RET_HEAD_DIM)
    s2 = state.reshape(n, D_RET, RET_HEAD_DIM)
    row = lambda i: (i, 0)
    row3 = lambda i: (i, 0, 0)
    y, s_new = pl.pallas_call(
        functools.partial(_ret_step_kernel, bb=bb),
        grid=(n // bb,),
        in_specs=[pl.BlockSpec((bb, D_RET), row), pl.BlockSpec((bb, D_RET), row),
                  pl.BlockSpec((bb, N_RET_HEADS, RET_HEAD_DIM), row3),
                  pl.BlockSpec((bb, N_RET_HEADS, RET_HEAD_DIM), row3),
                  pl.BlockSpec((bb, D_RET, RET_HEAD_DIM), row3),
                  _const_spec((D_RET, RET_HEAD_DIM)),
                  _const_spec((N_RET_HEADS, D_RET)),
                  _const_spec((N_RET_HEADS, RET_HEAD_DIM))],
        out_specs=[pl.BlockSpec((bb, N_RET_HEADS, RET_HEAD_DIM), row3),
                   pl.BlockSpec((bb, D_RET, RET_HEAD_DIM), row3)],
        out_shape=[jax.ShapeDtypeStruct((n, N_RET_HEADS, RET_HEAD_DIM), F32),
                   jax.ShapeDtypeStruct((n, D_RET, RET_HEAD_DIM), F32)],
        compiler_params=_cparams("parallel"),
        name="retention_step",
    )(q, k, vh, gh, s2, gam_rows, gam_heads, ret_norm.reshape(N_RET_HEADS, RET_HEAD_DIM))
    return (y.reshape(n, D_RET).astype(BF16),
            s_new.reshape(n, N_RET_HEADS, RET_HEAD_DIM, RET_HEAD_DIM))


def _first_max(vals, idxs, sentinel):
    m = jnp.max(vals, axis=0, keepdims=True)
    first = jnp.min(jnp.where(vals == m, idxs, sentinel), axis=0, keepdims=True)
    return m, idxs == first


def _route(scores, sel):
    t = sel.shape[1]
    neg = -jnp.inf
    member = lax.broadcasted_iota(jnp.int32, (GROUP_SIZE, t), 0).astype(F32)
    groups = [sel[g * GROUP_SIZE:(g + 1) * GROUP_SIZE, :] for g in range(N_EXPERT_GROUPS)]
    gscore = []
    for grp in groups:
        m1, pick = _first_max(grp, member, float(GROUP_SIZE))
        m2 = jnp.max(jnp.where(pick, neg, grp), axis=0, keepdims=True)
        gscore.append(m1 + m2)
    gs = jnp.concatenate(gscore, axis=0)
    gkeep = jnp.zeros(gs.shape, F32)
    for _ in range(TOPK_GROUPS):
        _, pick = _first_max(gs, member, float(N_EXPERT_GROUPS))
        gkeep = jnp.where(pick, 1.0, gkeep)
        gs = jnp.where(pick, neg, gs)
    cand = jnp.concatenate(
        [jnp.where(gkeep[g:g + 1, :] > 0.5, groups[g], neg) for g in range(N_EXPERT_GROUPS)],
        axis=0)
    expert = lax.broadcasted_iota(jnp.int32, (N_EXPERTS, t), 0).astype(F32)
    ids, ws = [], []
    for _ in range(TOP_K):
        _, pick = _first_max(cand, expert, float(N_EXPERTS))
        ids.append(jnp.sum(jnp.where(pick, expert, 0.0), axis=0, keepdims=True))
        ws.append(jnp.sum(jnp.where(pick, scores, 0.0), axis=0, keepdims=True))
        cand = jnp.where(pick, neg, cand)
    w = jnp.concatenate(ws, axis=0)
    gates = w / jnp.sum(w, axis=0, keepdims=True) * ROUTED_SCALE
    return jnp.concatenate(ids, axis=0), gates


def _mix_kernel(x_ref, ys_ref, yr_ref, woa_ref, wob_ref, n2_ref, wrt_ref, rb_ref,
                wgs_ref, wus_ref, wds_ref, xs_ref, hp_ref, meta_ref, eid_ref, *, n_blocks):
    @pl.when(pl.program_id(0) >= n_blocks)
    def _():
        hp_ref[...] = jnp.zeros_like(hp_ref)
        meta_ref[...] = jnp.zeros_like(meta_ref)

    @pl.when(pl.program_id(0) < n_blocks)
    def _():
        _mix_body(x_ref, ys_ref, yr_ref, woa_ref, wob_ref, n2_ref, wrt_ref, rb_ref,
                  wgs_ref, wus_ref, wds_ref, xs_ref, hp_ref, meta_ref, eid_ref)


def _mix_body(x_ref, ys_ref, yr_ref, woa_ref, wob_ref, n2_ref, wrt_ref, rb_ref,
              wgs_ref, wus_ref, wds_ref, xs_ref, hp_ref, meta_ref, eid_ref):
    tm = x_ref.shape[0]
    x2 = (x_ref[...]
          + jnp.dot(ys_ref[...], woa_ref[...], preferred_element_type=F32)
          + jnp.dot(yr_ref[...], wob_ref[...], preferred_element_type=F32))
    h2f = _rms(x2, n2_ref[...])
    h2 = h2f.astype(BF16)
    h2r = h2.astype(F32)
    for c in range(PACK_ROWS):
        hp_ref[pl.ds(c, tm, stride=PACK_ROWS), :] = h2r[:, c * LANES:(c + 1) * LANES]
    logits = lax.dot_general(wrt_ref[...], h2f, (((1,), (1,)), ((), ())),
                             precision=lax.Precision.HIGHEST, preferred_element_type=F32)
    scores = jax.nn.sigmoid(logits)
    ids, gates = _route(scores, scores + rb_ref[:, :1])
    eid_ref[...] = ids.astype(jnp.int32)
    rec = jnp.concatenate([gates, ids, jnp.zeros((LANES - 2 * TOP_K, tm), F32)], axis=0).T
    meta_ref[pl.ds(0, tm, stride=META_ROWS), :] = rec
    for c in range(1, META_ROWS):
        meta_ref[pl.ds(c, tm, stride=META_ROWS), :] = jnp.zeros((tm, LANES), F32)
    a = jnp.dot(h2, wgs_ref[...], preferred_element_type=F32)
    b = jnp.dot(h2, wus_ref[...], preferred_element_type=F32)
    xs_ref[...] = x2 + jnp.dot((jax.nn.silu(a) * b).astype(BF16), wds_ref[...],
                               preferred_element_type=F32)


def _mix(x, ys5, yret, mp, tm, out_tokens):
    rows = x.shape[0]
    n_blocks = rows // tm
    assert out_tokens % tm == 0
    packed = lambda i: (i, 0)
    row = lambda i: (jnp.minimum(i, n_blocks - 1), 0)
    col = lambda i: (0, jnp.minimum(i, n_blocks - 1))
    return pl.pallas_call(
        functools.partial(_mix_kernel, n_blocks=n_blocks),
        grid=(out_tokens // tm,),
        in_specs=[pl.BlockSpec((tm, D_MODEL), row),
                  pl.BlockSpec((tm, D_S5), row), pl.BlockSpec((tm, D_RET), row),
                  _const_spec((D_S5, D_MODEL)), _const_spec((D_RET, D_MODEL)),
                  _const_spec((1, D_MODEL)),
                  _const_spec((N_EXPERTS, D_MODEL)), _const_spec((N_EXPERTS, LANES)),
                  _const_spec((D_MODEL, D_EXPERT)), _const_spec((D_MODEL, D_EXPERT)),
                  _const_spec((D_EXPERT, D_MODEL))],
        out_specs=[pl.BlockSpec((tm, D_MODEL), row),
                   pl.BlockSpec((tm * PACK_ROWS, LANES), packed),
                   pl.BlockSpec((tm * META_ROWS, LANES), packed),
                   pl.BlockSpec((TOP_K, tm), col)],
        out_shape=[jax.ShapeDtypeStruct((rows, D_MODEL), F32),
                   jax.ShapeDtypeStruct((out_tokens * PACK_ROWS, LANES), F32),
                   jax.ShapeDtypeStruct((out_tokens * META_ROWS, LANES), F32),
                   jax.ShapeDtypeStruct((TOP_K, rows), jnp.int32)],
        compiler_params=_cparams("arbitrary"),
        name="out_proj_router",
    )(x, ys5, yret, mp["woa"], mp["wob"], mp["n2"], mp["wrt"], mp["rb"],
      mp["wgs"], mp["wus"], mp["wds"])


def _moe_kernel(t0_ref, nt_ref, src_ref, dst_ref, hp_ref, meta_ref, wg_ref, wu_ref, wd_ref,
                acc_ref, xta, xma, ota, xtb, xmb, otb):
    s = pl.program_id(0)
    tm = MOE_TILE
    half = D_MODEL // 2
    n_gap = 8
    rows_per_gap = tm // n_gap

    @pl.when(s % N_EXPERTS == 0)
    def _():
        zero = jnp.zeros((ZERO_ROWS, LANES), F32)

        def clear(i, carry):
            acc_ref[0, pl.ds(pl.multiple_of(i * ZERO_ROWS, ZERO_ROWS), ZERO_ROWS), :] = zero
            return carry

        lax.fori_loop(0, acc_ref.shape[1] // ZERO_ROWS, clear, 0)

    nt = nt_ref[s]
    expert = (s % N_EXPERTS).astype(F32)

    def fetch_rows(j, xt, xm, rows):
        for m in rows:
            r = pl.multiple_of(src_ref[j, 0, m], PACK_ROWS)
            xt[pl.ds(m, PACK_ROWS, stride=XT_PITCH), :] = hp_ref[pl.ds(r, PACK_ROWS), :]
            xm[pl.ds(m, META_ROWS, stride=XT_PITCH), :] = meta_ref[
                pl.ds(pl.multiple_of(r >> META_SHIFT, META_ROWS), META_ROWS), :]

    def add_rows(j, ot, rows):
        for m0 in range(rows.start, rows.stop, ADD_UNROLL):
            new = []
            for m in range(m0, m0 + ADD_UNROLL):
                base = pl.multiple_of(dst_ref[j, 0, m], ACC_ROWS)
                row = ot[pl.ds(m, ACC_ROWS, stride=OT_PITCH), :]
                new.append((base, acc_ref[0, pl.ds(base, ACC_ROWS), :] + row))
            for base, v in new:
                acc_ref[0, pl.ds(base, ACC_ROWS), :] = v

    def stage(xt, xm, ot, fetch=None, add=None):
        gap = [0]

        def row_work():
            rows = range(gap[0] * rows_per_gap, (gap[0] + 1) * rows_per_gap)
            gap[0] += 1
            if fetch is not None:
                fetch_rows(fetch[0], fetch[1], fetch[2], rows)
            if add is not None:
                add_rows(add[0], add[1], rows)

        x = jnp.concatenate([xt[c * XT_PITCH:c * XT_PITCH + tm, :].astype(BF16)
                             for c in range(PACK_ROWS)], axis=1)
        rec = xm[0:tm, :]
        ids = pltpu.roll(rec, LANES - TOP_K, 1)
        lane = lax.broadcasted_iota(jnp.int32, rec.shape, 1)
        gate = jnp.sum(jnp.where((lane < TOP_K) & (ids == expert), rec, 0.0),
                       axis=1, keepdims=True)
        a = jnp.dot(x[:, :half], wg_ref[0, :half], preferred_element_type=F32)
        row_work()
        a = a + jnp.dot(x[:, half:], wg_ref[0, half:], preferred_element_type=F32)
        row_work()
        b = jnp.dot(x[:, :half], wu_ref[0, :half], preferred_element_type=F32)
        row_work()
        b = b + jnp.dot(x[:, half:], wu_ref[0, half:], preferred_element_type=F32)
        row_work()
        act = (jax.nn.silu(a) * b * gate).astype(BF16)
        n_col = D_MODEL // (n_gap - 4)
        for p in range(n_gap - 4):
            out = jnp.dot(act, wd_ref[0, :, p * n_col:(p + 1) * n_col],
                          preferred_element_type=F32)
            for cc in range(n_col // LANES):
                c = p * (n_col // LANES) + cc
                ot[c * OT_PITCH:c * OT_PITCH + tm, :] = out[:, cc * LANES:(cc + 1) * LANES]
            row_work()

    all_rows = range(tm)

    @pl.when(nt > 0)
    def _():
        otb[...] = jnp.zeros_like(otb)
        fetch_rows(0, xta, xma, all_rows)

        def pair(i, carry):
            j = 2 * i
            stage(xta, xma, ota, fetch=(j + 1, xtb, xmb), add=(jnp.maximum(j - 1, 0), otb))
            stage(xtb, xmb, otb, fetch=(j + 2, xta, xma), add=(j, ota))
            return carry

        lax.fori_loop(0, nt // 2, pair, 0)

        @pl.when(nt % 2 == 1)
        def _():
            stage(xta, xma, ota, add=(jnp.maximum(nt - 2, 0), otb))
            add_rows(nt - 1, ota, all_rows)

        @pl.when(nt % 2 == 0)
        def _():
            add_rows(nt - 1, otb, all_rows)


def _dispatch_tables(eid_t, n_tok):
    tm, ts = MOE_TILE, MOE_BLOCK_TOKENS
    ng = MOE_BLOCKS * N_EXPERTS
    assert ts < (1 << TOKEN_BITS)
    tok = jnp.arange(n_tok, dtype=jnp.int32)
    grp = (tok // ts)[None, :] * N_EXPERTS + eid_t
    gids = jnp.arange(ng, dtype=jnp.int32)
    counts = jnp.sum((grp.reshape(1, -1) == gids[:, None]).astype(jnp.int32), axis=1)
    npad = (-counts) % tm
    fill = jnp.arange(tm, dtype=jnp.int32)[None, :] < npad[:, None]
    keys = jnp.concatenate([
        ((grp << TOKEN_BITS) | (tok % ts)[None, :]).reshape(-1),
        ((jnp.where(fill, gids[:, None], ng) << TOKEN_BITS) | ts).reshape(-1)])
    toks = lax.sort(keys, dimension=0, is_stable=False) & ((1 << TOKEN_BITS) - 1)
    n_tiles = (n_tok * TOP_K + ng * tm) // tm

    ntile = (counts + npad) // tm
    tile_start = jnp.cumsum(ntile) - ntile
    src = jnp.where(toks < ts, toks, 0) * PACK_ROWS
    dst = toks * ACC_ROWS
    tail = jnp.zeros((GROUP_TILES * tm,), jnp.int32)
    return {"t0": tile_start.astype(jnp.int32), "nt": ntile.astype(jnp.int32),
            "src": jnp.concatenate([src, tail]).reshape(n_tiles + GROUP_TILES, 1, tm),
            "dst": jnp.concatenate([dst, tail]).reshape(n_tiles + GROUP_TILES, 1, tm)}


def _moe(hp, meta, tabs, ep):
    tm, ts = MOE_TILE, MOE_BLOCK_TOKENS
    assert GROUP_TILES > -(-ts // tm)
    window = pl.BlockSpec((pl.Element(GROUP_TILES), pl.Element(1), pl.Element(tm)),
                          lambda s, t0, nt: (t0[s], 0, 0), memory_space=pltpu.SMEM)
    wspec = lambda shape: pl.BlockSpec((1,) + shape, lambda s, t0, nt: (s % N_EXPERTS, 0, 0))
    blk = lambda s, t0, nt: (s // N_EXPERTS, 0)
    acc_rows = (ts + SUBLANES) * ACC_ROWS
    assert acc_rows % ZERO_ROWS == 0
    stage_bufs = [pltpu.VMEM((PACK_ROWS * XT_PITCH, LANES), F32),
                  pltpu.VMEM((META_ROWS * XT_PITCH, LANES), F32),
                  pltpu.VMEM((ACC_ROWS * OT_PITCH, LANES), F32)]
    grid_spec = pltpu.PrefetchScalarGridSpec(
        num_scalar_prefetch=2,
        grid=(MOE_BLOCKS * N_EXPERTS,),
        in_specs=[window, window,
                  pl.BlockSpec((ts * PACK_ROWS, LANES), blk, pipeline_mode=pl.Buffered(1)),
                  pl.BlockSpec((ts * META_ROWS, LANES), blk, pipeline_mode=pl.Buffered(1)),
                  wspec((D_MODEL, D_EXPERT)), wspec((D_MODEL, D_EXPERT)),
                  wspec((D_EXPERT, D_MODEL))],
        out_specs=pl.BlockSpec((1, acc_rows, LANES), lambda s, t0, nt: (s // N_EXPERTS, 0, 0),
                               pipeline_mode=pl.Buffered(1)),
        scratch_shapes=stage_bufs + stage_bufs,
    )
    return pl.pallas_call(
        _moe_kernel,
        grid_spec=grid_spec,
        out_shape=jax.ShapeDtypeStruct((MOE_BLOCKS, acc_rows, LANES), F32),
        compiler_params=_cparams("arbitrary"),
        name="moe_experts",
    )(tabs["t0"], tabs["nt"], tabs["src"], tabs["dst"], hp, meta, ep["wg"], ep["wu"], ep["wd"])


def _final_kernel(xs_ref, r_ref, fn_ref, y_ref):
    tm = xs_ref.shape[0]
    routed = jnp.concatenate(
        [r_ref[0, pl.ds(c, tm, stride=ACC_ROWS), :] for c in range(ACC_ROWS)], axis=1)
    y_ref[...] = _rms(xs_ref[...] + routed, fn_ref[...])


def _finalize(xs, routed, final_norm, tm, tok0):
    rows = xs.shape[0]
    per_block = MOE_BLOCK_TOKENS // tm
    first = tok0 // tm
    assert tok0 % tm == 0 and MOE_BLOCK_TOKENS % tm == 0
    return pl.pallas_call(
        _final_kernel,
        grid=(rows // tm,),
        in_specs=[pl.BlockSpec((tm, D_MODEL), lambda i: (i, 0)),
                  pl.BlockSpec((1, tm * ACC_ROWS, LANES),
                               lambda i: ((first + i) // per_block, (first + i) % per_block, 0)),
                  _const_spec((1, D_MODEL))],
        out_specs=pl.BlockSpec((tm, D_MODEL), lambda i: (i, 0)),
        out_shape=jax.ShapeDtypeStruct((rows, D_MODEL), F32),
        compiler_params=_cparams("parallel"),
        name="final_norm",
    )(xs, routed, final_norm)


def _rope_tables(pos):
    half = RET_HEAD_DIM // 2
    inv_freq = ROPE_BASE ** (-jnp.arange(half, dtype=F32) / half)
    ang = pos.astype(F32)[:, None] * inv_freq[None, :]
    cos, sin = jnp.cos(ang), jnp.sin(ang)
    cos_t = jnp.concatenate([cos, cos, cos, cos], axis=1)
    sin_t = jnp.concatenate([-sin, sin, -sin, sin], axis=1)
    return cos_t, sin_t


def kernel(x_prompt, x_sample, state_s5_re, state_s5_im, state_ret, meta_tokens, norm1, w_in,
           s5_lam_re, s5_lam_im, s5_log_dt, s5_b_re, s5_b_im, s5_c_re, s5_c_im, s5_d, s5_w_glu,
           s5_norm, ret_norm, w_out, norm2, w_router, router_bias, w_gate_e, w_up_e, w_down_e,
           w_gate_sh, w_up_sh, w_down_sh, final_norm):
    assert norm1.shape[0] == 1, "single-layer model"
    bp, seq, _ = x_prompt.shape
    ns = x_sample.shape[0]
    l = 0

    n1 = norm1[l].reshape(1, D_MODEL)
    w_in_b = w_in[l].astype(BF16)
    s5p = _s5_params(s5_lam_re[l], s5_lam_im[l], s5_log_dt[l], s5_b_re[l], s5_b_im[l],
                     s5_c_re[l], s5_c_im[l], s5_d[l], s5_w_glu[l], s5_norm[l])
    rnorm = ret_norm[l].reshape(1, D_RET)
    w_out_b = w_out[l].astype(BF16)
    mp = {
        "woa": w_out_b[:D_S5], "wob": w_out_b[D_S5:], "n2": norm2[l].reshape(1, D_MODEL),
        "wrt": w_router[l].T,
        "rb": jnp.broadcast_to(router_bias[l][:, None], (N_EXPERTS, LANES)),
        "wgs": w_gate_sh[l].astype(BF16), "wus": w_up_sh[l].astype(BF16),
        "wds": w_down_sh[l].astype(BF16),
    }
    ep = {"wg": w_gate_e[l].astype(BF16), "wu": w_up_e[l].astype(BF16),
          "wd": w_down_e[l].astype(BF16)}
    fnorm = final_norm.reshape(1, D_MODEL)

    xp = x_prompt.reshape(bp * seq, D_MODEL)
    cos_p, sin_p = _rope_tables(N_META + jnp.arange(seq, dtype=jnp.int32))
    tm_a = 512
    up, qp, kp, vp, gp = _project(xp, n1, w_in_b, cos_p, sin_p, tm_a, seq // tm_a)

    meta_tm = jnp.repeat(meta_tokens, SUBLANES, axis=0)
    mchunk = LANES
    meta_chunk = jnp.concatenate([jnp.zeros((mchunk - N_META, D_MODEL), F32), meta_tokens], axis=0)
    x_small = jnp.concatenate([x_sample.reshape(ns, D_MODEL), meta_tm, meta_chunk], axis=0)
    meta_pos = jnp.arange(N_META, dtype=jnp.int32)
    pos_small = jnp.concatenate([jnp.full((ns,), PAST_LEN, jnp.int32),
                                 jnp.repeat(meta_pos, SUBLANES),
                                 jnp.zeros((mchunk - N_META,), jnp.int32), meta_pos])
    cos_s, sin_s = _rope_tables(pos_small)
    n_small = x_small.shape[0]
    n_tm = ns + N_META * SUBLANES
    us, qs, ks, vs, gs = _project(x_small, n1, w_in_b, cos_s, sin_s, n_small, 1)

    zero8 = jnp.zeros((SUBLANES, S5_LANES), F32)
    _, m_re, m_im = _s5(us[ns:n_tm], zero8, zero8, s5p, nb=SUBLANES, tt=N_META, nblk=1,
                        bt_major=False)
    tt = 64
    ys5_p, p_re, p_im = _s5(up.reshape(bp, seq, D_S5), m_re, m_im, s5p, nb=bp, tt=tt,
                            nblk=seq // tt, bt_major=True)
    ys5_p = ys5_p.reshape(bp * seq, D_S5)
    ys5_s, s_re, s_im = _s5(us[:ns], state_s5_re[l].reshape(ns, S5_LANES),
                            state_s5_im[l].reshape(ns, S5_LANES), s5p, nb=ns, tt=1, nblk=1,
                            bt_major=False)

    zero_pair = jnp.zeros((1, N_HEAD_PAIRS, LANES, LANES), F32)
    _, m_pair = _retention(qs[n_tm:], ks[n_tm:], vs[n_tm:], gs[n_tm:],
                           zero_pair, rnorm, nseq=1, chunk=mchunk, nchunk=1)
    chunk = 256
    yret_p, p_pair = _retention(qp, kp, vp, gp, jnp.broadcast_to(m_pair, (bp,) + m_pair.shape[1:]),
                                rnorm, nseq=bp, chunk=chunk, nchunk=seq // chunk)
    yret_s, ret_s = _retention_step(qs[:ns], ks[:ns], vs[:ns], gs[:ns], state_ret[l], ret_norm[l])

    n_prompt = bp * seq
    assert n_prompt + ns <= MOE_BLOCKS * MOE_BLOCK_TOKENS
    n_moe = MOE_BLOCKS * MOE_BLOCK_TOKENS
    xs_p, hp, meta, eid_p = _mix(xp, ys5_p, yret_p, mp, 512, n_moe)
    xs_s, hp_s, meta_s, eid_s = _mix(x_sample.reshape(ns, D_MODEL), ys5_s, yret_s, mp, ns, ns)
    hp = lax.dynamic_update_slice(hp, hp_s, (n_prompt * PACK_ROWS, 0))
    meta = lax.dynamic_update_slice(meta, meta_s, (n_prompt * META_ROWS, 0))
    tabs = _dispatch_tables(jnp.concatenate([eid_p, eid_s], axis=1), n_prompt + ns)
    routed = _moe(hp, meta, tabs, ep)
    y_p = _finalize(xs_p, routed, fnorm, 512, 0)
    y_s = _finalize(xs_s, routed, fnorm, ns, n_prompt)

    shape5 = (1, bp, N_S5_GROUPS, S5_STATE)
    return (y_p.reshape(bp, seq, D_MODEL),
            y_s.reshape(ns, 1, D_MODEL),
            p_re.reshape(shape5), p_im.reshape(shape5),
            _unpair_state(p_pair)[None],
            s_re.reshape(1, ns, N_S5_GROUPS, S5_STATE), s_im.reshape(1, ns, N_S5_GROUPS, S5_STATE),
            ret_s[None])
```

```python
import functools
import math

import jax
import jax.numpy as jnp
from jax import lax
from jax.experimental import pallas as pl
from jax.experimental.pallas import tpu as pltpu

F32 = jnp.float32
BF16 = jnp.bfloat16

D_MODEL = 1024
N_META = 16
PAST_LEN = 16384
D_S5 = 512
S5_GROUP = 16
N_S5_GROUPS = 32
S5_STATE = 64
S5_LANES = N_S5_GROUPS * S5_STATE
D_RET = 512
N_RET_HEADS = 8
RET_HEAD_DIM = 64
N_HEAD_PAIRS = 4
ROPE_BASE = 10000.0
D_IN = D_S5 + 4 * D_RET
N_EXPERTS = 64
TOP_K = 8
N_EXPERT_GROUPS = 8
GROUP_SIZE = 8
TOPK_GROUPS = 4
D_EXPERT = 256
ROUTED_SCALE = 2.5
EPS = 1e-6

LANES = 128
SUBLANES = 8
VMEM_LIMIT = 56 * 1024 * 1024

PACK_ROWS = D_MODEL // LANES
ACC_ROWS = D_MODEL // LANES
MOE_BLOCKS = 4
MOE_BLOCK_TOKENS = 4608
MOE_TILE = 256
GROUP_TILES = MOE_BLOCK_TOKENS // MOE_TILE + 2
ADD_UNROLL = 8
OT_PITCH = MOE_TILE + SUBLANES
XT_PITCH = MOE_TILE + SUBLANES
META_ROWS = 2
META_SHIFT = 2
assert PACK_ROWS == META_ROWS << META_SHIFT
TOKEN_BITS = 13
ZERO_ROWS = 64


def _cparams(*sem):
    return pltpu.CompilerParams(dimension_semantics=sem, vmem_limit_bytes=VMEM_LIMIT)


def _const_spec(shape):
    nd = len(shape)
    return pl.BlockSpec(shape, lambda *_: (0,) * nd)


def _rms(x, gain):
    return x * lax.rsqrt(jnp.mean(x * x, axis=-1, keepdims=True) + EPS) * gain


def _proj_kernel(x_ref, n1_ref, w_ref, cos_ref, sin_ref, u_ref, q_ref, k_ref, v_ref, g_ref):
    h = _rms(x_ref[...], n1_ref[...]).astype(BF16)
    proj = jnp.dot(h, w_ref[...], preferred_element_type=F32)
    cos = cos_ref[...]
    sin = sin_ref[...]
    lane = lax.broadcasted_iota(jnp.int32, cos.shape, 1)
    first_half = (lane % RET_HEAD_DIM) < (RET_HEAD_DIM // 2)

    def rotary(t):
        partner = jnp.where(first_half,
                            pltpu.roll(t, LANES - RET_HEAD_DIM // 2, 1),
                            pltpu.roll(t, RET_HEAD_DIM // 2, 1))
        return t * cos + partner * sin

    u_ref[...] = proj[:, :D_S5]
    for j in range(N_HEAD_PAIRS):
        lo = D_S5 + j * LANES
        q_ref[:, j * LANES:(j + 1) * LANES] = rotary(proj[:, lo:lo + LANES]).astype(BF16)
        lo += D_RET
        k_ref[:, j * LANES:(j + 1) * LANES] = (
            rotary(proj[:, lo:lo + LANES]) * (RET_HEAD_DIM ** -0.5)).astype(BF16)
    v_ref[...] = proj[:, D_S5 + 2 * D_RET:D_S5 + 3 * D_RET].astype(BF16)
    g_ref[...] = proj[:, D_S5 + 3 * D_RET:]


def _project(x, norm1, w_in_b, cos_t, sin_t, tm, table_blocks):
    rows = x.shape[0]
    row = lambda i: (i, 0)
    tab = lambda i: (i % table_blocks, 0)
    return pl.pallas_call(
        _proj_kernel,
        grid=(rows // tm,),
        in_specs=[pl.BlockSpec((tm, D_MODEL), row),
                  _const_spec((1, D_MODEL)),
                  _const_spec((D_MODEL, D_IN)),
                  pl.BlockSpec((tm, LANES), tab),
                  pl.BlockSpec((tm, LANES), tab)],
        out_specs=[pl.BlockSpec((tm, D_S5), row),
                   pl.BlockSpec((tm, D_RET), row),
                   pl.BlockSpec((tm, D_RET), row),
                   pl.BlockSpec((tm, D_RET), row),
                   pl.BlockSpec((tm, D_RET), row)],
        out_shape=[jax.ShapeDtypeStruct((rows, D_S5), F32),
                   jax.ShapeDtypeStruct((rows, D_RET), BF16),
                   jax.ShapeDtypeStruct((rows, D_RET), BF16),
                   jax.ShapeDtypeStruct((rows, D_RET), BF16),
                   jax.ShapeDtypeStruct((rows, D_RET), F32)],
        compiler_params=_cparams("parallel"),
        name="in_proj",
    )(x, norm1, w_in_b, cos_t, sin_t)


def _s5_kernel(u_ref, x0re_ref, x0im_ref, are_ref, aim_ref, b_ref, c_ref, d_ref, wglu_ref,
               nrm_ref, y_ref, sre_ref, sim_ref, st, sre, sim, utm, ytm, *, nb, tt, bt_major):
    i = pl.program_id(0)
    rows = nb * tt

    @pl.when(i == 0)
    def _():
        sre[...] = x0re_ref[...]
        sim[...] = x0im_ref[...]

    if bt_major:
        for t in range(tt):
            utm[t * nb:(t + 1) * nb, :] = u_ref[:, t, :]
    else:
        utm[...] = u_ref[...]

    ub = utm[...].astype(BF16)
    half = S5_LANES // 2
    kh = D_S5 // 2
    for part in range(2):
        for hf in range(2):
            st[:, part * S5_LANES + hf * half:part * S5_LANES + (hf + 1) * half] = jnp.dot(
                ub[:, hf * kh:(hf + 1) * kh], b_ref[part, hf], preferred_element_type=F32)

    lw = 512 if nb == SUBLANES else LANES
    for lg in range(S5_LANES // lw):
        re = slice(lg * lw, (lg + 1) * lw)
        im = slice(S5_LANES + lg * lw, S5_LANES + (lg + 1) * lw)
        a_re = are_ref[:, re]
        a_im = aim_ref[:, re]

        def step(t, carry):
            s_re, s_im = carry
            r0 = pl.multiple_of(t * nb, nb)
            n_re = a_re * s_re - a_im * s_im + st[pl.ds(r0, nb), re]
            n_im = a_re * s_im + a_im * s_re + st[pl.ds(r0, nb), im]
            st[pl.ds(r0, nb), re] = n_re
            st[pl.ds(r0, nb), im] = n_im
            return n_re, n_im

        f_re, f_im = lax.fori_loop(0, tt, step, (sre[:, re], sim[:, re]))
        sre[:, re] = f_re
        sim[:, re] = f_im

    sre_ref[...] = sre[...]
    sim_ref[...] = sim[...]

    ys = []
    for hf in range(2):
        xr = st[:, hf * half:(hf + 1) * half].astype(BF16)
        xi = st[:, S5_LANES + hf * half:S5_LANES + (hf + 1) * half].astype(BF16)
        ys.append(jnp.dot(xr, c_ref[0, hf], preferred_element_type=F32)
                  - jnp.dot(xi, c_ref[1, hf], preferred_element_type=F32))
    y = jnp.concatenate(ys, axis=1) + d_ref[...] * utm[...]
    y = jax.nn.gelu(y)
    y = y * jax.nn.sigmoid(jnp.dot(y.astype(BF16), wglu_ref[...], preferred_element_type=F32))
    y = _rms(y, nrm_ref[...]).astype(BF16)
    if bt_major:
        yf = y.astype(F32)
        for j in range(D_S5 // LANES):
            ytm[j] = yf[:, j * LANES:(j + 1) * LANES]
        for b in range(nb):
            for j in range(D_S5 // LANES):
                y_ref[b, :, j * LANES:(j + 1) * LANES] = (
                    ytm[j, pl.ds(b, tt, stride=nb), :].astype(BF16))
    else:
        y_ref[...] = y


def _s5(u, x0re, x0im, s5p, *, nb, tt, nblk, bt_major):
    rows = nb * tt
    if bt_major:
        u_spec = pl.BlockSpec((nb, tt, D_S5), lambda i: (0, i, 0))
        y_shape = jax.ShapeDtypeStruct((nb, tt * nblk, D_S5), BF16)
    else:
        u_spec = pl.BlockSpec((rows, D_S5), lambda i: (i, 0))
        y_shape = jax.ShapeDtypeStruct((rows * nblk, D_S5), BF16)
    state_spec = _const_spec((nb, S5_LANES))
    are = jnp.broadcast_to(s5p["are"], (nb, S5_LANES))
    aim = jnp.broadcast_to(s5p["aim"], (nb, S5_LANES))
    return pl.pallas_call(
        functools.partial(_s5_kernel, nb=nb, tt=tt, bt_major=bt_major),
        grid=(nblk,),
        in_specs=[u_spec, state_spec, state_spec, state_spec, state_spec,
                  _const_spec((2, 2, D_S5 // 2, S5_LANES // 2)),
                  _const_spec((2, 2, S5_LANES // 2, D_S5 // 2)),
                  _const_spec((1, D_S5)),
                  _const_spec((D_S5, D_S5)),
                  _const_spec((1, D_S5))],
        out_specs=[u_spec, state_spec, state_spec],
        out_shape=[y_shape,
                   jax.ShapeDtypeStruct((nb, S5_LANES), F32),
                   jax.ShapeDtypeStruct((nb, S5_LANES), F32)],
        scratch_shapes=[pltpu.VMEM((rows, 2 * S5_LANES), F32),
                        pltpu.VMEM((nb, S5_LANES), F32),
                        pltpu.VMEM((nb, S5_LANES), F32),
                        pltpu.VMEM((rows, D_S5), F32),
                        pltpu.VMEM((D_S5 // LANES, rows, LANES), F32)],
        compiler_params=_cparams("arbitrary"),
        name="s5_" + ("bt" if bt_major else "tm") + str(nb),
    )(u, x0re, x0im, are, aim, s5p["b"], s5p["c"], s5p["d"], s5p["wglu"], s5p["nrm"])


def _s5_params(lam_re, lam_im, log_dt, b_re, b_im, c_re, c_im, d_skip, w_glu, nrm):
    dt = jnp.exp(log_dt)[:, None]
    mag = jnp.exp(lam_re * dt)
    abar_re, abar_im = mag * jnp.cos(lam_im * dt), mag * jnp.sin(lam_im * dt)
    num_re, num_im = abar_re - 1.0, abar_im
    den = lam_re * lam_re + lam_im * lam_im
    f_re = (num_re * lam_re + num_im * lam_im) / den
    f_im = (num_im * lam_re - num_re * lam_im) / den
    bbar_re = f_re[..., None] * b_re - f_im[..., None] * b_im
    bbar_im = f_re[..., None] * b_im + f_im[..., None] * b_re
    hg = N_S5_GROUPS // 2
    eye = jnp.eye(hg, dtype=F32)

    def bdiag(bb):
        bb = bb.reshape(2, hg, S5_STATE, S5_GROUP)
        return jnp.einsum("zgph,gk->zghkp", bb, eye).reshape(2, hg * S5_GROUP, hg * S5_STATE)

    def cdiag(cc):
        cc = cc.reshape(2, hg, S5_GROUP, S5_STATE)
        return jnp.einsum("zgnp,gk->zgpkn", cc, eye).reshape(2, hg * S5_STATE, hg * S5_GROUP)

    return {
        "are": abar_re.reshape(1, S5_LANES), "aim": abar_im.reshape(1, S5_LANES),
        "b": jnp.stack([bdiag(bbar_re), bdiag(bbar_im)]).astype(BF16),
        "c": jnp.stack([cdiag(c_re), cdiag(c_im)]).astype(BF16),
        "d": d_skip.reshape(1, D_S5), "wglu": w_glu.astype(BF16), "nrm": nrm.reshape(1, D_S5),
    }


def _head_norm_gate(o, g, gain, lo):
    inv = 1.0 / RET_HEAD_DIM

    def seg_mean(t):
        s_lo = jnp.sum(jnp.where(lo, t, 0.0), axis=1, keepdims=True)
        s_hi = jnp.sum(jnp.where(lo, 0.0, t), axis=1, keepdims=True)
        return jnp.where(lo, s_lo, s_hi) * inv

    dlt = o - seg_mean(o)
    var = seg_mean(dlt * dlt)
    return jax.nn.silu(g) * (dlt * lax.rsqrt(var + EPS) * gain)


def _ret_kernel(q_ref, k_ref, v_ref, g_ref, s0_ref, dm_ref, qd_ref, kd_ref, gc_ref, bm_ref,
                nrm_ref, y_ref, so_ref, s_acc):
    c = pl.program_id(1)

    @pl.when(c == 0)
    def _():
        s_acc[...] = s0_ref[0]

    rows = q_ref.shape[0]
    lane = lax.broadcasted_iota(jnp.int32, (rows, LANES), 1)
    lo = lane < RET_HEAD_DIM
    for j in range(N_HEAD_PAIRS):
        sl = slice(j * LANES, (j + 1) * LANES)
        q2 = q_ref[:, sl].astype(F32)
        k2 = k_ref[:, sl]
        v2 = v_ref[:, sl]
        s_pair = s_acc[j]
        cross = jnp.dot((q2 * qd_ref[:, sl]).astype(BF16), s_pair.astype(BF16),
                        preferred_element_type=F32)
        k_dec = (k2.astype(F32) * kd_ref[:, sl]).astype(BF16)
        upd = lax.dot_general(k_dec, v2, (((0,), (0,)), ((), ())), preferred_element_type=F32)
        s_acc[j] = gc_ref[j] * s_pair + bm_ref[...] * upd
        inner = []
        for hh, qh in enumerate((jnp.where(lo, q2, 0.0), jnp.where(lo, 0.0, q2))):
            sc = lax.dot_general(qh.astype(BF16), k2, (((1,), (1,)), ((), ())),
                                 preferred_element_type=F32) * dm_ref[2 * j + hh]
            inner.append(jnp.dot(sc.astype(BF16), v2, preferred_element_type=F32))
        o = jnp.where(lo, inner[0], inner[1]) + cross
        y_ref[:, sl] = _head_norm_gate(o, g_ref[:, sl], nrm_ref[:, sl], lo).astype(BF16)
    so_ref[0] = s_acc[...]


def _ret_tables(chunk):
    log_g = jnp.log1p(-jnp.exp2(-5.0 - jnp.arange(N_RET_HEADS, dtype=F32)))
    n = jnp.arange(chunk, dtype=F32)
    diff = n[:, None] - n[None, :]
    dm = jnp.where(diff[None] >= 0.0,
                   jnp.exp(log_g[:, None, None] * jnp.maximum(diff, 0.0)[None]), 0.0)
    per_lane = lambda t: jnp.repeat(t.T, RET_HEAD_DIM, axis=1)
    qd = per_lane(jnp.exp(log_g[:, None] * (n + 1.0)[None]))
    kd = per_lane(jnp.exp(log_g[:, None] * (chunk - 1.0 - n)[None]))
    gch = jnp.exp(log_g * chunk)
    blk = jnp.kron(jnp.eye(2, dtype=F32), jnp.ones((RET_HEAD_DIM, RET_HEAD_DIM), F32))
    gc = jnp.repeat(gch.reshape(N_HEAD_PAIRS, 2), RET_HEAD_DIM, axis=1)[:, :, None] * blk[None]
    return dm, qd, kd, gc, blk


def _pair_state(s):
    b = s.shape[0]
    s = s.reshape(b, N_HEAD_PAIRS, 2, RET_HEAD_DIM, RET_HEAD_DIM)
    z = jnp.zeros_like(s[:, :, 0])
    top = jnp.concatenate([s[:, :, 0], z], axis=-1)
    bot = jnp.concatenate([z, s[:, :, 1]], axis=-1)
    return jnp.concatenate([top, bot], axis=-2)


def _unpair_state(s2):
    d = RET_HEAD_DIM
    return jnp.stack([s2[:, :, :d, :d], s2[:, :, d:, d:]], axis=2).reshape(
        s2.shape[0], N_RET_HEADS, d, d)


def _retention(q, k, v, g, s0_pair, ret_norm, *, nseq, chunk, nchunk):
    dm, qd, kd, gc, blk = _ret_tables(chunk)
    row = lambda b, c: (b * nchunk + c, 0)
    st_spec = pl.BlockSpec((1, N_HEAD_PAIRS, LANES, LANES), lambda b, c: (b, 0, 0, 0))
    blk_spec = pl.BlockSpec((chunk, D_RET), row)
    return pl.pallas_call(
        _ret_kernel,
        grid=(nseq, nchunk),
        in_specs=[blk_spec, blk_spec, blk_spec, blk_spec, st_spec,
                  _const_spec((N_RET_HEADS, chunk, chunk)),
                  _const_spec((chunk, D_RET)), _const_spec((chunk, D_RET)),
                  _const_spec((N_HEAD_PAIRS, LANES, LANES)), _const_spec((LANES, LANES)),
                  _const_spec((1, D_RET))],
        out_specs=[blk_spec, st_spec],
        out_shape=[jax.ShapeDtypeStruct((nseq * nchunk * chunk, D_RET), BF16),
                   jax.ShapeDtypeStruct((nseq, N_HEAD_PAIRS, LANES, LANES), F32)],
        scratch_shapes=[pltpu.VMEM((N_HEAD_PAIRS, LANES, LANES), F32)],
        compiler_params=_cparams("arbitrary", "arbitrary"),
        name="retention_c%d" % chunk,
    )(q, k, v, g, s0_pair, dm, qd, kd, gc, blk, ret_norm)


def _ret_step_kernel(q_ref, k_ref, vh_ref, gh_ref, s_ref, gamr_ref, gamh_ref, nrm_ref,
                     y_ref, so_ref, *, bb):
    head = lax.broadcasted_iota(jnp.int32, (N_RET_HEADS, D_RET), 0)
    lane = lax.broadcasted_iota(jnp.int32, (N_RET_HEADS, D_RET), 1)
    own = (lane // RET_HEAD_DIM) == head
    gam_rows = gamr_ref[...]
    gam_h = gamh_ref[...]
    qf = q_ref[...].astype(F32)
    kf = k_ref[...].astype(F32)
    hi = lax.Precision.HIGHEST
    for b in range(bb):
        qm = jnp.where(own, qf[b:b + 1, :], 0.0)
        km = jnp.where(own, kf[b:b + 1, :], 0.0)
        vh = vh_ref[b]
        s_b = s_ref[b]
        score = jnp.sum(qm * km, axis=1, keepdims=True)
        inner = score.astype(BF16).astype(F32) * vh
        q_dec = (qm * gam_h).astype(BF16).astype(F32)
        cross = jnp.dot(q_dec, s_b, precision=hi, preferred_element_type=F32)
        o = inner + cross
        dlt = o - jnp.mean(o, axis=1, keepdims=True)
        var = jnp.mean(dlt * dlt, axis=1, keepdims=True)
        y_ref[b] = jax.nn.silu(gh_ref[b]) * (dlt * lax.rsqrt(var + EPS) * nrm_ref[...])
        upd = lax.dot_general(km, vh, (((0,), (0,)), ((), ())), precision=hi,
                              preferred_element_type=F32)
        so_ref[b] = gam_rows * s_b + upd


def _retention_step(q, k, v, g, state, ret_norm, bb=16):
    n = q.shape[0]
    gam = jnp.exp(jnp.log1p(-jnp.exp2(-5.0 - jnp.arange(N_RET_HEADS, dtype=F32))))
    gam_rows = jnp.broadcast_to(jnp.repeat(gam, RET_HEAD_DIM)[:, None], (D_RET, RET_HEAD_DIM))
    gam_heads = jnp.broadcast_to(gam[:, None], (N_RET_HEADS, D_RET))
    vh = v.astype(F32).reshape(n, N_RET_HEADS, RET_HEAD_DIM)
    gh = g.reshape(n, N_RET_HEADS, RET_HEAD_DIM)
    s2 = state.reshape(n, D_RET, RET_HEAD_DIM)
    row = lambda i: (i, 0)
    row3 = lambda i: (i, 0, 0)
    y, s_new = pl.pallas_call(
        functools.partial(_ret_step_kernel, bb=bb),
        grid=(n // bb,),
        in_specs=[pl.BlockSpec((bb, D_RET), row), pl.BlockSpec((bb, D_RET), row),
                  pl.BlockSpec((bb, N_RET_HEADS, RET_HEAD_DIM), row3),
                  pl.BlockSpec((bb, N_RET_HEADS, RET_HEAD_DIM), row3),
                  pl.BlockSpec((bb, D_RET, RET_HEAD_DIM), row3),
                  _const_spec((D_RET, RET_HEAD_DIM)),
                  _const_spec((N_RET_HEADS, D_RET)),
                  _const_spec((N_RET_HEADS, RET_HEAD_DIM))],
        out_specs=[pl.BlockSpec((bb, N_RET_HEADS, RET_HEAD_DIM), row3),
                   pl.BlockSpec((bb, D_RET, RET_HEAD_DIM), row3)],
        out_shape=[jax.ShapeDtypeStruct((n, N_RET_HEADS, RET_HEAD_DIM), F32),
                   jax.ShapeDtypeStruct((n, D_RET, RET_HEAD_DIM), F32)],
        compiler_params=_cparams("parallel"),
        name="retention_step",
    )(q, k, vh, gh, s2, gam_rows, gam_heads, ret_norm.reshape(N_RET_HEADS, RET_HEAD_DIM))
    return (y.reshape(n, D_RET).astype(BF16),
            s_new.reshape(n, N_RET_HEADS, RET_HEAD_DIM, RET_HEAD_DIM))


def _first_max(vals, idxs, sentinel):
    m = jnp.max(vals, axis=0, keepdims=True)
    first = jnp.min(jnp.where(vals == m, idxs, sentinel), axis=0, keepdims=True)
    return m, idxs == first


def _route(scores, sel):
    t = sel.shape[1]
    neg = -jnp.inf
    member = lax.broadcasted_iota(jnp.int32, (GROUP_SIZE, t), 0).astype(F32)
    groups = [sel[g * GROUP_SIZE:(g + 1) * GROUP_SIZE, :] for g in range(N_EXPERT_GROUPS)]
    gscore = []
    for grp in groups:
        m1, pick = _first_max(grp, member, float(GROUP_SIZE))
        m2 = jnp.max(jnp.where(pick, neg, grp), axis=0, keepdims=True)
        gscore.append(m1 + m2)
    gs = jnp.concatenate(gscore, axis=0)
    gkeep = jnp.zeros(gs.shape, F32)
    for _ in range(TOPK_GROUPS):
        _, pick = _first_max(gs, member, float(N_EXPERT_GROUPS))
        gkeep = jnp.where(pick, 1.0, gkeep)
        gs = jnp.where(pick, neg, gs)
    cand = jnp.concatenate(
        [jnp.where(gkeep[g:g + 1, :] > 0.5, groups[g], neg) for g in range(N_EXPERT_GROUPS)],
        axis=0)
    expert = lax.broadcasted_iota(jnp.int32, (N_EXPERTS, t), 0).astype(F32)
    ids, ws = [], []
    for _ in range(TOP_K):
        _, pick = _first_max(cand, expert, float(N_EXPERTS))
        ids.append(jnp.sum(jnp.where(pick, expert, 0.0), axis=0, keepdims=True))
        ws.append(jnp.sum(jnp.where(pick, scores, 0.0), axis=0, keepdims=True))
        cand = jnp.where(pick, neg, cand)
    w = jnp.concatenate(ws, axis=0)
    gates = w / jnp.sum(w, axis=0, keepdims=True) * ROUTED_SCALE
    return jnp.concatenate(ids, axis=0), gates


def _mix_kernel(x_ref, ys_ref, yr_ref, woa_ref, wob_ref, n2_ref, wrt_ref, rb_ref,
                wgs_ref, wus_ref, wds_ref, xs_ref, hp_ref, meta_ref, eid_ref, *, n_blocks):
    @pl.when(pl.program_id(0) >= n_blocks)
    def _():
        hp_ref[...] = jnp.zeros_like(hp_ref)
        meta_ref[...] = jnp.zeros_like(meta_ref)

    @pl.when(pl.program_id(0) < n_blocks)
    def _():
        _mix_body(x_ref, ys_ref, yr_ref, woa_ref, wob_ref, n2_ref, wrt_ref, rb_ref,
                  wgs_ref, wus_ref, wds_ref, xs_ref, hp_ref, meta_ref, eid_ref)


def _mix_body(x_ref, ys_ref, yr_ref, woa_ref, wob_ref, n2_ref, wrt_ref, rb_ref,
              wgs_ref, wus_ref, wds_ref, xs_ref, hp_ref, meta_ref, eid_ref):
    tm = x_ref.shape[0]
    x2 = (x_ref[...]
          + jnp.dot(ys_ref[...], woa_ref[...], preferred_element_type=F32)
          + jnp.dot(yr_ref[...], wob_ref[...], preferred_element_type=F32))
    h2f = _rms(x2, n2_ref[...])
    h2 = h2f.astype(BF16)
    h2r = h2.astype(F32)
    for c in range(PACK_ROWS):
        hp_ref[pl.ds(c, tm, stride=PACK_ROWS), :] = h2r[:, c * LANES:(c + 1) * LANES]
    logits = lax.dot_general(wrt_ref[...], h2f, (((1,), (1,)), ((), ())),
                             precision=lax.Precision.HIGHEST, preferred_element_type=F32)
    scores = jax.nn.sigmoid(logits)
    ids, gates = _route(scores, scores + rb_ref[:, :1])
    eid_ref[...] = ids.astype(jnp.int32)
    rec = jnp.concatenate([gates, ids, jnp.zeros((LANES - 2 * TOP_K, tm), F32)], axis=0).T
    meta_ref[pl.ds(0, tm, stride=META_ROWS), :] = rec
    for c in range(1, META_ROWS):
        meta_ref[pl.ds(c, tm, stride=META_ROWS), :] = jnp.zeros((tm, LANES), F32)
    a = jnp.dot(h2, wgs_ref[...], preferred_element_type=F32)
    b = jnp.dot(h2, wus_ref[...], preferred_element_type=F32)
    xs_ref[...] = x2 + jnp.dot((jax.nn.silu(a) * b).astype(BF16), wds_ref[...],
                               preferred_element_type=F32)


def _mix(x, ys5, yret, mp, tm, out_tokens):
    rows = x.shape[0]
    n_blocks = rows // tm
    assert out_tokens % tm == 0
    packed = lambda i: (i, 0)
    row = lambda i: (jnp.minimum(i, n_blocks - 1), 0)
    col = lambda i: (0, jnp.minimum(i, n_blocks - 1))
    return pl.pallas_call(
        functools.partial(_mix_kernel, n_blocks=n_blocks),
        grid=(out_tokens // tm,),
        in_specs=[pl.BlockSpec((tm, D_MODEL), row),
                  pl.BlockSpec((tm, D_S5), row), pl.BlockSpec((tm, D_RET), row),
                  _const_spec((D_S5, D_MODEL)), _const_spec((D_RET, D_MODEL)),
                  _const_spec((1, D_MODEL)),
                  _const_spec((N_EXPERTS, D_MODEL)), _const_spec((N_EXPERTS, LANES)),
                  _const_spec((D_MODEL, D_EXPERT)), _const_spec((D_MODEL, D_EXPERT)),
                  _const_spec((D_EXPERT, D_MODEL))],
        out_specs=[pl.BlockSpec((tm, D_MODEL), row),
                   pl.BlockSpec((tm * PACK_ROWS, LANES), packed),
                   pl.BlockSpec((tm * META_ROWS, LANES), packed),
                   pl.BlockSpec((TOP_K, tm), col)],
        out_shape=[jax.ShapeDtypeStruct((rows, D_MODEL), F32),
                   jax.ShapeDtypeStruct((out_tokens * PACK_ROWS, LANES), F32),
                   jax.ShapeDtypeStruct((out_tokens * META_ROWS, LANES), F32),
                   jax.ShapeDtypeStruct((TOP_K, rows), jnp.int32)],
        compiler_params=_cparams("arbitrary"),
        name="out_proj_router",
    )(x, ys5, yret, mp["woa"], mp["wob"], mp["n2"], mp["wrt"], mp["rb"],
      mp["wgs"], mp["wus"], mp["wds"])


def _moe_kernel(t0_ref, nt_ref, src_ref, dst_ref, hp_ref, meta_ref, wg_ref, wu_ref, wd_ref,
                acc_ref, xt_ref, xm_ref, ot_ref, xb_ref):
    s = pl.program_id(0)
    tm = MOE_TILE
    half = D_MODEL // 2
    all_rows = range(tm)
    nt = nt_ref[s]
    expert = (s % N_EXPERTS).astype(F32)

    def fetch_rows(j, rows):
        for m in rows:
            r = pl.multiple_of(src_ref[j, 0, m], PACK_ROWS)
            xt_ref[pl.ds(m, PACK_ROWS, stride=XT_PITCH), :] = hp_ref[pl.ds(r, PACK_ROWS), :]
            xm_ref[pl.ds(m, META_ROWS, stride=XT_PITCH), :] = meta_ref[
                pl.ds(pl.multiple_of(r >> META_SHIFT, META_ROWS), META_ROWS), :]

    def add_rows(j, rows):
        for m0 in range(rows.start, rows.stop, ADD_UNROLL):
            new = []
            for m in range(m0, m0 + ADD_UNROLL):
                base = pl.multiple_of(dst_ref[j, 0, m], ACC_ROWS)
                row = ot_ref[pl.ds(m, ACC_ROWS, stride=OT_PITCH), :]
                new.append((base, acc_ref[0, pl.ds(base, ACC_ROWS), :] + row))
            for base, v in new:
                acc_ref[0, pl.ds(base, ACC_ROWS), :] = v

    @pl.when(s % N_EXPERTS == 0)
    def _():
        zero = jnp.zeros((ZERO_ROWS, LANES), F32)

        def clear(i, carry):
            acc_ref[0, pl.ds(pl.multiple_of(i * ZERO_ROWS, ZERO_ROWS), ZERO_ROWS), :] = zero
            return carry

        lax.fori_loop(0, acc_ref.shape[1] // ZERO_ROWS, clear, 0)
        ot_ref[...] = jnp.zeros_like(ot_ref)
        fetch_rows(1, all_rows)

    def stage(j, carry):
        n_slice = 4
        per = tm // n_slice
        rows = [range(i * per, (i + 1) * per) for i in range(n_slice)]
        for c in range(PACK_ROWS):
            xb_ref[:, c * LANES:(c + 1) * LANES] = (
                xt_ref[c * XT_PITCH:c * XT_PITCH + tm, :].astype(BF16))
        rec = xm_ref[0:tm, :]
        ids = pltpu.roll(rec, LANES - TOP_K, 1)
        lane = lax.broadcasted_iota(jnp.int32, rec.shape, 1)
        gate = jnp.sum(jnp.where((lane < TOP_K) & (ids == expert), rec, 0.0),
                       axis=1, keepdims=True)
        add_rows(j - 1, rows[0])
        a = jnp.dot(xb_ref[:, :half], wg_ref[0, :half], preferred_element_type=F32)
        add_rows(j - 1, rows[1])
        a = a + jnp.dot(xb_ref[:, half:], wg_ref[0, half:], preferred_element_type=F32)
        add_rows(j - 1, rows[2])
        b = jnp.dot(xb_ref[:, :half], wu_ref[0, :half], preferred_element_type=F32)
        add_rows(j - 1, rows[3])
        b = b + jnp.dot(xb_ref[:, half:], wu_ref[0, half:], preferred_element_type=F32)
        act = (jax.nn.silu(a) * b * gate).astype(BF16)
        n_col = D_MODEL // n_slice
        for piece in range(n_slice):
            fetch_rows(j + 1, rows[piece])
            out = jnp.dot(act, wd_ref[0, :, piece * n_col:(piece + 1) * n_col],
                          preferred_element_type=F32)
            for cc in range(n_col // LANES):
                c = piece * (n_col // LANES) + cc
                ot_ref[c * OT_PITCH:c * OT_PITCH + tm, :] = out[:, cc * LANES:(cc + 1) * LANES]
        return carry

    lax.fori_loop(1, nt + 1, stage, 0)

    @pl.when(s % N_EXPERTS == N_EXPERTS - 1)
    def _():
        add_rows(nt, all_rows)


def _dispatch_tables(eid_t, n_tok):
    tm, ts = MOE_TILE, MOE_BLOCK_TOKENS
    ng = MOE_BLOCKS * N_EXPERTS
    assert ts < (1 << TOKEN_BITS)
    tok = jnp.arange(n_tok, dtype=jnp.int32)
    grp = (tok // ts)[None, :] * N_EXPERTS + eid_t
    gids = jnp.arange(ng, dtype=jnp.int32)
    counts = jnp.sum((grp.reshape(1, -1) == gids[:, None]).astype(jnp.int32), axis=1)
    npad = (-counts) % tm
    fill = jnp.arange(tm, dtype=jnp.int32)[None, :] < npad[:, None]
    keys = jnp.concatenate([
        ((grp << TOKEN_BITS) | (tok % ts)[None, :]).reshape(-1),
        ((jnp.where(fill, gids[:, None], ng) << TOKEN_BITS) | ts).reshape(-1)])
    toks = lax.sort(keys, dimension=0, is_stable=False) & ((1 << TOKEN_BITS) - 1)
    n_tiles = (n_tok * TOP_K + ng * tm) // tm

    ntile = (counts + npad) // tm
    tile_start = jnp.cumsum(ntile) - ntile
    src = jnp.where(toks < ts, toks, 0) * PACK_ROWS
    dst = toks * ACC_ROWS
    head = jnp.zeros((tm,), jnp.int32)
    tail = jnp.zeros((GROUP_TILES * tm,), jnp.int32)
    rows = n_tiles + 1 + GROUP_TILES
    return {"t0": tile_start.astype(jnp.int32), "nt": ntile.astype(jnp.int32),
            "src": jnp.concatenate([head, src, tail]).reshape(rows, 1, tm),
            "dst": jnp.concatenate([head, dst, tail]).reshape(rows, 1, tm)}


def _moe(hp, meta, tabs, ep):
    tm, ts = MOE_TILE, MOE_BLOCK_TOKENS
    assert GROUP_TILES >= -(-ts // tm) + 2
    window = pl.BlockSpec((pl.Element(GROUP_TILES), pl.Element(1), pl.Element(tm)),
                          lambda s, t0, nt: (t0[s], 0, 0), memory_space=pltpu.SMEM)
    wspec = lambda shape: pl.BlockSpec((1,) + shape, lambda s, t0, nt: (s % N_EXPERTS, 0, 0))
    blk = lambda s, t0, nt: (s // N_EXPERTS, 0)
    acc_rows = (ts + SUBLANES) * ACC_ROWS
    assert acc_rows % ZERO_ROWS == 0
    stage_bufs = [pltpu.VMEM((PACK_ROWS * XT_PITCH, LANES), F32),
                  pltpu.VMEM((META_ROWS * XT_PITCH, LANES), F32),
                  pltpu.VMEM((ACC_ROWS * OT_PITCH, LANES), F32),
                  pltpu.VMEM((tm, D_MODEL), BF16)]
    grid_spec = pltpu.PrefetchScalarGridSpec(
        num_scalar_prefetch=2,
        grid=(MOE_BLOCKS * N_EXPERTS,),
        in_specs=[window, window,
                  pl.BlockSpec((ts * PACK_ROWS, LANES), blk, pipeline_mode=pl.Buffered(1)),
                  pl.BlockSpec((ts * META_ROWS, LANES), blk, pipeline_mode=pl.Buffered(1)),
                  wspec((D_MODEL, D_EXPERT)), wspec((D_MODEL, D_EXPERT)),
                  wspec((D_EXPERT, D_MODEL))],
        out_specs=pl.BlockSpec((1, acc_rows, LANES), lambda s, t0, nt: (s // N_EXPERTS, 0, 0),
                               pipeline_mode=pl.Buffered(1)),
        scratch_shapes=stage_bufs,
    )
    return pl.pallas_call(
        _moe_kernel,
        grid_spec=grid_spec,
        out_shape=jax.ShapeDtypeStruct((MOE_BLOCKS, acc_rows, LANES), F32),
        compiler_params=_cparams("arbitrary"),
        name="moe_experts",
    )(tabs["t0"], tabs["nt"], tabs["src"], tabs["dst"], hp, meta, ep["wg"], ep["wu"], ep["wd"])


def _final_kernel(xs_ref, r_ref, fn_ref, y_ref):
    tm = xs_ref.shape[0]
    routed = jnp.concatenate(
        [r_ref[0, pl.ds(c, tm, stride=ACC_ROWS), :] for c in range(ACC_ROWS)], axis=1)
    y_ref[...] = _rms(xs_ref[...] + routed, fn_ref[...])


def _finalize(xs, routed, final_norm, tm, tok0):
    rows = xs.shape[0]
    per_block = MOE_BLOCK_TOKENS // tm
    first = tok0 // tm
    assert tok0 % tm == 0 and MOE_BLOCK_TOKENS % tm == 0
    return pl.pallas_call(
        _final_kernel,
        grid=(rows // tm,),
        in_specs=[pl.BlockSpec((tm, D_MODEL), lambda i: (i, 0)),
                  pl.BlockSpec((1, tm * ACC_ROWS, LANES),
                               lambda i: ((first + i) // per_block, (first + i) % per_block, 0)),
                  _const_spec((1, D_MODEL))],
        out_specs=pl.BlockSpec((tm, D_MODEL), lambda i: (i, 0)),
        out_shape=jax.ShapeDtypeStruct((rows, D_MODEL), F32),
        compiler_params=_cparams("parallel"),
        name="final_norm",
    )(xs, routed, final_norm)


def _rope_tables(pos):
    half = RET_HEAD_DIM // 2
    inv_freq = ROPE_BASE ** (-jnp.arange(half, dtype=F32) / half)
    ang = pos.astype(F32)[:, None] * inv_freq[None, :]
    cos, sin = jnp.cos(ang), jnp.sin(ang)
    cos_t = jnp.concatenate([cos, cos, cos, cos], axis=1)
    sin_t = jnp.concatenate([-sin, sin, -sin, sin], axis=1)
    return cos_t, sin_t


def kernel(x_prompt, x_sample, state_s5_re, state_s5_im, state_ret, meta_tokens, norm1, w_in,
           s5_lam_re, s5_lam_im, s5_log_dt, s5_b_re, s5_b_im, s5_c_re, s5_c_im, s5_d, s5_w_glu,
           s5_norm, ret_norm, w_out, norm2, w_router, router_bias, w_gate_e, w_up_e, w_down_e,
           w_gate_sh, w_up_sh, w_down_sh, final_norm):
    assert norm1.shape[0] == 1, "single-layer model"
    bp, seq, _ = x_prompt.shape
    ns = x_sample.shape[0]
    l = 0

    n1 = norm1[l].reshape(1, D_MODEL)
    w_in_b = w_in[l].astype(BF16)
    s5p = _s5_params(s5_lam_re[l], s5_lam_im[l], s5_log_dt[l], s5_b_re[l], s5_b_im[l],
                     s5_c_re[l], s5_c_im[l], s5_d[l], s5_w_glu[l], s5_norm[l])
    rnorm = ret_norm[l].reshape(1, D_RET)
    w_out_b = w_out[l].astype(BF16)
    mp = {
        "woa": w_out_b[:D_S5], "wob": w_out_b[D_S5:], "n2": norm2[l].reshape(1, D_MODEL),
        "wrt": w_router[l].T,
        "rb": jnp.broadcast_to(router_bias[l][:, None], (N_EXPERTS, LANES)),
        "wgs": w_gate_sh[l].astype(BF16), "wus": w_up_sh[l].astype(BF16),
        "wds": w_down_sh[l].astype(BF16),
    }
    ep = {"wg": w_gate_e[l].astype(BF16), "wu": w_up_e[l].astype(BF16),
          "wd": w_down_e[l].astype(BF16)}
    fnorm = final_norm.reshape(1, D_MODEL)

    xp = x_prompt.reshape(bp * seq, D_MODEL)
    cos_p, sin_p = _rope_tables(N_META + jnp.arange(seq, dtype=jnp.int32))
    tm_a = 512
    up, qp, kp, vp, gp = _project(xp, n1, w_in_b, cos_p, sin_p, tm_a, seq // tm_a)

    meta_tm = jnp.repeat(meta_tokens, SUBLANES, axis=0)
    mchunk = LANES
    meta_chunk = jnp.concatenate([jnp.zeros((mchunk - N_META, D_MODEL), F32), meta_tokens], axis=0)
    x_small = jnp.concatenate([x_sample.reshape(ns, D_MODEL), meta_tm, meta_chunk], axis=0)
    meta_pos = jnp.arange(N_META, dtype=jnp.int32)
    pos_small = jnp.concatenate([jnp.full((ns,), PAST_LEN, jnp.int32),
                                 jnp.repeat(meta_pos, SUBLANES),
                                 jnp.zeros((mchunk - N_META,), jnp.int32), meta_pos])
    cos_s, sin_s = _rope_tables(pos_small)
    n_small = x_small.shape[0]
    n_tm = ns + N_META * SUBLANES
    us, qs, ks, vs, gs = _project(x_small, n1, w_in_b, cos_s, sin_s, n_small, 1)

    zero8 = jnp.zeros((SUBLANES, S5_LANES), F32)
    _, m_re, m_im = _s5(us[ns:n_tm], zero8, zero8, s5p, nb=SUBLANES, tt=N_META, nblk=1,
                        bt_major=False)
    tt = 64
    ys5_p, p_re, p_im = _s5(up.reshape(bp, seq, D_S5), m_re, m_im, s5p, nb=bp, tt=tt,
                            nblk=seq // tt, bt_major=True)
    ys5_p = ys5_p.reshape(bp * seq, D_S5)
    ys5_s, s_re, s_im = _s5(us[:ns], state_s5_re[l].reshape(ns, S5_LANES),
                            state_s5_im[l].reshape(ns, S5_LANES), s5p, nb=ns, tt=1, nblk=1,
                            bt_major=False)

    zero_pair = jnp.zeros((1, N_HEAD_PAIRS, LANES, LANES), F32)
    _, m_pair = _retention(qs[n_tm:], ks[n_tm:], vs[n_tm:], gs[n_tm:],
                           zero_pair, rnorm, nseq=1, chunk=mchunk, nchunk=1)
    chunk = 256
    yret_p, p_pair = _retention(qp, kp, vp, gp, jnp.broadcast_to(m_pair, (bp,) + m_pair.shape[1:]),
                                rnorm, nseq=bp, chunk=chunk, nchunk=seq // chunk)
    yret_s, ret_s = _retention_step(qs[:ns], ks[:ns], vs[:ns], gs[:ns], state_ret[l], ret_norm[l])

    n_prompt = bp * seq
    assert n_prompt + ns <= MOE_BLOCKS * MOE_BLOCK_TOKENS
    n_moe = MOE_BLOCKS * MOE_BLOCK_TOKENS
    xs_p, hp, meta, eid_p = _mix(xp, ys5_p, yret_p, mp, 512, n_moe)
    xs_s, hp_s, meta_s, eid_s = _mix(x_sample.reshape(ns, D_MODEL), ys5_s, yret_s, mp, ns, ns)
    hp = lax.dynamic_update_slice(hp, hp_s, (n_prompt * PACK_ROWS, 0))
    meta = lax.dynamic_update_slice(meta, meta_s, (n_prompt * META_ROWS, 0))
    tabs = _dispatch_tables(jnp.concatenate([eid_p, eid_s], axis=1), n_prompt + ns)
    routed = _moe(hp, meta, tabs, ep)
    y_p = _finalize(xs_p, routed, fnorm, 512, 0)
    y_s = _finalize(xs_s, routed, fnorm, ns, n_prompt)

    shape5 = (1, bp, N_S5_GROUPS, S5_STATE)
    return (y_p.reshape(bp, seq, D_MODEL),
            y_s.reshape(ns, 1, D_MODEL),
            p_re.reshape(shape5), p_im.reshape(shape5),
            _unpair_state(p_pair)[None],
            s_re.reshape(1, ns, N_S5_GROUPS, S5_STATE), s_im.reshape(1, ns, N_S5_GROUPS, S5_STATE),
            ret_s[None])
```

```python
import functools
import math

import jax
import jax.numpy as jnp
from jax import lax
from jax.experimental import pallas as pl
from jax.experimental.pallas import tpu as pltpu

F32 = jnp.float32
BF16 = jnp.bfloat16

D_MODEL = 1024
N_META = 16
PAST_LEN = 16384
D_S5 = 512
S5_GROUP = 16
N_S5_GROUPS = 32
S5_STATE = 64
S5_LANES = N_S5_GROUPS * S5_STATE
D_RET = 512
N_RET_HEADS = 8
RET_HEAD_DIM = 64
N_HEAD_PAIRS = 4
ROPE_BASE = 10000.0
D_IN = D_S5 + 4 * D_RET
N_EXPERTS = 64
TOP_K = 8
N_EXPERT_GROUPS = 8
GROUP_SIZE = 8
TOPK_GROUPS = 4
D_EXPERT = 256
ROUTED_SCALE = 2.5
EPS = 1e-6

LANES = 128
SUBLANES = 8
VMEM_LIMIT = 56 * 1024 * 1024

PACK_ROWS = D_MODEL // LANES
ACC_ROWS = D_MODEL // LANES
MOE_BLOCKS = 4
MOE_BLOCK_TOKENS = 4608
MOE_TILE = 256
GROUP_TILES = MOE_BLOCK_TOKENS // MOE_TILE + 2
ADD_UNROLL = 8
OT_PITCH = MOE_TILE + SUBLANES
XT_PITCH = MOE_TILE + SUBLANES
META_ROWS = 2
META_SHIFT = 2
assert PACK_ROWS == META_ROWS << META_SHIFT
TOKEN_BITS = 13
ZERO_ROWS = 64
CAST_ROWS = 128


def _cparams(*sem):
    return pltpu.CompilerParams(dimension_semantics=sem, vmem_limit_bytes=VMEM_LIMIT)


def _const_spec(shape):
    nd = len(shape)
    return pl.BlockSpec(shape, lambda *_: (0,) * nd)


def _rms(x, gain):
    return x * lax.rsqrt(jnp.mean(x * x, axis=-1, keepdims=True) + EPS) * gain


def _proj_kernel(x_ref, n1_ref, w_ref, cos_ref, sin_ref, u_ref, q_ref, k_ref, v_ref, g_ref):
    h = _rms(x_ref[...], n1_ref[...]).astype(BF16)
    proj = jnp.dot(h, w_ref[...], preferred_element_type=F32)
    cos = cos_ref[...]
    sin = sin_ref[...]
    lane = lax.broadcasted_iota(jnp.int32, cos.shape, 1)
    first_half = (lane % RET_HEAD_DIM) < (RET_HEAD_DIM // 2)

    def rotary(t):
        partner = jnp.where(first_half,
                            pltpu.roll(t, LANES - RET_HEAD_DIM // 2, 1),
                            pltpu.roll(t, RET_HEAD_DIM // 2, 1))
        return t * cos + partner * sin

    u_ref[...] = proj[:, :D_S5]
    for j in range(N_HEAD_PAIRS):
        lo = D_S5 + j * LANES
        q_ref[:, j * LANES:(j + 1) * LANES] = rotary(proj[:, lo:lo + LANES]).astype(BF16)
        lo += D_RET
        k_ref[:, j * LANES:(j + 1) * LANES] = (
            rotary(proj[:, lo:lo + LANES]) * (RET_HEAD_DIM ** -0.5)).astype(BF16)
    v_ref[...] = proj[:, D_S5 + 2 * D_RET:D_S5 + 3 * D_RET].astype(BF16)
    g_ref[...] = proj[:, D_S5 + 3 * D_RET:]


def _project(x, norm1, w_in_b, cos_t, sin_t, tm, table_blocks):
    rows = x.shape[0]
    row = lambda i: (i, 0)
    tab = lambda i: (i % table_blocks, 0)
    return pl.pallas_call(
        _proj_kernel,
        grid=(rows // tm,),
        in_specs=[pl.BlockSpec((tm, D_MODEL), row),
                  _const_spec((1, D_MODEL)),
                  _const_spec((D_MODEL, D_IN)),
                  pl.BlockSpec((tm, LANES), tab),
                  pl.BlockSpec((tm, LANES), tab)],
        out_specs=[pl.BlockSpec((tm, D_S5), row),
                   pl.BlockSpec((tm, D_RET), row),
                   pl.BlockSpec((tm, D_RET), row),
                   pl.BlockSpec((tm, D_RET), row),
                   pl.BlockSpec((tm, D_RET), row)],
        out_shape=[jax.ShapeDtypeStruct((rows, D_S5), F32),
                   jax.ShapeDtypeStruct((rows, D_RET), BF16),
                   jax.ShapeDtypeStruct((rows, D_RET), BF16),
                   jax.ShapeDtypeStruct((rows, D_RET), BF16),
                   jax.ShapeDtypeStruct((rows, D_RET), F32)],
        compiler_params=_cparams("parallel"),
        name="in_proj",
    )(x, norm1, w_in_b, cos_t, sin_t)


def _s5_kernel(u_ref, x0re_ref, x0im_ref, are_ref, aim_ref, b_ref, c_ref, d_ref, wglu_ref,
               nrm_ref, y_ref, sre_ref, sim_ref, st, sre, sim, utm, ytm, *, nb, tt, bt_major):
    i = pl.program_id(0)
    rows = nb * tt

    @pl.when(i == 0)
    def _():
        sre[...] = x0re_ref[...]
        sim[...] = x0im_ref[...]

    if bt_major:
        for t in range(tt):
            utm[t * nb:(t + 1) * nb, :] = u_ref[:, t, :]
    else:
        utm[...] = u_ref[...]

    ub = utm[...].astype(BF16)
    half = S5_LANES // 2
    kh = D_S5 // 2
    for part in range(2):
        for hf in range(2):
            st[:, part * S5_LANES + hf * half:part * S5_LANES + (hf + 1) * half] = jnp.dot(
                ub[:, hf * kh:(hf + 1) * kh], b_ref[part, hf], preferred_element_type=F32)

    lw = 512 if nb == SUBLANES else LANES
    for lg in range(S5_LANES // lw):
        re = slice(lg * lw, (lg + 1) * lw)
        im = slice(S5_LANES + lg * lw, S5_LANES + (lg + 1) * lw)
        a_re = are_ref[:, re]
        a_im = aim_ref[:, re]

        def step(t, carry):
            s_re, s_im = carry
            r0 = pl.multiple_of(t * nb, nb)
            n_re = a_re * s_re - a_im * s_im + st[pl.ds(r0, nb), re]
            n_im = a_re * s_im + a_im * s_re + st[pl.ds(r0, nb), im]
            st[pl.ds(r0, nb), re] = n_re
            st[pl.ds(r0, nb), im] = n_im
            return n_re, n_im

        f_re, f_im = lax.fori_loop(0, tt, step, (sre[:, re], sim[:, re]))
        sre[:, re] = f_re
        sim[:, re] = f_im

    sre_ref[...] = sre[...]
    sim_ref[...] = sim[...]

    ys = []
    for hf in range(2):
        xr = st[:, hf * half:(hf + 1) * half].astype(BF16)
        xi = st[:, S5_LANES + hf * half:S5_LANES + (hf + 1) * half].astype(BF16)
        ys.append(jnp.dot(xr, c_ref[0, hf], preferred_element_type=F32)
                  - jnp.dot(xi, c_ref[1, hf], preferred_element_type=F32))
    y = jnp.concatenate(ys, axis=1) + d_ref[...] * utm[...]
    y = jax.nn.gelu(y)
    y = y * jax.nn.sigmoid(jnp.dot(y.astype(BF16), wglu_ref[...], preferred_element_type=F32))
    y = _rms(y, nrm_ref[...]).astype(BF16)
    if bt_major:
        yf = y.astype(F32)
        for j in range(D_S5 // LANES):
            ytm[j] = yf[:, j * LANES:(j + 1) * LANES]
        for b in range(nb):
            for j in range(D_S5 // LANES):
                y_ref[b, :, j * LANES:(j + 1) * LANES] = (
                    ytm[j, pl.ds(b, tt, stride=nb), :].astype(BF16))
    else:
        y_ref[...] = y


def _s5(u, x0re, x0im, s5p, *, nb, tt, nblk, bt_major):
    rows = nb * tt
    if bt_major:
        u_spec = pl.BlockSpec((nb, tt, D_S5), lambda i: (0, i, 0))
        y_shape = jax.ShapeDtypeStruct((nb, tt * nblk, D_S5), BF16)
    else:
        u_spec = pl.BlockSpec((rows, D_S5), lambda i: (i, 0))
        y_shape = jax.ShapeDtypeStruct((rows * nblk, D_S5), BF16)
    state_spec = _const_spec((nb, S5_LANES))
    are = jnp.broadcast_to(s5p["are"], (nb, S5_LANES))
    aim = jnp.broadcast_to(s5p["aim"], (nb, S5_LANES))
    return pl.pallas_call(
        functools.partial(_s5_kernel, nb=nb, tt=tt, bt_major=bt_major),
        grid=(nblk,),
        in_specs=[u_spec, state_spec, state_spec, state_spec, state_spec,
                  _const_spec((2, 2, D_S5 // 2, S5_LANES // 2)),
                  _const_spec((2, 2, S5_LANES // 2, D_S5 // 2)),
                  _const_spec((1, D_S5)),
                  _const_spec((D_S5, D_S5)),
                  _const_spec((1, D_S5))],
        out_specs=[u_spec, state_spec, state_spec],
        out_shape=[y_shape,
                   jax.ShapeDtypeStruct((nb, S5_LANES), F32),
                   jax.ShapeDtypeStruct((nb, S5_LANES), F32)],
        scratch_shapes=[pltpu.VMEM((rows, 2 * S5_LANES), F32),
                        pltpu.VMEM((nb, S5_LANES), F32),
                        pltpu.VMEM((nb, S5_LANES), F32),
                        pltpu.VMEM((rows, D_S5), F32),
                        pltpu.VMEM((D_S5 // LANES, rows, LANES), F32)],
        compiler_params=_cparams("arbitrary"),
        name="s5_" + ("bt" if bt_major else "tm") + str(nb),
    )(u, x0re, x0im, are, aim, s5p["b"], s5p["c"], s5p["d"], s5p["wglu"], s5p["nrm"])


def _s5_params(lam_re, lam_im, log_dt, b_re, b_im, c_re, c_im, d_skip, w_glu, nrm):
    dt = jnp.exp(log_dt)[:, None]
    mag = jnp.exp(lam_re * dt)
    abar_re, abar_im = mag * jnp.cos(lam_im * dt), mag * jnp.sin(lam_im * dt)
    num_re, num_im = abar_re - 1.0, abar_im
    den = lam_re * lam_re + lam_im * lam_im
    f_re = (num_re * lam_re + num_im * lam_im) / den
    f_im = (num_im * lam_re - num_re * lam_im) / den
    bbar_re = f_re[..., None] * b_re - f_im[..., None] * b_im
    bbar_im = f_re[..., None] * b_im + f_im[..., None] * b_re
    hg = N_S5_GROUPS // 2
    eye = jnp.eye(hg, dtype=F32)

    def bdiag(bb):
        bb = bb.reshape(2, hg, S5_STATE, S5_GROUP)
        return jnp.einsum("zgph,gk->zghkp", bb, eye).reshape(2, hg * S5_GROUP, hg * S5_STATE)

    def cdiag(cc):
        cc = cc.reshape(2, hg, S5_GROUP, S5_STATE)
        return jnp.einsum("zgnp,gk->zgpkn", cc, eye).reshape(2, hg * S5_STATE, hg * S5_GROUP)

    return {
        "are": abar_re.reshape(1, S5_LANES), "aim": abar_im.reshape(1, S5_LANES),
        "b": jnp.stack([bdiag(bbar_re), bdiag(bbar_im)]).astype(BF16),
        "c": jnp.stack([cdiag(c_re), cdiag(c_im)]).astype(BF16),
        "d": d_skip.reshape(1, D_S5), "wglu": w_glu.astype(BF16), "nrm": nrm.reshape(1, D_S5),
    }


def _head_norm_gate(o, g, gain, lo):
    inv = 1.0 / RET_HEAD_DIM

    def seg_mean(t):
        s_lo = jnp.sum(jnp.where(lo, t, 0.0), axis=1, keepdims=True)
        s_hi = jnp.sum(jnp.where(lo, 0.0, t), axis=1, keepdims=True)
        return jnp.where(lo, s_lo, s_hi) * inv

    dlt = o - seg_mean(o)
    var = seg_mean(dlt * dlt)
    return jax.nn.silu(g) * (dlt * lax.rsqrt(var + EPS) * gain)


def _ret_kernel(q_ref, k_ref, v_ref, g_ref, s0_ref, dm_ref, qd_ref, kd_ref, gc_ref, bm_ref,
                nrm_ref, y_ref, so_ref, s_acc):
    c = pl.program_id(1)

    @pl.when(c == 0)
    def _():
        s_acc[...] = s0_ref[0]

    rows = q_ref.shape[0]
    lane = lax.broadcasted_iota(jnp.int32, (rows, LANES), 1)
    lo = lane < RET_HEAD_DIM
    for j in range(N_HEAD_PAIRS):
        sl = slice(j * LANES, (j + 1) * LANES)
        q2 = q_ref[:, sl].astype(F32)
        k2 = k_ref[:, sl]
        v2 = v_ref[:, sl]
        s_pair = s_acc[j]
        cross = jnp.dot((q2 * qd_ref[:, sl]).astype(BF16), s_pair.astype(BF16),
                        preferred_element_type=F32)
        k_dec = (k2.astype(F32) * kd_ref[:, sl]).astype(BF16)
        upd = lax.dot_general(k_dec, v2, (((0,), (0,)), ((), ())), preferred_element_type=F32)
        s_acc[j] = gc_ref[j] * s_pair + bm_ref[...] * upd
        inner = []
        for hh, qh in enumerate((jnp.where(lo, q2, 0.0), jnp.where(lo, 0.0, q2))):
            sc = lax.dot_general(qh.astype(BF16), k2, (((1,), (1,)), ((), ())),
                                 preferred_element_type=F32) * dm_ref[2 * j + hh]
            inner.append(jnp.dot(sc.astype(BF16), v2, preferred_element_type=F32))
        o = jnp.where(lo, inner[0], inner[1]) + cross
        y_ref[:, sl] = _head_norm_gate(o, g_ref[:, sl], nrm_ref[:, sl], lo).astype(BF16)
    so_ref[0] = s_acc[...]


def _ret_tables(chunk):
    log_g = jnp.log1p(-jnp.exp2(-5.0 - jnp.arange(N_RET_HEADS, dtype=F32)))
    n = jnp.arange(chunk, dtype=F32)
    diff = n[:, None] - n[None, :]
    dm = jnp.where(diff[None] >= 0.0,
                   jnp.exp(log_g[:, None, None] * jnp.maximum(diff, 0.0)[None]), 0.0)
    per_lane = lambda t: jnp.repeat(t.T, RET_HEAD_DIM, axis=1)
    qd = per_lane(jnp.exp(log_g[:, None] * (n + 1.0)[None]))
    kd = per_lane(jnp.exp(log_g[:, None] * (chunk - 1.0 - n)[None]))
    gch = jnp.exp(log_g * chunk)
    blk = jnp.kron(jnp.eye(2, dtype=F32), jnp.ones((RET_HEAD_DIM, RET_HEAD_DIM), F32))
    gc = jnp.repeat(gch.reshape(N_HEAD_PAIRS, 2), RET_HEAD_DIM, axis=1)[:, :, None] * blk[None]
    return dm, qd, kd, gc, blk


def _pair_state(s):
    b = s.shape[0]
    s = s.reshape(b, N_HEAD_PAIRS, 2, RET_HEAD_DIM, RET_HEAD_DIM)
    z = jnp.zeros_like(s[:, :, 0])
    top = jnp.concatenate([s[:, :, 0], z], axis=-1)
    bot = jnp.concatenate([z, s[:, :, 1]], axis=-1)
    return jnp.concatenate([top, bot], axis=-2)


def _unpair_state(s2):
    d = RET_HEAD_DIM
    return jnp.stack([s2[:, :, :d, :d], s2[:, :, d:, d:]], axis=2).reshape(
        s2.shape[0], N_RET_HEADS, d, d)


def _retention(q, k, v, g, s0_pair, ret_norm, *, nseq, chunk, nchunk):
    dm, qd, kd, gc, blk = _ret_tables(chunk)
    row = lambda b, c: (b * nchunk + c, 0)
    st_spec = pl.BlockSpec((1, N_HEAD_PAIRS, LANES, LANES), lambda b, c: (b, 0, 0, 0))
    blk_spec = pl.BlockSpec((chunk, D_RET), row)
    return pl.pallas_call(
        _ret_kernel,
        grid=(nseq, nchunk),
        in_specs=[blk_spec, blk_spec, blk_spec, blk_spec, st_spec,
                  _const_spec((N_RET_HEADS, chunk, chunk)),
                  _const_spec((chunk, D_RET)), _const_spec((chunk, D_RET)),
                  _const_spec((N_HEAD_PAIRS, LANES, LANES)), _const_spec((LANES, LANES)),
                  _const_spec((1, D_RET))],
        out_specs=[blk_spec, st_spec],
        out_shape=[jax.ShapeDtypeStruct((nseq * nchunk * chunk, D_RET), BF16),
                   jax.ShapeDtypeStruct((nseq, N_HEAD_PAIRS, LANES, LANES), F32)],
        scratch_shapes=[pltpu.VMEM((N_HEAD_PAIRS, LANES, LANES), F32)],
        compiler_params=_cparams("arbitrary", "arbitrary"),
        name="retention_c%d" % chunk,
    )(q, k, v, g, s0_pair, dm, qd, kd, gc, blk, ret_norm)


def _ret_step_kernel(q_ref, k_ref, vh_ref, gh_ref, s_ref, gamr_ref, gamh_ref, nrm_ref,
                     y_ref, so_ref, *, bb):
    head = lax.broadcasted_iota(jnp.int32, (N_RET_HEADS, D_RET), 0)
    lane = lax.broadcasted_iota(jnp.int32, (N_RET_HEADS, D_RET), 1)
    own = (lane // RET_HEAD_DIM) == head
    gam_rows = gamr_ref[...]
    gam_h = gamh_ref[...]
    qf = q_ref[...].astype(F32)
    kf = k_ref[...].astype(F32)
    pad_k = jnp.zeros((N_RET_HEADS, D_RET), F32)
    pad_v = jnp.zeros((N_RET_HEADS, RET_HEAD_DIM), F32)
    for b in range(bb):
        qm = jnp.where(own, qf[b:b + 1, :], 0.0)
        km = jnp.where(own, kf[b:b + 1, :], 0.0)
        vh = vh_ref[b]
        s_b = s_ref[b]
        score = jnp.sum(qm * km, axis=1, keepdims=True)
        inner = score.astype(BF16).astype(F32) * vh
        cross = jnp.dot((qm * gam_h).astype(BF16), s_b.astype(BF16), preferred_element_type=F32)
        o = inner + cross
        dlt = o - jnp.mean(o, axis=1, keepdims=True)
        var = jnp.mean(dlt * dlt, axis=1, keepdims=True)
        y_ref[b] = jax.nn.silu(gh_ref[b]) * (dlt * lax.rsqrt(var + EPS) * nrm_ref[...])
        upd = lax.dot_general(jnp.concatenate([km, pad_k], axis=0).astype(BF16),
                              jnp.concatenate([vh, pad_v], axis=0).astype(BF16),
                              (((0,), (0,)), ((), ())), preferred_element_type=F32)
        so_ref[b] = gam_rows * s_b + upd


def _retention_step(q, k, v, g, state, ret_norm, bb=16):
    n = q.shape[0]
    gam = jnp.exp(jnp.log1p(-jnp.exp2(-5.0 - jnp.arange(N_RET_HEADS, dtype=F32))))
    gam_rows = jnp.broadcast_to(jnp.repeat(gam, RET_HEAD_DIM)[:, None], (D_RET, RET_HEAD_DIM))
    gam_heads = jnp.broadcast_to(gam[:, None], (N_RET_HEADS, D_RET))
    vh = v.astype(F32).reshape(n, N_RET_HEADS, RET_HEAD_DIM)
    gh = g.reshape(n, N_RET_HEADS, RET_HEAD_DIM)
    s2 = state.reshape(n, D_RET, RET_HEAD_DIM)
    row = lambda i: (i, 0)
    row3 = lambda i: (i, 0, 0)
    y, s_new = pl.pallas_call(
        functools.partial(_ret_step_kernel, bb=bb),
        grid=(n // bb,),
        in_specs=[pl.BlockSpec((bb, D_RET), row), pl.BlockSpec((bb, D_RET), row),
                  pl.BlockSpec((bb, N_RET_HEADS, RET_HEAD_DIM), row3),
                  pl.BlockSpec((bb, N_RET_HEADS, RET_HEAD_DIM), row3),
                  pl.BlockSpec((bb, D_RET, RET_HEAD_DIM), row3),
                  _const_spec((D_RET, RET_HEAD_DIM)),
                  _const_spec((N_RET_HEADS, D_RET)),
                  _const_spec((N_RET_HEADS, RET_HEAD_DIM))],
        out_specs=[pl.BlockSpec((bb, N_RET_HEADS, RET_HEAD_DIM), row3),
                   pl.BlockSpec((bb, D_RET, RET_HEAD_DIM), row3)],
        out_shape=[jax.ShapeDtypeStruct((n, N_RET_HEADS, RET_HEAD_DIM), F32),
                   jax.ShapeDtypeStruct((n, D_RET, RET_HEAD_DIM), F32)],
        compiler_params=_cparams("parallel"),
        name="retention_step",
    )(q, k, vh, gh, s2, gam_rows, gam_heads, ret_norm.reshape(N_RET_HEADS, RET_HEAD_DIM))
    return (y.reshape(n, D_RET).astype(BF16),
            s_new.reshape(n, N_RET_HEADS, RET_HEAD_DIM, RET_HEAD_DIM))


def _first_max(vals, idxs, sentinel):
    m = jnp.max(vals, axis=0, keepdims=True)
    first = jnp.min(jnp.where(vals == m, idxs, sentinel), axis=0, keepdims=True)
    return m, idxs == first


def _route(scores, sel):
    t = sel.shape[1]
    neg = -jnp.inf
    member = lax.broadcasted_iota(jnp.int32, (GROUP_SIZE, t), 0).astype(F32)
    groups = [sel[g * GROUP_SIZE:(g + 1) * GROUP_SIZE, :] for g in range(N_EXPERT_GROUPS)]
    gscore = []
    for grp in groups:
        m1, pick = _first_max(grp, member, float(GROUP_SIZE))
        m2 = jnp.max(jnp.where(pick, neg, grp), axis=0, keepdims=True)
        gscore.append(m1 + m2)
    gs = jnp.concatenate(gscore, axis=0)
    gkeep = jnp.zeros(gs.shape, F32)
    for _ in range(TOPK_GROUPS):
        _, pick = _first_max(gs, member, float(N_EXPERT_GROUPS))
        gkeep = jnp.where(pick, 1.0, gkeep)
        gs = jnp.where(pick, neg, gs)
    cand = jnp.concatenate(
        [jnp.where(gkeep[g:g + 1, :] > 0.5, groups[g], neg) for g in range(N_EXPERT_GROUPS)],
        axis=0)
    expert = lax.broadcasted_iota(jnp.int32, (N_EXPERTS, t), 0).astype(F32)
    ids, ws = [], []
    for _ in range(TOP_K):
        _, pick = _first_max(cand, expert, float(N_EXPERTS))
        ids.append(jnp.sum(jnp.where(pick, expert, 0.0), axis=0, keepdims=True))
        ws.append(jnp.sum(jnp.where(pick, scores, 0.0), axis=0, keepdims=True))
        cand = jnp.where(pick, neg, cand)
    w = jnp.concatenate(ws, axis=0)
    gates = w / jnp.sum(w, axis=0, keepdims=True) * ROUTED_SCALE
    return jnp.concatenate(ids, axis=0), gates


def _mix_kernel(x_ref, ys_ref, yr_ref, woa_ref, wob_ref, n2_ref, wrt_ref, rb_ref,
                wgs_ref, wus_ref, wds_ref, xs_ref, hp_ref, meta_ref, eid_ref, *, n_blocks):
    @pl.when(pl.program_id(0) >= n_blocks)
    def _():
        hp_ref[...] = jnp.zeros_like(hp_ref)
        meta_ref[...] = jnp.zeros_like(meta_ref)

    @pl.when(pl.program_id(0) < n_blocks)
    def _():
        _mix_body(x_ref, ys_ref, yr_ref, woa_ref, wob_ref, n2_ref, wrt_ref, rb_ref,
                  wgs_ref, wus_ref, wds_ref, xs_ref, hp_ref, meta_ref, eid_ref)


def _mix_body(x_ref, ys_ref, yr_ref, woa_ref, wob_ref, n2_ref, wrt_ref, rb_ref,
              wgs_ref, wus_ref, wds_ref, xs_ref, hp_ref, meta_ref, eid_ref):
    tm = x_ref.shape[0]
    x2 = (x_ref[...]
          + jnp.dot(ys_ref[...], woa_ref[...], preferred_element_type=F32)
          + jnp.dot(yr_ref[...], wob_ref[...], preferred_element_type=F32))
    h2f = _rms(x2, n2_ref[...])
    h2 = h2f.astype(BF16)
    h2r = h2.astype(F32)
    for c in range(PACK_ROWS):
        hp_ref[pl.ds(c, tm, stride=PACK_ROWS), :] = h2r[:, c * LANES:(c + 1) * LANES]
    logits = lax.dot_general(wrt_ref[...], h2f, (((1,), (1,)), ((), ())),
                             precision=lax.Precision.HIGHEST, preferred_element_type=F32)
    scores = jax.nn.sigmoid(logits)
    ids, gates = _route(scores, scores + rb_ref[:, :1])
    eid_ref[...] = ids.astype(jnp.int32)
    rec = jnp.concatenate([gates, ids, jnp.zeros((LANES - 2 * TOP_K, tm), F32)], axis=0).T
    meta_ref[pl.ds(0, tm, stride=META_ROWS), :] = rec
    for c in range(1, META_ROWS):
        meta_ref[pl.ds(c, tm, stride=META_ROWS), :] = jnp.zeros((tm, LANES), F32)
    a = jnp.dot(h2, wgs_ref[...], preferred_element_type=F32)
    b = jnp.dot(h2, wus_ref[...], preferred_element_type=F32)
    xs_ref[...] = x2 + jnp.dot((jax.nn.silu(a) * b).astype(BF16), wds_ref[...],
                               preferred_element_type=F32)


def _mix(x, ys5, yret, mp, tm, out_tokens):
    rows = x.shape[0]
    n_blocks = rows // tm
    assert out_tokens % tm == 0
    packed = lambda i: (i, 0)
    row = lambda i: (jnp.minimum(i, n_blocks - 1), 0)
    col = lambda i: (0, jnp.minimum(i, n_blocks - 1))
    return pl.pallas_call(
        functools.partial(_mix_kernel, n_blocks=n_blocks),
        grid=(out_tokens // tm,),
        in_specs=[pl.BlockSpec((tm, D_MODEL), row),
                  pl.BlockSpec((tm, D_S5), row), pl.BlockSpec((tm, D_RET), row),
                  _const_spec((D_S5, D_MODEL)), _const_spec((D_RET, D_MODEL)),
                  _const_spec((1, D_MODEL)),
                  _const_spec((N_EXPERTS, D_MODEL)), _const_spec((N_EXPERTS, LANES)),
                  _const_spec((D_MODEL, D_EXPERT)), _const_spec((D_MODEL, D_EXPERT)),
                  _const_spec((D_EXPERT, D_MODEL))],
        out_specs=[pl.BlockSpec((tm, D_MODEL), row),
                   pl.BlockSpec((tm * PACK_ROWS, LANES), packed),
                   pl.BlockSpec((tm * META_ROWS, LANES), packed),
                   pl.BlockSpec((TOP_K, tm), col)],
        out_shape=[jax.ShapeDtypeStruct((rows, D_MODEL), F32),
                   jax.ShapeDtypeStruct((out_tokens * PACK_ROWS, LANES), F32),
                   jax.ShapeDtypeStruct((out_tokens * META_ROWS, LANES), F32),
                   jax.ShapeDtypeStruct((TOP_K, rows), jnp.int32)],
        compiler_params=_cparams("arbitrary"),
        name="out_proj_router",
    )(x, ys5, yret, mp["woa"], mp["wob"], mp["n2"], mp["wrt"], mp["rb"],
      mp["wgs"], mp["wus"], mp["wds"])


def _moe_kernel(t0_ref, nt_ref, src_ref, dst_ref, hp_ref, meta_ref, wg32_ref, wu32_ref, wd32_ref,
                acc_ref, xt_ref, xm_ref, ot_ref, xb_ref, wg_ref, wu_ref, wd_ref):
    s = pl.program_id(0)
    tm = MOE_TILE
    half = D_MODEL // 2
    all_rows = range(tm)
    nt = nt_ref[s]
    expert = (s % N_EXPERTS).astype(F32)

    @pl.when(nt > 0)
    def _():
        for src32, dst16 in ((wg32_ref, wg_ref), (wu32_ref, wu_ref), (wd32_ref, wd_ref)):
            rows = src32.shape[1]
            for i in range(0, rows, CAST_ROWS):
                dst16[0, i:i + CAST_ROWS, :] = src32[0, i:i + CAST_ROWS, :].astype(BF16)

    def fetch_rows(j, rows):
        for m in rows:
            r = pl.multiple_of(src_ref[j, 0, m], PACK_ROWS)
            xt_ref[pl.ds(m, PACK_ROWS, stride=XT_PITCH), :] = hp_ref[pl.ds(r, PACK_ROWS), :]
            xm_ref[pl.ds(m, META_ROWS, stride=XT_PITCH), :] = meta_ref[
                pl.ds(pl.multiple_of(r >> META_SHIFT, META_ROWS), META_ROWS), :]

    def add_rows(j, rows):
        for m0 in range(rows.start, rows.stop, ADD_UNROLL):
            new = []
            for m in range(m0, m0 + ADD_UNROLL):
                base = pl.multiple_of(dst_ref[j, 0, m], ACC_ROWS)
                row = ot_ref[pl.ds(m, ACC_ROWS, stride=OT_PITCH), :]
                new.append((base, acc_ref[0, pl.ds(base, ACC_ROWS), :] + row))
            for base, v in new:
                acc_ref[0, pl.ds(base, ACC_ROWS), :] = v

    @pl.when(s % N_EXPERTS == 0)
    def _():
        zero = jnp.zeros((ZERO_ROWS, LANES), F32)

        def clear(i, carry):
            acc_ref[0, pl.ds(pl.multiple_of(i * ZERO_ROWS, ZERO_ROWS), ZERO_ROWS), :] = zero
            return carry

        lax.fori_loop(0, acc_ref.shape[1] // ZERO_ROWS, clear, 0)
        ot_ref[...] = jnp.zeros_like(ot_ref)
        fetch_rows(1, all_rows)

    def stage(j, carry):
        n_slice = 4
        per = tm // n_slice
        rows = [range(i * per, (i + 1) * per) for i in range(n_slice)]
        for c in range(PACK_ROWS):
            xb_ref[:, c * LANES:(c + 1) * LANES] = (
                xt_ref[c * XT_PITCH:c * XT_PITCH + tm, :].astype(BF16))
        rec = xm_ref[0:tm, :]
        ids = pltpu.roll(rec, LANES - TOP_K, 1)
        lane = lax.broadcasted_iota(jnp.int32, rec.shape, 1)
        gate = jnp.sum(jnp.where((lane < TOP_K) & (ids == expert), rec, 0.0),
                       axis=1, keepdims=True)
        add_rows(j - 1, rows[0])
        a = jnp.dot(xb_ref[:, :half], wg_ref[0, :half], preferred_element_type=F32)
        add_rows(j - 1, rows[1])
        a = a + jnp.dot(xb_ref[:, half:], wg_ref[0, half:], preferred_element_type=F32)
        add_rows(j - 1, rows[2])
        b = jnp.dot(xb_ref[:, :half], wu_ref[0, :half], preferred_element_type=F32)
        add_rows(j - 1, rows[3])
        b = b + jnp.dot(xb_ref[:, half:], wu_ref[0, half:], preferred_element_type=F32)
        act = (jax.nn.silu(a) * b * gate).astype(BF16)
        n_col = D_MODEL // n_slice
        for piece in range(n_slice):
            fetch_rows(j + 1, rows[piece])
            out = jnp.dot(act, wd_ref[0, :, piece * n_col:(piece + 1) * n_col],
                          preferred_element_type=F32)
            for cc in range(n_col // LANES):
                c = piece * (n_col // LANES) + cc
                ot_ref[c * OT_PITCH:c * OT_PITCH + tm, :] = out[:, cc * LANES:(cc + 1) * LANES]
        return carry

    lax.fori_loop(1, nt + 1, stage, 0)

    @pl.when(s % N_EXPERTS == N_EXPERTS - 1)
    def _():
        add_rows(nt, all_rows)


def _dispatch_tables(eid_t, n_tok):
    tm, ts = MOE_TILE, MOE_BLOCK_TOKENS
    ng = MOE_BLOCKS * N_EXPERTS
    assert ts < (1 << TOKEN_BITS)
    tok = jnp.arange(n_tok, dtype=jnp.int32)
    grp = (tok // ts)[None, :] * N_EXPERTS + eid_t
    gids = jnp.arange(ng, dtype=jnp.int32)
    counts = jnp.sum((grp.reshape(1, -1) == gids[:, None]).astype(jnp.int32), axis=1)
    npad = (-counts) % tm
    fill = jnp.arange(tm, dtype=jnp.int32)[None, :] < npad[:, None]
    keys = jnp.concatenate([
        ((grp << TOKEN_BITS) | (tok % ts)[None, :]).reshape(-1),
        ((jnp.where(fill, gids[:, None], ng) << TOKEN_BITS) | ts).reshape(-1)])
    toks = lax.sort(keys, dimension=0, is_stable=False) & ((1 << TOKEN_BITS) - 1)
    n_tiles = (n_tok * TOP_K + ng * tm) // tm

    ntile = (counts + npad) // tm
    tile_start = jnp.cumsum(ntile) - ntile
    src = jnp.where(toks < ts, toks, 0) * PACK_ROWS
    dst = toks * ACC_ROWS
    head = jnp.zeros((tm,), jnp.int32)
    tail = jnp.zeros((GROUP_TILES * tm,), jnp.int32)
    rows = n_tiles + 1 + GROUP_TILES
    return {"t0": tile_start.astype(jnp.int32), "nt": ntile.astype(jnp.int32),
            "src": jnp.concatenate([head, src, tail]).reshape(rows, 1, tm),
            "dst": jnp.concatenate([head, dst, tail]).reshape(rows, 1, tm)}


def _moe(hp, meta, tabs, ep):
    tm, ts = MOE_TILE, MOE_BLOCK_TOKENS
    assert GROUP_TILES >= -(-ts // tm) + 2
    window = pl.BlockSpec((pl.Element(GROUP_TILES), pl.Element(1), pl.Element(tm)),
                          lambda s, t0, nt: (t0[s], 0, 0), memory_space=pltpu.SMEM)
    wspec = lambda shape: pl.BlockSpec((1,) + shape, lambda s, t0, nt: (s % N_EXPERTS, 0, 0))
    blk = lambda s, t0, nt: (s // N_EXPERTS, 0)
    acc_rows = (ts + SUBLANES) * ACC_ROWS
    assert acc_rows % ZERO_ROWS == 0
    stage_bufs = [pltpu.VMEM((PACK_ROWS * XT_PITCH, LANES), F32),
                  pltpu.VMEM((META_ROWS * XT_PITCH, LANES), F32),
                  pltpu.VMEM((ACC_ROWS * OT_PITCH, LANES), F32),
                  pltpu.VMEM((tm, D_MODEL), BF16),
                  pltpu.VMEM((1, D_MODEL, D_EXPERT), BF16),
                  pltpu.VMEM((1, D_MODEL, D_EXPERT), BF16),
                  pltpu.VMEM((1, D_EXPERT, D_MODEL), BF16)]
    grid_spec = pltpu.PrefetchScalarGridSpec(
        num_scalar_prefetch=2,
        grid=(MOE_BLOCKS * N_EXPERTS,),
        in_specs=[window, window,
                  pl.BlockSpec((ts * PACK_ROWS, LANES), blk, pipeline_mode=pl.Buffered(1)),
                  pl.BlockSpec((ts * META_ROWS, LANES), blk, pipeline_mode=pl.Buffered(1)),
                  wspec((D_MODEL, D_EXPERT)), wspec((D_MODEL, D_EXPERT)),
                  wspec((D_EXPERT, D_MODEL))],
        out_specs=pl.BlockSpec((1, acc_rows, LANES), lambda s, t0, nt: (s // N_EXPERTS, 0, 0),
                               pipeline_mode=pl.Buffered(1)),
        scratch_shapes=stage_bufs,
    )
    return pl.pallas_call(
        _moe_kernel,
        grid_spec=grid_spec,
        out_shape=jax.ShapeDtypeStruct((MOE_BLOCKS, acc_rows, LANES), F32),
        compiler_params=_cparams("arbitrary"),
        name="moe_experts",
    )(tabs["t0"], tabs["nt"], tabs["src"], tabs["dst"], hp, meta, ep["wg"], ep["wu"], ep["wd"])


def _final_kernel(xs_ref, r_ref, fn_ref, y_ref):
    tm = xs_ref.shape[0]
    routed = jnp.concatenate(
        [r_ref[0, pl.ds(c, tm, stride=ACC_ROWS), :] for c in range(ACC_ROWS)], axis=1)
    y_ref[...] = _rms(xs_ref[...] + routed, fn_ref[...])


def _finalize(xs, routed, final_norm, tm, tok0):
    rows = xs.shape[0]
    per_block = MOE_BLOCK_TOKENS // tm
    first = tok0 // tm
    assert tok0 % tm == 0 and MOE_BLOCK_TOKENS % tm == 0
    return pl.pallas_call(
        _final_kernel,
        grid=(rows // tm,),
        in_specs=[pl.BlockSpec((tm, D_MODEL), lambda i: (i, 0)),
                  pl.BlockSpec((1, tm * ACC_ROWS, LANES),
                               lambda i: ((first + i) // per_block, (first + i) % per_block, 0)),
                  _const_spec((1, D_MODEL))],
        out_specs=pl.BlockSpec((tm, D_MODEL), lambda i: (i, 0)),
        out_shape=jax.ShapeDtypeStruct((rows, D_MODEL), F32),
        compiler_params=_cparams("parallel"),
        name="final_norm",
    )(xs, routed, final_norm)


def _rope_tables(pos):
    half = RET_HEAD_DIM // 2
    inv_freq = ROPE_BASE ** (-jnp.arange(half, dtype=F32) / half)
    ang = pos.astype(F32)[:, None] * inv_freq[None, :]
    cos, sin = jnp.cos(ang), jnp.sin(ang)
    cos_t = jnp.concatenate([cos, cos, cos, cos], axis=1)
    sin_t = jnp.concatenate([-sin, sin, -sin, sin], axis=1)
    return cos_t, sin_t


def kernel(x_prompt, x_sample, state_s5_re, state_s5_im, state_ret, meta_tokens, norm1, w_in,
           s5_lam_re, s5_lam_im, s5_log_dt, s5_b_re, s5_b_im, s5_c_re, s5_c_im, s5_d, s5_w_glu,
           s5_norm, ret_norm, w_out, norm2, w_router, router_bias, w_gate_e, w_up_e, w_down_e,
           w_gate_sh, w_up_sh, w_down_sh, final_norm):
    assert norm1.shape[0] == 1, "single-layer model"
    bp, seq, _ = x_prompt.shape
    ns = x_sample.shape[0]
    l = 0

    n1 = norm1[l].reshape(1, D_MODEL)
    w_in_b = w_in[l].astype(BF16)
    s5p = _s5_params(s5_lam_re[l], s5_lam_im[l], s5_log_dt[l], s5_b_re[l], s5_b_im[l],
                     s5_c_re[l], s5_c_im[l], s5_d[l], s5_w_glu[l], s5_norm[l])
    rnorm = ret_norm[l].reshape(1, D_RET)
    w_out_b = w_out[l].astype(BF16)
    mp = {
        "woa": w_out_b[:D_S5], "wob": w_out_b[D_S5:], "n2": norm2[l].reshape(1, D_MODEL),
        "wrt": w_router[l].T,
        "rb": jnp.broadcast_to(router_bias[l][:, None], (N_EXPERTS, LANES)),
        "wgs": w_gate_sh[l].astype(BF16), "wus": w_up_sh[l].astype(BF16),
        "wds": w_down_sh[l].astype(BF16),
    }
    ep = {"wg": w_gate_e[l], "wu": w_up_e[l],
          "wd": w_down_e[l]}
    fnorm = final_norm.reshape(1, D_MODEL)

    xp = x_prompt.reshape(bp * seq, D_MODEL)
    cos_p, sin_p = _rope_tables(N_META + jnp.arange(seq, dtype=jnp.int32))
    tm_a = 512
    up, qp, kp, vp, gp = _project(xp, n1, w_in_b, cos_p, sin_p, tm_a, seq // tm_a)

    meta_tm = jnp.repeat(meta_tokens, SUBLANES, axis=0)
    mchunk = LANES
    meta_chunk = jnp.concatenate([jnp.zeros((mchunk - N_META, D_MODEL), F32), meta_tokens], axis=0)
    x_small = jnp.concatenate([x_sample.reshape(ns, D_MODEL), meta_tm, meta_chunk], axis=0)
    meta_pos = jnp.arange(N_META, dtype=jnp.int32)
    pos_small = jnp.concatenate([jnp.full((ns,), PAST_LEN, jnp.int32),
                                 jnp.repeat(meta_pos, SUBLANES),
                                 jnp.zeros((mchunk - N_META,), jnp.int32), meta_pos])
    cos_s, sin_s = _rope_tables(pos_small)
    n_small = x_small.shape[0]
    n_tm = ns + N_META * SUBLANES
    us, qs, ks, vs, gs = _project(x_small, n1, w_in_b, cos_s, sin_s, n_small, 1)

    zero8 = jnp.zeros((SUBLANES, S5_LANES), F32)
    _, m_re, m_im = _s5(us[ns:n_tm], zero8, zero8, s5p, nb=SUBLANES, tt=N_META, nblk=1,
                        bt_major=False)
    tt = 64
    ys5_p, p_re, p_im = _s5(up.reshape(bp, seq, D_S5), m_re, m_im, s5p, nb=bp, tt=tt,
                            nblk=seq // tt, bt_major=True)
    ys5_p = ys5_p.reshape(bp * seq, D_S5)
    ys5_s, s_re, s_im = _s5(us[:ns], state_s5_re[l].reshape(ns, S5_LANES),
                            state_s5_im[l].reshape(ns, S5_LANES), s5p, nb=ns, tt=1, nblk=1,
                            bt_major=False)

    zero_pair = jnp.zeros((1, N_HEAD_PAIRS, LANES, LANES), F32)
    _, m_pair = _retention(qs[n_tm:], ks[n_tm:], vs[n_tm:], gs[n_tm:],
                           zero_pair, rnorm, nseq=1, chunk=mchunk, nchunk=1)
    chunk = 256
    yret_p, p_pair = _retention(qp, kp, vp, gp, jnp.broadcast_to(m_pair, (bp,) + m_pair.shape[1:]),
                                rnorm, nseq=bp, chunk=chunk, nchunk=seq // chunk)
    yret_s, ret_s = _retention_step(qs[:ns], ks[:ns], vs[:ns], gs[:ns], state_ret[l], ret_norm[l])

    n_prompt = bp * seq
    assert n_prompt + ns <= MOE_BLOCKS * MOE_BLOCK_TOKENS
    n_moe = MOE_BLOCKS * MOE_BLOCK_TOKENS
    xs_p, hp, meta, eid_p = _mix(xp, ys5_p, yret_p, mp, 512, n_moe)
    xs_s, hp_s, meta_s, eid_s = _mix(x_sample.reshape(ns, D_MODEL), ys5_s, yret_s, mp, ns, ns)
    hp = lax.dynamic_update_slice(hp, hp_s, (n_prompt * PACK_ROWS, 0))
    meta = lax.dynamic_update_slice(meta, meta_s, (n_prompt * META_ROWS, 0))
    tabs = _dispatch_tables(jnp.concatenate([eid_p, eid_s], axis=1), n_prompt + ns)
    routed = _moe(hp, meta, tabs, ep)
    y_p = _finalize(xs_p, routed, fnorm, 512, 0)
    y_s = _finalize(xs_s, routed, fnorm, ns, n_prompt)

    shape5 = (1, bp, N_S5_GROUPS, S5_STATE)
    return (y_p.reshape(bp, seq, D_MODEL),
            y_s.reshape(ns, 1, D_MODEL),
            p_re.reshape(shape5), p_im.reshape(shape5),
            _unpair_state(p_pair)[None],
            s_re.reshape(1, ns, N_S5_GROUPS, S5_STATE), s_im.reshape(1, ns, N_S5_GROUPS, S5_STATE),
            ret_s[None])
```

```python
import functools
import math

import jax
import jax.numpy as jnp
from jax import lax
from jax.experimental import pallas as pl
from jax.experimental.pallas import tpu as pltpu

F32 = jnp.float32
BF16 = jnp.bfloat16

D_MODEL = 1024
N_META = 16
PAST_LEN = 16384
D_S5 = 512
S5_GROUP = 16
N_S5_GROUPS = 32
S5_STATE = 64
S5_LANES = N_S5_GROUPS * S5_STATE
D_RET = 512
N_RET_HEADS = 8
RET_HEAD_DIM = 64
N_HEAD_PAIRS = 4
ROPE_BASE = 10000.0
D_IN = D_S5 + 4 * D_RET
N_EXPERTS = 64
TOP_K = 8
N_EXPERT_GROUPS = 8
GROUP_SIZE = 8
TOPK_GROUPS = 4
D_EXPERT = 256
ROUTED_SCALE = 2.5
EPS = 1e-6

LANES = 128
SUBLANES = 8
VMEM_LIMIT = 56 * 1024 * 1024

PACK_ROWS = D_MODEL // LANES
ACC_ROWS = D_MODEL // LANES
MOE_BLOCKS = 4
MOE_BLOCK_TOKENS = 4608
MOE_TILE = 256
GROUP_TILES = MOE_BLOCK_TOKENS // MOE_TILE + 2
ADD_UNROLL = 8
OT_PITCH = MOE_TILE + SUBLANES
XT_PITCH = MOE_TILE + SUBLANES
META_ROWS = 2
META_SHIFT = 2
assert PACK_ROWS == META_ROWS << META_SHIFT
TOKEN_BITS = 13
ZERO_ROWS = 64
CAST_ROWS = 128


def _cparams(*sem):
    return pltpu.CompilerParams(dimension_semantics=sem, vmem_limit_bytes=VMEM_LIMIT)


def _const_spec(shape):
    nd = len(shape)
    return pl.BlockSpec(shape, lambda *_: (0,) * nd)


def _rms(x, gain):
    return x * lax.rsqrt(jnp.mean(x * x, axis=-1, keepdims=True) + EPS) * gain


def _proj_kernel(x_ref, n1_ref, w_ref, cos_ref, sin_ref, u_ref, q_ref, k_ref, v_ref, g_ref):
    h = _rms(x_ref[...], n1_ref[...]).astype(BF16)
    proj = jnp.dot(h, w_ref[...], preferred_element_type=F32)
    cos = cos_ref[...]
    sin = sin_ref[...]
    lane = lax.broadcasted_iota(jnp.int32, cos.shape, 1)
    first_half = (lane % RET_HEAD_DIM) < (RET_HEAD_DIM // 2)

    def rotary(t):
        partner = jnp.where(first_half,
                            pltpu.roll(t, LANES - RET_HEAD_DIM // 2, 1),
                            pltpu.roll(t, RET_HEAD_DIM // 2, 1))
        return t * cos + partner * sin

    u_ref[...] = proj[:, :D_S5]
    for j in range(N_HEAD_PAIRS):
        lo = D_S5 + j * LANES
        q_ref[:, j * LANES:(j + 1) * LANES] = rotary(proj[:, lo:lo + LANES]).astype(BF16)
        lo += D_RET
        k_ref[:, j * LANES:(j + 1) * LANES] = (
            rotary(proj[:, lo:lo + LANES]) * (RET_HEAD_DIM ** -0.5)).astype(BF16)
    v_ref[...] = proj[:, D_S5 + 2 * D_RET:D_S5 + 3 * D_RET].astype(BF16)
    g_ref[...] = proj[:, D_S5 + 3 * D_RET:]


def _project(x, norm1, w_in_b, cos_t, sin_t, tm, table_blocks):
    rows = x.shape[0]
    row = lambda i: (i, 0)
    tab = lambda i: (i % table_blocks, 0)
    return pl.pallas_call(
        _proj_kernel,
        grid=(rows // tm,),
        in_specs=[pl.BlockSpec((tm, D_MODEL), row),
                  _const_spec((1, D_MODEL)),
                  _const_spec((D_MODEL, D_IN)),
                  pl.BlockSpec((tm, LANES), tab),
                  pl.BlockSpec((tm, LANES), tab)],
        out_specs=[pl.BlockSpec((tm, D_S5), row),
                   pl.BlockSpec((tm, D_RET), row),
                   pl.BlockSpec((tm, D_RET), row),
                   pl.BlockSpec((tm, D_RET), row),
                   pl.BlockSpec((tm, D_RET), row)],
        out_shape=[jax.ShapeDtypeStruct((rows, D_S5), F32),
                   jax.ShapeDtypeStruct((rows, D_RET), BF16),
                   jax.ShapeDtypeStruct((rows, D_RET), BF16),
                   jax.ShapeDtypeStruct((rows, D_RET), BF16),
                   jax.ShapeDtypeStruct((rows, D_RET), F32)],
        compiler_params=_cparams("parallel"),
        name="in_proj",
    )(x, norm1, w_in_b, cos_t, sin_t)


def _s5_kernel(u_ref, x0re_ref, x0im_ref, are_ref, aim_ref, b_ref, c_ref, d_ref, wglu_ref,
               nrm_ref, y_ref, sre_ref, sim_ref, st, sre, sim, utm, ytm, *, nb, tt, bt_major):
    i = pl.program_id(0)
    rows = nb * tt

    @pl.when(i == 0)
    def _():
        sre[...] = x0re_ref[...]
        sim[...] = x0im_ref[...]

    if bt_major:
        for t in range(tt):
            utm[t * nb:(t + 1) * nb, :] = u_ref[:, t, :]
    else:
        utm[...] = u_ref[...]

    ub = utm[...].astype(BF16)
    half = S5_LANES // 2
    kh = D_S5 // 2
    for part in range(2):
        for hf in range(2):
            st[:, part * S5_LANES + hf * half:part * S5_LANES + (hf + 1) * half] = jnp.dot(
                ub[:, hf * kh:(hf + 1) * kh], b_ref[part, hf], preferred_element_type=F32)

    lw = 512 if nb == SUBLANES else LANES
    for lg in range(S5_LANES // lw):
        re = slice(lg * lw, (lg + 1) * lw)
        im = slice(S5_LANES + lg * lw, S5_LANES + (lg + 1) * lw)
        a_re = are_ref[:, re]
        a_im = aim_ref[:, re]

        def step(t, carry):
            s_re, s_im = carry
            r0 = pl.multiple_of(t * nb, nb)
            n_re = a_re * s_re - a_im * s_im + st[pl.ds(r0, nb), re]
            n_im = a_re * s_im + a_im * s_re + st[pl.ds(r0, nb), im]
            st[pl.ds(r0, nb), re] = n_re
            st[pl.ds(r0, nb), im] = n_im
            return n_re, n_im

        f_re, f_im = lax.fori_loop(0, tt, step, (sre[:, re], sim[:, re]))
        sre[:, re] = f_re
        sim[:, re] = f_im

    sre_ref[...] = sre[...]
    sim_ref[...] = sim[...]

    ys = []
    for hf in range(2):
        xr = st[:, hf * half:(hf + 1) * half].astype(BF16)
        xi = st[:, S5_LANES + hf * half:S5_LANES + (hf + 1) * half].astype(BF16)
        ys.append(jnp.dot(xr, c_ref[0, hf], preferred_element_type=F32)
                  - jnp.dot(xi, c_ref[1, hf], preferred_element_type=F32))
    y = jnp.concatenate(ys, axis=1) + d_ref[...] * utm[...]
    y = jax.nn.gelu(y)
    y = y * jax.nn.sigmoid(jnp.dot(y.astype(BF16), wglu_ref[...], preferred_element_type=F32))
    y = _rms(y, nrm_ref[...]).astype(BF16)
    if bt_major:
        yf = y.astype(F32)
        for j in range(D_S5 // LANES):
            ytm[j] = yf[:, j * LANES:(j + 1) * LANES]
        for b in range(nb):
            for j in range(D_S5 // LANES):
                y_ref[b, :, j * LANES:(j + 1) * LANES] = (
                    ytm[j, pl.ds(b, tt, stride=nb), :].astype(BF16))
    else:
        y_ref[...] = y


def _s5(u, x0re, x0im, s5p, *, nb, tt, nblk, bt_major):
    rows = nb * tt
    if bt_major:
        u_spec = pl.BlockSpec((nb, tt, D_S5), lambda i: (0, i, 0))
        y_shape = jax.ShapeDtypeStruct((nb, tt * nblk, D_S5), BF16)
    else:
        u_spec = pl.BlockSpec((rows, D_S5), lambda i: (i, 0))
        y_shape = jax.ShapeDtypeStruct((rows * nblk, D_S5), BF16)
    state_spec = _const_spec((nb, S5_LANES))
    are = jnp.broadcast_to(s5p["are"], (nb, S5_LANES))
    aim = jnp.broadcast_to(s5p["aim"], (nb, S5_LANES))
    return pl.pallas_call(
        functools.partial(_s5_kernel, nb=nb, tt=tt, bt_major=bt_major),
        grid=(nblk,),
        in_specs=[u_spec, state_spec, state_spec, state_spec, state_spec,
                  _const_spec((2, 2, D_S5 // 2, S5_LANES // 2)),
                  _const_spec((2, 2, S5_LANES // 2, D_S5 // 2)),
                  _const_spec((1, D_S5)),
                  _const_spec((D_S5, D_S5)),
                  _const_spec((1, D_S5))],
        out_specs=[u_spec, state_spec, state_spec],
        out_shape=[y_shape,
                   jax.ShapeDtypeStruct((nb, S5_LANES), F32),
                   jax.ShapeDtypeStruct((nb, S5_LANES), F32)],
        scratch_shapes=[pltpu.VMEM((rows, 2 * S5_LANES), F32),
                        pltpu.VMEM((nb, S5_LANES), F32),
                        pltpu.VMEM((nb, S5_LANES), F32),
                        pltpu.VMEM((rows, D_S5), F32),
                        pltpu.VMEM((D_S5 // LANES, rows, LANES), F32)],
        compiler_params=_cparams("arbitrary"),
        name="s5_" + ("bt" if bt_major else "tm") + str(nb),
    )(u, x0re, x0im, are, aim, s5p["b"], s5p["c"], s5p["d"], s5p["wglu"], s5p["nrm"])


def _s5_params(lam_re, lam_im, log_dt, b_re, b_im, c_re, c_im, d_skip, w_glu, nrm):
    dt = jnp.exp(log_dt)[:, None]
    mag = jnp.exp(lam_re * dt)
    abar_re, abar_im = mag * jnp.cos(lam_im * dt), mag * jnp.sin(lam_im * dt)
    num_re, num_im = abar_re - 1.0, abar_im
    den = lam_re * lam_re + lam_im * lam_im
    f_re = (num_re * lam_re + num_im * lam_im) / den
    f_im = (num_im * lam_re - num_re * lam_im) / den
    bbar_re = f_re[..., None] * b_re - f_im[..., None] * b_im
    bbar_im = f_re[..., None] * b_im + f_im[..., None] * b_re
    hg = N_S5_GROUPS // 2
    eye = jnp.eye(hg, dtype=F32)

    def bdiag(bb):
        bb = bb.reshape(2, hg, S5_STATE, S5_GROUP)
        return jnp.einsum("zgph,gk->zghkp", bb, eye).reshape(2, hg * S5_GROUP, hg * S5_STATE)

    def cdiag(cc):
        cc = cc.reshape(2, hg, S5_GROUP, S5_STATE)
        return jnp.einsum("zgnp,gk->zgpkn", cc, eye).reshape(2, hg * S5_STATE, hg * S5_GROUP)

    return {
        "are": abar_re.reshape(1, S5_LANES), "aim": abar_im.reshape(1, S5_LANES),
        "b": jnp.stack([bdiag(bbar_re), bdiag(bbar_im)]).astype(BF16),
        "c": jnp.stack([cdiag(c_re), cdiag(c_im)]).astype(BF16),
        "d": d_skip.reshape(1, D_S5), "wglu": w_glu.astype(BF16), "nrm": nrm.reshape(1, D_S5),
    }


def _head_norm_gate(o, g, gain, lo):
    inv = 1.0 / RET_HEAD_DIM

    def seg_mean(t):
        s_lo = jnp.sum(jnp.where(lo, t, 0.0), axis=1, keepdims=True)
        s_hi = jnp.sum(jnp.where(lo, 0.0, t), axis=1, keepdims=True)
        return jnp.where(lo, s_lo, s_hi) * inv

    dlt = o - seg_mean(o)
    var = seg_mean(dlt * dlt)
    return jax.nn.silu(g) * (dlt * lax.rsqrt(var + EPS) * gain)


def _ret_kernel(q_ref, k_ref, v_ref, g_ref, s0_ref, dm_ref, qd_ref, kd_ref, gc_ref, bm_ref,
                nrm_ref, y_ref, so_ref, s_acc):
    c = pl.program_id(1)

    @pl.when(c == 0)
    def _():
        s_acc[...] = s0_ref[0]

    rows = q_ref.shape[0]
    lane = lax.broadcasted_iota(jnp.int32, (rows, LANES), 1)
    lo = lane < RET_HEAD_DIM
    for j in range(N_HEAD_PAIRS):
        sl = slice(j * LANES, (j + 1) * LANES)
        q2 = q_ref[:, sl].astype(F32)
        k2 = k_ref[:, sl]
        v2 = v_ref[:, sl]
        s_pair = s_acc[j]
        cross = jnp.dot((q2 * qd_ref[:, sl]).astype(BF16), s_pair.astype(BF16),
                        preferred_element_type=F32)
        k_dec = (k2.astype(F32) * kd_ref[:, sl]).astype(BF16)
        upd = lax.dot_general(k_dec, v2, (((0,), (0,)), ((), ())), preferred_element_type=F32)
        s_acc[j] = gc_ref[j] * s_pair + bm_ref[...] * upd
        inner = []
        for hh, qh in enumerate((jnp.where(lo, q2, 0.0), jnp.where(lo, 0.0, q2))):
            sc = lax.dot_general(qh.astype(BF16), k2, (((1,), (1,)), ((), ())),
                                 preferred_element_type=F32) * dm_ref[2 * j + hh]
            inner.append(jnp.dot(sc.astype(BF16), v2, preferred_element_type=F32))
        o = jnp.where(lo, inner[0], inner[1]) + cross
        y_ref[:, sl] = _head_norm_gate(o, g_ref[:, sl], nrm_ref[:, sl], lo).astype(BF16)
    so_ref[0] = s_acc[...]


def _ret_tables(chunk):
    log_g = jnp.log1p(-jnp.exp2(-5.0 - jnp.arange(N_RET_HEADS, dtype=F32)))
    n = jnp.arange(chunk, dtype=F32)
    diff = n[:, None] - n[None, :]
    dm = jnp.where(diff[None] >= 0.0,
                   jnp.exp(log_g[:, None, None] * jnp.maximum(diff, 0.0)[None]), 0.0)
    per_lane = lambda t: jnp.repeat(t.T, RET_HEAD_DIM, axis=1)
    qd = per_lane(jnp.exp(log_g[:, None] * (n + 1.0)[None]))
    kd = per_lane(jnp.exp(log_g[:, None] * (chunk - 1.0 - n)[None]))
    gch = jnp.exp(log_g * chunk)
    blk = jnp.kron(jnp.eye(2, dtype=F32), jnp.ones((RET_HEAD_DIM, RET_HEAD_DIM), F32))
    gc = jnp.repeat(gch.reshape(N_HEAD_PAIRS, 2), RET_HEAD_DIM, axis=1)[:, :, None] * blk[None]
    return dm, qd, kd, gc, blk


def _pair_state(s):
    b = s.shape[0]
    s = s.reshape(b, N_HEAD_PAIRS, 2, RET_HEAD_DIM, RET_HEAD_DIM)
    z = jnp.zeros_like(s[:, :, 0])
    top = jnp.concatenate([s[:, :, 0], z], axis=-1)
    bot = jnp.concatenate([z, s[:, :, 1]], axis=-1)
    return jnp.concatenate([top, bot], axis=-2)


def _unpair_state(s2):
    d = RET_HEAD_DIM
    return jnp.stack([s2[:, :, :d, :d], s2[:, :, d:, d:]], axis=2).reshape(
        s2.shape[0], N_RET_HEADS, d, d)


def _retention(q, k, v, g, s0_pair, ret_norm, *, nseq, chunk, nchunk):
    dm, qd, kd, gc, blk = _ret_tables(chunk)
    row = lambda b, c: (b * nchunk + c, 0)
    st_spec = pl.BlockSpec((1, N_HEAD_PAIRS, LANES, LANES), lambda b, c: (b, 0, 0, 0))
    blk_spec = pl.BlockSpec((chunk, D_RET), row)
    return pl.pallas_call(
        _ret_kernel,
        grid=(nseq, nchunk),
        in_specs=[blk_spec, blk_spec, blk_spec, blk_spec, st_spec,
                  _const_spec((N_RET_HEADS, chunk, chunk)),
                  _const_spec((chunk, D_RET)), _const_spec((chunk, D_RET)),
                  _const_spec((N_HEAD_PAIRS, LANES, LANES)), _const_spec((LANES, LANES)),
                  _const_spec((1, D_RET))],
        out_specs=[blk_spec, st_spec],
        out_shape=[jax.ShapeDtypeStruct((nseq * nchunk * chunk, D_RET), BF16),
                   jax.ShapeDtypeStruct((nseq, N_HEAD_PAIRS, LANES, LANES), F32)],
        scratch_shapes=[pltpu.VMEM((N_HEAD_PAIRS, LANES, LANES), F32)],
        compiler_params=_cparams("arbitrary", "arbitrary"),
        name="retention_c%d" % chunk,
    )(q, k, v, g, s0_pair, dm, qd, kd, gc, blk, ret_norm)


def _ret_step_kernel(q_ref, k_ref, v_ref, g_ref, s_ref, gam_ref, nrm_ref, y_ref, so_ref):
    q = q_ref[...]
    k = k_ref[...]
    v = v_ref[...]
    gam = gam_ref[0]
    score = jnp.sum(q * k, axis=0, keepdims=True)
    q_dec = (q * gam).astype(BF16).astype(F32)
    cross = jnp.zeros(v.shape, F32)
    for d in range(RET_HEAD_DIM):
        s_d = s_ref[0, d]
        cross = cross + q_dec[d:d + 1, :] * s_d
        so_ref[0, d] = gam * s_d + k[d:d + 1, :] * v
    o = score.astype(BF16).astype(F32) * v + cross
    dlt = o - jnp.mean(o, axis=0, keepdims=True)
    var = jnp.mean(dlt * dlt, axis=0, keepdims=True)
    y_ref[...] = jax.nn.silu(g_ref[...]) * (dlt * lax.rsqrt(var + EPS) * nrm_ref[...])


def _retention_step(q, k, v, g, state, ret_norm):
    n = q.shape[0]
    gam = jnp.exp(jnp.log1p(-jnp.exp2(-5.0 - jnp.arange(N_RET_HEADS, dtype=F32))))
    gam_tab = jnp.broadcast_to(gam[:, None, None], (N_RET_HEADS, 1, n))
    seq_minor = lambda t: t.astype(F32).T
    head = lambda h: (h, 0)
    vec_spec = pl.BlockSpec((RET_HEAD_DIM, n), head)
    st_spec = pl.BlockSpec((1, RET_HEAD_DIM, RET_HEAD_DIM, n), lambda h: (h, 0, 0, 0))
    y_t, s_new = pl.pallas_call(
        _ret_step_kernel,
        grid=(N_RET_HEADS,),
        in_specs=[vec_spec, vec_spec, vec_spec, vec_spec, st_spec,
                  pl.BlockSpec((1, 1, n), lambda h: (h, 0, 0)), vec_spec],
        out_specs=[vec_spec, st_spec],
        out_shape=[jax.ShapeDtypeStruct((D_RET, n), F32),
                   jax.ShapeDtypeStruct((N_RET_HEADS, RET_HEAD_DIM, RET_HEAD_DIM, n), F32)],
        compiler_params=_cparams("parallel"),
        name="retention_step",
    )(seq_minor(q), seq_minor(k), seq_minor(v), seq_minor(g),
      jnp.transpose(state, (1, 2, 3, 0)), gam_tab,
      jnp.broadcast_to(ret_norm[:, None], (D_RET, n)))
    return y_t.T.astype(BF16), jnp.transpose(s_new, (3, 0, 1, 2))


def _first_max(vals, idxs, sentinel):
    m = jnp.max(vals, axis=0, keepdims=True)
    first = jnp.min(jnp.where(vals == m, idxs, sentinel), axis=0, keepdims=True)
    return m, idxs == first


def _route(scores, sel):
    t = sel.shape[1]
    neg = -jnp.inf
    member = lax.broadcasted_iota(jnp.int32, (GROUP_SIZE, t), 0).astype(F32)
    groups = [sel[g * GROUP_SIZE:(g + 1) * GROUP_SIZE, :] for g in range(N_EXPERT_GROUPS)]
    gscore = []
    for grp in groups:
        m1, pick = _first_max(grp, member, float(GROUP_SIZE))
        m2 = jnp.max(jnp.where(pick, neg, grp), axis=0, keepdims=True)
        gscore.append(m1 + m2)
    gs = jnp.concatenate(gscore, axis=0)
    gkeep = jnp.zeros(gs.shape, F32)
    for _ in range(TOPK_GROUPS):
        _, pick = _first_max(gs, member, float(N_EXPERT_GROUPS))
        gkeep = jnp.where(pick, 1.0, gkeep)
        gs = jnp.where(pick, neg, gs)
    cand = jnp.concatenate(
        [jnp.where(gkeep[g:g + 1, :] > 0.5, groups[g], neg) for g in range(N_EXPERT_GROUPS)],
        axis=0)
    expert = lax.broadcasted_iota(jnp.int32, (N_EXPERTS, t), 0).astype(F32)
    ids, ws = [], []
    for _ in range(TOP_K):
        _, pick = _first_max(cand, expert, float(N_EXPERTS))
        ids.append(jnp.sum(jnp.where(pick, expert, 0.0), axis=0, keepdims=True))
        ws.append(jnp.sum(jnp.where(pick, scores, 0.0), axis=0, keepdims=True))
        cand = jnp.where(pick, neg, cand)
    w = jnp.concatenate(ws, axis=0)
    gates = w / jnp.sum(w, axis=0, keepdims=True) * ROUTED_SCALE
    return jnp.concatenate(ids, axis=0), gates


def _mix_kernel(x_ref, ys_ref, yr_ref, woa_ref, wob_ref, n2_ref, wrt_ref, rb_ref,
                wgs_ref, wus_ref, wds_ref, xs_ref, hp_ref, meta_ref, eid_ref, *, n_blocks):
    @pl.when(pl.program_id(0) >= n_blocks)
    def _():
        hp_ref[...] = jnp.zeros_like(hp_ref)
        meta_ref[...] = jnp.zeros_like(meta_ref)

    @pl.when(pl.program_id(0) < n_blocks)
    def _():
        _mix_body(x_ref, ys_ref, yr_ref, woa_ref, wob_ref, n2_ref, wrt_ref, rb_ref,
                  wgs_ref, wus_ref, wds_ref, xs_ref, hp_ref, meta_ref, eid_ref)


def _mix_body(x_ref, ys_ref, yr_ref, woa_ref, wob_ref, n2_ref, wrt_ref, rb_ref,
              wgs_ref, wus_ref, wds_ref, xs_ref, hp_ref, meta_ref, eid_ref):
    tm = x_ref.shape[0]
    x2 = (x_ref[...]
          + jnp.dot(ys_ref[...], woa_ref[...], preferred_element_type=F32)
          + jnp.dot(yr_ref[...], wob_ref[...], preferred_element_type=F32))
    h2f = _rms(x2, n2_ref[...])
    h2 = h2f.astype(BF16)
    h2r = h2.astype(F32)
    for c in range(PACK_ROWS):
        hp_ref[pl.ds(c, tm, stride=PACK_ROWS), :] = h2r[:, c * LANES:(c + 1) * LANES]
    h2_lo = (h2f - h2r).astype(BF16)
    nt_dims = (((1,), (1,)), ((), ()))
    logits = (lax.dot_general(wrt_ref[0], h2, nt_dims, preferred_element_type=F32)
              + lax.dot_general(wrt_ref[1], h2, nt_dims, preferred_element_type=F32)
              + lax.dot_general(wrt_ref[0], h2_lo, nt_dims, preferred_element_type=F32))
    scores = jax.nn.sigmoid(logits)
    ids, gates = _route(scores, scores + rb_ref[:, :1])
    eid_ref[...] = ids.astype(jnp.int32)
    rec = jnp.concatenate([gates, ids, jnp.zeros((LANES - 2 * TOP_K, tm), F32)], axis=0).T
    meta_ref[pl.ds(0, tm, stride=META_ROWS), :] = rec
    for c in range(1, META_ROWS):
        meta_ref[pl.ds(c, tm, stride=META_ROWS), :] = jnp.zeros((tm, LANES), F32)
    a = jnp.dot(h2, wgs_ref[...], preferred_element_type=F32)
    b = jnp.dot(h2, wus_ref[...], preferred_element_type=F32)
    xs_ref[...] = x2 + jnp.dot((jax.nn.silu(a) * b).astype(BF16), wds_ref[...],
                               preferred_element_type=F32)


def _mix(x, ys5, yret, mp, tm, out_tokens):
    rows = x.shape[0]
    n_blocks = rows // tm
    assert out_tokens % tm == 0
    packed = lambda i: (i, 0)
    row = lambda i: (jnp.minimum(i, n_blocks - 1), 0)
    col = lambda i: (0, jnp.minimum(i, n_blocks - 1))
    return pl.pallas_call(
        functools.partial(_mix_kernel, n_blocks=n_blocks),
        grid=(out_tokens // tm,),
        in_specs=[pl.BlockSpec((tm, D_MODEL), row),
                  pl.BlockSpec((tm, D_S5), row), pl.BlockSpec((tm, D_RET), row),
                  _const_spec((D_S5, D_MODEL)), _const_spec((D_RET, D_MODEL)),
                  _const_spec((1, D_MODEL)),
                  _const_spec((2, N_EXPERTS, D_MODEL)), _const_spec((N_EXPERTS, LANES)),
                  _const_spec((D_MODEL, D_EXPERT)), _const_spec((D_MODEL, D_EXPERT)),
                  _const_spec((D_EXPERT, D_MODEL))],
        out_specs=[pl.BlockSpec((tm, D_MODEL), row),
                   pl.BlockSpec((tm * PACK_ROWS, LANES), packed),
                   pl.BlockSpec((tm * META_ROWS, LANES), packed),
                   pl.BlockSpec((TOP_K, tm), col)],
        out_shape=[jax.ShapeDtypeStruct((rows, D_MODEL), F32),
                   jax.ShapeDtypeStruct((out_tokens * PACK_ROWS, LANES), F32),
                   jax.ShapeDtypeStruct((out_tokens * META_ROWS, LANES), F32),
                   jax.ShapeDtypeStruct((TOP_K, rows), jnp.int32)],
        compiler_params=_cparams("arbitrary"),
        name="out_proj_router",
    )(x, ys5, yret, mp["woa"], mp["wob"], mp["n2"], mp["wrt"], mp["rb"],
      mp["wgs"], mp["wus"], mp["wds"])


def _moe_kernel(t0_ref, nt_ref, src_ref, dst_ref, hp_ref, meta_ref, wg32_ref, wu32_ref, wd32_ref,
                acc_ref, xt_ref, xm_ref, ot_ref, xb_ref, wg_ref, wu_ref, wd_ref):
    s = pl.program_id(0)
    tm = MOE_TILE
    half = D_MODEL // 2
    all_rows = range(tm)
    nt = nt_ref[s]
    expert = (s % N_EXPERTS).astype(F32)

    @pl.when(nt > 0)
    def _():
        for src32, dst16 in ((wg32_ref, wg_ref), (wu32_ref, wu_ref), (wd32_ref, wd_ref)):
            rows = src32.shape[1]
            for i in range(0, rows, CAST_ROWS):
                dst16[0, i:i + CAST_ROWS, :] = src32[0, i:i + CAST_ROWS, :].astype(BF16)

    def fetch_rows(j, rows):
        for m in rows:
            r = pl.multiple_of(src_ref[j, 0, m], PACK_ROWS)
            xt_ref[pl.ds(m, PACK_ROWS, stride=XT_PITCH), :] = hp_ref[pl.ds(r, PACK_ROWS), :]
            xm_ref[pl.ds(m, META_ROWS, stride=XT_PITCH), :] = meta_ref[
                pl.ds(pl.multiple_of(r >> META_SHIFT, META_ROWS), META_ROWS), :]

    def add_rows(j, rows):
        for m0 in range(rows.start, rows.stop, ADD_UNROLL):
            new = []
            for m in range(m0, m0 + ADD_UNROLL):
                base = pl.multiple_of(dst_ref[j, 0, m], ACC_ROWS)
                row = ot_ref[pl.ds(m, ACC_ROWS, stride=OT_PITCH), :]
                new.append((base, acc_ref[0, pl.ds(base, ACC_ROWS), :] + row))
            for base, v in new:
                acc_ref[0, pl.ds(base, ACC_ROWS), :] = v

    @pl.when(s % N_EXPERTS == 0)
    def _():
        zero = jnp.zeros((ZERO_ROWS, LANES), F32)

        def clear(i, carry):
            acc_ref[0, pl.ds(pl.multiple_of(i * ZERO_ROWS, ZERO_ROWS), ZERO_ROWS), :] = zero
            return carry

        lax.fori_loop(0, acc_ref.shape[1] // ZERO_ROWS, clear, 0)
        ot_ref[...] = jnp.zeros_like(ot_ref)
        fetch_rows(1, all_rows)

    def stage(j, carry):
        n_slice = 4
        per = tm // n_slice
        rows = [range(i * per, (i + 1) * per) for i in range(n_slice)]
        for c in range(PACK_ROWS):
            xb_ref[:, c * LANES:(c + 1) * LANES] = (
                xt_ref[c * XT_PITCH:c * XT_PITCH + tm, :].astype(BF16))
        rec = xm_ref[0:tm, :]
        ids = pltpu.roll(rec, LANES - TOP_K, 1)
        lane = lax.broadcasted_iota(jnp.int32, rec.shape, 1)
        gate = jnp.sum(jnp.where((lane < TOP_K) & (ids == expert), rec, 0.0),
                       axis=1, keepdims=True)
        add_rows(j - 1, rows[0])
        a = jnp.dot(xb_ref[:, :half], wg_ref[0, :half], preferred_element_type=F32)
        add_rows(j - 1, rows[1])
        a = a + jnp.dot(xb_ref[:, half:], wg_ref[0, half:], preferred_element_type=F32)
        add_rows(j - 1, rows[2])
        b = jnp.dot(xb_ref[:, :half], wu_ref[0, :half], preferred_element_type=F32)
        add_rows(j - 1, rows[3])
        b = b + jnp.dot(xb_ref[:, half:], wu_ref[0, half:], preferred_element_type=F32)
        act = (jax.nn.silu(a) * b * gate).astype(BF16)
        n_col = D_MODEL // n_slice
        for piece in range(n_slice):
            fetch_rows(j + 1, rows[piece])
            out = jnp.dot(act, wd_ref[0, :, piece * n_col:(piece + 1) * n_col],
                          preferred_element_type=F32)
            for cc in range(n_col // LANES):
                c = piece * (n_col // LANES) + cc
                ot_ref[c * OT_PITCH:c * OT_PITCH + tm, :] = out[:, cc * LANES:(cc + 1) * LANES]
        return carry

    lax.fori_loop(1, nt + 1, stage, 0)

    @pl.when(s % N_EXPERTS == N_EXPERTS - 1)
    def _():
        add_rows(nt, all_rows)


def _dispatch_tables(eid_t, n_tok):
    tm, ts = MOE_TILE, MOE_BLOCK_TOKENS
    ng = MOE_BLOCKS * N_EXPERTS
    assert ts < (1 << TOKEN_BITS)
    tok = jnp.arange(n_tok, dtype=jnp.int32)
    grp = (tok // ts)[None, :] * N_EXPERTS + eid_t
    gids = jnp.arange(ng, dtype=jnp.int32)
    counts = jnp.sum((grp.reshape(1, -1) == gids[:, None]).astype(jnp.int32), axis=1)
    npad = (-counts) % tm
    fill = jnp.arange(tm, dtype=jnp.int32)[None, :] < npad[:, None]
    keys = jnp.concatenate([
        ((grp << TOKEN_BITS) | (tok % ts)[None, :]).reshape(-1),
        ((jnp.where(fill, gids[:, None], ng) << TOKEN_BITS) | ts).reshape(-1)])
    toks = lax.sort(keys, dimension=0, is_stable=False) & ((1 << TOKEN_BITS) - 1)
    n_tiles = (n_tok * TOP_K + ng * tm) // tm

    ntile = (counts + npad) // tm
    tile_start = jnp.cumsum(ntile) - ntile
    src = jnp.where(toks < ts, toks, 0) * PACK_ROWS
    dst = toks * ACC_ROWS
    head = jnp.zeros((tm,), jnp.int32)
    tail = jnp.zeros((GROUP_TILES * tm,), jnp.int32)
    rows = n_tiles + 1 + GROUP_TILES
    return {"t0": tile_start.astype(jnp.int32), "nt": ntile.astype(jnp.int32),
            "src": jnp.concatenate([head, src, tail]).reshape(rows, 1, tm),
            "dst": jnp.concatenate([head, dst, tail]).reshape(rows, 1, tm)}


def _moe(hp, meta, tabs, ep):
    tm, ts = MOE_TILE, MOE_BLOCK_TOKENS
    assert GROUP_TILES >= -(-ts // tm) + 2
    window = pl.BlockSpec((pl.Element(GROUP_TILES), pl.Element(1), pl.Element(tm)),
                          lambda s, t0, nt: (t0[s], 0, 0), memory_space=pltpu.SMEM)
    wspec = lambda shape: pl.BlockSpec((1,) + shape, lambda s, t0, nt: (s % N_EXPERTS, 0, 0))
    blk = lambda s, t0, nt: (s // N_EXPERTS, 0)
    acc_rows = (ts + SUBLANES) * ACC_ROWS
    assert acc_rows % ZERO_ROWS == 0
    stage_bufs = [pltpu.VMEM((PACK_ROWS * XT_PITCH, LANES), F32),
                  pltpu.VMEM((META_ROWS * XT_PITCH, LANES), F32),
                  pltpu.VMEM((ACC_ROWS * OT_PITCH, LANES), F32),
                  pltpu.VMEM((tm, D_MODEL), BF16),
                  pltpu.VMEM((1, D_MODEL, D_EXPERT), BF16),
                  pltpu.VMEM((1, D_MODEL, D_EXPERT), BF16),
                  pltpu.VMEM((1, D_EXPERT, D_MODEL), BF16)]
    grid_spec = pltpu.PrefetchScalarGridSpec(
        num_scalar_prefetch=2,
        grid=(MOE_BLOCKS * N_EXPERTS,),
        in_specs=[window, window,
                  pl.BlockSpec((ts * PACK_ROWS, LANES), blk, pipeline_mode=pl.Buffered(1)),
                  pl.BlockSpec((ts * META_ROWS, LANES), blk, pipeline_mode=pl.Buffered(1)),
                  wspec((D_MODEL, D_EXPERT)), wspec((D_MODEL, D_EXPERT)),
                  wspec((D_EXPERT, D_MODEL))],
        out_specs=pl.BlockSpec((1, acc_rows, LANES), lambda s, t0, nt: (s // N_EXPERTS, 0, 0),
                               pipeline_mode=pl.Buffered(1)),
        scratch_shapes=stage_bufs,
    )
    return pl.pallas_call(
        _moe_kernel,
        grid_spec=grid_spec,
        out_shape=jax.ShapeDtypeStruct((MOE_BLOCKS, acc_rows, LANES), F32),
        compiler_params=_cparams("arbitrary"),
        name="moe_experts",
    )(tabs["t0"], tabs["nt"], tabs["src"], tabs["dst"], hp, meta, ep["wg"], ep["wu"], ep["wd"])


def _final_kernel(xs_ref, r_ref, fn_ref, y_ref):
    tm = xs_ref.shape[0]
    routed = jnp.concatenate(
        [r_ref[0, pl.ds(c, tm, stride=ACC_ROWS), :] for c in range(ACC_ROWS)], axis=1)
    y_ref[...] = _rms(xs_ref[...] + routed, fn_ref[...])


def _finalize(xs, routed, final_norm, tm, tok0):
    rows = xs.shape[0]
    per_block = MOE_BLOCK_TOKENS // tm
    first = tok0 // tm
    assert tok0 % tm == 0 and MOE_BLOCK_TOKENS % tm == 0
    return pl.pallas_call(
        _final_kernel,
        grid=(rows // tm,),
        in_specs=[pl.BlockSpec((tm, D_MODEL), lambda i: (i, 0)),
                  pl.BlockSpec((1, tm * ACC_ROWS, LANES),
                               lambda i: ((first + i) // per_block, (first + i) % per_block, 0)),
                  _const_spec((1, D_MODEL))],
        out_specs=pl.BlockSpec((tm, D_MODEL), lambda i: (i, 0)),
        out_shape=jax.ShapeDtypeStruct((rows, D_MODEL), F32),
        compiler_params=_cparams("parallel"),
        name="final_norm",
    )(xs, routed, final_norm)


def _rope_tables(pos):
    half = RET_HEAD_DIM // 2
    inv_freq = ROPE_BASE ** (-jnp.arange(half, dtype=F32) / half)
    ang = pos.astype(F32)[:, None] * inv_freq[None, :]
    cos, sin = jnp.cos(ang), jnp.sin(ang)
    cos_t = jnp.concatenate([cos, cos, cos, cos], axis=1)
    sin_t = jnp.concatenate([-sin, sin, -sin, sin], axis=1)
    return cos_t, sin_t


def kernel(x_prompt, x_sample, state_s5_re, state_s5_im, state_ret, meta_tokens, norm1, w_in,
           s5_lam_re, s5_lam_im, s5_log_dt, s5_b_re, s5_b_im, s5_c_re, s5_c_im, s5_d, s5_w_glu,
           s5_norm, ret_norm, w_out, norm2, w_router, router_bias, w_gate_e, w_up_e, w_down_e,
           w_gate_sh, w_up_sh, w_down_sh, final_norm):
    assert norm1.shape[0] == 1, "single-layer model"
    bp, seq, _ = x_prompt.shape
    ns = x_sample.shape[0]
    l = 0

    n1 = norm1[l].reshape(1, D_MODEL)
    w_in_b = w_in[l].astype(BF16)
    s5p = _s5_params(s5_lam_re[l], s5_lam_im[l], s5_log_dt[l], s5_b_re[l], s5_b_im[l],
                     s5_c_re[l], s5_c_im[l], s5_d[l], s5_w_glu[l], s5_norm[l])
    rnorm = ret_norm[l].reshape(1, D_RET)
    w_out_b = w_out[l].astype(BF16)
    wr_t = w_router[l].T
    wr_hi = wr_t.astype(BF16)
    wr_lo = (wr_t - wr_hi.astype(F32)).astype(BF16)
    mp = {
        "woa": w_out_b[:D_S5], "wob": w_out_b[D_S5:], "n2": norm2[l].reshape(1, D_MODEL),
        "wrt": jnp.stack([wr_hi, wr_lo]),
        "rb": jnp.broadcast_to(router_bias[l][:, None], (N_EXPERTS, LANES)),
        "wgs": w_gate_sh[l].astype(BF16), "wus": w_up_sh[l].astype(BF16),
        "wds": w_down_sh[l].astype(BF16),
    }
    ep = {"wg": w_gate_e[l], "wu": w_up_e[l],
          "wd": w_down_e[l]}
    fnorm = final_norm.reshape(1, D_MODEL)

    xp = x_prompt.reshape(bp * seq, D_MODEL)
    cos_p, sin_p = _rope_tables(N_META + jnp.arange(seq, dtype=jnp.int32))
    tm_a = 512
    up, qp, kp, vp, gp = _project(xp, n1, w_in_b, cos_p, sin_p, tm_a, seq // tm_a)

    meta_tm = jnp.repeat(meta_tokens, SUBLANES, axis=0)
    mchunk = LANES
    meta_chunk = jnp.concatenate([jnp.zeros((mchunk - N_META, D_MODEL), F32), meta_tokens], axis=0)
    x_small = jnp.concatenate([x_sample.reshape(ns, D_MODEL), meta_tm, meta_chunk], axis=0)
    meta_pos = jnp.arange(N_META, dtype=jnp.int32)
    pos_small = jnp.concatenate([jnp.full((ns,), PAST_LEN, jnp.int32),
                                 jnp.repeat(meta_pos, SUBLANES),
                                 jnp.zeros((mchunk - N_META,), jnp.int32), meta_pos])
    cos_s, sin_s = _rope_tables(pos_small)
    n_small = x_small.shape[0]
    n_tm = ns + N_META * SUBLANES
    us, qs, ks, vs, gs = _project(x_small, n1, w_in_b, cos_s, sin_s, n_small, 1)

    zero8 = jnp.zeros((SUBLANES, S5_LANES), F32)
    _, m_re, m_im = _s5(us[ns:n_tm], zero8, zero8, s5p, nb=SUBLANES, tt=N_META, nblk=1,
                        bt_major=False)
    tt = 64
    ys5_p, p_re, p_im = _s5(up.reshape(bp, seq, D_S5), m_re, m_im, s5p, nb=bp, tt=tt,
                            nblk=seq // tt, bt_major=True)
    ys5_p = ys5_p.reshape(bp * seq, D_S5)
    ys5_s, s_re, s_im = _s5(us[:ns], state_s5_re[l].reshape(ns, S5_LANES),
                            state_s5_im[l].reshape(ns, S5_LANES), s5p, nb=ns, tt=1, nblk=1,
                            bt_major=False)

    zero_pair = jnp.zeros((1, N_HEAD_PAIRS, LANES, LANES), F32)
    _, m_pair = _retention(qs[n_tm:], ks[n_tm:], vs[n_tm:], gs[n_tm:],
                           zero_pair, rnorm, nseq=1, chunk=mchunk, nchunk=1)
    chunk = 256
    yret_p, p_pair = _retention(qp, kp, vp, gp, jnp.broadcast_to(m_pair, (bp,) + m_pair.shape[1:]),
                                rnorm, nseq=bp, chunk=chunk, nchunk=seq // chunk)
    yret_s, ret_s = _retention_step(qs[:ns], ks[:ns], vs[:ns], gs[:ns], state_ret[l], ret_norm[l])

    n_prompt = bp * seq
    assert n_prompt + ns <= MOE_BLOCKS * MOE_BLOCK_TOKENS
    n_moe = MOE_BLOCKS * MOE_BLOCK_TOKENS
    xs_p, hp, meta, eid_p = _mix(xp, ys5_p, yret_p, mp, 512, n_moe)
    xs_s, hp_s, meta_s, eid_s = _mix(x_sample.reshape(ns, D_MODEL), ys5_s, yret_s, mp, ns, ns)
    hp = lax.dynamic_update_slice(hp, hp_s, (n_prompt * PACK_ROWS, 0))
    meta = lax.dynamic_update_slice(meta, meta_s, (n_prompt * META_ROWS, 0))
    tabs = _dispatch_tables(jnp.concatenate([eid_p, eid_s], axis=1), n_prompt + ns)
    routed = _moe(hp, meta, tabs, ep)
    y_p = _finalize(xs_p, routed, fnorm, 512, 0)
    y_s = _finalize(xs_s, routed, fnorm, ns, n_prompt)

    shape5 = (1, bp, N_S5_GROUPS, S5_STATE)
    return (y_p.reshape(bp, seq, D_MODEL),
            y_s.reshape(ns, 1, D_MODEL),
            p_re.reshape(shape5), p_im.reshape(shape5),
            _unpair_state(p_pair)[None],
            s_re.reshape(1, ns, N_S5_GROUPS, S5_STATE), s_im.reshape(1, ns, N_S5_GROUPS, S5_STATE),
            ret_s[None])
```

```python
import functools
import math

import jax
import jax.numpy as jnp
from jax import lax
from jax.experimental import pallas as pl
from jax.experimental.pallas import tpu as pltpu

F32 = jnp.float32
BF16 = jnp.bfloat16

D_MODEL = 1024
N_META = 16
PAST_LEN = 16384
D_S5 = 512
S5_GROUP = 16
N_S5_GROUPS = 32
S5_STATE = 64
S5_LANES = N_S5_GROUPS * S5_STATE
D_RET = 512
N_RET_HEADS = 8
RET_HEAD_DIM = 64
N_HEAD_PAIRS = 4
ROPE_BASE = 10000.0
D_IN = D_S5 + 4 * D_RET
N_EXPERTS = 64
TOP_K = 8
N_EXPERT_GROUPS = 8
GROUP_SIZE = 8
TOPK_GROUPS = 4
D_EXPERT = 256
ROUTED_SCALE = 2.5
EPS = 1e-6

LANES = 128
SUBLANES = 8
VMEM_LIMIT = 56 * 1024 * 1024

PACK_ROWS = D_MODEL // LANES
ACC_ROWS = D_MODEL // LANES
MOE_BLOCKS = 4
MOE_BLOCK_TOKENS = 4608
MOE_TILE = 256
GROUP_TILES = MOE_BLOCK_TOKENS // MOE_TILE + 2
ADD_UNROLL = 8
OT_PITCH = MOE_TILE + SUBLANES
XT_PITCH = MOE_TILE + SUBLANES
META_ROWS = 2
META_SHIFT = 2
assert PACK_ROWS == META_ROWS << META_SHIFT
TOKEN_BITS = 13
ZERO_ROWS = 64
CAST_ROWS = 128


def _cparams(*sem):
    return pltpu.CompilerParams(dimension_semantics=sem, vmem_limit_bytes=VMEM_LIMIT)


def _const_spec(shape):
    nd = len(shape)
    return pl.BlockSpec(shape, lambda *_: (0,) * nd)


def _rms(x, gain):
    return x * lax.rsqrt(jnp.mean(x * x, axis=-1, keepdims=True) + EPS) * gain


def _proj_kernel(x_ref, n1_ref, w_ref, cos_ref, sin_ref, u_ref, q_ref, k_ref, v_ref, g_ref):
    h = _rms(x_ref[...], n1_ref[...]).astype(BF16)
    proj = jnp.dot(h, w_ref[...], preferred_element_type=F32)
    cos = cos_ref[...]
    sin = sin_ref[...]
    lane = lax.broadcasted_iota(jnp.int32, cos.shape, 1)
    first_half = (lane % RET_HEAD_DIM) < (RET_HEAD_DIM // 2)

    def rotary(t):
        partner = jnp.where(first_half,
                            pltpu.roll(t, LANES - RET_HEAD_DIM // 2, 1),
                            pltpu.roll(t, RET_HEAD_DIM // 2, 1))
        return t * cos + partner * sin

    u_ref[...] = proj[:, :D_S5]
    for j in range(N_HEAD_PAIRS):
        lo = D_S5 + j * LANES
        q_ref[:, j * LANES:(j + 1) * LANES] = rotary(proj[:, lo:lo + LANES]).astype(BF16)
        lo += D_RET
        k_ref[:, j * LANES:(j + 1) * LANES] = (
            rotary(proj[:, lo:lo + LANES]) * (RET_HEAD_DIM ** -0.5)).astype(BF16)
    v_ref[...] = proj[:, D_S5 + 2 * D_RET:D_S5 + 3 * D_RET].astype(BF16)
    g_ref[...] = proj[:, D_S5 + 3 * D_RET:]


def _project(x, norm1, w_in_b, cos_t, sin_t, tm, table_blocks):
    rows = x.shape[0]
    row = lambda i: (i, 0)
    tab = lambda i: (i % table_blocks, 0)
    return pl.pallas_call(
        _proj_kernel,
        grid=(rows // tm,),
        in_specs=[pl.BlockSpec((tm, D_MODEL), row),
                  _const_spec((1, D_MODEL)),
                  _const_spec((D_MODEL, D_IN)),
                  pl.BlockSpec((tm, LANES), tab),
                  pl.BlockSpec((tm, LANES), tab)],
        out_specs=[pl.BlockSpec((tm, D_S5), row),
                   pl.BlockSpec((tm, D_RET), row),
                   pl.BlockSpec((tm, D_RET), row),
                   pl.BlockSpec((tm, D_RET), row),
                   pl.BlockSpec((tm, D_RET), row)],
        out_shape=[jax.ShapeDtypeStruct((rows, D_S5), F32),
                   jax.ShapeDtypeStruct((rows, D_RET), BF16),
                   jax.ShapeDtypeStruct((rows, D_RET), BF16),
                   jax.ShapeDtypeStruct((rows, D_RET), BF16),
                   jax.ShapeDtypeStruct((rows, D_RET), F32)],
        compiler_params=_cparams("parallel"),
        name="in_proj",
    )(x, norm1, w_in_b, cos_t, sin_t)


def _s5_kernel(u_ref, x0re_ref, x0im_ref, are_ref, aim_ref, b_ref, c_ref, d_ref, wglu_ref,
               nrm_ref, y_ref, sre_ref, sim_ref, st, sre, sim, utm, ytm, *, nb, tt, bt_major):
    i = pl.program_id(0)
    rows = nb * tt

    @pl.when(i == 0)
    def _():
        sre[...] = x0re_ref[...]
        sim[...] = x0im_ref[...]

    if bt_major:
        for t in range(tt):
            utm[t * nb:(t + 1) * nb, :] = u_ref[:, t, :]
    else:
        utm[...] = u_ref[...]

    ub = utm[...].astype(BF16)
    half = S5_LANES // 2
    kh = D_S5 // 2
    for part in range(2):
        for hf in range(2):
            st[:, part * S5_LANES + hf * half:part * S5_LANES + (hf + 1) * half] = jnp.dot(
                ub[:, hf * kh:(hf + 1) * kh], b_ref[part, hf], preferred_element_type=F32)

    lw = 512 if nb == SUBLANES else LANES
    for lg in range(S5_LANES // lw):
        re = slice(lg * lw, (lg + 1) * lw)
        im = slice(S5_LANES + lg * lw, S5_LANES + (lg + 1) * lw)
        a_re = are_ref[:, re]
        a_im = aim_ref[:, re]

        def step(t, carry):
            s_re, s_im = carry
            r0 = pl.multiple_of(t * nb, nb)
            n_re = a_re * s_re - a_im * s_im + st[pl.ds(r0, nb), re]
            n_im = a_re * s_im + a_im * s_re + st[pl.ds(r0, nb), im]
            st[pl.ds(r0, nb), re] = n_re
            st[pl.ds(r0, nb), im] = n_im
            return n_re, n_im

        f_re, f_im = lax.fori_loop(0, tt, step, (sre[:, re], sim[:, re]))
        sre[:, re] = f_re
        sim[:, re] = f_im

    sre_ref[...] = sre[...]
    sim_ref[...] = sim[...]

    ys = []
    for hf in range(2):
        xr = st[:, hf * half:(hf + 1) * half].astype(BF16)
        xi = st[:, S5_LANES + hf * half:S5_LANES + (hf + 1) * half].astype(BF16)
        ys.append(jnp.dot(xr, c_ref[0, hf], preferred_element_type=F32)
                  - jnp.dot(xi, c_ref[1, hf], preferred_element_type=F32))
    y = jnp.concatenate(ys, axis=1) + d_ref[...] * utm[...]
    y = jax.nn.gelu(y)
    y = y * jax.nn.sigmoid(jnp.dot(y.astype(BF16), wglu_ref[...], preferred_element_type=F32))
    y = _rms(y, nrm_ref[...]).astype(BF16)
    if bt_major:
        yf = y.astype(F32)
        for j in range(D_S5 // LANES):
            ytm[j] = yf[:, j * LANES:(j + 1) * LANES]
        for b in range(nb):
            for j in range(D_S5 // LANES):
                y_ref[b, :, j * LANES:(j + 1) * LANES] = (
                    ytm[j, pl.ds(b, tt, stride=nb), :].astype(BF16))
    else:
        y_ref[...] = y


def _s5(u, x0re, x0im, s5p, *, nb, tt, nblk, bt_major):
    rows = nb * tt
    if bt_major:
        u_spec = pl.BlockSpec((nb, tt, D_S5), lambda i: (0, i, 0))
        y_shape = jax.ShapeDtypeStruct((nb, tt * nblk, D_S5), BF16)
    else:
        u_spec = pl.BlockSpec((rows, D_S5), lambda i: (i, 0))
        y_shape = jax.ShapeDtypeStruct((rows * nblk, D_S5), BF16)
    state_spec = _const_spec((nb, S5_LANES))
    are = jnp.broadcast_to(s5p["are"], (nb, S5_LANES))
    aim = jnp.broadcast_to(s5p["aim"], (nb, S5_LANES))
    return pl.pallas_call(
        functools.partial(_s5_kernel, nb=nb, tt=tt, bt_major=bt_major),
        grid=(nblk,),
        in_specs=[u_spec, state_spec, state_spec, state_spec, state_spec,
                  _const_spec((2, 2, D_S5 // 2, S5_LANES // 2)),
                  _const_spec((2, 2, S5_LANES // 2, D_S5 // 2)),
                  _const_spec((1, D_S5)),
                  _const_spec((D_S5, D_S5)),
                  _const_spec((1, D_S5))],
        out_specs=[u_spec, state_spec, state_spec],
        out_shape=[y_shape,
                   jax.ShapeDtypeStruct((nb, S5_LANES), F32),
                   jax.ShapeDtypeStruct((nb, S5_LANES), F32)],
        scratch_shapes=[pltpu.VMEM((rows, 2 * S5_LANES), F32),
                        pltpu.VMEM((nb, S5_LANES), F32),
                        pltpu.VMEM((nb, S5_LANES), F32),
                        pltpu.VMEM((rows, D_S5), F32),
                        pltpu.VMEM((D_S5 // LANES, rows, LANES), F32)],
        compiler_params=_cparams("arbitrary"),
        name="s5_" + ("bt" if bt_major else "tm") + str(nb),
    )(u, x0re, x0im, are, aim, s5p["b"], s5p["c"], s5p["d"], s5p["wglu"], s5p["nrm"])


def _s5_params(lam_re, lam_im, log_dt, b_re, b_im, c_re, c_im, d_skip, w_glu, nrm):
    dt = jnp.exp(log_dt)[:, None]
    mag = jnp.exp(lam_re * dt)
    abar_re, abar_im = mag * jnp.cos(lam_im * dt), mag * jnp.sin(lam_im * dt)
    num_re, num_im = abar_re - 1.0, abar_im
    den = lam_re * lam_re + lam_im * lam_im
    f_re = (num_re * lam_re + num_im * lam_im) / den
    f_im = (num_im * lam_re - num_re * lam_im) / den
    bbar_re = f_re[..., None] * b_re - f_im[..., None] * b_im
    bbar_im = f_re[..., None] * b_im + f_im[..., None] * b_re
    hg = N_S5_GROUPS // 2
    eye = jnp.eye(hg, dtype=F32)

    def bdiag(bb):
        bb = bb.reshape(2, hg, S5_STATE, S5_GROUP)
        return jnp.einsum("zgph,gk->zghkp", bb, eye).reshape(2, hg * S5_GROUP, hg * S5_STATE)

    def cdiag(cc):
        cc = cc.reshape(2, hg, S5_GROUP, S5_STATE)
        return jnp.einsum("zgnp,gk->zgpkn", cc, eye).reshape(2, hg * S5_STATE, hg * S5_GROUP)

    return {
        "are": abar_re.reshape(1, S5_LANES), "aim": abar_im.reshape(1, S5_LANES),
        "b": jnp.stack([bdiag(bbar_re), bdiag(bbar_im)]).astype(BF16),
        "c": jnp.stack([cdiag(c_re), cdiag(c_im)]).astype(BF16),
        "d": d_skip.reshape(1, D_S5), "wglu": w_glu.astype(BF16), "nrm": nrm.reshape(1, D_S5),
    }


def _head_norm_gate(o, g, gain, lo):
    inv = 1.0 / RET_HEAD_DIM

    def seg_mean(t):
        s_lo = jnp.sum(jnp.where(lo, t, 0.0), axis=1, keepdims=True)
        s_hi = jnp.sum(jnp.where(lo, 0.0, t), axis=1, keepdims=True)
        return jnp.where(lo, s_lo, s_hi) * inv

    dlt = o - seg_mean(o)
    var = seg_mean(dlt * dlt)
    return jax.nn.silu(g) * (dlt * lax.rsqrt(var + EPS) * gain)


def _ret_kernel(q_ref, k_ref, v_ref, g_ref, s0_ref, dm_ref, qd_ref, kd_ref, gc_ref, bm_ref,
                nrm_ref, y_ref, so_ref, s_acc):
    c = pl.program_id(1)

    @pl.when(c == 0)
    def _():
        s_acc[...] = s0_ref[0]

    rows = q_ref.shape[0]
    lane = lax.broadcasted_iota(jnp.int32, (rows, LANES), 1)
    lo = lane < RET_HEAD_DIM
    for j in range(N_HEAD_PAIRS):
        sl = slice(j * LANES, (j + 1) * LANES)
        q2 = q_ref[:, sl].astype(F32)
        k2 = k_ref[:, sl]
        v2 = v_ref[:, sl]
        s_pair = s_acc[j]
        cross = jnp.dot((q2 * qd_ref[:, sl]).astype(BF16), s_pair.astype(BF16),
                        preferred_element_type=F32)
        k_dec = (k2.astype(F32) * kd_ref[:, sl]).astype(BF16)
        upd = lax.dot_general(k_dec, v2, (((0,), (0,)), ((), ())), preferred_element_type=F32)
        s_acc[j] = gc_ref[j] * s_pair + bm_ref[...] * upd
        inner = []
        for hh, qh in enumerate((jnp.where(lo, q2, 0.0), jnp.where(lo, 0.0, q2))):
            sc = lax.dot_general(qh.astype(BF16), k2, (((1,), (1,)), ((), ())),
                                 preferred_element_type=F32) * dm_ref[2 * j + hh]
            inner.append(jnp.dot(sc.astype(BF16), v2, preferred_element_type=F32))
        o = jnp.where(lo, inner[0], inner[1]) + cross
        y_ref[:, sl] = _head_norm_gate(o, g_ref[:, sl], nrm_ref[:, sl], lo).astype(BF16)
    so_ref[0] = s_acc[...]


def _ret_tables(chunk):
    log_g = jnp.log1p(-jnp.exp2(-5.0 - jnp.arange(N_RET_HEADS, dtype=F32)))
    n = jnp.arange(chunk, dtype=F32)
    diff = n[:, None] - n[None, :]
    dm = jnp.where(diff[None] >= 0.0,
                   jnp.exp(log_g[:, None, None] * jnp.maximum(diff, 0.0)[None]), 0.0)
    per_lane = lambda t: jnp.repeat(t.T, RET_HEAD_DIM, axis=1)
    qd = per_lane(jnp.exp(log_g[:, None] * (n + 1.0)[None]))
    kd = per_lane(jnp.exp(log_g[:, None] * (chunk - 1.0 - n)[None]))
    gch = jnp.exp(log_g * chunk)
    blk = jnp.kron(jnp.eye(2, dtype=F32), jnp.ones((RET_HEAD_DIM, RET_HEAD_DIM), F32))
    gc = jnp.repeat(gch.reshape(N_HEAD_PAIRS, 2), RET_HEAD_DIM, axis=1)[:, :, None] * blk[None]
    return dm, qd, kd, gc, blk


def _pair_state(s):
    b = s.shape[0]
    s = s.reshape(b, N_HEAD_PAIRS, 2, RET_HEAD_DIM, RET_HEAD_DIM)
    z = jnp.zeros_like(s[:, :, 0])
    top = jnp.concatenate([s[:, :, 0], z], axis=-1)
    bot = jnp.concatenate([z, s[:, :, 1]], axis=-1)
    return jnp.concatenate([top, bot], axis=-2)


def _unpair_state(s2):
    d = RET_HEAD_DIM
    return jnp.stack([s2[:, :, :d, :d], s2[:, :, d:, d:]], axis=2).reshape(
        s2.shape[0], N_RET_HEADS, d, d)


def _retention(q, k, v, g, s0_pair, ret_norm, *, nseq, chunk, nchunk):
    dm, qd, kd, gc, blk = _ret_tables(chunk)
    row = lambda b, c: (b * nchunk + c, 0)
    st_spec = pl.BlockSpec((1, N_HEAD_PAIRS, LANES, LANES), lambda b, c: (b, 0, 0, 0))
    blk_spec = pl.BlockSpec((chunk, D_RET), row)
    return pl.pallas_call(
        _ret_kernel,
        grid=(nseq, nchunk),
        in_specs=[blk_spec, blk_spec, blk_spec, blk_spec, st_spec,
                  _const_spec((N_RET_HEADS, chunk, chunk)),
                  _const_spec((chunk, D_RET)), _const_spec((chunk, D_RET)),
                  _const_spec((N_HEAD_PAIRS, LANES, LANES)), _const_spec((LANES, LANES)),
                  _const_spec((1, D_RET))],
        out_specs=[blk_spec, st_spec],
        out_shape=[jax.ShapeDtypeStruct((nseq * nchunk * chunk, D_RET), BF16),
                   jax.ShapeDtypeStruct((nseq, N_HEAD_PAIRS, LANES, LANES), F32)],
        scratch_shapes=[pltpu.VMEM((N_HEAD_PAIRS, LANES, LANES), F32)],
        compiler_params=_cparams("arbitrary", "arbitrary"),
        name="retention_c%d" % chunk,
    )(q, k, v, g, s0_pair, dm, qd, kd, gc, blk, ret_norm)


def _ret_step_kernel(q_ref, k_ref, v_ref, g_ref, s_ref, gam_ref, nrm_ref, y_ref, so_ref):
    q = q_ref[...]
    k = k_ref[...]
    v = v_ref[...]
    gam = gam_ref[0]
    score = jnp.sum(q * k, axis=0, keepdims=True)
    q_dec = (q * gam).astype(BF16).astype(F32)
    cross = jnp.zeros(v.shape, F32)
    for d in range(RET_HEAD_DIM):
        s_d = s_ref[0, d]
        cross = cross + q_dec[d:d + 1, :] * s_d
        so_ref[0, d] = gam * s_d + k[d:d + 1, :] * v
    o = score.astype(BF16).astype(F32) * v + cross
    dlt = o - jnp.mean(o, axis=0, keepdims=True)
    var = jnp.mean(dlt * dlt, axis=0, keepdims=True)
    y_ref[...] = jax.nn.silu(g_ref[...]) * (dlt * lax.rsqrt(var + EPS) * nrm_ref[...])


def _retention_step(q, k, v, g, state, ret_norm):
    n = q.shape[0]
    gam = jnp.exp(jnp.log1p(-jnp.exp2(-5.0 - jnp.arange(N_RET_HEADS, dtype=F32))))
    gam_tab = jnp.broadcast_to(gam[:, None, None], (N_RET_HEADS, 1, n))
    seq_minor = lambda t: t.astype(F32).T
    head = lambda h: (h, 0)
    vec_spec = pl.BlockSpec((RET_HEAD_DIM, n), head)
    st_spec = pl.BlockSpec((1, RET_HEAD_DIM, RET_HEAD_DIM, n), lambda h: (h, 0, 0, 0))
    y_t, s_new = pl.pallas_call(
        _ret_step_kernel,
        grid=(N_RET_HEADS,),
        in_specs=[vec_spec, vec_spec, vec_spec, vec_spec, st_spec,
                  pl.BlockSpec((1, 1, n), lambda h: (h, 0, 0)), vec_spec],
        out_specs=[vec_spec, st_spec],
        out_shape=[jax.ShapeDtypeStruct((D_RET, n), F32),
                   jax.ShapeDtypeStruct((N_RET_HEADS, RET_HEAD_DIM, RET_HEAD_DIM, n), F32)],
        compiler_params=_cparams("parallel"),
        name="retention_step",
    )(seq_minor(q), seq_minor(k), seq_minor(v), seq_minor(g),
      jnp.transpose(state, (1, 2, 3, 0)), gam_tab,
      jnp.broadcast_to(ret_norm[:, None], (D_RET, n)))
    return y_t.T.astype(BF16), jnp.transpose(s_new, (3, 0, 1, 2))


def _first_max(vals, idxs, sentinel):
    m = jnp.max(vals, axis=0, keepdims=True)
    first = jnp.min(jnp.where(vals == m, idxs, sentinel), axis=0, keepdims=True)
    return m, idxs == first


def _route(scores, sel):
    t = sel.shape[1]
    neg = -jnp.inf
    member = lax.broadcasted_iota(jnp.int32, (GROUP_SIZE, t), 0).astype(F32)
    groups = [sel[g * GROUP_SIZE:(g + 1) * GROUP_SIZE, :] for g in range(N_EXPERT_GROUPS)]
    gscore = []
    for grp in groups:
        m1, pick = _first_max(grp, member, float(GROUP_SIZE))
        m2 = jnp.max(jnp.where(pick, neg, grp), axis=0, keepdims=True)
        gscore.append(m1 + m2)
    gs = jnp.concatenate(gscore, axis=0)
    gkeep = jnp.zeros(gs.shape, F32)
    for _ in range(TOPK_GROUPS):
        _, pick = _first_max(gs, member, float(N_EXPERT_GROUPS))
        gkeep = jnp.where(pick, 1.0, gkeep)
        gs = jnp.where(pick, neg, gs)
    cand = jnp.concatenate(
        [jnp.where(gkeep[g:g + 1, :] > 0.5, groups[g], neg) for g in range(N_EXPERT_GROUPS)],
        axis=0)
    expert = lax.broadcasted_iota(jnp.int32, (N_EXPERTS, t), 0).astype(F32)
    ids, ws = [], []
    chosen = jnp.zeros(cand.shape, F32)
    for _ in range(TOP_K):
        _, pick = _first_max(cand, expert, float(N_EXPERTS))
        ids.append(jnp.sum(jnp.where(pick, expert, 0.0), axis=0, keepdims=True))
        ws.append(jnp.sum(jnp.where(pick, scores, 0.0), axis=0, keepdims=True))
        chosen = jnp.where(pick, 1.0, chosen)
        cand = jnp.where(pick, neg, cand)
    w = jnp.concatenate(ws, axis=0)
    gates = w / jnp.sum(w, axis=0, keepdims=True) * ROUTED_SCALE
    return jnp.concatenate(ids, axis=0), gates, jnp.sum(chosen, axis=1, keepdims=True)


def _mix_kernel(x_ref, ys_ref, yr_ref, woa_ref, wob_ref, n2_ref, wrt_ref, rb_ref,
                wgs_ref, wus_ref, wds_ref, xs_ref, hp_ref, meta_ref, eid_ref, cnt_ref, *,
                n_blocks):
    @pl.when(pl.program_id(0) >= n_blocks)
    def _():
        hp_ref[...] = jnp.zeros_like(hp_ref)
        meta_ref[...] = jnp.zeros_like(meta_ref)
        cnt_ref[...] = jnp.zeros_like(cnt_ref)

    @pl.when(pl.program_id(0) < n_blocks)
    def _():
        _mix_body(x_ref, ys_ref, yr_ref, woa_ref, wob_ref, n2_ref, wrt_ref, rb_ref,
                  wgs_ref, wus_ref, wds_ref, xs_ref, hp_ref, meta_ref, eid_ref, cnt_ref)


def _mix_body(x_ref, ys_ref, yr_ref, woa_ref, wob_ref, n2_ref, wrt_ref, rb_ref,
              wgs_ref, wus_ref, wds_ref, xs_ref, hp_ref, meta_ref, eid_ref, cnt_ref):
    tm = x_ref.shape[0]
    x2 = (x_ref[...]
          + jnp.dot(ys_ref[...], woa_ref[...], preferred_element_type=F32)
          + jnp.dot(yr_ref[...], wob_ref[...], preferred_element_type=F32))
    h2f = _rms(x2, n2_ref[...])
    h2 = h2f.astype(BF16)
    h2r = h2.astype(F32)
    for c in range(PACK_ROWS):
        hp_ref[pl.ds(c, tm, stride=PACK_ROWS), :] = h2r[:, c * LANES:(c + 1) * LANES]
    h2_lo = (h2f - h2r).astype(BF16)
    nt_dims = (((1,), (1,)), ((), ()))
    logits = (lax.dot_general(wrt_ref[0], h2, nt_dims, preferred_element_type=F32)
              + lax.dot_general(wrt_ref[1], h2, nt_dims, preferred_element_type=F32)
              + lax.dot_general(wrt_ref[0], h2_lo, nt_dims, preferred_element_type=F32))
    scores = jax.nn.sigmoid(logits)
    ids, gates, counts = _route(scores, scores + rb_ref[:, :1])
    eid_ref[...] = ids.astype(jnp.int32)
    cnt_ref[0] = jnp.broadcast_to(counts, (N_EXPERTS, LANES))
    rec = jnp.concatenate([gates, ids, jnp.zeros((LANES - 2 * TOP_K, tm), F32)], axis=0).T
    meta_ref[pl.ds(0, tm, stride=META_ROWS), :] = rec
    for c in range(1, META_ROWS):
        meta_ref[pl.ds(c, tm, stride=META_ROWS), :] = jnp.zeros((tm, LANES), F32)
    a = jnp.dot(h2, wgs_ref[...], preferred_element_type=F32)
    b = jnp.dot(h2, wus_ref[...], preferred_element_type=F32)
    xs_ref[...] = x2 + jnp.dot((jax.nn.silu(a) * b).astype(BF16), wds_ref[...],
                               preferred_element_type=F32)


def _mix(x, ys5, yret, mp, tm, out_tokens):
    rows = x.shape[0]
    n_blocks = rows // tm
    assert out_tokens % tm == 0
    packed = lambda i: (i, 0)
    row = lambda i: (jnp.minimum(i, n_blocks - 1), 0)
    col = lambda i: (0, jnp.minimum(i, n_blocks - 1))
    return pl.pallas_call(
        functools.partial(_mix_kernel, n_blocks=n_blocks),
        grid=(out_tokens // tm,),
        in_specs=[pl.BlockSpec((tm, D_MODEL), row),
                  pl.BlockSpec((tm, D_S5), row), pl.BlockSpec((tm, D_RET), row),
                  _const_spec((D_S5, D_MODEL)), _const_spec((D_RET, D_MODEL)),
                  _const_spec((1, D_MODEL)),
                  _const_spec((2, N_EXPERTS, D_MODEL)), _const_spec((N_EXPERTS, LANES)),
                  _const_spec((D_MODEL, D_EXPERT)), _const_spec((D_MODEL, D_EXPERT)),
                  _const_spec((D_EXPERT, D_MODEL))],
        out_specs=[pl.BlockSpec((tm, D_MODEL), row),
                   pl.BlockSpec((tm * PACK_ROWS, LANES), packed),
                   pl.BlockSpec((tm * META_ROWS, LANES), packed),
                   pl.BlockSpec((TOP_K, tm), col),
                   pl.BlockSpec((1, N_EXPERTS, LANES), lambda i: (i, 0, 0))],
        out_shape=[jax.ShapeDtypeStruct((rows, D_MODEL), F32),
                   jax.ShapeDtypeStruct((out_tokens * PACK_ROWS, LANES), F32),
                   jax.ShapeDtypeStruct((out_tokens * META_ROWS, LANES), F32),
                   jax.ShapeDtypeStruct((TOP_K, rows), jnp.int32),
                   jax.ShapeDtypeStruct((out_tokens // tm, N_EXPERTS, LANES), F32)],
        compiler_params=_cparams("arbitrary"),
        name="out_proj_router",
    )(x, ys5, yret, mp["woa"], mp["wob"], mp["n2"], mp["wrt"], mp["rb"],
      mp["wgs"], mp["wus"], mp["wds"])


def _moe_kernel(t0_ref, nt_ref, src_ref, dst_ref, hp_ref, meta_ref, wg32_ref, wu32_ref, wd32_ref,
                acc_ref, xt_ref, xm_ref, ot_ref, xb_ref, wg_ref, wu_ref, wd_ref):
    s = pl.program_id(0)
    tm = MOE_TILE
    half = D_MODEL // 2
    all_rows = range(tm)
    nt = nt_ref[s]
    expert = (s % N_EXPERTS).astype(F32)

    @pl.when(nt > 0)
    def _():
        for src32, dst16 in ((wg32_ref, wg_ref), (wu32_ref, wu_ref), (wd32_ref, wd_ref)):
            rows = src32.shape[1]
            for i in range(0, rows, CAST_ROWS):
                dst16[0, i:i + CAST_ROWS, :] = src32[0, i:i + CAST_ROWS, :].astype(BF16)

    def fetch_rows(j, rows):
        for m in rows:
            r = pl.multiple_of(src_ref[j, 0, m], PACK_ROWS)
            xt_ref[pl.ds(m, PACK_ROWS, stride=XT_PITCH), :] = hp_ref[pl.ds(r, PACK_ROWS), :]
            xm_ref[pl.ds(m, META_ROWS, stride=XT_PITCH), :] = meta_ref[
                pl.ds(pl.multiple_of(r >> META_SHIFT, META_ROWS), META_ROWS), :]

    def add_rows(j, rows):
        for m0 in range(rows.start, rows.stop, ADD_UNROLL):
            new = []
            for m in range(m0, m0 + ADD_UNROLL):
                base = pl.multiple_of(dst_ref[j, 0, m], ACC_ROWS)
                row = ot_ref[pl.ds(m, ACC_ROWS, stride=OT_PITCH), :]
                new.append((base, acc_ref[0, pl.ds(base, ACC_ROWS), :] + row))
            for base, v in new:
                acc_ref[0, pl.ds(base, ACC_ROWS), :] = v

    @pl.when(s % N_EXPERTS == 0)
    def _():
        zero = jnp.zeros((ZERO_ROWS, LANES), F32)

        def clear(i, carry):
            acc_ref[0, pl.ds(pl.multiple_of(i * ZERO_ROWS, ZERO_ROWS), ZERO_ROWS), :] = zero
            return carry

        lax.fori_loop(0, acc_ref.shape[1] // ZERO_ROWS, clear, 0)
        ot_ref[...] = jnp.zeros_like(ot_ref)
        fetch_rows(1, all_rows)

    def stage(j, carry):
        n_slice = 4
        per = tm // n_slice
        rows = [range(i * per, (i + 1) * per) for i in range(n_slice)]
        for c in range(PACK_ROWS):
            xb_ref[:, c * LANES:(c + 1) * LANES] = (
                xt_ref[c * XT_PITCH:c * XT_PITCH + tm, :].astype(BF16))
        rec = xm_ref[0:tm, :]
        ids = pltpu.roll(rec, LANES - TOP_K, 1)
        lane = lax.broadcasted_iota(jnp.int32, rec.shape, 1)
        gate = jnp.sum(jnp.where((lane < TOP_K) & (ids == expert), rec, 0.0),
                       axis=1, keepdims=True)
        add_rows(j - 1, rows[0])
        a = jnp.dot(xb_ref[:, :half], wg_ref[0, :half], preferred_element_type=F32)
        add_rows(j - 1, rows[1])
        a = a + jnp.dot(xb_ref[:, half:], wg_ref[0, half:], preferred_element_type=F32)
        add_rows(j - 1, rows[2])
        b = jnp.dot(xb_ref[:, :half], wu_ref[0, :half], preferred_element_type=F32)
        add_rows(j - 1, rows[3])
        b = b + jnp.dot(xb_ref[:, half:], wu_ref[0, half:], preferred_element_type=F32)
        act = (jax.nn.silu(a) * b * gate).astype(BF16)
        n_col = D_MODEL // n_slice
        for piece in range(n_slice):
            fetch_rows(j + 1, rows[piece])
            out = jnp.dot(act, wd_ref[0, :, piece * n_col:(piece + 1) * n_col],
                          preferred_element_type=F32)
            for cc in range(n_col // LANES):
                c = piece * (n_col // LANES) + cc
                ot_ref[c * OT_PITCH:c * OT_PITCH + tm, :] = out[:, cc * LANES:(cc + 1) * LANES]
        return carry

    lax.fori_loop(1, nt + 1, stage, 0)

    @pl.when(s % N_EXPERTS == N_EXPERTS - 1)
    def _():
        add_rows(nt, all_rows)


def _dispatch_tables(eid_t, counts, n_tok):
    tm, ts = MOE_TILE, MOE_BLOCK_TOKENS
    ng = MOE_BLOCKS * N_EXPERTS
    assert ts < (1 << TOKEN_BITS)
    tok = jnp.arange(n_tok, dtype=jnp.int32)
    grp = (tok // ts)[None, :] * N_EXPERTS + eid_t
    gids = jnp.arange(ng, dtype=jnp.int32)
    npad = (-counts) % tm
    fill = jnp.arange(tm, dtype=jnp.int32)[None, :] < npad[:, None]
    keys = jnp.concatenate([
        ((grp << TOKEN_BITS) | (tok % ts)[None, :]).reshape(-1),
        ((jnp.where(fill, gids[:, None], ng) << TOKEN_BITS) | ts).reshape(-1)])
    toks = lax.sort(keys, dimension=0, is_stable=False) & ((1 << TOKEN_BITS) - 1)
    n_tiles = (n_tok * TOP_K + ng * tm) // tm

    ntile = (counts + npad) // tm
    tile_start = jnp.cumsum(ntile) - ntile
    src = jnp.where(toks < ts, toks, 0) * PACK_ROWS
    dst = toks * ACC_ROWS
    head = jnp.zeros((tm,), jnp.int32)
    tail = jnp.zeros((GROUP_TILES * tm,), jnp.int32)
    rows = n_tiles + 1 + GROUP_TILES
    return {"t0": tile_start.astype(jnp.int32), "nt": ntile.astype(jnp.int32),
            "src": jnp.concatenate([head, src, tail]).reshape(rows, 1, tm),
            "dst": jnp.concatenate([head, dst, tail]).reshape(rows, 1, tm)}


def _moe(hp, meta, tabs, ep):
    tm, ts = MOE_TILE, MOE_BLOCK_TOKENS
    assert GROUP_TILES >= -(-ts // tm) + 2
    window = pl.BlockSpec((pl.Element(GROUP_TILES), pl.Element(1), pl.Element(tm)),
                          lambda s, t0, nt: (t0[s], 0, 0), memory_space=pltpu.SMEM)
    wspec = lambda shape: pl.BlockSpec((1,) + shape, lambda s, t0, nt: (s % N_EXPERTS, 0, 0))
    blk = lambda s, t0, nt: (s // N_EXPERTS, 0)
    acc_rows = (ts + SUBLANES) * ACC_ROWS
    assert acc_rows % ZERO_ROWS == 0
    stage_bufs = [pltpu.VMEM((PACK_ROWS * XT_PITCH, LANES), F32),
                  pltpu.VMEM((META_ROWS * XT_PITCH, LANES), F32),
                  pltpu.VMEM((ACC_ROWS * OT_PITCH, LANES), F32),
                  pltpu.VMEM((tm, D_MODEL), BF16),
                  pltpu.VMEM((1, D_MODEL, D_EXPERT), BF16),
                  pltpu.VMEM((1, D_MODEL, D_EXPERT), BF16),
                  pltpu.VMEM((1, D_EXPERT, D_MODEL), BF16)]
    grid_spec = pltpu.PrefetchScalarGridSpec(
        num_scalar_prefetch=2,
        grid=(MOE_BLOCKS * N_EXPERTS,),
        in_specs=[window, window,
                  pl.BlockSpec((ts * PACK_ROWS, LANES), blk, pipeline_mode=pl.Buffered(1)),
                  pl.BlockSpec((ts * META_ROWS, LANES), blk, pipeline_mode=pl.Buffered(1)),
                  wspec((D_MODEL, D_EXPERT)), wspec((D_MODEL, D_EXPERT)),
                  wspec((D_EXPERT, D_MODEL))],
        out_specs=pl.BlockSpec((1, acc_rows, LANES), lambda s, t0, nt: (s // N_EXPERTS, 0, 0),
                               pipeline_mode=pl.Buffered(1)),
        scratch_shapes=stage_bufs,
    )
    return pl.pallas_call(
        _moe_kernel,
        grid_spec=grid_spec,
        out_shape=jax.ShapeDtypeStruct((MOE_BLOCKS, acc_rows, LANES), F32),
        compiler_params=_cparams("arbitrary"),
        name="moe_experts",
    )(tabs["t0"], tabs["nt"], tabs["src"], tabs["dst"], hp, meta, ep["wg"], ep["wu"], ep["wd"])


def _final_kernel(xs_ref, r_ref, fn_ref, y_ref):
    tm = xs_ref.shape[0]
    routed = jnp.concatenate(
        [r_ref[0, pl.ds(c, tm, stride=ACC_ROWS), :] for c in range(ACC_ROWS)], axis=1)
    y_ref[...] = _rms(xs_ref[...] + routed, fn_ref[...])


def _finalize(xs, routed, final_norm, tm, tok0):
    rows = xs.shape[0]
    per_block = MOE_BLOCK_TOKENS // tm
    first = tok0 // tm
    assert tok0 % tm == 0 and MOE_BLOCK_TOKENS % tm == 0
    return pl.pallas_call(
        _final_kernel,
        grid=(rows // tm,),
        in_specs=[pl.BlockSpec((tm, D_MODEL), lambda i: (i, 0)),
                  pl.BlockSpec((1, tm * ACC_ROWS, LANES),
                               lambda i: ((first + i) // per_block, (first + i) % per_block, 0)),
                  _const_spec((1, D_MODEL))],
        out_specs=pl.BlockSpec((tm, D_MODEL), lambda i: (i, 0)),
        out_shape=jax.ShapeDtypeStruct((rows, D_MODEL), F32),
        compiler_params=_cparams("parallel"),
        name="final_norm",
    )(xs, routed, final_norm)


def _rope_tables(pos):
    half = RET_HEAD_DIM // 2
    inv_freq = ROPE_BASE ** (-jnp.arange(half, dtype=F32) / half)
    ang = pos.astype(F32)[:, None] * inv_freq[None, :]
    cos, sin = jnp.cos(ang), jnp.sin(ang)
    cos_t = jnp.concatenate([cos, cos, cos, cos], axis=1)
    sin_t = jnp.concatenate([-sin, sin, -sin, sin], axis=1)
    return cos_t, sin_t


def kernel(x_prompt, x_sample, state_s5_re, state_s5_im, state_ret, meta_tokens, norm1, w_in,
           s5_lam_re, s5_lam_im, s5_log_dt, s5_b_re, s5_b_im, s5_c_re, s5_c_im, s5_d, s5_w_glu,
           s5_norm, ret_norm, w_out, norm2, w_router, router_bias, w_gate_e, w_up_e, w_down_e,
           w_gate_sh, w_up_sh, w_down_sh, final_norm):
    assert norm1.shape[0] == 1, "single-layer model"
    bp, seq, _ = x_prompt.shape
    ns = x_sample.shape[0]
    l = 0

    n1 = norm1[l].reshape(1, D_MODEL)
    w_in_b = w_in[l].astype(BF16)
    s5p = _s5_params(s5_lam_re[l], s5_lam_im[l], s5_log_dt[l], s5_b_re[l], s5_b_im[l],
                     s5_c_re[l], s5_c_im[l], s5_d[l], s5_w_glu[l], s5_norm[l])
    rnorm = ret_norm[l].reshape(1, D_RET)
    w_out_b = w_out[l].astype(BF16)
    wr_t = w_router[l].T
    wr_hi = wr_t.astype(BF16)
    wr_lo = (wr_t - wr_hi.astype(F32)).astype(BF16)
    mp = {
        "woa": w_out_b[:D_S5], "wob": w_out_b[D_S5:], "n2": norm2[l].reshape(1, D_MODEL),
        "wrt": jnp.stack([wr_hi, wr_lo]),
        "rb": jnp.broadcast_to(router_bias[l][:, None], (N_EXPERTS, LANES)),
        "wgs": w_gate_sh[l].astype(BF16), "wus": w_up_sh[l].astype(BF16),
        "wds": w_down_sh[l].astype(BF16),
    }
    ep = {"wg": w_gate_e[l], "wu": w_up_e[l],
          "wd": w_down_e[l]}
    fnorm = final_norm.reshape(1, D_MODEL)

    xp = x_prompt.reshape(bp * seq, D_MODEL)
    cos_p, sin_p = _rope_tables(N_META + jnp.arange(seq, dtype=jnp.int32))
    tm_a = 512
    up, qp, kp, vp, gp = _project(xp, n1, w_in_b, cos_p, sin_p, tm_a, seq // tm_a)

    meta_tm = jnp.repeat(meta_tokens, SUBLANES, axis=0)
    mchunk = LANES
    meta_chunk = jnp.concatenate([jnp.zeros((mchunk - N_META, D_MODEL), F32), meta_tokens], axis=0)
    x_small = jnp.concatenate([x_sample.reshape(ns, D_MODEL), meta_tm, meta_chunk], axis=0)
    meta_pos = jnp.arange(N_META, dtype=jnp.int32)
    pos_small = jnp.concatenate([jnp.full((ns,), PAST_LEN, jnp.int32),
                                 jnp.repeat(meta_pos, SUBLANES),
                                 jnp.zeros((mchunk - N_META,), jnp.int32), meta_pos])
    cos_s, sin_s = _rope_tables(pos_small)
    n_small = x_small.shape[0]
    n_tm = ns + N_META * SUBLANES
    us, qs, ks, vs, gs = _project(x_small, n1, w_in_b, cos_s, sin_s, n_small, 1)

    zero8 = jnp.zeros((SUBLANES, S5_LANES), F32)
    _, m_re, m_im = _s5(us[ns:n_tm], zero8, zero8, s5p, nb=SUBLANES, tt=N_META, nblk=1,
                        bt_major=False)
    tt = 64
    ys5_p, p_re, p_im = _s5(up.reshape(bp, seq, D_S5), m_re, m_im, s5p, nb=bp, tt=tt,
                            nblk=seq // tt, bt_major=True)
    ys5_p = ys5_p.reshape(bp * seq, D_S5)
    ys5_s, s_re, s_im = _s5(us[:ns], state_s5_re[l].reshape(ns, S5_LANES),
                            state_s5_im[l].reshape(ns, S5_LANES), s5p, nb=ns, tt=1, nblk=1,
                            bt_major=False)

    zero_pair = jnp.zeros((1, N_HEAD_PAIRS, LANES, LANES), F32)
    _, m_pair = _retention(qs[n_tm:], ks[n_tm:], vs[n_tm:], gs[n_tm:],
                           zero_pair, rnorm, nseq=1, chunk=mchunk, nchunk=1)
    chunk = 256
    yret_p, p_pair = _retention(qp, kp, vp, gp, jnp.broadcast_to(m_pair, (bp,) + m_pair.shape[1:]),
                                rnorm, nseq=bp, chunk=chunk, nchunk=seq // chunk)
    yret_s, ret_s = _retention_step(qs[:ns], ks[:ns], vs[:ns], gs[:ns], state_ret[l], ret_norm[l])

    n_prompt = bp * seq
    assert n_prompt + ns <= MOE_BLOCKS * MOE_BLOCK_TOKENS
    n_moe = MOE_BLOCKS * MOE_BLOCK_TOKENS
    tm_mix = 512
    xs_p, hp, meta, eid_p, cnt_p = _mix(xp, ys5_p, yret_p, mp, tm_mix, n_moe)
    xs_s, hp_s, meta_s, eid_s, cnt_s = _mix(x_sample.reshape(ns, D_MODEL), ys5_s, yret_s, mp,
                                            ns, ns)
    hp = lax.dynamic_update_slice(hp, hp_s, (n_prompt * PACK_ROWS, 0))
    meta = lax.dynamic_update_slice(meta, meta_s, (n_prompt * META_ROWS, 0))
    blk_s = n_prompt // MOE_BLOCK_TOKENS
    assert MOE_BLOCK_TOKENS % tm_mix == 0 and (n_prompt + ns - 1) // MOE_BLOCK_TOKENS == blk_s
    counts = cnt_p[:, :, 0].reshape(MOE_BLOCKS, MOE_BLOCK_TOKENS // tm_mix, N_EXPERTS).sum(axis=1)
    counts = counts.at[blk_s].add(cnt_s[0, :, 0]).reshape(-1).astype(jnp.int32)
    tabs = _dispatch_tables(jnp.concatenate([eid_p, eid_s], axis=1), counts, n_prompt + ns)
    routed = _moe(hp, meta, tabs, ep)
    y_p = _finalize(xs_p, routed, fnorm, 512, 0)
    y_s = _finalize(xs_s, routed, fnorm, ns, n_prompt)

    shape5 = (1, bp, N_S5_GROUPS, S5_STATE)
    return (y_p.reshape(bp, seq, D_MODEL),
            y_s.reshape(ns, 1, D_MODEL),
            p_re.reshape(shape5), p_im.reshape(shape5),
            _unpair_state(p_pair)[None],
            s_re.reshape(1, ns, N_S5_GROUPS, S5_STATE), s_im.reshape(1, ns, N_S5_GROUPS, S5_STATE),
            ret_s[None])
```

```python
import functools
import math

import jax
import jax.numpy as jnp
from jax import lax
from jax.experimental import pallas as pl
from jax.experimental.pallas import tpu as pltpu

F32 = jnp.float32
BF16 = jnp.bfloat16

D_MODEL = 1024
N_META = 16
PAST_LEN = 16384
D_S5 = 512
S5_GROUP = 16
N_S5_GROUPS = 32
S5_STATE = 64
S5_LANES = N_S5_GROUPS * S5_STATE
D_RET = 512
N_RET_HEADS = 8
RET_HEAD_DIM = 64
N_HEAD_PAIRS = 4
ROPE_BASE = 10000.0
D_IN = D_S5 + 4 * D_RET
N_EXPERTS = 64
TOP_K = 8
N_EXPERT_GROUPS = 8
GROUP_SIZE = 8
TOPK_GROUPS = 4
D_EXPERT = 256
ROUTED_SCALE = 2.5
EPS = 1e-6

LANES = 128
SUBLANES = 8
VMEM_LIMIT = 56 * 1024 * 1024

PACK_ROWS = D_MODEL // LANES
ACC_ROWS = D_MODEL // LANES
MOE_BLOCKS = 4
MOE_BLOCK_TOKENS = 4608
MOE_TILE = 256
GROUP_TILES = MOE_BLOCK_TOKENS // MOE_TILE + 2
ADD_UNROLL = 8
OT_PITCH = MOE_TILE + SUBLANES
XT_PITCH = MOE_TILE + SUBLANES
META_ROWS = 2
META_SHIFT = 2
assert PACK_ROWS == META_ROWS << META_SHIFT
TOKEN_BITS = 13
ZERO_ROWS = 64
CAST_ROWS = 128


def _cparams(*sem):
    return pltpu.CompilerParams(dimension_semantics=sem, vmem_limit_bytes=VMEM_LIMIT)


def _const_spec(shape):
    nd = len(shape)
    return pl.BlockSpec(shape, lambda *_: (0,) * nd)


def _rms(x, gain):
    return x * lax.rsqrt(jnp.mean(x * x, axis=-1, keepdims=True) + EPS) * gain


def _proj_kernel(x_ref, n1_ref, w_ref, cos_ref, sin_ref, u_ref, q_ref, k_ref, v_ref, g_ref):
    h = _rms(x_ref[...], n1_ref[...]).astype(BF16)
    proj = jnp.dot(h, w_ref[...], preferred_element_type=F32)
    cos = cos_ref[...]
    sin = sin_ref[...]
    lane = lax.broadcasted_iota(jnp.int32, cos.shape, 1)
    first_half = (lane % RET_HEAD_DIM) < (RET_HEAD_DIM // 2)

    def rotary(t):
        partner = jnp.where(first_half,
                            pltpu.roll(t, LANES - RET_HEAD_DIM // 2, 1),
                            pltpu.roll(t, RET_HEAD_DIM // 2, 1))
        return t * cos + partner * sin

    u_ref[...] = proj[:, :D_S5]
    for j in range(N_HEAD_PAIRS):
        lo = D_S5 + j * LANES
        q_ref[:, j * LANES:(j + 1) * LANES] = rotary(proj[:, lo:lo + LANES]).astype(BF16)
        lo += D_RET
        k_ref[:, j * LANES:(j + 1) * LANES] = (
            rotary(proj[:, lo:lo + LANES]) * (RET_HEAD_DIM ** -0.5)).astype(BF16)
    v_ref[...] = proj[:, D_S5 + 2 * D_RET:D_S5 + 3 * D_RET].astype(BF16)
    g_ref[...] = proj[:, D_S5 + 3 * D_RET:]


def _project(x, norm1, w_in_b, cos_t, sin_t, tm, table_blocks):
    rows = x.shape[0]
    row = lambda i: (i, 0)
    tab = lambda i: (i % table_blocks, 0)
    return pl.pallas_call(
        _proj_kernel,
        grid=(rows // tm,),
        in_specs=[pl.BlockSpec((tm, D_MODEL), row),
                  _const_spec((1, D_MODEL)),
                  _const_spec((D_MODEL, D_IN)),
                  pl.BlockSpec((tm, LANES), tab),
                  pl.BlockSpec((tm, LANES), tab)],
        out_specs=[pl.BlockSpec((tm, D_S5), row),
                   pl.BlockSpec((tm, D_RET), row),
                   pl.BlockSpec((tm, D_RET), row),
                   pl.BlockSpec((tm, D_RET), row),
                   pl.BlockSpec((tm, D_RET), row)],
        out_shape=[jax.ShapeDtypeStruct((rows, D_S5), F32),
                   jax.ShapeDtypeStruct((rows, D_RET), BF16),
                   jax.ShapeDtypeStruct((rows, D_RET), BF16),
                   jax.ShapeDtypeStruct((rows, D_RET), BF16),
                   jax.ShapeDtypeStruct((rows, D_RET), F32)],
        compiler_params=_cparams("parallel"),
        name="in_proj",
    )(x, norm1, w_in_b, cos_t, sin_t)


def _s5_kernel(u_ref, x0re_ref, x0im_ref, are_ref, aim_ref, b_ref, c_ref, d_ref, wglu_ref,
               nrm_ref, y_ref, sre_ref, sim_ref, st, sre, sim, utm, ytm, *, nb, tt, bt_major):
    i = pl.program_id(0)
    rows = nb * tt

    @pl.when(i == 0)
    def _():
        sre[...] = x0re_ref[...]
        sim[...] = x0im_ref[...]

    if bt_major:
        for t in range(tt):
            utm[t * nb:(t + 1) * nb, :] = u_ref[:, t, :]
    else:
        utm[...] = u_ref[...]

    ub = utm[...].astype(BF16)
    half = S5_LANES // 2
    kh = D_S5 // 2
    for part in range(2):
        for hf in range(2):
            st[:, part * S5_LANES + hf * half:part * S5_LANES + (hf + 1) * half] = jnp.dot(
                ub[:, hf * kh:(hf + 1) * kh], b_ref[part, hf], preferred_element_type=F32)

    lw = 512 if nb == SUBLANES else LANES
    for lg in range(S5_LANES // lw):
        re = slice(lg * lw, (lg + 1) * lw)
        im = slice(S5_LANES + lg * lw, S5_LANES + (lg + 1) * lw)
        a_re = are_ref[:, re]
        a_im = aim_ref[:, re]

        def step(t, carry):
            s_re, s_im = carry
            r0 = pl.multiple_of(t * nb, nb)
            n_re = a_re * s_re - a_im * s_im + st[pl.ds(r0, nb), re]
            n_im = a_re * s_im + a_im * s_re + st[pl.ds(r0, nb), im]
            st[pl.ds(r0, nb), re] = n_re
            st[pl.ds(r0, nb), im] = n_im
            return n_re, n_im

        if nb == SUBLANES:
            f_re, f_im = sre[:, re], sim[:, re]
            for t in range(tt):
                r = slice(t * nb, (t + 1) * nb)
                n_re = a_re * f_re - a_im * f_im + st[r, re]
                n_im = a_re * f_im + a_im * f_re + st[r, im]
                st[r, re] = n_re
                st[r, im] = n_im
                f_re, f_im = n_re, n_im
        else:
            f_re, f_im = lax.fori_loop(0, tt, step, (sre[:, re], sim[:, re]))
        sre[:, re] = f_re
        sim[:, re] = f_im

    sre_ref[...] = sre[...]
    sim_ref[...] = sim[...]

    ys = []
    for hf in range(2):
        xr = st[:, hf * half:(hf + 1) * half].astype(BF16)
        xi = st[:, S5_LANES + hf * half:S5_LANES + (hf + 1) * half].astype(BF16)
        ys.append(jnp.dot(xr, c_ref[0, hf], preferred_element_type=F32)
                  - jnp.dot(xi, c_ref[1, hf], preferred_element_type=F32))
    y = jnp.concatenate(ys, axis=1) + d_ref[...] * utm[...]
    y = jax.nn.gelu(y)
    y = y * jax.nn.sigmoid(jnp.dot(y.astype(BF16), wglu_ref[...], preferred_element_type=F32))
    y = _rms(y, nrm_ref[...]).astype(BF16)
    if bt_major:
        yf = y.astype(F32)
        for j in range(D_S5 // LANES):
            ytm[j] = yf[:, j * LANES:(j + 1) * LANES]
        for b in range(nb):
            for j in range(D_S5 // LANES):
                y_ref[b, :, j * LANES:(j + 1) * LANES] = (
                    ytm[j, pl.ds(b, tt, stride=nb), :].astype(BF16))
    else:
        y_ref[...] = y


def _s5(u, x0re, x0im, s5p, *, nb, tt, nblk, bt_major):
    rows = nb * tt
    if bt_major:
        u_spec = pl.BlockSpec((nb, tt, D_S5), lambda i: (0, i, 0))
        y_shape = jax.ShapeDtypeStruct((nb, tt * nblk, D_S5), BF16)
    else:
        u_spec = pl.BlockSpec((rows, D_S5), lambda i: (i, 0))
        y_shape = jax.ShapeDtypeStruct((rows * nblk, D_S5), BF16)
    state_spec = _const_spec((nb, S5_LANES))
    are = jnp.broadcast_to(s5p["are"], (nb, S5_LANES))
    aim = jnp.broadcast_to(s5p["aim"], (nb, S5_LANES))
    return pl.pallas_call(
        functools.partial(_s5_kernel, nb=nb, tt=tt, bt_major=bt_major),
        grid=(nblk,),
        in_specs=[u_spec, state_spec, state_spec, state_spec, state_spec,
                  _const_spec((2, 2, D_S5 // 2, S5_LANES // 2)),
                  _const_spec((2, 2, S5_LANES // 2, D_S5 // 2)),
                  _const_spec((1, D_S5)),
                  _const_spec((D_S5, D_S5)),
                  _const_spec((1, D_S5))],
        out_specs=[u_spec, state_spec, state_spec],
        out_shape=[y_shape,
                   jax.ShapeDtypeStruct((nb, S5_LANES), F32),
                   jax.ShapeDtypeStruct((nb, S5_LANES), F32)],
        scratch_shapes=[pltpu.VMEM((rows, 2 * S5_LANES), F32),
                        pltpu.VMEM((nb, S5_LANES), F32),
                        pltpu.VMEM((nb, S5_LANES), F32),
                        pltpu.VMEM((rows, D_S5), F32),
                        pltpu.VMEM((D_S5 // LANES, rows, LANES), F32)],
        compiler_params=_cparams("arbitrary"),
        name="s5_" + ("bt" if bt_major else "tm") + str(nb),
    )(u, x0re, x0im, are, aim, s5p["b"], s5p["c"], s5p["d"], s5p["wglu"], s5p["nrm"])


def _s5_params(lam_re, lam_im, log_dt, b_re, b_im, c_re, c_im, d_skip, w_glu, nrm):
    dt = jnp.exp(log_dt)[:, None]
    mag = jnp.exp(lam_re * dt)
    abar_re, abar_im = mag * jnp.cos(lam_im * dt), mag * jnp.sin(lam_im * dt)
    num_re, num_im = abar_re - 1.0, abar_im
    den = lam_re * lam_re + lam_im * lam_im
    f_re = (num_re * lam_re + num_im * lam_im) / den
    f_im = (num_im * lam_re - num_re * lam_im) / den
    bbar_re = f_re[..., None] * b_re - f_im[..., None] * b_im
    bbar_im = f_re[..., None] * b_im + f_im[..., None] * b_re
    hg = N_S5_GROUPS // 2
    eye = jnp.eye(hg, dtype=F32)

    def bdiag(bb):
        bb = bb.reshape(2, hg, S5_STATE, S5_GROUP)
        return jnp.einsum("zgph,gk->zghkp", bb, eye).reshape(2, hg * S5_GROUP, hg * S5_STATE)

    def cdiag(cc):
        cc = cc.reshape(2, hg, S5_GROUP, S5_STATE)
        return jnp.einsum("zgnp,gk->zgpkn", cc, eye).reshape(2, hg * S5_STATE, hg * S5_GROUP)

    return {
        "are": abar_re.reshape(1, S5_LANES), "aim": abar_im.reshape(1, S5_LANES),
        "b": jnp.stack([bdiag(bbar_re), bdiag(bbar_im)]).astype(BF16),
        "c": jnp.stack([cdiag(c_re), cdiag(c_im)]).astype(BF16),
        "d": d_skip.reshape(1, D_S5), "wglu": w_glu.astype(BF16), "nrm": nrm.reshape(1, D_S5),
    }


def _head_norm_gate(o, g, gain, lo):
    inv = 1.0 / RET_HEAD_DIM

    def seg_mean(t):
        s_lo = jnp.sum(jnp.where(lo, t, 0.0), axis=1, keepdims=True)
        s_hi = jnp.sum(jnp.where(lo, 0.0, t), axis=1, keepdims=True)
        return jnp.where(lo, s_lo, s_hi) * inv

    dlt = o - seg_mean(o)
    var = seg_mean(dlt * dlt)
    return jax.nn.silu(g) * (dlt * lax.rsqrt(var + EPS) * gain)


def _ret_kernel(q_ref, k_ref, v_ref, g_ref, s0_ref, dm_ref, qd_ref, kd_ref, gc_ref, bm_ref,
                nrm_ref, y_ref, so_ref, s_acc):
    c = pl.program_id(1)

    @pl.when(c == 0)
    def _():
        s_acc[...] = s0_ref[0]

    rows = q_ref.shape[0]
    lane = lax.broadcasted_iota(jnp.int32, (rows, LANES), 1)
    lo = lane < RET_HEAD_DIM
    for j in range(N_HEAD_PAIRS):
        sl = slice(j * LANES, (j + 1) * LANES)
        q2 = q_ref[:, sl].astype(F32)
        k2 = k_ref[:, sl]
        v2 = v_ref[:, sl]
        s_pair = s_acc[j]
        cross = jnp.dot((q2 * qd_ref[:, sl]).astype(BF16), s_pair.astype(BF16),
                        preferred_element_type=F32)
        k_dec = (k2.astype(F32) * kd_ref[:, sl]).astype(BF16)
        upd = lax.dot_general(k_dec, v2, (((0,), (0,)), ((), ())), preferred_element_type=F32)
        s_acc[j] = gc_ref[j] * s_pair + bm_ref[...] * upd
        inner = []
        for hh, qh in enumerate((jnp.where(lo, q2, 0.0), jnp.where(lo, 0.0, q2))):
            sc = lax.dot_general(qh.astype(BF16), k2, (((1,), (1,)), ((), ())),
                                 preferred_element_type=F32) * dm_ref[2 * j + hh]
            inner.append(jnp.dot(sc.astype(BF16), v2, preferred_element_type=F32))
        o = jnp.where(lo, inner[0], inner[1]) + cross
        y_ref[:, sl] = _head_norm_gate(o, g_ref[:, sl], nrm_ref[:, sl], lo).astype(BF16)
    so_ref[0] = s_acc[...]


def _ret_tables(chunk):
    log_g = jnp.log1p(-jnp.exp2(-5.0 - jnp.arange(N_RET_HEADS, dtype=F32)))
    n = jnp.arange(chunk, dtype=F32)
    diff = n[:, None] - n[None, :]
    dm = jnp.where(diff[None] >= 0.0,
                   jnp.exp(log_g[:, None, None] * jnp.maximum(diff, 0.0)[None]), 0.0)
    per_lane = lambda t: jnp.repeat(t.T, RET_HEAD_DIM, axis=1)
    qd = per_lane(jnp.exp(log_g[:, None] * (n + 1.0)[None]))
    kd = per_lane(jnp.exp(log_g[:, None] * (chunk - 1.0 - n)[None]))
    gch = jnp.exp(log_g * chunk)
    blk = jnp.kron(jnp.eye(2, dtype=F32), jnp.ones((RET_HEAD_DIM, RET_HEAD_DIM), F32))
    gc = jnp.repeat(gch.reshape(N_HEAD_PAIRS, 2), RET_HEAD_DIM, axis=1)[:, :, None] * blk[None]
    return dm, qd, kd, gc, blk


def _pair_state(s):
    b = s.shape[0]
    s = s.reshape(b, N_HEAD_PAIRS, 2, RET_HEAD_DIM, RET_HEAD_DIM)
    z = jnp.zeros_like(s[:, :, 0])
    top = jnp.concatenate([s[:, :, 0], z], axis=-1)
    bot = jnp.concatenate([z, s[:, :, 1]], axis=-1)
    return jnp.concatenate([top, bot], axis=-2)


def _unpair_state(s2):
    d = RET_HEAD_DIM
    return jnp.stack([s2[:, :, :d, :d], s2[:, :, d:, d:]], axis=2).reshape(
        s2.shape[0], N_RET_HEADS, d, d)


def _retention(q, k, v, g, s0_pair, ret_norm, *, nseq, chunk, nchunk):
    dm, qd, kd, gc, blk = _ret_tables(chunk)
    row = lambda b, c: (b * nchunk + c, 0)
    st_spec = pl.BlockSpec((1, N_HEAD_PAIRS, LANES, LANES), lambda b, c: (b, 0, 0, 0))
    blk_spec = pl.BlockSpec((chunk, D_RET), row)
    return pl.pallas_call(
        _ret_kernel,
        grid=(nseq, nchunk),
        in_specs=[blk_spec, blk_spec, blk_spec, blk_spec, st_spec,
                  _const_spec((N_RET_HEADS, chunk, chunk)),
                  _const_spec((chunk, D_RET)), _const_spec((chunk, D_RET)),
                  _const_spec((N_HEAD_PAIRS, LANES, LANES)), _const_spec((LANES, LANES)),
                  _const_spec((1, D_RET))],
        out_specs=[blk_spec, st_spec],
        out_shape=[jax.ShapeDtypeStruct((nseq * nchunk * chunk, D_RET), BF16),
                   jax.ShapeDtypeStruct((nseq, N_HEAD_PAIRS, LANES, LANES), F32)],
        scratch_shapes=[pltpu.VMEM((N_HEAD_PAIRS, LANES, LANES), F32)],
        compiler_params=_cparams("arbitrary", "arbitrary"),
        name="retention_c%d" % chunk,
    )(q, k, v, g, s0_pair, dm, qd, kd, gc, blk, ret_norm)


def _ret_step_kernel(q_ref, k_ref, v_ref, g_ref, s_ref, gam_ref, nrm_ref, y_ref, so_ref):
    q = q_ref[...]
    k = k_ref[...]
    v = v_ref[...]
    gam = gam_ref[0]
    score = jnp.sum(q * k, axis=0, keepdims=True)
    q_dec = (q * gam).astype(BF16).astype(F32)
    cross = jnp.zeros(v.shape, F32)
    for d in range(RET_HEAD_DIM):
        s_d = s_ref[0, d]
        cross = cross + q_dec[d:d + 1, :] * s_d
        so_ref[0, d] = gam * s_d + k[d:d + 1, :] * v
    o = score.astype(BF16).astype(F32) * v + cross
    dlt = o - jnp.mean(o, axis=0, keepdims=True)
    var = jnp.mean(dlt * dlt, axis=0, keepdims=True)
    y_ref[...] = jax.nn.silu(g_ref[...]) * (dlt * lax.rsqrt(var + EPS) * nrm_ref[...])


def _retention_step(q, k, v, g, state, ret_norm):
    n = q.shape[0]
    gam = jnp.exp(jnp.log1p(-jnp.exp2(-5.0 - jnp.arange(N_RET_HEADS, dtype=F32))))
    gam_tab = jnp.broadcast_to(gam[:, None, None], (N_RET_HEADS, 1, n))
    seq_minor = lambda t: t.astype(F32).T
    head = lambda h: (h, 0)
    vec_spec = pl.BlockSpec((RET_HEAD_DIM, n), head)
    st_spec = pl.BlockSpec((1, RET_HEAD_DIM, RET_HEAD_DIM, n), lambda h: (h, 0, 0, 0))
    y_t, s_new = pl.pallas_call(
        _ret_step_kernel,
        grid=(N_RET_HEADS,),
        in_specs=[vec_spec, vec_spec, vec_spec, vec_spec, st_spec,
                  pl.BlockSpec((1, 1, n), lambda h: (h, 0, 0)), vec_spec],
        out_specs=[vec_spec, st_spec],
        out_shape=[jax.ShapeDtypeStruct((D_RET, n), F32),
                   jax.ShapeDtypeStruct((N_RET_HEADS, RET_HEAD_DIM, RET_HEAD_DIM, n), F32)],
        compiler_params=_cparams("parallel"),
        name="retention_step",
    )(seq_minor(q), seq_minor(k), seq_minor(v), seq_minor(g),
      jnp.transpose(state, (1, 2, 3, 0)), gam_tab,
      jnp.broadcast_to(ret_norm[:, None], (D_RET, n)))
    return y_t.T.astype(BF16), jnp.transpose(s_new, (3, 0, 1, 2))


def _first_max(vals, idxs, sentinel):
    m = jnp.max(vals, axis=0, keepdims=True)
    first = jnp.min(jnp.where(vals == m, idxs, sentinel), axis=0, keepdims=True)
    return m, idxs == first


def _route(scores, sel):
    t = sel.shape[1]
    neg = -jnp.inf
    member = lax.broadcasted_iota(jnp.int32, (GROUP_SIZE, t), 0).astype(F32)
    groups = [sel[g * GROUP_SIZE:(g + 1) * GROUP_SIZE, :] for g in range(N_EXPERT_GROUPS)]
    gscore = []
    for grp in groups:
        m1, pick = _first_max(grp, member, float(GROUP_SIZE))
        m2 = jnp.max(jnp.where(pick, neg, grp), axis=0, keepdims=True)
        gscore.append(m1 + m2)
    gs = jnp.concatenate(gscore, axis=0)
    gkeep = jnp.zeros(gs.shape, F32)
    for _ in range(TOPK_GROUPS):
        _, pick = _first_max(gs, member, float(N_EXPERT_GROUPS))
        gkeep = jnp.where(pick, 1.0, gkeep)
        gs = jnp.where(pick, neg, gs)
    cand = jnp.concatenate(
        [jnp.where(gkeep[g:g + 1, :] > 0.5, groups[g], neg) for g in range(N_EXPERT_GROUPS)],
        axis=0)
    expert = lax.broadcasted_iota(jnp.int32, (N_EXPERTS, t), 0).astype(F32)
    ids, ws = [], []
    chosen = jnp.zeros(cand.shape, F32)
    for _ in range(TOP_K):
        _, pick = _first_max(cand, expert, float(N_EXPERTS))
        ids.append(jnp.sum(jnp.where(pick, expert, 0.0), axis=0, keepdims=True))
        ws.append(jnp.sum(jnp.where(pick, scores, 0.0), axis=0, keepdims=True))
        chosen = jnp.where(pick, 1.0, chosen)
        cand = jnp.where(pick, neg, cand)
    w = jnp.concatenate(ws, axis=0)
    gates = w / jnp.sum(w, axis=0, keepdims=True) * ROUTED_SCALE
    return jnp.concatenate(ids, axis=0), gates, jnp.sum(chosen, axis=1, keepdims=True)


def _mix_kernel(x_ref, ys_ref, yr_ref, woa_ref, wob_ref, n2_ref, wrt_ref, rb_ref,
                wgs_ref, wus_ref, wds_ref, xs_ref, hp_ref, meta_ref, eid_ref, cnt_ref, *,
                n_blocks):
    @pl.when(pl.program_id(0) >= n_blocks)
    def _():
        hp_ref[...] = jnp.zeros_like(hp_ref)
        meta_ref[...] = jnp.zeros_like(meta_ref)
        cnt_ref[...] = jnp.zeros_like(cnt_ref)

    @pl.when(pl.program_id(0) < n_blocks)
    def _():
        _mix_body(x_ref, ys_ref, yr_ref, woa_ref, wob_ref, n2_ref, wrt_ref, rb_ref,
                  wgs_ref, wus_ref, wds_ref, xs_ref, hp_ref, meta_ref, eid_ref, cnt_ref)


def _mix_body(x_ref, ys_ref, yr_ref, woa_ref, wob_ref, n2_ref, wrt_ref, rb_ref,
              wgs_ref, wus_ref, wds_ref, xs_ref, hp_ref, meta_ref, eid_ref, cnt_ref):
    tm = x_ref.shape[0]
    x2 = (x_ref[...]
          + jnp.dot(ys_ref[...], woa_ref[...], preferred_element_type=F32)
          + jnp.dot(yr_ref[...], wob_ref[...], preferred_element_type=F32))
    h2f = _rms(x2, n2_ref[...])
    h2 = h2f.astype(BF16)
    h2r = h2.astype(F32)
    for c in range(PACK_ROWS):
        hp_ref[pl.ds(c, tm, stride=PACK_ROWS), :] = h2r[:, c * LANES:(c + 1) * LANES]
    h2_lo = (h2f - h2r).astype(BF16)
    nt_dims = (((1,), (1,)), ((), ()))
    logits = (lax.dot_general(wrt_ref[0], h2, nt_dims, preferred_element_type=F32)
              + lax.dot_general(wrt_ref[1], h2, nt_dims, preferred_element_type=F32)
              + lax.dot_general(wrt_ref[0], h2_lo, nt_dims, preferred_element_type=F32))
    scores = jax.nn.sigmoid(logits)
    ids, gates, counts = _route(scores, scores + rb_ref[:, :1])
    eid_ref[...] = ids.astype(jnp.int32)
    cnt_ref[0] = jnp.broadcast_to(counts, (N_EXPERTS, LANES))
    rec = jnp.concatenate([gates, ids, jnp.zeros((LANES - 2 * TOP_K, tm), F32)], axis=0).T
    meta_ref[pl.ds(0, tm, stride=META_ROWS), :] = rec
    for c in range(1, META_ROWS):
        meta_ref[pl.ds(c, tm, stride=META_ROWS), :] = jnp.zeros((tm, LANES), F32)
    a = jnp.dot(h2, wgs_ref[...], preferred_element_type=F32)
    b = jnp.dot(h2, wus_ref[...], preferred_element_type=F32)
    xs_ref[...] = x2 + jnp.dot((jax.nn.silu(a) * b).astype(BF16), wds_ref[...],
                               preferred_element_type=F32)


def _mix(x, ys5, yret, mp, tm, out_tokens):
    rows = x.shape[0]
    n_blocks = rows // tm
    assert out_tokens % tm == 0
    packed = lambda i: (i, 0)
    row = lambda i: (jnp.minimum(i, n_blocks - 1), 0)
    col = lambda i: (0, jnp.minimum(i, n_blocks - 1))
    return pl.pallas_call(
        functools.partial(_mix_kernel, n_blocks=n_blocks),
        grid=(out_tokens // tm,),
        in_specs=[pl.BlockSpec((tm, D_MODEL), row),
                  pl.BlockSpec((tm, D_S5), row), pl.BlockSpec((tm, D_RET), row),
                  _const_spec((D_S5, D_MODEL)), _const_spec((D_RET, D_MODEL)),
                  _const_spec((1, D_MODEL)),
                  _const_spec((2, N_EXPERTS, D_MODEL)), _const_spec((N_EXPERTS, LANES)),
                  _const_spec((D_MODEL, D_EXPERT)), _const_spec((D_MODEL, D_EXPERT)),
                  _const_spec((D_EXPERT, D_MODEL))],
        out_specs=[pl.BlockSpec((tm, D_MODEL), row),
                   pl.BlockSpec((tm * PACK_ROWS, LANES), packed),
                   pl.BlockSpec((tm * META_ROWS, LANES), packed),
                   pl.BlockSpec((TOP_K, tm), col),
                   pl.BlockSpec((1, N_EXPERTS, LANES), lambda i: (i, 0, 0))],
        out_shape=[jax.ShapeDtypeStruct((rows, D_MODEL), F32),
                   jax.ShapeDtypeStruct((out_tokens * PACK_ROWS, LANES), F32),
                   jax.ShapeDtypeStruct((out_tokens * META_ROWS, LANES), F32),
                   jax.ShapeDtypeStruct((TOP_K, rows), jnp.int32),
                   jax.ShapeDtypeStruct((out_tokens // tm, N_EXPERTS, LANES), F32)],
        compiler_params=_cparams("arbitrary"),
        name="out_proj_router",
    )(x, ys5, yret, mp["woa"], mp["wob"], mp["n2"], mp["wrt"], mp["rb"],
      mp["wgs"], mp["wus"], mp["wds"])


def _moe_kernel(t0_ref, nt_ref, src_ref, dst_ref, hp_ref, meta_ref, wg32_ref, wu32_ref, wd32_ref,
                acc_ref, xt_ref, xm_ref, ot_ref, xb_ref, wg_ref, wu_ref, wd_ref):
    s = pl.program_id(0)
    tm = MOE_TILE
    half = D_MODEL // 2
    all_rows = range(tm)
    nt = nt_ref[s]
    expert = (s % N_EXPERTS).astype(F32)

    @pl.when(nt > 0)
    def _():
        for src32, dst16 in ((wg32_ref, wg_ref), (wu32_ref, wu_ref), (wd32_ref, wd_ref)):
            rows = src32.shape[1]
            for i in range(0, rows, CAST_ROWS):
                dst16[0, i:i + CAST_ROWS, :] = src32[0, i:i + CAST_ROWS, :].astype(BF16)

    def fetch_rows(j, rows):
        for m in rows:
            r = pl.multiple_of(src_ref[j, 0, m], PACK_ROWS)
            xt_ref[pl.ds(m, PACK_ROWS, stride=XT_PITCH), :] = hp_ref[pl.ds(r, PACK_ROWS), :]
            xm_ref[pl.ds(m, META_ROWS, stride=XT_PITCH), :] = meta_ref[
                pl.ds(pl.multiple_of(r >> META_SHIFT, META_ROWS), META_ROWS), :]

    def add_rows(j, rows):
        for m0 in range(rows.start, rows.stop, ADD_UNROLL):
            new = []
            for m in range(m0, m0 + ADD_UNROLL):
                base = pl.multiple_of(dst_ref[j, 0, m], ACC_ROWS)
                row = ot_ref[pl.ds(m, ACC_ROWS, stride=OT_PITCH), :]
                new.append((base, acc_ref[0, pl.ds(base, ACC_ROWS), :] + row))
            for base, v in new:
                acc_ref[0, pl.ds(base, ACC_ROWS), :] = v

    @pl.when(s % N_EXPERTS == 0)
    def _():
        zero = jnp.zeros((ZERO_ROWS, LANES), F32)

        def clear(i, carry):
            acc_ref[0, pl.ds(pl.multiple_of(i * ZERO_ROWS, ZERO_ROWS), ZERO_ROWS), :] = zero
            return carry

        lax.fori_loop(0, acc_ref.shape[1] // ZERO_ROWS, clear, 0)
        ot_ref[...] = jnp.zeros_like(ot_ref)
        fetch_rows(1, all_rows)

    def stage(j, carry):
        n_slice = 4
        per = tm // n_slice
        rows = [range(i * per, (i + 1) * per) for i in range(n_slice)]
        for c in range(PACK_ROWS):
            xb_ref[:, c * LANES:(c + 1) * LANES] = (
                xt_ref[c * XT_PITCH:c * XT_PITCH + tm, :].astype(BF16))
        rec = xm_ref[0:tm, :]
        ids = pltpu.roll(rec, LANES - TOP_K, 1)
        lane = lax.broadcasted_iota(jnp.int32, rec.shape, 1)
        gate = jnp.sum(jnp.where((lane < TOP_K) & (ids == expert), rec, 0.0),
                       axis=1, keepdims=True)
        add_rows(j - 1, rows[0])
        a = jnp.dot(xb_ref[:, :half], wg_ref[0, :half], preferred_element_type=F32)
        add_rows(j - 1, rows[1])
        a = a + jnp.dot(xb_ref[:, half:], wg_ref[0, half:], preferred_element_type=F32)
        add_rows(j - 1, rows[2])
        b = jnp.dot(xb_ref[:, :half], wu_ref[0, :half], preferred_element_type=F32)
        add_rows(j - 1, rows[3])
        b = b + jnp.dot(xb_ref[:, half:], wu_ref[0, half:], preferred_element_type=F32)
        act = (jax.nn.silu(a) * b * gate).astype(BF16)
        n_col = D_MODEL // n_slice
        for piece in range(n_slice):
            fetch_rows(j + 1, rows[piece])
            out = jnp.dot(act, wd_ref[0, :, piece * n_col:(piece + 1) * n_col],
                          preferred_element_type=F32)
            for cc in range(n_col // LANES):
                c = piece * (n_col // LANES) + cc
                ot_ref[c * OT_PITCH:c * OT_PITCH + tm, :] = out[:, cc * LANES:(cc + 1) * LANES]
        return carry

    lax.fori_loop(1, nt + 1, stage, 0)

    @pl.when(s % N_EXPERTS == N_EXPERTS - 1)
    def _():
        add_rows(nt, all_rows)


def _dispatch_tables(eid_t, counts, n_tok):
    tm, ts = MOE_TILE, MOE_BLOCK_TOKENS
    ng = MOE_BLOCKS * N_EXPERTS
    assert ts < (1 << TOKEN_BITS)
    tok = jnp.arange(n_tok, dtype=jnp.int32)
    grp = (tok // ts)[None, :] * N_EXPERTS + eid_t
    gids = jnp.arange(ng, dtype=jnp.int32)
    npad = (-counts) % tm
    fill = jnp.arange(tm, dtype=jnp.int32)[None, :] < npad[:, None]
    keys = jnp.concatenate([
        ((grp << TOKEN_BITS) | (tok % ts)[None, :]).reshape(-1),
        ((jnp.where(fill, gids[:, None], ng) << TOKEN_BITS) | ts).reshape(-1)])
    toks = lax.sort(keys, dimension=0, is_stable=False) & ((1 << TOKEN_BITS) - 1)
    n_tiles = (n_tok * TOP_K + ng * tm) // tm

    ntile = (counts + npad) // tm
    tile_start = jnp.cumsum(ntile) - ntile
    src = jnp.where(toks < ts, toks, 0) * PACK_ROWS
    dst = toks * ACC_ROWS
    head = jnp.zeros((tm,), jnp.int32)
    tail = jnp.zeros((GROUP_TILES * tm,), jnp.int32)
    rows = n_tiles + 1 + GROUP_TILES
    return {"t0": tile_start.astype(jnp.int32), "nt": ntile.astype(jnp.int32),
            "src": jnp.concatenate([head, src, tail]).reshape(rows, 1, tm),
            "dst": jnp.concatenate([head, dst, tail]).reshape(rows, 1, tm)}


def _moe(hp, meta, tabs, ep):
    tm, ts = MOE_TILE, MOE_BLOCK_TOKENS
    assert GROUP_TILES >= -(-ts // tm) + 2
    window = pl.BlockSpec((pl.Element(GROUP_TILES), pl.Element(1), pl.Element(tm)),
                          lambda s, t0, nt: (t0[s], 0, 0), memory_space=pltpu.SMEM)
    wspec = lambda shape: pl.BlockSpec((1,) + shape, lambda s, t0, nt: (s % N_EXPERTS, 0, 0))
    blk = lambda s, t0, nt: (s // N_EXPERTS, 0)
    acc_rows = (ts + SUBLANES) * ACC_ROWS
    assert acc_rows % ZERO_ROWS == 0
    stage_bufs = [pltpu.VMEM((PACK_ROWS * XT_PITCH, LANES), F32),
                  pltpu.VMEM((META_ROWS * XT_PITCH, LANES), F32),
                  pltpu.VMEM((ACC_ROWS * OT_PITCH, LANES), F32),
                  pltpu.VMEM((tm, D_MODEL), BF16),
                  pltpu.VMEM((1, D_MODEL, D_EXPERT), BF16),
                  pltpu.VMEM((1, D_MODEL, D_EXPERT), BF16),
                  pltpu.VMEM((1, D_EXPERT, D_MODEL), BF16)]
    grid_spec = pltpu.PrefetchScalarGridSpec(
        num_scalar_prefetch=2,
        grid=(MOE_BLOCKS * N_EXPERTS,),
        in_specs=[window, window,
                  pl.BlockSpec((ts * PACK_ROWS, LANES), blk, pipeline_mode=pl.Buffered(1)),
                  pl.BlockSpec((ts * META_ROWS, LANES), blk, pipeline_mode=pl.Buffered(1)),
                  wspec((D_MODEL, D_EXPERT)), wspec((D_MODEL, D_EXPERT)),
                  wspec((D_EXPERT, D_MODEL))],
        out_specs=pl.BlockSpec((1, acc_rows, LANES), lambda s, t0, nt: (s // N_EXPERTS, 0, 0),
                               pipeline_mode=pl.Buffered(1)),
        scratch_shapes=stage_bufs,
    )
    return pl.pallas_call(
        _moe_kernel,
        grid_spec=grid_spec,
        out_shape=jax.ShapeDtypeStruct((MOE_BLOCKS, acc_rows, LANES), F32),
        compiler_params=_cparams("arbitrary"),
        name="moe_experts",
    )(tabs["t0"], tabs["nt"], tabs["src"], tabs["dst"], hp, meta, ep["wg"], ep["wu"], ep["wd"])


def _final_kernel(xs_ref, r_ref, fn_ref, y_ref):
    tm = xs_ref.shape[0]
    routed = jnp.concatenate(
        [r_ref[0, pl.ds(c, tm, stride=ACC_ROWS), :] for c in range(ACC_ROWS)], axis=1)
    y_ref[...] = _rms(xs_ref[...] + routed, fn_ref[...])


def _finalize(xs, routed, final_norm, tm, tok0):
    rows = xs.shape[0]
    per_block = MOE_BLOCK_TOKENS // tm
    first = tok0 // tm
    assert tok0 % tm == 0 and MOE_BLOCK_TOKENS % tm == 0
    return pl.pallas_call(
        _final_kernel,
        grid=(rows // tm,),
        in_specs=[pl.BlockSpec((tm, D_MODEL), lambda i: (i, 0)),
                  pl.BlockSpec((1, tm * ACC_ROWS, LANES),
                               lambda i: ((first + i) // per_block, (first + i) % per_block, 0)),
                  _const_spec((1, D_MODEL))],
        out_specs=pl.BlockSpec((tm, D_MODEL), lambda i: (i, 0)),
        out_shape=jax.ShapeDtypeStruct((rows, D_MODEL), F32),
        compiler_params=_cparams("parallel"),
        name="final_norm",
    )(xs, routed, final_norm)


def _rope_tables(pos):
    half = RET_HEAD_DIM // 2
    inv_freq = ROPE_BASE ** (-jnp.arange(half, dtype=F32) / half)
    ang = pos.astype(F32)[:, None] * inv_freq[None, :]
    cos, sin = jnp.cos(ang), jnp.sin(ang)
    cos_t = jnp.concatenate([cos, cos, cos, cos], axis=1)
    sin_t = jnp.concatenate([-sin, sin, -sin, sin], axis=1)
    return cos_t, sin_t


def kernel(x_prompt, x_sample, state_s5_re, state_s5_im, state_ret, meta_tokens, norm1, w_in,
           s5_lam_re, s5_lam_im, s5_log_dt, s5_b_re, s5_b_im, s5_c_re, s5_c_im, s5_d, s5_w_glu,
           s5_norm, ret_norm, w_out, norm2, w_router, router_bias, w_gate_e, w_up_e, w_down_e,
           w_gate_sh, w_up_sh, w_down_sh, final_norm):
    assert norm1.shape[0] == 1, "single-layer model"
    bp, seq, _ = x_prompt.shape
    ns = x_sample.shape[0]
    l = 0

    n1 = norm1[l].reshape(1, D_MODEL)
    w_in_b = w_in[l].astype(BF16)
    s5p = _s5_params(s5_lam_re[l], s5_lam_im[l], s5_log_dt[l], s5_b_re[l], s5_b_im[l],
                     s5_c_re[l], s5_c_im[l], s5_d[l], s5_w_glu[l], s5_norm[l])
    rnorm = ret_norm[l].reshape(1, D_RET)
    w_out_b = w_out[l].astype(BF16)
    wr_t = w_router[l].T
    wr_hi = wr_t.astype(BF16)
    wr_lo = (wr_t - wr_hi.astype(F32)).astype(BF16)
    mp = {
        "woa": w_out_b[:D_S5], "wob": w_out_b[D_S5:], "n2": norm2[l].reshape(1, D_MODEL),
        "wrt": jnp.stack([wr_hi, wr_lo]),
        "rb": jnp.broadcast_to(router_bias[l][:, None], (N_EXPERTS, LANES)),
        "wgs": w_gate_sh[l].astype(BF16), "wus": w_up_sh[l].astype(BF16),
        "wds": w_down_sh[l].astype(BF16),
    }
    ep = {"wg": w_gate_e[l], "wu": w_up_e[l],
          "wd": w_down_e[l]}
    fnorm = final_norm.reshape(1, D_MODEL)

    xp = x_prompt.reshape(bp * seq, D_MODEL)
    cos_p, sin_p = _rope_tables(N_META + jnp.arange(seq, dtype=jnp.int32))
    tm_a = 512
    up, qp, kp, vp, gp = _project(xp, n1, w_in_b, cos_p, sin_p, tm_a, seq // tm_a)

    meta_tm = jnp.repeat(meta_tokens, SUBLANES, axis=0)
    mchunk = LANES
    meta_chunk = jnp.concatenate([jnp.zeros((mchunk - N_META, D_MODEL), F32), meta_tokens], axis=0)
    x_small = jnp.concatenate([x_sample.reshape(ns, D_MODEL), meta_tm, meta_chunk], axis=0)
    meta_pos = jnp.arange(N_META, dtype=jnp.int32)
    pos_small = jnp.concatenate([jnp.full((ns,), PAST_LEN, jnp.int32),
                                 jnp.repeat(meta_pos, SUBLANES),
                                 jnp.zeros((mchunk - N_META,), jnp.int32), meta_pos])
    cos_s, sin_s = _rope_tables(pos_small)
    n_small = x_small.shape[0]
    n_tm = ns + N_META * SUBLANES
    us, qs, ks, vs, gs = _project(x_small, n1, w_in_b, cos_s, sin_s, n_small, 1)

    zero8 = jnp.zeros((SUBLANES, S5_LANES), F32)
    _, m_re, m_im = _s5(us[ns:n_tm], zero8, zero8, s5p, nb=SUBLANES, tt=N_META, nblk=1,
                        bt_major=False)
    tt = 64
    ys5_p, p_re, p_im = _s5(up.reshape(bp, seq, D_S5), m_re, m_im, s5p, nb=bp, tt=tt,
                            nblk=seq // tt, bt_major=True)
    ys5_p = ys5_p.reshape(bp * seq, D_S5)
    ys5_s, s_re, s_im = _s5(us[:ns], state_s5_re[l].reshape(ns, S5_LANES),
                            state_s5_im[l].reshape(ns, S5_LANES), s5p, nb=ns, tt=1, nblk=1,
                            bt_major=False)

    zero_pair = jnp.zeros((1, N_HEAD_PAIRS, LANES, LANES), F32)
    _, m_pair = _retention(qs[n_tm:], ks[n_tm:], vs[n_tm:], gs[n_tm:],
                           zero_pair, rnorm, nseq=1, chunk=mchunk, nchunk=1)
    chunk = 256
    yret_p, p_pair = _retention(qp, kp, vp, gp, jnp.broadcast_to(m_pair, (bp,) + m_pair.shape[1:]),
                                rnorm, nseq=bp, chunk=chunk, nchunk=seq // chunk)
    yret_s, ret_s = _retention_step(qs[:ns], ks[:ns], vs[:ns], gs[:ns], state_ret[l], ret_norm[l])

    n_prompt = bp * seq
    assert n_prompt + ns <= MOE_BLOCKS * MOE_BLOCK_TOKENS
    n_moe = MOE_BLOCKS * MOE_BLOCK_TOKENS
    tm_mix = 512
    xs_p, hp, meta, eid_p, cnt_p = _mix(xp, ys5_p, yret_p, mp, tm_mix, n_moe)
    xs_s, hp_s, meta_s, eid_s, cnt_s = _mix(x_sample.reshape(ns, D_MODEL), ys5_s, yret_s, mp,
                                            ns, ns)
    hp = lax.dynamic_update_slice(hp, hp_s, (n_prompt * PACK_ROWS, 0))
    meta = lax.dynamic_update_slice(meta, meta_s, (n_prompt * META_ROWS, 0))
    blk_s = n_prompt // MOE_BLOCK_TOKENS
    assert MOE_BLOCK_TOKENS % tm_mix == 0 and (n_prompt + ns - 1) // MOE_BLOCK_TOKENS == blk_s
    counts = cnt_p[:, :, 0].reshape(MOE_BLOCKS, MOE_BLOCK_TOKENS // tm_mix, N_EXPERTS).sum(axis=1)
    counts = counts.at[blk_s].add(cnt_s[0, :, 0]).reshape(-1).astype(jnp.int32)
    tabs = _dispatch_tables(jnp.concatenate([eid_p, eid_s], axis=1), counts, n_prompt + ns)
    routed = _moe(hp, meta, tabs, ep)
    y_p = _finalize(xs_p, routed, fnorm, 512, 0)
    y_s = _finalize(xs_s, routed, fnorm, ns, n_prompt)

    shape5 = (1, bp, N_S5_GROUPS, S5_STATE)
    return (y_p.reshape(bp, seq, D_MODEL),
            y_s.reshape(ns, 1, D_MODEL),
            p_re.reshape(shape5), p_im.reshape(shape5),
            _unpair_state(p_pair)[None],
            s_re.reshape(1, ns, N_S5_GROUPS, S5_STATE), s_im.reshape(1, ns, N_S5_GROUPS, S5_STATE),
            ret_s[None])
```

```python
import functools
import math

import jax
import jax.numpy as jnp
from jax import lax
from jax.experimental import pallas as pl
from jax.experimental.pallas import tpu as pltpu

F32 = jnp.float32
BF16 = jnp.bfloat16

D_MODEL = 1024
N_META = 16
PAST_LEN = 16384
D_S5 = 512
S5_GROUP = 16
N_S5_GROUPS = 32
S5_STATE = 64
S5_LANES = N_S5_GROUPS * S5_STATE
D_RET = 512
N_RET_HEADS = 8
RET_HEAD_DIM = 64
N_HEAD_PAIRS = 4
ROPE_BASE = 10000.0
D_IN = D_S5 + 4 * D_RET
N_EXPERTS = 64
TOP_K = 8
N_EXPERT_GROUPS = 8
GROUP_SIZE = 8
TOPK_GROUPS = 4
D_EXPERT = 256
ROUTED_SCALE = 2.5
EPS = 1e-6

LANES = 128
SUBLANES = 8
VMEM_LIMIT = 56 * 1024 * 1024

PACK_ROWS = D_MODEL // LANES
ACC_ROWS = D_MODEL // LANES
MOE_BLOCKS = 4
MOE_BLOCK_TOKENS = 4608
MOE_TILE = 320
GROUP_TILES = -(-MOE_BLOCK_TOKENS // MOE_TILE) + 2
ADD_UNROLL = 8
OT_PITCH = MOE_TILE + SUBLANES
XT_PITCH = MOE_TILE + SUBLANES
META_ROWS = 2
META_SHIFT = 2
assert PACK_ROWS == META_ROWS << META_SHIFT
TOKEN_BITS = 13
ZERO_ROWS = 64
CAST_ROWS = 128


def _cparams(*sem):
    return pltpu.CompilerParams(dimension_semantics=sem, vmem_limit_bytes=VMEM_LIMIT)


def _const_spec(shape):
    nd = len(shape)
    return pl.BlockSpec(shape, lambda *_: (0,) * nd)


def _rms(x, gain):
    return x * lax.rsqrt(jnp.mean(x * x, axis=-1, keepdims=True) + EPS) * gain


def _proj_kernel(x_ref, n1_ref, w_ref, cos_ref, sin_ref, u_ref, q_ref, k_ref, v_ref, g_ref):
    h = _rms(x_ref[...], n1_ref[...]).astype(BF16)
    proj = jnp.dot(h, w_ref[...], preferred_element_type=F32)
    cos = cos_ref[...]
    sin = sin_ref[...]
    lane = lax.broadcasted_iota(jnp.int32, cos.shape, 1)
    first_half = (lane % RET_HEAD_DIM) < (RET_HEAD_DIM // 2)

    def rotary(t):
        partner = jnp.where(first_half,
                            pltpu.roll(t, LANES - RET_HEAD_DIM // 2, 1),
                            pltpu.roll(t, RET_HEAD_DIM // 2, 1))
        return t * cos + partner * sin

    u_ref[...] = proj[:, :D_S5]
    for j in range(N_HEAD_PAIRS):
        lo = D_S5 + j * LANES
        q_ref[:, j * LANES:(j + 1) * LANES] = rotary(proj[:, lo:lo + LANES]).astype(BF16)
        lo += D_RET
        k_ref[:, j * LANES:(j + 1) * LANES] = (
            rotary(proj[:, lo:lo + LANES]) * (RET_HEAD_DIM ** -0.5)).astype(BF16)
    v_ref[...] = proj[:, D_S5 + 2 * D_RET:D_S5 + 3 * D_RET].astype(BF16)
    g_ref[...] = proj[:, D_S5 + 3 * D_RET:]


def _project(x, norm1, w_in_b, cos_t, sin_t, tm, table_blocks):
    rows = x.shape[0]
    row = lambda i: (i, 0)
    tab = lambda i: (i % table_blocks, 0)
    return pl.pallas_call(
        _proj_kernel,
        grid=(rows // tm,),
        in_specs=[pl.BlockSpec((tm, D_MODEL), row),
                  _const_spec((1, D_MODEL)),
                  _const_spec((D_MODEL, D_IN)),
                  pl.BlockSpec((tm, LANES), tab),
                  pl.BlockSpec((tm, LANES), tab)],
        out_specs=[pl.BlockSpec((tm, D_S5), row),
                   pl.BlockSpec((tm, D_RET), row),
                   pl.BlockSpec((tm, D_RET), row),
                   pl.BlockSpec((tm, D_RET), row),
                   pl.BlockSpec((tm, D_RET), row)],
        out_shape=[jax.ShapeDtypeStruct((rows, D_S5), F32),
                   jax.ShapeDtypeStruct((rows, D_RET), BF16),
                   jax.ShapeDtypeStruct((rows, D_RET), BF16),
                   jax.ShapeDtypeStruct((rows, D_RET), BF16),
                   jax.ShapeDtypeStruct((rows, D_RET), F32)],
        compiler_params=_cparams("parallel"),
        name="in_proj",
    )(x, norm1, w_in_b, cos_t, sin_t)


def _s5_kernel(u_ref, x0re_ref, x0im_ref, are_ref, aim_ref, b_ref, c_ref, d_ref, wglu_ref,
               nrm_ref, y_ref, sre_ref, sim_ref, st, sre, sim, utm, ytm, *, nb, tt, bt_major):
    i = pl.program_id(0)
    rows = nb * tt

    @pl.when(i == 0)
    def _():
        sre[...] = x0re_ref[...]
        sim[...] = x0im_ref[...]

    if bt_major:
        for t in range(tt):
            utm[t * nb:(t + 1) * nb, :] = u_ref[:, t, :]
    else:
        utm[...] = u_ref[...]

    ub = utm[...].astype(BF16)
    half = S5_LANES // 2
    kh = D_S5 // 2
    for part in range(2):
        for hf in range(2):
            st[:, part * S5_LANES + hf * half:part * S5_LANES + (hf + 1) * half] = jnp.dot(
                ub[:, hf * kh:(hf + 1) * kh], b_ref[part, hf], preferred_element_type=F32)

    lw = 512 if nb == SUBLANES else LANES
    for lg in range(S5_LANES // lw):
        re = slice(lg * lw, (lg + 1) * lw)
        im = slice(S5_LANES + lg * lw, S5_LANES + (lg + 1) * lw)
        a_re = are_ref[:, re]
        a_im = aim_ref[:, re]

        def step(t, carry):
            s_re, s_im = carry
            r0 = pl.multiple_of(t * nb, nb)
            n_re = a_re * s_re - a_im * s_im + st[pl.ds(r0, nb), re]
            n_im = a_re * s_im + a_im * s_re + st[pl.ds(r0, nb), im]
            st[pl.ds(r0, nb), re] = n_re
            st[pl.ds(r0, nb), im] = n_im
            return n_re, n_im

        if nb == SUBLANES:
            f_re, f_im = sre[:, re], sim[:, re]
            for t in range(tt):
                r = slice(t * nb, (t + 1) * nb)
                n_re = a_re * f_re - a_im * f_im + st[r, re]
                n_im = a_re * f_im + a_im * f_re + st[r, im]
                st[r, re] = n_re
                st[r, im] = n_im
                f_re, f_im = n_re, n_im
        else:
            f_re, f_im = lax.fori_loop(0, tt, step, (sre[:, re], sim[:, re]))
        sre[:, re] = f_re
        sim[:, re] = f_im

    sre_ref[...] = sre[...]
    sim_ref[...] = sim[...]

    ys = []
    for hf in range(2):
        xr = st[:, hf * half:(hf + 1) * half].astype(BF16)
        xi = st[:, S5_LANES + hf * half:S5_LANES + (hf + 1) * half].astype(BF16)
        ys.append(jnp.dot(xr, c_ref[0, hf], preferred_element_type=F32)
                  - jnp.dot(xi, c_ref[1, hf], preferred_element_type=F32))
    y = jnp.concatenate(ys, axis=1) + d_ref[...] * utm[...]
    y = jax.nn.gelu(y)
    y = y * jax.nn.sigmoid(jnp.dot(y.astype(BF16), wglu_ref[...], preferred_element_type=F32))
    y = _rms(y, nrm_ref[...]).astype(BF16)
    if bt_major:
        yf = y.astype(F32)
        for j in range(D_S5 // LANES):
            ytm[j] = yf[:, j * LANES:(j + 1) * LANES]
        for b in range(nb):
            for j in range(D_S5 // LANES):
                y_ref[b, :, j * LANES:(j + 1) * LANES] = (
                    ytm[j, pl.ds(b, tt, stride=nb), :].astype(BF16))
    else:
        y_ref[...] = y


def _s5(u, x0re, x0im, s5p, *, nb, tt, nblk, bt_major):
    rows = nb * tt
    if bt_major:
        u_spec = pl.BlockSpec((nb, tt, D_S5), lambda i: (0, i, 0))
        y_shape = jax.ShapeDtypeStruct((nb, tt * nblk, D_S5), BF16)
    else:
        u_spec = pl.BlockSpec((rows, D_S5), lambda i: (i, 0))
        y_shape = jax.ShapeDtypeStruct((rows * nblk, D_S5), BF16)
    state_spec = _const_spec((nb, S5_LANES))
    are = jnp.broadcast_to(s5p["are"], (nb, S5_LANES))
    aim = jnp.broadcast_to(s5p["aim"], (nb, S5_LANES))
    return pl.pallas_call(
        functools.partial(_s5_kernel, nb=nb, tt=tt, bt_major=bt_major),
        grid=(nblk,),
        in_specs=[u_spec, state_spec, state_spec, state_spec, state_spec,
                  _const_spec((2, 2, D_S5 // 2, S5_LANES // 2)),
                  _const_spec((2, 2, S5_LANES // 2, D_S5 // 2)),
                  _const_spec((1, D_S5)),
                  _const_spec((D_S5, D_S5)),
                  _const_spec((1, D_S5))],
        out_specs=[u_spec, state_spec, state_spec],
        out_shape=[y_shape,
                   jax.ShapeDtypeStruct((nb, S5_LANES), F32),
                   jax.ShapeDtypeStruct((nb, S5_LANES), F32)],
        scratch_shapes=[pltpu.VMEM((rows, 2 * S5_LANES), F32),
                        pltpu.VMEM((nb, S5_LANES), F32),
                        pltpu.VMEM((nb, S5_LANES), F32),
                        pltpu.VMEM((rows, D_S5), F32),
                        pltpu.VMEM((D_S5 // LANES, rows, LANES), F32)],
        compiler_params=_cparams("arbitrary"),
        name="s5_" + ("bt" if bt_major else "tm") + str(nb),
    )(u, x0re, x0im, are, aim, s5p["b"], s5p["c"], s5p["d"], s5p["wglu"], s5p["nrm"])


def _s5_params(lam_re, lam_im, log_dt, b_re, b_im, c_re, c_im, d_skip, w_glu, nrm):
    dt = jnp.exp(log_dt)[:, None]
    mag = jnp.exp(lam_re * dt)
    abar_re, abar_im = mag * jnp.cos(lam_im * dt), mag * jnp.sin(lam_im * dt)
    num_re, num_im = abar_re - 1.0, abar_im
    den = lam_re * lam_re + lam_im * lam_im
    f_re = (num_re * lam_re + num_im * lam_im) / den
    f_im = (num_im * lam_re - num_re * lam_im) / den
    bbar_re = f_re[..., None] * b_re - f_im[..., None] * b_im
    bbar_im = f_re[..., None] * b_im + f_im[..., None] * b_re
    hg = N_S5_GROUPS // 2
    eye = jnp.eye(hg, dtype=F32)

    def bdiag(bb):
        bb = bb.reshape(2, hg, S5_STATE, S5_GROUP)
        return jnp.einsum("zgph,gk->zghkp", bb, eye).reshape(2, hg * S5_GROUP, hg * S5_STATE)

    def cdiag(cc):
        cc = cc.reshape(2, hg, S5_GROUP, S5_STATE)
        return jnp.einsum("zgnp,gk->zgpkn", cc, eye).reshape(2, hg * S5_STATE, hg * S5_GROUP)

    return {
        "are": abar_re.reshape(1, S5_LANES), "aim": abar_im.reshape(1, S5_LANES),
        "b": jnp.stack([bdiag(bbar_re), bdiag(bbar_im)]).astype(BF16),
        "c": jnp.stack([cdiag(c_re), cdiag(c_im)]).astype(BF16),
        "d": d_skip.reshape(1, D_S5), "wglu": w_glu.astype(BF16), "nrm": nrm.reshape(1, D_S5),
    }


def _head_norm_gate(o, g, gain, lo):
    inv = 1.0 / RET_HEAD_DIM

    def seg_mean(t):
        s_lo = jnp.sum(jnp.where(lo, t, 0.0), axis=1, keepdims=True)
        s_hi = jnp.sum(jnp.where(lo, 0.0, t), axis=1, keepdims=True)
        return jnp.where(lo, s_lo, s_hi) * inv

    dlt = o - seg_mean(o)
    var = seg_mean(dlt * dlt)
    return jax.nn.silu(g) * (dlt * lax.rsqrt(var + EPS) * gain)


def _ret_kernel(q_ref, k_ref, v_ref, g_ref, s0_ref, dm_ref, qd_ref, kd_ref, gc_ref, bm_ref,
                nrm_ref, y_ref, so_ref, s_acc):
    c = pl.program_id(1)

    @pl.when(c == 0)
    def _():
        s_acc[...] = s0_ref[0]

    rows = q_ref.shape[0]
    lane = lax.broadcasted_iota(jnp.int32, (rows, LANES), 1)
    lo = lane < RET_HEAD_DIM
    for j in range(N_HEAD_PAIRS):
        sl = slice(j * LANES, (j + 1) * LANES)
        q2 = q_ref[:, sl].astype(F32)
        k2 = k_ref[:, sl]
        v2 = v_ref[:, sl]
        s_pair = s_acc[j]
        cross = jnp.dot((q2 * qd_ref[:, sl]).astype(BF16), s_pair.astype(BF16),
                        preferred_element_type=F32)
        k_dec = (k2.astype(F32) * kd_ref[:, sl]).astype(BF16)
        upd = lax.dot_general(k_dec, v2, (((0,), (0,)), ((), ())), preferred_element_type=F32)
        s_acc[j] = gc_ref[j] * s_pair + bm_ref[...] * upd
        inner = []
        for hh, qh in enumerate((jnp.where(lo, q2, 0.0), jnp.where(lo, 0.0, q2))):
            sc = lax.dot_general(qh.astype(BF16), k2, (((1,), (1,)), ((), ())),
                                 preferred_element_type=F32) * dm_ref[2 * j + hh]
            inner.append(jnp.dot(sc.astype(BF16), v2, preferred_element_type=F32))
        o = jnp.where(lo, inner[0], inner[1]) + cross
        y_ref[:, sl] = _head_norm_gate(o, g_ref[:, sl], nrm_ref[:, sl], lo).astype(BF16)
    so_ref[0] = s_acc[...]


def _ret_tables(chunk):
    log_g = jnp.log1p(-jnp.exp2(-5.0 - jnp.arange(N_RET_HEADS, dtype=F32)))
    n = jnp.arange(chunk, dtype=F32)
    diff = n[:, None] - n[None, :]
    dm = jnp.where(diff[None] >= 0.0,
                   jnp.exp(log_g[:, None, None] * jnp.maximum(diff, 0.0)[None]), 0.0)
    per_lane = lambda t: jnp.repeat(t.T, RET_HEAD_DIM, axis=1)
    qd = per_lane(jnp.exp(log_g[:, None] * (n + 1.0)[None]))
    kd = per_lane(jnp.exp(log_g[:, None] * (chunk - 1.0 - n)[None]))
    gch = jnp.exp(log_g * chunk)
    blk = jnp.kron(jnp.eye(2, dtype=F32), jnp.ones((RET_HEAD_DIM, RET_HEAD_DIM), F32))
    gc = jnp.repeat(gch.reshape(N_HEAD_PAIRS, 2), RET_HEAD_DIM, axis=1)[:, :, None] * blk[None]
    return dm, qd, kd, gc, blk


def _pair_state(s):
    b = s.shape[0]
    s = s.reshape(b, N_HEAD_PAIRS, 2, RET_HEAD_DIM, RET_HEAD_DIM)
    z = jnp.zeros_like(s[:, :, 0])
    top = jnp.concatenate([s[:, :, 0], z], axis=-1)
    bot = jnp.concatenate([z, s[:, :, 1]], axis=-1)
    return jnp.concatenate([top, bot], axis=-2)


def _unpair_state(s2):
    d = RET_HEAD_DIM
    return jnp.stack([s2[:, :, :d, :d], s2[:, :, d:, d:]], axis=2).reshape(
        s2.shape[0], N_RET_HEADS, d, d)


def _retention(q, k, v, g, s0_pair, ret_norm, *, nseq, chunk, nchunk):
    dm, qd, kd, gc, blk = _ret_tables(chunk)
    row = lambda b, c: (b * nchunk + c, 0)
    st_spec = pl.BlockSpec((1, N_HEAD_PAIRS, LANES, LANES), lambda b, c: (b, 0, 0, 0))
    blk_spec = pl.BlockSpec((chunk, D_RET), row)
    return pl.pallas_call(
        _ret_kernel,
        grid=(nseq, nchunk),
        in_specs=[blk_spec, blk_spec, blk_spec, blk_spec, st_spec,
                  _const_spec((N_RET_HEADS, chunk, chunk)),
                  _const_spec((chunk, D_RET)), _const_spec((chunk, D_RET)),
                  _const_spec((N_HEAD_PAIRS, LANES, LANES)), _const_spec((LANES, LANES)),
                  _const_spec((1, D_RET))],
        out_specs=[blk_spec, st_spec],
        out_shape=[jax.ShapeDtypeStruct((nseq * nchunk * chunk, D_RET), BF16),
                   jax.ShapeDtypeStruct((nseq, N_HEAD_PAIRS, LANES, LANES), F32)],
        scratch_shapes=[pltpu.VMEM((N_HEAD_PAIRS, LANES, LANES), F32)],
        compiler_params=_cparams("arbitrary", "arbitrary"),
        name="retention_c%d" % chunk,
    )(q, k, v, g, s0_pair, dm, qd, kd, gc, blk, ret_norm)


def _ret_step_kernel(q_ref, k_ref, v_ref, g_ref, s_ref, gam_ref, nrm_ref, y_ref, so_ref):
    q = q_ref[...]
    k = k_ref[...]
    v = v_ref[...]
    gam = gam_ref[0]
    score = jnp.sum(q * k, axis=0, keepdims=True)
    q_dec = (q * gam).astype(BF16).astype(F32)
    cross = jnp.zeros(v.shape, F32)
    for d in range(RET_HEAD_DIM):
        s_d = s_ref[0, d]
        cross = cross + q_dec[d:d + 1, :] * s_d
        so_ref[0, d] = gam * s_d + k[d:d + 1, :] * v
    o = score.astype(BF16).astype(F32) * v + cross
    dlt = o - jnp.mean(o, axis=0, keepdims=True)
    var = jnp.mean(dlt * dlt, axis=0, keepdims=True)
    y_ref[...] = jax.nn.silu(g_ref[...]) * (dlt * lax.rsqrt(var + EPS) * nrm_ref[...])


def _retention_step(q, k, v, g, state, ret_norm):
    n = q.shape[0]
    gam = jnp.exp(jnp.log1p(-jnp.exp2(-5.0 - jnp.arange(N_RET_HEADS, dtype=F32))))
    gam_tab = jnp.broadcast_to(gam[:, None, None], (N_RET_HEADS, 1, n))
    seq_minor = lambda t: t.astype(F32).T
    head = lambda h: (h, 0)
    vec_spec = pl.BlockSpec((RET_HEAD_DIM, n), head)
    st_spec = pl.BlockSpec((1, RET_HEAD_DIM, RET_HEAD_DIM, n), lambda h: (h, 0, 0, 0))
    y_t, s_new = pl.pallas_call(
        _ret_step_kernel,
        grid=(N_RET_HEADS,),
        in_specs=[vec_spec, vec_spec, vec_spec, vec_spec, st_spec,
                  pl.BlockSpec((1, 1, n), lambda h: (h, 0, 0)), vec_spec],
        out_specs=[vec_spec, st_spec],
        out_shape=[jax.ShapeDtypeStruct((D_RET, n), F32),
                   jax.ShapeDtypeStruct((N_RET_HEADS, RET_HEAD_DIM, RET_HEAD_DIM, n), F32)],
        compiler_params=_cparams("parallel"),
        name="retention_step",
    )(seq_minor(q), seq_minor(k), seq_minor(v), seq_minor(g),
      jnp.transpose(state, (1, 2, 3, 0)), gam_tab,
      jnp.broadcast_to(ret_norm[:, None], (D_RET, n)))
    return y_t.T.astype(BF16), jnp.transpose(s_new, (3, 0, 1, 2))


def _first_max(vals, idxs, sentinel):
    m = jnp.max(vals, axis=0, keepdims=True)
    first = jnp.min(jnp.where(vals == m, idxs, sentinel), axis=0, keepdims=True)
    return m, idxs == first


def _route(scores, sel):
    t = sel.shape[1]
    neg = -jnp.inf
    member = lax.broadcasted_iota(jnp.int32, (GROUP_SIZE, t), 0).astype(F32)
    groups = [sel[g * GROUP_SIZE:(g + 1) * GROUP_SIZE, :] for g in range(N_EXPERT_GROUPS)]
    gscore = []
    for grp in groups:
        m1, pick = _first_max(grp, member, float(GROUP_SIZE))
        m2 = jnp.max(jnp.where(pick, neg, grp), axis=0, keepdims=True)
        gscore.append(m1 + m2)
    gs = jnp.concatenate(gscore, axis=0)
    gkeep = jnp.zeros(gs.shape, F32)
    for _ in range(TOPK_GROUPS):
        _, pick = _first_max(gs, member, float(N_EXPERT_GROUPS))
        gkeep = jnp.where(pick, 1.0, gkeep)
        gs = jnp.where(pick, neg, gs)
    cand = jnp.concatenate(
        [jnp.where(gkeep[g:g + 1, :] > 0.5, groups[g], neg) for g in range(N_EXPERT_GROUPS)],
        axis=0)
    expert = lax.broadcasted_iota(jnp.int32, (N_EXPERTS, t), 0).astype(F32)
    ids, ws = [], []
    chosen = jnp.zeros(cand.shape, F32)
    for _ in range(TOP_K):
        _, pick = _first_max(cand, expert, float(N_EXPERTS))
        ids.append(jnp.sum(jnp.where(pick, expert, 0.0), axis=0, keepdims=True))
        ws.append(jnp.sum(jnp.where(pick, scores, 0.0), axis=0, keepdims=True))
        chosen = jnp.where(pick, 1.0, chosen)
        cand = jnp.where(pick, neg, cand)
    w = jnp.concatenate(ws, axis=0)
    gates = w / jnp.sum(w, axis=0, keepdims=True) * ROUTED_SCALE
    return jnp.concatenate(ids, axis=0), gates, jnp.sum(chosen, axis=1, keepdims=True)


def _mix_kernel(x_ref, ys_ref, yr_ref, woa_ref, wob_ref, n2_ref, wrt_ref, rb_ref,
                wgs_ref, wus_ref, wds_ref, xs_ref, hp_ref, meta_ref, eid_ref, cnt_ref, *,
                n_blocks):
    @pl.when(pl.program_id(0) >= n_blocks)
    def _():
        hp_ref[...] = jnp.zeros_like(hp_ref)
        meta_ref[...] = jnp.zeros_like(meta_ref)
        cnt_ref[...] = jnp.zeros_like(cnt_ref)

    @pl.when(pl.program_id(0) < n_blocks)
    def _():
        _mix_body(x_ref, ys_ref, yr_ref, woa_ref, wob_ref, n2_ref, wrt_ref, rb_ref,
                  wgs_ref, wus_ref, wds_ref, xs_ref, hp_ref, meta_ref, eid_ref, cnt_ref)


def _mix_body(x_ref, ys_ref, yr_ref, woa_ref, wob_ref, n2_ref, wrt_ref, rb_ref,
              wgs_ref, wus_ref, wds_ref, xs_ref, hp_ref, meta_ref, eid_ref, cnt_ref):
    tm = x_ref.shape[0]
    x2 = (x_ref[...]
          + jnp.dot(ys_ref[...], woa_ref[...], preferred_element_type=F32)
          + jnp.dot(yr_ref[...], wob_ref[...], preferred_element_type=F32))
    h2f = _rms(x2, n2_ref[...])
    h2 = h2f.astype(BF16)
    h2r = h2.astype(F32)
    for c in range(PACK_ROWS):
        hp_ref[pl.ds(c, tm, stride=PACK_ROWS), :] = h2r[:, c * LANES:(c + 1) * LANES]
    h2_lo = (h2f - h2r).astype(BF16)
    nt_dims = (((1,), (1,)), ((), ()))
    logits = (lax.dot_general(wrt_ref[0], h2, nt_dims, preferred_element_type=F32)
              + lax.dot_general(wrt_ref[1], h2, nt_dims, preferred_element_type=F32)
              + lax.dot_general(wrt_ref[0], h2_lo, nt_dims, preferred_element_type=F32))
    scores = jax.nn.sigmoid(logits)
    ids, gates, counts = _route(scores, scores + rb_ref[:, :1])
    eid_ref[...] = ids.astype(jnp.int32)
    cnt_ref[0] = jnp.broadcast_to(counts, (N_EXPERTS, LANES))
    rec = jnp.concatenate([gates, ids, jnp.zeros((LANES - 2 * TOP_K, tm), F32)], axis=0).T
    meta_ref[pl.ds(0, tm, stride=META_ROWS), :] = rec
    for c in range(1, META_ROWS):
        meta_ref[pl.ds(c, tm, stride=META_ROWS), :] = jnp.zeros((tm, LANES), F32)
    a = jnp.dot(h2, wgs_ref[...], preferred_element_type=F32)
    b = jnp.dot(h2, wus_ref[...], preferred_element_type=F32)
    xs_ref[...] = x2 + jnp.dot((jax.nn.silu(a) * b).astype(BF16), wds_ref[...],
                               preferred_element_type=F32)


def _mix(x, ys5, yret, mp, tm, out_tokens):
    rows = x.shape[0]
    n_blocks = rows // tm
    assert out_tokens % tm == 0
    packed = lambda i: (i, 0)
    row = lambda i: (jnp.minimum(i, n_blocks - 1), 0)
    col = lambda i: (0, jnp.minimum(i, n_blocks - 1))
    return pl.pallas_call(
        functools.partial(_mix_kernel, n_blocks=n_blocks),
        grid=(out_tokens // tm,),
        in_specs=[pl.BlockSpec((tm, D_MODEL), row),
                  pl.BlockSpec((tm, D_S5), row), pl.BlockSpec((tm, D_RET), row),
                  _const_spec((D_S5, D_MODEL)), _const_spec((D_RET, D_MODEL)),
                  _const_spec((1, D_MODEL)),
                  _const_spec((2, N_EXPERTS, D_MODEL)), _const_spec((N_EXPERTS, LANES)),
                  _const_spec((D_MODEL, D_EXPERT)), _const_spec((D_MODEL, D_EXPERT)),
                  _const_spec((D_EXPERT, D_MODEL))],
        out_specs=[pl.BlockSpec((tm, D_MODEL), row),
                   pl.BlockSpec((tm * PACK_ROWS, LANES), packed),
                   pl.BlockSpec((tm * META_ROWS, LANES), packed),
                   pl.BlockSpec((TOP_K, tm), col),
                   pl.BlockSpec((1, N_EXPERTS, LANES), lambda i: (i, 0, 0))],
        out_shape=[jax.ShapeDtypeStruct((rows, D_MODEL), F32),
                   jax.ShapeDtypeStruct((out_tokens * PACK_ROWS, LANES), F32),
                   jax.ShapeDtypeStruct((out_tokens * META_ROWS, LANES), F32),
                   jax.ShapeDtypeStruct((TOP_K, rows), jnp.int32),
                   jax.ShapeDtypeStruct((out_tokens // tm, N_EXPERTS, LANES), F32)],
        compiler_params=_cparams("arbitrary"),
        name="out_proj_router",
    )(x, ys5, yret, mp["woa"], mp["wob"], mp["n2"], mp["wrt"], mp["rb"],
      mp["wgs"], mp["wus"], mp["wds"])


def _moe_kernel(t0_ref, nt_ref, src_ref, dst_ref, hp_ref, meta_ref, wg32_ref, wu32_ref, wd32_ref,
                acc_ref, xt_ref, xm_ref, ot_ref, xb_ref, wg_ref, wu_ref, wd_ref):
    s = pl.program_id(0)
    tm = MOE_TILE
    half = D_MODEL // 2
    all_rows = range(tm)
    nt = nt_ref[s]
    expert = (s % N_EXPERTS).astype(F32)

    @pl.when(nt > 0)
    def _():
        for src32, dst16 in ((wg32_ref, wg_ref), (wu32_ref, wu_ref), (wd32_ref, wd_ref)):
            rows = src32.shape[1]
            for i in range(0, rows, CAST_ROWS):
                dst16[0, i:i + CAST_ROWS, :] = src32[0, i:i + CAST_ROWS, :].astype(BF16)

    def fetch_rows(j, rows):
        for m in rows:
            r = pl.multiple_of(src_ref[j, 0, m], PACK_ROWS)
            xt_ref[pl.ds(m, PACK_ROWS, stride=XT_PITCH), :] = hp_ref[pl.ds(r, PACK_ROWS), :]
            xm_ref[pl.ds(m, META_ROWS, stride=XT_PITCH), :] = meta_ref[
                pl.ds(pl.multiple_of(r >> META_SHIFT, META_ROWS), META_ROWS), :]

    def add_rows(j, rows):
        for m0 in range(rows.start, rows.stop, ADD_UNROLL):
            new = []
            for m in range(m0, m0 + ADD_UNROLL):
                base = pl.multiple_of(dst_ref[j, 0, m], ACC_ROWS)
                row = ot_ref[pl.ds(m, ACC_ROWS, stride=OT_PITCH), :]
                new.append((base, acc_ref[0, pl.ds(base, ACC_ROWS), :] + row))
            for base, v in new:
                acc_ref[0, pl.ds(base, ACC_ROWS), :] = v

    @pl.when(s % N_EXPERTS == 0)
    def _():
        zero = jnp.zeros((ZERO_ROWS, LANES), F32)

        def clear(i, carry):
            acc_ref[0, pl.ds(pl.multiple_of(i * ZERO_ROWS, ZERO_ROWS), ZERO_ROWS), :] = zero
            return carry

        lax.fori_loop(0, acc_ref.shape[1] // ZERO_ROWS, clear, 0)
        ot_ref[...] = jnp.zeros_like(ot_ref)
        fetch_rows(1, all_rows)

    def stage(j, carry):
        n_slice = 4
        per = tm // n_slice
        rows = [range(i * per, (i + 1) * per) for i in range(n_slice)]
        for c in range(PACK_ROWS):
            xb_ref[:, c * LANES:(c + 1) * LANES] = (
                xt_ref[c * XT_PITCH:c * XT_PITCH + tm, :].astype(BF16))
        rec = xm_ref[0:tm, :]
        ids = pltpu.roll(rec, LANES - TOP_K, 1)
        lane = lax.broadcasted_iota(jnp.int32, rec.shape, 1)
        gate = jnp.sum(jnp.where((lane < TOP_K) & (ids == expert), rec, 0.0),
                       axis=1, keepdims=True)
        add_rows(j - 1, rows[0])
        a = jnp.dot(xb_ref[:, :half], wg_ref[0, :half], preferred_element_type=F32)
        add_rows(j - 1, rows[1])
        a = a + jnp.dot(xb_ref[:, half:], wg_ref[0, half:], preferred_element_type=F32)
        add_rows(j - 1, rows[2])
        b = jnp.dot(xb_ref[:, :half], wu_ref[0, :half], preferred_element_type=F32)
        add_rows(j - 1, rows[3])
        b = b + jnp.dot(xb_ref[:, half:], wu_ref[0, half:], preferred_element_type=F32)
        act = (jax.nn.silu(a) * b * gate).astype(BF16)
        n_col = D_MODEL // n_slice
        for piece in range(n_slice):
            fetch_rows(j + 1, rows[piece])
            out = jnp.dot(act, wd_ref[0, :, piece * n_col:(piece + 1) * n_col],
                          preferred_element_type=F32)
            for cc in range(n_col // LANES):
                c = piece * (n_col // LANES) + cc
                ot_ref[c * OT_PITCH:c * OT_PITCH + tm, :] = out[:, cc * LANES:(cc + 1) * LANES]
        return carry

    lax.fori_loop(1, nt + 1, stage, 0)

    @pl.when(s % N_EXPERTS == N_EXPERTS - 1)
    def _():
        add_rows(nt, all_rows)


def _dispatch_tables(eid_t, counts, n_tok):
    tm, ts = MOE_TILE, MOE_BLOCK_TOKENS
    ng = MOE_BLOCKS * N_EXPERTS
    assert ts < (1 << TOKEN_BITS)
    tok = jnp.arange(n_tok, dtype=jnp.int32)
    grp = (tok // ts)[None, :] * N_EXPERTS + eid_t
    gids = jnp.arange(ng, dtype=jnp.int32)
    npad = (-counts) % tm
    fill = jnp.arange(tm, dtype=jnp.int32)[None, :] < npad[:, None]
    n_extra = (-n_tok * TOP_K) % tm
    keys = jnp.concatenate([
        ((grp << TOKEN_BITS) | (tok % ts)[None, :]).reshape(-1),
        ((jnp.where(fill, gids[:, None], ng) << TOKEN_BITS) | ts).reshape(-1),
        jnp.full((n_extra,), (ng << TOKEN_BITS) | ts, jnp.int32)])
    toks = lax.sort(keys, dimension=0, is_stable=False) & ((1 << TOKEN_BITS) - 1)
    n_tiles = (n_tok * TOP_K + n_extra + ng * tm) // tm

    ntile = (counts + npad) // tm
    tile_start = jnp.cumsum(ntile) - ntile
    src = jnp.where(toks < ts, toks, 0) * PACK_ROWS
    dst = toks * ACC_ROWS
    head = jnp.zeros((tm,), jnp.int32)
    tail = jnp.zeros((GROUP_TILES * tm,), jnp.int32)
    rows = n_tiles + 1 + GROUP_TILES
    return {"t0": tile_start.astype(jnp.int32), "nt": ntile.astype(jnp.int32),
            "src": jnp.concatenate([head, src, tail]).reshape(rows, 1, tm),
            "dst": jnp.concatenate([head, dst, tail]).reshape(rows, 1, tm)}


def _moe(hp, meta, tabs, ep):
    tm, ts = MOE_TILE, MOE_BLOCK_TOKENS
    assert GROUP_TILES >= -(-ts // tm) + 2
    window = pl.BlockSpec((pl.Element(GROUP_TILES), pl.Element(1), pl.Element(tm)),
                          lambda s, t0, nt: (t0[s], 0, 0), memory_space=pltpu.SMEM)
    wspec = lambda shape: pl.BlockSpec((1,) + shape, lambda s, t0, nt: (s % N_EXPERTS, 0, 0))
    blk = lambda s, t0, nt: (s // N_EXPERTS, 0)
    acc_rows = (ts + SUBLANES) * ACC_ROWS
    assert acc_rows % ZERO_ROWS == 0
    stage_bufs = [pltpu.VMEM((PACK_ROWS * XT_PITCH, LANES), F32),
                  pltpu.VMEM((META_ROWS * XT_PITCH, LANES), F32),
                  pltpu.VMEM((ACC_ROWS * OT_PITCH, LANES), F32),
                  pltpu.VMEM((tm, D_MODEL), BF16),
                  pltpu.VMEM((1, D_MODEL, D_EXPERT), BF16),
                  pltpu.VMEM((1, D_MODEL, D_EXPERT), BF16),
                  pltpu.VMEM((1, D_EXPERT, D_MODEL), BF16)]
    grid_spec = pltpu.PrefetchScalarGridSpec(
        num_scalar_prefetch=2,
        grid=(MOE_BLOCKS * N_EXPERTS,),
        in_specs=[window, window,
                  pl.BlockSpec((ts * PACK_ROWS, LANES), blk, pipeline_mode=pl.Buffered(1)),
                  pl.BlockSpec((ts * META_ROWS, LANES), blk, pipeline_mode=pl.Buffered(1)),
                  wspec((D_MODEL, D_EXPERT)), wspec((D_MODEL, D_EXPERT)),
                  wspec((D_EXPERT, D_MODEL))],
        out_specs=pl.BlockSpec((1, acc_rows, LANES), lambda s, t0, nt: (s // N_EXPERTS, 0, 0),
                               pipeline_mode=pl.Buffered(1)),
        scratch_shapes=stage_bufs,
    )
    return pl.pallas_call(
        _moe_kernel,
        grid_spec=grid_spec,
        out_shape=jax.ShapeDtypeStruct((MOE_BLOCKS, acc_rows, LANES), F32),
        compiler_params=_cparams("arbitrary"),
        name="moe_experts",
    )(tabs["t0"], tabs["nt"], tabs["src"], tabs["dst"], hp, meta, ep["wg"], ep["wu"], ep["wd"])


def _final_kernel(xs_ref, r_ref, fn_ref, y_ref):
    tm = xs_ref.shape[0]
    routed = jnp.concatenate(
        [r_ref[0, pl.ds(c, tm, stride=ACC_ROWS), :] for c in range(ACC_ROWS)], axis=1)
    y_ref[...] = _rms(xs_ref[...] + routed, fn_ref[...])


def _finalize(xs, routed, final_norm, tm, tok0):
    rows = xs.shape[0]
    per_block = MOE_BLOCK_TOKENS // tm
    first = tok0 // tm
    assert tok0 % tm == 0 and MOE_BLOCK_TOKENS % tm == 0
    return pl.pallas_call(
        _final_kernel,
        grid=(rows // tm,),
        in_specs=[pl.BlockSpec((tm, D_MODEL), lambda i: (i, 0)),
                  pl.BlockSpec((1, tm * ACC_ROWS, LANES),
                               lambda i: ((first + i) // per_block, (first + i) % per_block, 0)),
                  _const_spec((1, D_MODEL))],
        out_specs=pl.BlockSpec((tm, D_MODEL), lambda i: (i, 0)),
        out_shape=jax.ShapeDtypeStruct((rows, D_MODEL), F32),
        compiler_params=_cparams("parallel"),
        name="final_norm",
    )(xs, routed, final_norm)


def _rope_tables(pos):
    half = RET_HEAD_DIM // 2
    inv_freq = ROPE_BASE ** (-jnp.arange(half, dtype=F32) / half)
    ang = pos.astype(F32)[:, None] * inv_freq[None, :]
    cos, sin = jnp.cos(ang), jnp.sin(ang)
    cos_t = jnp.concatenate([cos, cos, cos, cos], axis=1)
    sin_t = jnp.concatenate([-sin, sin, -sin, sin], axis=1)
    return cos_t, sin_t


def kernel(x_prompt, x_sample, state_s5_re, state_s5_im, state_ret, meta_tokens, norm1, w_in,
           s5_lam_re, s5_lam_im, s5_log_dt, s5_b_re, s5_b_im, s5_c_re, s5_c_im, s5_d, s5_w_glu,
           s5_norm, ret_norm, w_out, norm2, w_router, router_bias, w_gate_e, w_up_e, w_down_e,
           w_gate_sh, w_up_sh, w_down_sh, final_norm):
    assert norm1.shape[0] == 1, "single-layer model"
    bp, seq, _ = x_prompt.shape
    ns = x_sample.shape[0]
    l = 0

    n1 = norm1[l].reshape(1, D_MODEL)
    w_in_b = w_in[l].astype(BF16)
    s5p = _s5_params(s5_lam_re[l], s5_lam_im[l], s5_log_dt[l], s5_b_re[l], s5_b_im[l],
                     s5_c_re[l], s5_c_im[l], s5_d[l], s5_w_glu[l], s5_norm[l])
    rnorm = ret_norm[l].reshape(1, D_RET)
    w_out_b = w_out[l].astype(BF16)
    wr_t = w_router[l].T
    wr_hi = wr_t.astype(BF16)
    wr_lo = (wr_t - wr_hi.astype(F32)).astype(BF16)
    mp = {
        "woa": w_out_b[:D_S5], "wob": w_out_b[D_S5:], "n2": norm2[l].reshape(1, D_MODEL),
        "wrt": jnp.stack([wr_hi, wr_lo]),
        "rb": jnp.broadcast_to(router_bias[l][:, None], (N_EXPERTS, LANES)),
        "wgs": w_gate_sh[l].astype(BF16), "wus": w_up_sh[l].astype(BF16),
        "wds": w_down_sh[l].astype(BF16),
    }
    ep = {"wg": w_gate_e[l], "wu": w_up_e[l],
          "wd": w_down_e[l]}
    fnorm = final_norm.reshape(1, D_MODEL)

    xp = x_prompt.reshape(bp * seq, D_MODEL)
    cos_p, sin_p = _rope_tables(N_META + jnp.arange(seq, dtype=jnp.int32))
    tm_a = 512
    up, qp, kp, vp, gp = _project(xp, n1, w_in_b, cos_p, sin_p, tm_a, seq // tm_a)

    meta_tm = jnp.repeat(meta_tokens, SUBLANES, axis=0)
    mchunk = LANES
    meta_chunk = jnp.concatenate([jnp.zeros((mchunk - N_META, D_MODEL), F32), meta_tokens], axis=0)
    x_small = jnp.concatenate([x_sample.reshape(ns, D_MODEL), meta_tm, meta_chunk], axis=0)
    meta_pos = jnp.arange(N_META, dtype=jnp.int32)
    pos_small = jnp.concatenate([jnp.full((ns,), PAST_LEN, jnp.int32),
                                 jnp.repeat(meta_pos, SUBLANES),
                                 jnp.zeros((mchunk - N_META,), jnp.int32), meta_pos])
    cos_s, sin_s = _rope_tables(pos_small)
    n_small = x_small.shape[0]
    n_tm = ns + N_META * SUBLANES
    us, qs, ks, vs, gs = _project(x_small, n1, w_in_b, cos_s, sin_s, n_small, 1)

    zero8 = jnp.zeros((SUBLANES, S5_LANES), F32)
    _, m_re, m_im = _s5(us[ns:n_tm], zero8, zero8, s5p, nb=SUBLANES, tt=N_META, nblk=1,
                        bt_major=False)
    tt = 64
    ys5_p, p_re, p_im = _s5(up.reshape(bp, seq, D_S5), m_re, m_im, s5p, nb=bp, tt=tt,
                            nblk=seq // tt, bt_major=True)
    ys5_p = ys5_p.reshape(bp * seq, D_S5)
    ys5_s, s_re, s_im = _s5(us[:ns], state_s5_re[l].reshape(ns, S5_LANES),
                            state_s5_im[l].reshape(ns, S5_LANES), s5p, nb=ns, tt=1, nblk=1,
                            bt_major=False)

    zero_pair = jnp.zeros((1, N_HEAD_PAIRS, LANES, LANES), F32)
    _, m_pair = _retention(qs[n_tm:], ks[n_tm:], vs[n_tm:], gs[n_tm:],
                           zero_pair, rnorm, nseq=1, chunk=mchunk, nchunk=1)
    chunk = 256
    yret_p, p_pair = _retention(qp, kp, vp, gp, jnp.broadcast_to(m_pair, (bp,) + m_pair.shape[1:]),
                                rnorm, nseq=bp, chunk=chunk, nchunk=seq // chunk)
    yret_s, ret_s = _retention_step(qs[:ns], ks[:ns], vs[:ns], gs[:ns], state_ret[l], ret_norm[l])

    n_prompt = bp * seq
    assert n_prompt + ns <= MOE_BLOCKS * MOE_BLOCK_TOKENS
    n_moe = MOE_BLOCKS * MOE_BLOCK_TOKENS
    tm_mix = 512
    xs_p, hp, meta, eid_p, cnt_p = _mix(xp, ys5_p, yret_p, mp, tm_mix, n_moe)
    xs_s, hp_s, meta_s, eid_s, cnt_s = _mix(x_sample.reshape(ns, D_MODEL), ys5_s, yret_s, mp,
                                            ns, ns)
    hp = lax.dynamic_update_slice(hp, hp_s, (n_prompt * PACK_ROWS, 0))
    meta = lax.dynamic_update_slice(meta, meta_s, (n_prompt * META_ROWS, 0))
    blk_s = n_prompt // MOE_BLOCK_TOKENS
    assert MOE_BLOCK_TOKENS % tm_mix == 0 and (n_prompt + ns - 1) // MOE_BLOCK_TOKENS == blk_s
    counts = cnt_p[:, :, 0].reshape(MOE_BLOCKS, MOE_BLOCK_TOKENS // tm_mix, N_EXPERTS).sum(axis=1)
    counts = counts.at[blk_s].add(cnt_s[0, :, 0]).reshape(-1).astype(jnp.int32)
    tabs = _dispatch_tables(jnp.concatenate([eid_p, eid_s], axis=1), counts, n_prompt + ns)
    routed = _moe(hp, meta, tabs, ep)
    y_p = _finalize(xs_p, routed, fnorm, 512, 0)
    y_s = _finalize(xs_s, routed, fnorm, ns, n_prompt)

    shape5 = (1, bp, N_S5_GROUPS, S5_STATE)
    return (y_p.reshape(bp, seq, D_MODEL),
            y_s.reshape(ns, 1, D_MODEL),
            p_re.reshape(shape5), p_im.reshape(shape5),
            _unpair_state(p_pair)[None],
            s_re.reshape(1, ns, N_S5_GROUPS, S5_STATE), s_im.reshape(1, ns, N_S5_GROUPS, S5_STATE),
            ret_s[None])
```

```python
import functools
import math

import jax
import jax.numpy as jnp
from jax import lax
from jax.experimental import pallas as pl
from jax.experimental.pallas import tpu as pltpu

F32 = jnp.float32
BF16 = jnp.bfloat16

D_MODEL = 1024
N_META = 16
PAST_LEN = 16384
D_S5 = 512
S5_GROUP = 16
N_S5_GROUPS = 32
S5_STATE = 64
S5_LANES = N_S5_GROUPS * S5_STATE
D_RET = 512
N_RET_HEADS = 8
RET_HEAD_DIM = 64
N_HEAD_PAIRS = 4
ROPE_BASE = 10000.0
D_IN = D_S5 + 4 * D_RET
N_EXPERTS = 64
TOP_K = 8
N_EXPERT_GROUPS = 8
GROUP_SIZE = 8
TOPK_GROUPS = 4
D_EXPERT = 256
ROUTED_SCALE = 2.5
EPS = 1e-6

LANES = 128
SUBLANES = 8
VMEM_LIMIT = 62 * 1024 * 1024

PACK_ROWS = D_MODEL // LANES
ACC_ROWS = D_MODEL // LANES
MOE_BLOCKS = 3
MOE_BLOCK_TOKENS = 5632
MOE_TILE = 256
GROUP_TILES = MOE_BLOCK_TOKENS // MOE_TILE + 2
ADD_UNROLL = 8
OT_PITCH = MOE_TILE + SUBLANES
XT_PITCH = MOE_TILE + SUBLANES
META_ROWS = 2
META_SHIFT = 2
assert PACK_ROWS == META_ROWS << META_SHIFT
TOKEN_BITS = 13
ZERO_ROWS = 64
CAST_ROWS = 128


def _cparams(*sem):
    return pltpu.CompilerParams(dimension_semantics=sem, vmem_limit_bytes=VMEM_LIMIT)


def _const_spec(shape):
    nd = len(shape)
    return pl.BlockSpec(shape, lambda *_: (0,) * nd)


def _rms(x, gain):
    return x * lax.rsqrt(jnp.mean(x * x, axis=-1, keepdims=True) + EPS) * gain


def _proj_kernel(x_ref, n1_ref, w_ref, cos_ref, sin_ref, u_ref, q_ref, k_ref, v_ref, g_ref):
    h = _rms(x_ref[...], n1_ref[...]).astype(BF16)
    proj = jnp.dot(h, w_ref[...], preferred_element_type=F32)
    cos = cos_ref[...]
    sin = sin_ref[...]
    lane = lax.broadcasted_iota(jnp.int32, cos.shape, 1)
    first_half = (lane % RET_HEAD_DIM) < (RET_HEAD_DIM // 2)

    def rotary(t):
        partner = jnp.where(first_half,
                            pltpu.roll(t, LANES - RET_HEAD_DIM // 2, 1),
                            pltpu.roll(t, RET_HEAD_DIM // 2, 1))
        return t * cos + partner * sin

    u_ref[...] = proj[:, :D_S5]
    for j in range(N_HEAD_PAIRS):
        lo = D_S5 + j * LANES
        q_ref[:, j * LANES:(j + 1) * LANES] = rotary(proj[:, lo:lo + LANES]).astype(BF16)
        lo += D_RET
        k_ref[:, j * LANES:(j + 1) * LANES] = (
            rotary(proj[:, lo:lo + LANES]) * (RET_HEAD_DIM ** -0.5)).astype(BF16)
    v_ref[...] = proj[:, D_S5 + 2 * D_RET:D_S5 + 3 * D_RET].astype(BF16)
    g_ref[...] = proj[:, D_S5 + 3 * D_RET:]


def _project(x, norm1, w_in_b, cos_t, sin_t, tm, table_blocks):
    rows = x.shape[0]
    row = lambda i: (i, 0)
    tab = lambda i: (i % table_blocks, 0)
    return pl.pallas_call(
        _proj_kernel,
        grid=(rows // tm,),
        in_specs=[pl.BlockSpec((tm, D_MODEL), row),
                  _const_spec((1, D_MODEL)),
                  _const_spec((D_MODEL, D_IN)),
                  pl.BlockSpec((tm, LANES), tab),
                  pl.BlockSpec((tm, LANES), tab)],
        out_specs=[pl.BlockSpec((tm, D_S5), row),
                   pl.BlockSpec((tm, D_RET), row),
                   pl.BlockSpec((tm, D_RET), row),
                   pl.BlockSpec((tm, D_RET), row),
                   pl.BlockSpec((tm, D_RET), row)],
        out_shape=[jax.ShapeDtypeStruct((rows, D_S5), F32),
                   jax.ShapeDtypeStruct((rows, D_RET), BF16),
                   jax.ShapeDtypeStruct((rows, D_RET), BF16),
                   jax.ShapeDtypeStruct((rows, D_RET), BF16),
                   jax.ShapeDtypeStruct((rows, D_RET), F32)],
        compiler_params=_cparams("parallel"),
        name="in_proj",
    )(x, norm1, w_in_b, cos_t, sin_t)


def _s5_kernel(u_ref, x0re_ref, x0im_ref, are_ref, aim_ref, b_ref, c_ref, d_ref, wglu_ref,
               nrm_ref, y_ref, sre_ref, sim_ref, st, sre, sim, utm, ytm, *, nb, tt, bt_major):
    i = pl.program_id(0)
    rows = nb * tt

    @pl.when(i == 0)
    def _():
        sre[...] = x0re_ref[...]
        sim[...] = x0im_ref[...]

    if bt_major:
        for t in range(tt):
            utm[t * nb:(t + 1) * nb, :] = u_ref[:, t, :]
    else:
        utm[...] = u_ref[...]

    ub = utm[...].astype(BF16)
    half = S5_LANES // 2
    kh = D_S5 // 2
    for part in range(2):
        for hf in range(2):
            st[:, part * S5_LANES + hf * half:part * S5_LANES + (hf + 1) * half] = jnp.dot(
                ub[:, hf * kh:(hf + 1) * kh], b_ref[part, hf], preferred_element_type=F32)

    lw = 512 if nb == SUBLANES else LANES
    for lg in range(S5_LANES // lw):
        re = slice(lg * lw, (lg + 1) * lw)
        im = slice(S5_LANES + lg * lw, S5_LANES + (lg + 1) * lw)
        a_re = are_ref[:, re]
        a_im = aim_ref[:, re]

        def step(t, carry):
            s_re, s_im = carry
            r0 = pl.multiple_of(t * nb, nb)
            n_re = a_re * s_re - a_im * s_im + st[pl.ds(r0, nb), re]
            n_im = a_re * s_im + a_im * s_re + st[pl.ds(r0, nb), im]
            st[pl.ds(r0, nb), re] = n_re
            st[pl.ds(r0, nb), im] = n_im
            return n_re, n_im

        if nb == SUBLANES:
            f_re, f_im = sre[:, re], sim[:, re]
            for t in range(tt):
                r = slice(t * nb, (t + 1) * nb)
                n_re = a_re * f_re - a_im * f_im + st[r, re]
                n_im = a_re * f_im + a_im * f_re + st[r, im]
                st[r, re] = n_re
                st[r, im] = n_im
                f_re, f_im = n_re, n_im
        else:
            f_re, f_im = lax.fori_loop(0, tt, step, (sre[:, re], sim[:, re]))
        sre[:, re] = f_re
        sim[:, re] = f_im

    sre_ref[...] = sre[...]
    sim_ref[...] = sim[...]

    ys = []
    for hf in range(2):
        xr = st[:, hf * half:(hf + 1) * half].astype(BF16)
        xi = st[:, S5_LANES + hf * half:S5_LANES + (hf + 1) * half].astype(BF16)
        ys.append(jnp.dot(xr, c_ref[0, hf], preferred_element_type=F32)
                  - jnp.dot(xi, c_ref[1, hf], preferred_element_type=F32))
    y = jnp.concatenate(ys, axis=1) + d_ref[...] * utm[...]
    y = jax.nn.gelu(y)
    y = y * jax.nn.sigmoid(jnp.dot(y.astype(BF16), wglu_ref[...], preferred_element_type=F32))
    y = _rms(y, nrm_ref[...]).astype(BF16)
    if bt_major:
        yf = y.astype(F32)
        for j in range(D_S5 // LANES):
            ytm[j] = yf[:, j * LANES:(j + 1) * LANES]
        for b in range(nb):
            for j in range(D_S5 // LANES):
                y_ref[b, :, j * LANES:(j + 1) * LANES] = (
                    ytm[j, pl.ds(b, tt, stride=nb), :].astype(BF16))
    else:
        y_ref[...] = y


def _s5(u, x0re, x0im, s5p, *, nb, tt, nblk, bt_major):
    rows = nb * tt
    if bt_major:
        u_spec = pl.BlockSpec((nb, tt, D_S5), lambda i: (0, i, 0))
        y_shape = jax.ShapeDtypeStruct((nb, tt * nblk, D_S5), BF16)
    else:
        u_spec = pl.BlockSpec((rows, D_S5), lambda i: (i, 0))
        y_shape = jax.ShapeDtypeStruct((rows * nblk, D_S5), BF16)
    state_spec = _const_spec((nb, S5_LANES))
    are = jnp.broadcast_to(s5p["are"], (nb, S5_LANES))
    aim = jnp.broadcast_to(s5p["aim"], (nb, S5_LANES))
    return pl.pallas_call(
        functools.partial(_s5_kernel, nb=nb, tt=tt, bt_major=bt_major),
        grid=(nblk,),
        in_specs=[u_spec, state_spec, state_spec, state_spec, state_spec,
                  _const_spec((2, 2, D_S5 // 2, S5_LANES // 2)),
                  _const_spec((2, 2, S5_LANES // 2, D_S5 // 2)),
                  _const_spec((1, D_S5)),
                  _const_spec((D_S5, D_S5)),
                  _const_spec((1, D_S5))],
        out_specs=[u_spec, state_spec, state_spec],
        out_shape=[y_shape,
                   jax.ShapeDtypeStruct((nb, S5_LANES), F32),
                   jax.ShapeDtypeStruct((nb, S5_LANES), F32)],
        scratch_shapes=[pltpu.VMEM((rows, 2 * S5_LANES), F32),
                        pltpu.VMEM((nb, S5_LANES), F32),
                        pltpu.VMEM((nb, S5_LANES), F32),
                        pltpu.VMEM((rows, D_S5), F32),
                        pltpu.VMEM((D_S5 // LANES, rows, LANES), F32)],
        compiler_params=_cparams("arbitrary"),
        name="s5_" + ("bt" if bt_major else "tm") + str(nb),
    )(u, x0re, x0im, are, aim, s5p["b"], s5p["c"], s5p["d"], s5p["wglu"], s5p["nrm"])


def _s5_params(lam_re, lam_im, log_dt, b_re, b_im, c_re, c_im, d_skip, w_glu, nrm):
    dt = jnp.exp(log_dt)[:, None]
    mag = jnp.exp(lam_re * dt)
    abar_re, abar_im = mag * jnp.cos(lam_im * dt), mag * jnp.sin(lam_im * dt)
    num_re, num_im = abar_re - 1.0, abar_im
    den = lam_re * lam_re + lam_im * lam_im
    f_re = (num_re * lam_re + num_im * lam_im) / den
    f_im = (num_im * lam_re - num_re * lam_im) / den
    bbar_re = f_re[..., None] * b_re - f_im[..., None] * b_im
    bbar_im = f_re[..., None] * b_im + f_im[..., None] * b_re
    hg = N_S5_GROUPS // 2
    eye = jnp.eye(hg, dtype=F32)

    def bdiag(bb):
        bb = bb.reshape(2, hg, S5_STATE, S5_GROUP)
        return jnp.einsum("zgph,gk->zghkp", bb, eye).reshape(2, hg * S5_GROUP, hg * S5_STATE)

    def cdiag(cc):
        cc = cc.reshape(2, hg, S5_GROUP, S5_STATE)
        return jnp.einsum("zgnp,gk->zgpkn", cc, eye).reshape(2, hg * S5_STATE, hg * S5_GROUP)

    return {
        "are": abar_re.reshape(1, S5_LANES), "aim": abar_im.reshape(1, S5_LANES),
        "b": jnp.stack([bdiag(bbar_re), bdiag(bbar_im)]).astype(BF16),
        "c": jnp.stack([cdiag(c_re), cdiag(c_im)]).astype(BF16),
        "d": d_skip.reshape(1, D_S5), "wglu": w_glu.astype(BF16), "nrm": nrm.reshape(1, D_S5),
    }


def _head_norm_gate(o, g, gain, lo):
    inv = 1.0 / RET_HEAD_DIM

    def seg_mean(t):
        s_lo = jnp.sum(jnp.where(lo, t, 0.0), axis=1, keepdims=True)
        s_hi = jnp.sum(jnp.where(lo, 0.0, t), axis=1, keepdims=True)
        return jnp.where(lo, s_lo, s_hi) * inv

    dlt = o - seg_mean(o)
    var = seg_mean(dlt * dlt)
    return jax.nn.silu(g) * (dlt * lax.rsqrt(var + EPS) * gain)


def _ret_kernel(q_ref, k_ref, v_ref, g_ref, s0_ref, dm_ref, qd_ref, kd_ref, gc_ref, bm_ref,
                nrm_ref, y_ref, so_ref, s_acc):
    c = pl.program_id(1)

    @pl.when(c == 0)
    def _():
        s_acc[...] = s0_ref[0]

    rows = q_ref.shape[0]
    lane = lax.broadcasted_iota(jnp.int32, (rows, LANES), 1)
    lo = lane < RET_HEAD_DIM
    for j in range(N_HEAD_PAIRS):
        sl = slice(j * LANES, (j + 1) * LANES)
        q2 = q_ref[:, sl].astype(F32)
        k2 = k_ref[:, sl]
        v2 = v_ref[:, sl]
        s_pair = s_acc[j]
        cross = jnp.dot((q2 * qd_ref[:, sl]).astype(BF16), s_pair.astype(BF16),
                        preferred_element_type=F32)
        k_dec = (k2.astype(F32) * kd_ref[:, sl]).astype(BF16)
        upd = lax.dot_general(k_dec, v2, (((0,), (0,)), ((), ())), preferred_element_type=F32)
        s_acc[j] = gc_ref[j] * s_pair + bm_ref[...] * upd
        inner = []
        for hh, qh in enumerate((jnp.where(lo, q2, 0.0), jnp.where(lo, 0.0, q2))):
            sc = lax.dot_general(qh.astype(BF16), k2, (((1,), (1,)), ((), ())),
                                 preferred_element_type=F32) * dm_ref[2 * j + hh]
            inner.append(jnp.dot(sc.astype(BF16), v2, preferred_element_type=F32))
        o = jnp.where(lo, inner[0], inner[1]) + cross
        y_ref[:, sl] = _head_norm_gate(o, g_ref[:, sl], nrm_ref[:, sl], lo).astype(BF16)
    so_ref[0] = s_acc[...]


def _ret_tables(chunk):
    log_g = jnp.log1p(-jnp.exp2(-5.0 - jnp.arange(N_RET_HEADS, dtype=F32)))
    n = jnp.arange(chunk, dtype=F32)
    diff = n[:, None] - n[None, :]
    dm = jnp.where(diff[None] >= 0.0,
                   jnp.exp(log_g[:, None, None] * jnp.maximum(diff, 0.0)[None]), 0.0)
    per_lane = lambda t: jnp.repeat(t.T, RET_HEAD_DIM, axis=1)
    qd = per_lane(jnp.exp(log_g[:, None] * (n + 1.0)[None]))
    kd = per_lane(jnp.exp(log_g[:, None] * (chunk - 1.0 - n)[None]))
    gch = jnp.exp(log_g * chunk)
    blk = jnp.kron(jnp.eye(2, dtype=F32), jnp.ones((RET_HEAD_DIM, RET_HEAD_DIM), F32))
    gc = jnp.repeat(gch.reshape(N_HEAD_PAIRS, 2), RET_HEAD_DIM, axis=1)[:, :, None] * blk[None]
    return dm, qd, kd, gc, blk


def _pair_state(s):
    b = s.shape[0]
    s = s.reshape(b, N_HEAD_PAIRS, 2, RET_HEAD_DIM, RET_HEAD_DIM)
    z = jnp.zeros_like(s[:, :, 0])
    top = jnp.concatenate([s[:, :, 0], z], axis=-1)
    bot = jnp.concatenate([z, s[:, :, 1]], axis=-1)
    return jnp.concatenate([top, bot], axis=-2)


def _unpair_state(s2):
    d = RET_HEAD_DIM
    return jnp.stack([s2[:, :, :d, :d], s2[:, :, d:, d:]], axis=2).reshape(
        s2.shape[0], N_RET_HEADS, d, d)


def _retention(q, k, v, g, s0_pair, ret_norm, *, nseq, chunk, nchunk):
    dm, qd, kd, gc, blk = _ret_tables(chunk)
    row = lambda b, c: (b * nchunk + c, 0)
    st_spec = pl.BlockSpec((1, N_HEAD_PAIRS, LANES, LANES), lambda b, c: (b, 0, 0, 0))
    blk_spec = pl.BlockSpec((chunk, D_RET), row)
    return pl.pallas_call(
        _ret_kernel,
        grid=(nseq, nchunk),
        in_specs=[blk_spec, blk_spec, blk_spec, blk_spec, st_spec,
                  _const_spec((N_RET_HEADS, chunk, chunk)),
                  _const_spec((chunk, D_RET)), _const_spec((chunk, D_RET)),
                  _const_spec((N_HEAD_PAIRS, LANES, LANES)), _const_spec((LANES, LANES)),
                  _const_spec((1, D_RET))],
        out_specs=[blk_spec, st_spec],
        out_shape=[jax.ShapeDtypeStruct((nseq * nchunk * chunk, D_RET), BF16),
                   jax.ShapeDtypeStruct((nseq, N_HEAD_PAIRS, LANES, LANES), F32)],
        scratch_shapes=[pltpu.VMEM((N_HEAD_PAIRS, LANES, LANES), F32)],
        compiler_params=_cparams("arbitrary", "arbitrary"),
        name="retention_c%d" % chunk,
    )(q, k, v, g, s0_pair, dm, qd, kd, gc, blk, ret_norm)


def _ret_step_kernel(q_ref, k_ref, v_ref, g_ref, s_ref, gam_ref, nrm_ref, y_ref, so_ref):
    q = q_ref[...]
    k = k_ref[...]
    v = v_ref[...]
    gam = gam_ref[0]
    score = jnp.sum(q * k, axis=0, keepdims=True)
    q_dec = (q * gam).astype(BF16).astype(F32)
    cross = jnp.zeros(v.shape, F32)
    for d in range(RET_HEAD_DIM):
        s_d = s_ref[0, d]
        cross = cross + q_dec[d:d + 1, :] * s_d
        so_ref[0, d] = gam * s_d + k[d:d + 1, :] * v
    o = score.astype(BF16).astype(F32) * v + cross
    dlt = o - jnp.mean(o, axis=0, keepdims=True)
    var = jnp.mean(dlt * dlt, axis=0, keepdims=True)
    y_ref[...] = jax.nn.silu(g_ref[...]) * (dlt * lax.rsqrt(var + EPS) * nrm_ref[...])


def _retention_step(q, k, v, g, state, ret_norm):
    n = q.shape[0]
    gam = jnp.exp(jnp.log1p(-jnp.exp2(-5.0 - jnp.arange(N_RET_HEADS, dtype=F32))))
    gam_tab = jnp.broadcast_to(gam[:, None, None], (N_RET_HEADS, 1, n))
    seq_minor = lambda t: t.astype(F32).T
    head = lambda h: (h, 0)
    vec_spec = pl.BlockSpec((RET_HEAD_DIM, n), head)
    st_spec = pl.BlockSpec((1, RET_HEAD_DIM, RET_HEAD_DIM, n), lambda h: (h, 0, 0, 0))
    y_t, s_new = pl.pallas_call(
        _ret_step_kernel,
        grid=(N_RET_HEADS,),
        in_specs=[vec_spec, vec_spec, vec_spec, vec_spec, st_spec,
                  pl.BlockSpec((1, 1, n), lambda h: (h, 0, 0)), vec_spec],
        out_specs=[vec_spec, st_spec],
        out_shape=[jax.ShapeDtypeStruct((D_RET, n), F32),
                   jax.ShapeDtypeStruct((N_RET_HEADS, RET_HEAD_DIM, RET_HEAD_DIM, n), F32)],
        compiler_params=_cparams("parallel"),
        name="retention_step",
    )(seq_minor(q), seq_minor(k), seq_minor(v), seq_minor(g),
      jnp.transpose(state, (1, 2, 3, 0)), gam_tab,
      jnp.broadcast_to(ret_norm[:, None], (D_RET, n)))
    return y_t.T.astype(BF16), jnp.transpose(s_new, (3, 0, 1, 2))


def _first_max(vals, idxs, sentinel):
    m = jnp.max(vals, axis=0, keepdims=True)
    first = jnp.min(jnp.where(vals == m, idxs, sentinel), axis=0, keepdims=True)
    return m, idxs == first


def _route(scores, sel):
    t = sel.shape[1]
    neg = -jnp.inf
    member = lax.broadcasted_iota(jnp.int32, (GROUP_SIZE, t), 0).astype(F32)
    groups = [sel[g * GROUP_SIZE:(g + 1) * GROUP_SIZE, :] for g in range(N_EXPERT_GROUPS)]
    gscore = []
    for grp in groups:
        m1, pick = _first_max(grp, member, float(GROUP_SIZE))
        m2 = jnp.max(jnp.where(pick, neg, grp), axis=0, keepdims=True)
        gscore.append(m1 + m2)
    gs = jnp.concatenate(gscore, axis=0)
    gkeep = jnp.zeros(gs.shape, F32)
    for _ in range(TOPK_GROUPS):
        _, pick = _first_max(gs, member, float(N_EXPERT_GROUPS))
        gkeep = jnp.where(pick, 1.0, gkeep)
        gs = jnp.where(pick, neg, gs)
    cand = jnp.concatenate(
        [jnp.where(gkeep[g:g + 1, :] > 0.5, groups[g], neg) for g in range(N_EXPERT_GROUPS)],
        axis=0)
    expert = lax.broadcasted_iota(jnp.int32, (N_EXPERTS, t), 0).astype(F32)
    ids, ws = [], []
    chosen = jnp.zeros(cand.shape, F32)
    for _ in range(TOP_K):
        _, pick = _first_max(cand, expert, float(N_EXPERTS))
        ids.append(jnp.sum(jnp.where(pick, expert, 0.0), axis=0, keepdims=True))
        ws.append(jnp.sum(jnp.where(pick, scores, 0.0), axis=0, keepdims=True))
        chosen = jnp.where(pick, 1.0, chosen)
        cand = jnp.where(pick, neg, cand)
    w = jnp.concatenate(ws, axis=0)
    gates = w / jnp.sum(w, axis=0, keepdims=True) * ROUTED_SCALE
    return jnp.concatenate(ids, axis=0), gates, jnp.sum(chosen, axis=1, keepdims=True)


def _mix_kernel(x_ref, ys_ref, yr_ref, woa_ref, wob_ref, n2_ref, wrt_ref, rb_ref,
                wgs_ref, wus_ref, wds_ref, xs_ref, hp_ref, meta_ref, eid_ref, cnt_ref, *,
                n_blocks):
    @pl.when(pl.program_id(0) >= n_blocks)
    def _():
        hp_ref[...] = jnp.zeros_like(hp_ref)
        meta_ref[...] = jnp.zeros_like(meta_ref)
        cnt_ref[...] = jnp.zeros_like(cnt_ref)

    @pl.when(pl.program_id(0) < n_blocks)
    def _():
        _mix_body(x_ref, ys_ref, yr_ref, woa_ref, wob_ref, n2_ref, wrt_ref, rb_ref,
                  wgs_ref, wus_ref, wds_ref, xs_ref, hp_ref, meta_ref, eid_ref, cnt_ref)


def _mix_body(x_ref, ys_ref, yr_ref, woa_ref, wob_ref, n2_ref, wrt_ref, rb_ref,
              wgs_ref, wus_ref, wds_ref, xs_ref, hp_ref, meta_ref, eid_ref, cnt_ref):
    tm = x_ref.shape[0]
    x2 = (x_ref[...]
          + jnp.dot(ys_ref[...], woa_ref[...], preferred_element_type=F32)
          + jnp.dot(yr_ref[...], wob_ref[...], preferred_element_type=F32))
    h2f = _rms(x2, n2_ref[...])
    h2 = h2f.astype(BF16)
    h2r = h2.astype(F32)
    for c in range(PACK_ROWS):
        hp_ref[pl.ds(c, tm, stride=PACK_ROWS), :] = h2r[:, c * LANES:(c + 1) * LANES]
    h2_lo = (h2f - h2r).astype(BF16)
    nt_dims = (((1,), (1,)), ((), ()))
    logits = (lax.dot_general(wrt_ref[0], h2, nt_dims, preferred_element_type=F32)
              + lax.dot_general(wrt_ref[1], h2, nt_dims, preferred_element_type=F32)
              + lax.dot_general(wrt_ref[0], h2_lo, nt_dims, preferred_element_type=F32))
    scores = jax.nn.sigmoid(logits)
    ids, gates, counts = _route(scores, scores + rb_ref[:, :1])
    eid_ref[...] = ids.astype(jnp.int32)
    cnt_ref[0] = jnp.broadcast_to(counts, (N_EXPERTS, LANES))
    rec = jnp.concatenate([gates, ids, jnp.zeros((LANES - 2 * TOP_K, tm), F32)], axis=0).T
    meta_ref[pl.ds(0, tm, stride=META_ROWS), :] = rec
    for c in range(1, META_ROWS):
        meta_ref[pl.ds(c, tm, stride=META_ROWS), :] = jnp.zeros((tm, LANES), F32)
    a = jnp.dot(h2, wgs_ref[...], preferred_element_type=F32)
    b = jnp.dot(h2, wus_ref[...], preferred_element_type=F32)
    xs_ref[...] = x2 + jnp.dot((jax.nn.silu(a) * b).astype(BF16), wds_ref[...],
                               preferred_element_type=F32)


def _mix(x, ys5, yret, mp, tm, out_tokens):
    rows = x.shape[0]
    n_blocks = rows // tm
    assert out_tokens % tm == 0
    packed = lambda i: (i, 0)
    row = lambda i: (jnp.minimum(i, n_blocks - 1), 0)
    col = lambda i: (0, jnp.minimum(i, n_blocks - 1))
    return pl.pallas_call(
        functools.partial(_mix_kernel, n_blocks=n_blocks),
        grid=(out_tokens // tm,),
        in_specs=[pl.BlockSpec((tm, D_MODEL), row),
                  pl.BlockSpec((tm, D_S5), row), pl.BlockSpec((tm, D_RET), row),
                  _const_spec((D_S5, D_MODEL)), _const_spec((D_RET, D_MODEL)),
                  _const_spec((1, D_MODEL)),
                  _const_spec((2, N_EXPERTS, D_MODEL)), _const_spec((N_EXPERTS, LANES)),
                  _const_spec((D_MODEL, D_EXPERT)), _const_spec((D_MODEL, D_EXPERT)),
                  _const_spec((D_EXPERT, D_MODEL))],
        out_specs=[pl.BlockSpec((tm, D_MODEL), row),
                   pl.BlockSpec((tm * PACK_ROWS, LANES), packed),
                   pl.BlockSpec((tm * META_ROWS, LANES), packed),
                   pl.BlockSpec((TOP_K, tm), col),
                   pl.BlockSpec((1, N_EXPERTS, LANES), lambda i: (i, 0, 0))],
        out_shape=[jax.ShapeDtypeStruct((rows, D_MODEL), F32),
                   jax.ShapeDtypeStruct((out_tokens * PACK_ROWS, LANES), F32),
                   jax.ShapeDtypeStruct((out_tokens * META_ROWS, LANES), F32),
                   jax.ShapeDtypeStruct((TOP_K, rows), jnp.int32),
                   jax.ShapeDtypeStruct((out_tokens // tm, N_EXPERTS, LANES), F32)],
        compiler_params=_cparams("arbitrary"),
        name="out_proj_router",
    )(x, ys5, yret, mp["woa"], mp["wob"], mp["n2"], mp["wrt"], mp["rb"],
      mp["wgs"], mp["wus"], mp["wds"])


def _moe_kernel(t0_ref, nt_ref, src_ref, dst_ref, hp_ref, meta_ref, wg32_ref, wu32_ref, wd32_ref,
                acc_ref, xt_ref, xm_ref, ot_ref, xb_ref, wg_ref, wu_ref, wd_ref):
    s = pl.program_id(0)
    tm = MOE_TILE
    half = D_MODEL // 2
    all_rows = range(tm)
    nt = nt_ref[s]
    expert = (s % N_EXPERTS).astype(F32)

    @pl.when(nt > 0)
    def _():
        for src32, dst16 in ((wg32_ref, wg_ref), (wu32_ref, wu_ref), (wd32_ref, wd_ref)):
            rows = src32.shape[1]
            for i in range(0, rows, CAST_ROWS):
                dst16[0, i:i + CAST_ROWS, :] = src32[0, i:i + CAST_ROWS, :].astype(BF16)

    def fetch_rows(j, rows):
        for m in rows:
            r = pl.multiple_of(src_ref[j, 0, m], PACK_ROWS)
            xt_ref[pl.ds(m, PACK_ROWS, stride=XT_PITCH), :] = hp_ref[pl.ds(r, PACK_ROWS), :]
            xm_ref[pl.ds(m, META_ROWS, stride=XT_PITCH), :] = meta_ref[
                pl.ds(pl.multiple_of(r >> META_SHIFT, META_ROWS), META_ROWS), :]

    def add_rows(j, rows):
        for m0 in range(rows.start, rows.stop, ADD_UNROLL):
            new = []
            for m in range(m0, m0 + ADD_UNROLL):
                base = pl.multiple_of(dst_ref[j, 0, m], ACC_ROWS)
                row = ot_ref[pl.ds(m, ACC_ROWS, stride=OT_PITCH), :]
                new.append((base, acc_ref[0, pl.ds(base, ACC_ROWS), :] + row))
            for base, v in new:
                acc_ref[0, pl.ds(base, ACC_ROWS), :] = v

    @pl.when(s % N_EXPERTS == 0)
    def _():
        zero = jnp.zeros((ZERO_ROWS, LANES), F32)

        def clear(i, carry):
            acc_ref[0, pl.ds(pl.multiple_of(i * ZERO_ROWS, ZERO_ROWS), ZERO_ROWS), :] = zero
            return carry

        lax.fori_loop(0, acc_ref.shape[1] // ZERO_ROWS, clear, 0)
        ot_ref[...] = jnp.zeros_like(ot_ref)
        fetch_rows(1, all_rows)

    def stage(j, carry):
        n_slice = 4
        per = tm // n_slice
        rows = [range(i * per, (i + 1) * per) for i in range(n_slice)]
        for c in range(PACK_ROWS):
            xb_ref[:, c * LANES:(c + 1) * LANES] = (
                xt_ref[c * XT_PITCH:c * XT_PITCH + tm, :].astype(BF16))
        rec = xm_ref[0:tm, :]
        ids = pltpu.roll(rec, LANES - TOP_K, 1)
        lane = lax.broadcasted_iota(jnp.int32, rec.shape, 1)
        gate = jnp.sum(jnp.where((lane < TOP_K) & (ids == expert), rec, 0.0),
                       axis=1, keepdims=True)
        add_rows(j - 1, rows[0])
        a = jnp.dot(xb_ref[:, :half], wg_ref[0, :half], preferred_element_type=F32)
        add_rows(j - 1, rows[1])
        a = a + jnp.dot(xb_ref[:, half:], wg_ref[0, half:], preferred_element_type=F32)
        add_rows(j - 1, rows[2])
        b = jnp.dot(xb_ref[:, :half], wu_ref[0, :half], preferred_element_type=F32)
        add_rows(j - 1, rows[3])
        b = b + jnp.dot(xb_ref[:, half:], wu_ref[0, half:], preferred_element_type=F32)
        act = (jax.nn.silu(a) * b * gate).astype(BF16)
        n_col = D_MODEL // n_slice
        for piece in range(n_slice):
            fetch_rows(j + 1, rows[piece])
            out = jnp.dot(act, wd_ref[0, :, piece * n_col:(piece + 1) * n_col],
                          preferred_element_type=F32)
            for cc in range(n_col // LANES):
                c = piece * (n_col // LANES) + cc
                ot_ref[c * OT_PITCH:c * OT_PITCH + tm, :] = out[:, cc * LANES:(cc + 1) * LANES]
        return carry

    lax.fori_loop(1, nt + 1, stage, 0)

    @pl.when(s % N_EXPERTS == N_EXPERTS - 1)
    def _():
        add_rows(nt, all_rows)


def _dispatch_tables(eid_t, counts, n_tok):
    tm, ts = MOE_TILE, MOE_BLOCK_TOKENS
    ng = MOE_BLOCKS * N_EXPERTS
    assert ts < (1 << TOKEN_BITS)
    tok = jnp.arange(n_tok, dtype=jnp.int32)
    grp = (tok // ts)[None, :] * N_EXPERTS + eid_t
    gids = jnp.arange(ng, dtype=jnp.int32)
    npad = (-counts) % tm
    fill = jnp.arange(tm, dtype=jnp.int32)[None, :] < npad[:, None]
    keys = jnp.concatenate([
        ((grp << TOKEN_BITS) | (tok % ts)[None, :]).reshape(-1),
        ((jnp.where(fill, gids[:, None], ng) << TOKEN_BITS) | ts).reshape(-1)])
    toks = lax.sort(keys, dimension=0, is_stable=False) & ((1 << TOKEN_BITS) - 1)
    n_tiles = (n_tok * TOP_K + ng * tm) // tm

    ntile = (counts + npad) // tm
    tile_start = jnp.cumsum(ntile) - ntile
    src = jnp.where(toks < ts, toks, 0) * PACK_ROWS
    dst = toks * ACC_ROWS
    head = jnp.zeros((tm,), jnp.int32)
    tail = jnp.zeros((GROUP_TILES * tm,), jnp.int32)
    rows = n_tiles + 1 + GROUP_TILES
    return {"t0": tile_start.astype(jnp.int32), "nt": ntile.astype(jnp.int32),
            "src": jnp.concatenate([head, src, tail]).reshape(rows, 1, tm),
            "dst": jnp.concatenate([head, dst, tail]).reshape(rows, 1, tm)}


def _moe(hp, meta, tabs, ep):
    tm, ts = MOE_TILE, MOE_BLOCK_TOKENS
    assert GROUP_TILES >= -(-ts // tm) + 2
    window = pl.BlockSpec((pl.Element(GROUP_TILES), pl.Element(1), pl.Element(tm)),
                          lambda s, t0, nt: (t0[s], 0, 0), memory_space=pltpu.SMEM)
    wspec = lambda shape: pl.BlockSpec((1,) + shape, lambda s, t0, nt: (s % N_EXPERTS, 0, 0))
    blk = lambda s, t0, nt: (s // N_EXPERTS, 0)
    acc_rows = (ts + SUBLANES) * ACC_ROWS
    assert acc_rows % ZERO_ROWS == 0
    stage_bufs = [pltpu.VMEM((PACK_ROWS * XT_PITCH, LANES), F32),
                  pltpu.VMEM((META_ROWS * XT_PITCH, LANES), F32),
                  pltpu.VMEM((ACC_ROWS * OT_PITCH, LANES), F32),
                  pltpu.VMEM((tm, D_MODEL), BF16),
                  pltpu.VMEM((1, D_MODEL, D_EXPERT), BF16),
                  pltpu.VMEM((1, D_MODEL, D_EXPERT), BF16),
                  pltpu.VMEM((1, D_EXPERT, D_MODEL), BF16)]
    grid_spec = pltpu.PrefetchScalarGridSpec(
        num_scalar_prefetch=2,
        grid=(MOE_BLOCKS * N_EXPERTS,),
        in_specs=[window, window,
                  pl.BlockSpec((ts * PACK_ROWS, LANES), blk, pipeline_mode=pl.Buffered(1)),
                  pl.BlockSpec((ts * META_ROWS, LANES), blk, pipeline_mode=pl.Buffered(1)),
                  wspec((D_MODEL, D_EXPERT)), wspec((D_MODEL, D_EXPERT)),
                  wspec((D_EXPERT, D_MODEL))],
        out_specs=pl.BlockSpec((1, acc_rows, LANES), lambda s, t0, nt: (s // N_EXPERTS, 0, 0),
                               pipeline_mode=pl.Buffered(1)),
        scratch_shapes=stage_bufs,
    )
    return pl.pallas_call(
        _moe_kernel,
        grid_spec=grid_spec,
        out_shape=jax.ShapeDtypeStruct((MOE_BLOCKS, acc_rows, LANES), F32),
        compiler_params=_cparams("arbitrary"),
        name="moe_experts",
    )(tabs["t0"], tabs["nt"], tabs["src"], tabs["dst"], hp, meta, ep["wg"], ep["wu"], ep["wd"])


def _final_kernel(xs_ref, r_ref, fn_ref, y_ref):
    tm = xs_ref.shape[0]
    routed = jnp.concatenate(
        [r_ref[0, pl.ds(c, tm, stride=ACC_ROWS), :] for c in range(ACC_ROWS)], axis=1)
    y_ref[...] = _rms(xs_ref[...] + routed, fn_ref[...])


def _finalize(xs, routed, final_norm, tm, tok0):
    rows = xs.shape[0]
    per_block = MOE_BLOCK_TOKENS // tm
    first = tok0 // tm
    assert tok0 % tm == 0 and MOE_BLOCK_TOKENS % tm == 0
    return pl.pallas_call(
        _final_kernel,
        grid=(rows // tm,),
        in_specs=[pl.BlockSpec((tm, D_MODEL), lambda i: (i, 0)),
                  pl.BlockSpec((1, tm * ACC_ROWS, LANES),
                               lambda i: ((first + i) // per_block, (first + i) % per_block, 0)),
                  _const_spec((1, D_MODEL))],
        out_specs=pl.BlockSpec((tm, D_MODEL), lambda i: (i, 0)),
        out_shape=jax.ShapeDtypeStruct((rows, D_MODEL), F32),
        compiler_params=_cparams("parallel"),
        name="final_norm",
    )(xs, routed, final_norm)


def _rope_tables(pos):
    half = RET_HEAD_DIM // 2
    inv_freq = ROPE_BASE ** (-jnp.arange(half, dtype=F32) / half)
    ang = pos.astype(F32)[:, None] * inv_freq[None, :]
    cos, sin = jnp.cos(ang), jnp.sin(ang)
    cos_t = jnp.concatenate([cos, cos, cos, cos], axis=1)
    sin_t = jnp.concatenate([-sin, sin, -sin, sin], axis=1)
    return cos_t, sin_t


def kernel(x_prompt, x_sample, state_s5_re, state_s5_im, state_ret, meta_tokens, norm1, w_in,
           s5_lam_re, s5_lam_im, s5_log_dt, s5_b_re, s5_b_im, s5_c_re, s5_c_im, s5_d, s5_w_glu,
           s5_norm, ret_norm, w_out, norm2, w_router, router_bias, w_gate_e, w_up_e, w_down_e,
           w_gate_sh, w_up_sh, w_down_sh, final_norm):
    assert norm1.shape[0] == 1, "single-layer model"
    bp, seq, _ = x_prompt.shape
    ns = x_sample.shape[0]
    l = 0

    n1 = norm1[l].reshape(1, D_MODEL)
    w_in_b = w_in[l].astype(BF16)
    s5p = _s5_params(s5_lam_re[l], s5_lam_im[l], s5_log_dt[l], s5_b_re[l], s5_b_im[l],
                     s5_c_re[l], s5_c_im[l], s5_d[l], s5_w_glu[l], s5_norm[l])
    rnorm = ret_norm[l].reshape(1, D_RET)
    w_out_b = w_out[l].astype(BF16)
    wr_t = w_router[l].T
    wr_hi = wr_t.astype(BF16)
    wr_lo = (wr_t - wr_hi.astype(F32)).astype(BF16)
    mp = {
        "woa": w_out_b[:D_S5], "wob": w_out_b[D_S5:], "n2": norm2[l].reshape(1, D_MODEL),
        "wrt": jnp.stack([wr_hi, wr_lo]),
        "rb": jnp.broadcast_to(router_bias[l][:, None], (N_EXPERTS, LANES)),
        "wgs": w_gate_sh[l].astype(BF16), "wus": w_up_sh[l].astype(BF16),
        "wds": w_down_sh[l].astype(BF16),
    }
    ep = {"wg": w_gate_e[l], "wu": w_up_e[l],
          "wd": w_down_e[l]}
    fnorm = final_norm.reshape(1, D_MODEL)

    xp = x_prompt.reshape(bp * seq, D_MODEL)
    cos_p, sin_p = _rope_tables(N_META + jnp.arange(seq, dtype=jnp.int32))
    tm_a = 512
    up, qp, kp, vp, gp = _project(xp, n1, w_in_b, cos_p, sin_p, tm_a, seq // tm_a)

    meta_tm = jnp.repeat(meta_tokens, SUBLANES, axis=0)
    mchunk = LANES
    meta_chunk = jnp.concatenate([jnp.zeros((mchunk - N_META, D_MODEL), F32), meta_tokens], axis=0)
    x_small = jnp.concatenate([x_sample.reshape(ns, D_MODEL), meta_tm, meta_chunk], axis=0)
    meta_pos = jnp.arange(N_META, dtype=jnp.int32)
    pos_small = jnp.concatenate([jnp.full((ns,), PAST_LEN, jnp.int32),
                                 jnp.repeat(meta_pos, SUBLANES),
                                 jnp.zeros((mchunk - N_META,), jnp.int32), meta_pos])
    cos_s, sin_s = _rope_tables(pos_small)
    n_small = x_small.shape[0]
    n_tm = ns + N_META * SUBLANES
    us, qs, ks, vs, gs = _project(x_small, n1, w_in_b, cos_s, sin_s, n_small, 1)

    zero8 = jnp.zeros((SUBLANES, S5_LANES), F32)
    _, m_re, m_im = _s5(us[ns:n_tm], zero8, zero8, s5p, nb=SUBLANES, tt=N_META, nblk=1,
                        bt_major=False)
    tt = 64
    ys5_p, p_re, p_im = _s5(up.reshape(bp, seq, D_S5), m_re, m_im, s5p, nb=bp, tt=tt,
                            nblk=seq // tt, bt_major=True)
    ys5_p = ys5_p.reshape(bp * seq, D_S5)
    ys5_s, s_re, s_im = _s5(us[:ns], state_s5_re[l].reshape(ns, S5_LANES),
                            state_s5_im[l].reshape(ns, S5_LANES), s5p, nb=ns, tt=1, nblk=1,
                            bt_major=False)

    zero_pair = jnp.zeros((1, N_HEAD_PAIRS, LANES, LANES), F32)
    _, m_pair = _retention(qs[n_tm:], ks[n_tm:], vs[n_tm:], gs[n_tm:],
                           zero_pair, rnorm, nseq=1, chunk=mchunk, nchunk=1)
    chunk = 256
    yret_p, p_pair = _retention(qp, kp, vp, gp, jnp.broadcast_to(m_pair, (bp,) + m_pair.shape[1:]),
                                rnorm, nseq=bp, chunk=chunk, nchunk=seq // chunk)
    yret_s, ret_s = _retention_step(qs[:ns], ks[:ns], vs[:ns], gs[:ns], state_ret[l], ret_norm[l])

    n_prompt = bp * seq
    assert n_prompt + ns <= MOE_BLOCKS * MOE_BLOCK_TOKENS
    n_moe = MOE_BLOCKS * MOE_BLOCK_TOKENS
    tm_mix = 512
    xs_p, hp, meta, eid_p, cnt_p = _mix(xp, ys5_p, yret_p, mp, tm_mix, n_moe)
    xs_s, hp_s, meta_s, eid_s, cnt_s = _mix(x_sample.reshape(ns, D_MODEL), ys5_s, yret_s, mp,
                                            ns, ns)
    hp = lax.dynamic_update_slice(hp, hp_s, (n_prompt * PACK_ROWS, 0))
    meta = lax.dynamic_update_slice(meta, meta_s, (n_prompt * META_ROWS, 0))
    blk_s = n_prompt // MOE_BLOCK_TOKENS
    assert MOE_BLOCK_TOKENS % tm_mix == 0 and (n_prompt + ns - 1) // MOE_BLOCK_TOKENS == blk_s
    counts = cnt_p[:, :, 0].reshape(MOE_BLOCKS, MOE_BLOCK_TOKENS // tm_mix, N_EXPERTS).sum(axis=1)
    counts = counts.at[blk_s].add(cnt_s[0, :, 0]).reshape(-1).astype(jnp.int32)
    tabs = _dispatch_tables(jnp.concatenate([eid_p, eid_s], axis=1), counts, n_prompt + ns)
    routed = _moe(hp, meta, tabs, ep)
    y_p = _finalize(xs_p, routed, fnorm, 512, 0)
    y_s = _finalize(xs_s, routed, fnorm, ns, n_prompt)

    shape5 = (1, bp, N_S5_GROUPS, S5_STATE)
    return (y_p.reshape(bp, seq, D_MODEL),
            y_s.reshape(ns, 1, D_MODEL),
            p_re.reshape(shape5), p_im.reshape(shape5),
            _unpair_state(p_pair)[None],
            s_re.reshape(1, ns, N_S5_GROUPS, S5_STATE), s_im.reshape(1, ns, N_S5_GROUPS, S5_STATE),
            ret_s[None])
```

```python
import functools

import jax
import jax.numpy as jnp
from jax import lax
from jax.experimental import pallas as pl
from jax.experimental.pallas import tpu as pltpu

F32 = jnp.float32
BF16 = jnp.bfloat16

D_MODEL = 1024
N_META = 16
PAST_LEN = 16384
D_S5 = 512
S5_GROUP = 16
N_S5_GROUPS = 32
S5_STATE = 64
S5_LANES = N_S5_GROUPS * S5_STATE
D_RET = 512
N_RET_HEADS = 8
RET_HEAD_DIM = 64
N_HEAD_PAIRS = 4
ROPE_BASE = 10000.0
D_IN = D_S5 + 4 * D_RET
N_EXPERTS = 64
TOP_K = 8
N_EXPERT_GROUPS = 8
GROUP_SIZE = 8
TOPK_GROUPS = 4
D_EXPERT = 256
ROUTED_SCALE = 2.5
EPS = 1e-6

LANES = 128
SUBLANES = 8
VMEM_LIMIT = 62 * 1024 * 1024

PACK_ROWS = D_MODEL // LANES
ACC_ROWS = D_MODEL // LANES
MOE_BLOCKS = 3
MOE_BLOCK_TOKENS = 5632
MOE_TILE = 256
GROUP_TILES = MOE_BLOCK_TOKENS // MOE_TILE + 2
ADD_UNROLL = 8
OT_PITCH = MOE_TILE + SUBLANES
XT_PITCH = MOE_TILE + SUBLANES
META_ROWS = 2
META_SHIFT = 2
assert PACK_ROWS == META_ROWS << META_SHIFT
TOKEN_BITS = 13
ZERO_ROWS = 64
CAST_ROWS = 128


def _cparams(*sem):
    return pltpu.CompilerParams(dimension_semantics=sem, vmem_limit_bytes=VMEM_LIMIT)


def _const_spec(shape):
    nd = len(shape)
    return pl.BlockSpec(shape, lambda *_: (0,) * nd)


def _rms(x, gain):
    return x * lax.rsqrt(jnp.mean(x * x, axis=-1, keepdims=True) + EPS) * gain


def _proj_kernel(x_ref, n1_ref, w_ref, cos_ref, sin_ref, u_ref, q_ref, k_ref, v_ref, g_ref):
    h = _rms(x_ref[...], n1_ref[...]).astype(BF16)
    proj = jnp.dot(h, w_ref[...], preferred_element_type=F32)
    cos = cos_ref[...]
    sin = sin_ref[...]
    lane = lax.broadcasted_iota(jnp.int32, cos.shape, 1)
    first_half = (lane % RET_HEAD_DIM) < (RET_HEAD_DIM // 2)

    def rotary(t):
        partner = jnp.where(first_half,
                            pltpu.roll(t, LANES - RET_HEAD_DIM // 2, 1),
                            pltpu.roll(t, RET_HEAD_DIM // 2, 1))
        return t * cos + partner * sin

    u_ref[...] = proj[:, :D_S5]
    for j in range(N_HEAD_PAIRS):
        lo = D_S5 + j * LANES
        q_ref[:, j * LANES:(j + 1) * LANES] = rotary(proj[:, lo:lo + LANES]).astype(BF16)
        lo += D_RET
        k_ref[:, j * LANES:(j + 1) * LANES] = (
            rotary(proj[:, lo:lo + LANES]) * (RET_HEAD_DIM ** -0.5)).astype(BF16)
    v_ref[...] = proj[:, D_S5 + 2 * D_RET:D_S5 + 3 * D_RET].astype(BF16)
    g_ref[...] = proj[:, D_S5 + 3 * D_RET:]


def _project(x, norm1, w_in_b, cos_t, sin_t, tm, table_blocks):
    rows = x.shape[0]
    row = lambda i: (i, 0)
    tab = lambda i: (i % table_blocks, 0)
    return pl.pallas_call(
        _proj_kernel,
        grid=(rows // tm,),
        in_specs=[pl.BlockSpec((tm, D_MODEL), row),
                  _const_spec((1, D_MODEL)),
                  _const_spec((D_MODEL, D_IN)),
                  pl.BlockSpec((tm, LANES), tab),
                  pl.BlockSpec((tm, LANES), tab)],
        out_specs=[pl.BlockSpec((tm, D_S5), row),
                   pl.BlockSpec((tm, D_RET), row),
                   pl.BlockSpec((tm, D_RET), row),
                   pl.BlockSpec((tm, D_RET), row),
                   pl.BlockSpec((tm, D_RET), row)],
        out_shape=[jax.ShapeDtypeStruct((rows, D_S5), F32),
                   jax.ShapeDtypeStruct((rows, D_RET), BF16),
                   jax.ShapeDtypeStruct((rows, D_RET), BF16),
                   jax.ShapeDtypeStruct((rows, D_RET), BF16),
                   jax.ShapeDtypeStruct((rows, D_RET), F32)],
        compiler_params=_cparams("parallel"),
        name="in_proj",
    )(x, norm1, w_in_b, cos_t, sin_t)


def _s5_kernel(u_ref, x0re_ref, x0im_ref, are_ref, aim_ref, b_ref, c_ref, d_ref, wglu_ref,
               nrm_ref, y_ref, sre_ref, sim_ref, st, sre, sim, utm, ytm, *, nb, tt, bt_major):
    i = pl.program_id(0)
    rows = nb * tt

    @pl.when(i == 0)
    def _():
        sre[...] = x0re_ref[...]
        sim[...] = x0im_ref[...]

    if bt_major:
        for t in range(tt):
            utm[t * nb:(t + 1) * nb, :] = u_ref[:, t, :]
    else:
        utm[...] = u_ref[...]

    ub = utm[...].astype(BF16)
    half = S5_LANES // 2
    kh = D_S5 // 2
    for part in range(2):
        for hf in range(2):
            st[:, part * S5_LANES + hf * half:part * S5_LANES + (hf + 1) * half] = jnp.dot(
                ub[:, hf * kh:(hf + 1) * kh], b_ref[part, hf], preferred_element_type=F32)

    lw = 512 if nb == SUBLANES else LANES
    for lg in range(S5_LANES // lw):
        re = slice(lg * lw, (lg + 1) * lw)
        im = slice(S5_LANES + lg * lw, S5_LANES + (lg + 1) * lw)
        a_re = are_ref[:, re]
        a_im = aim_ref[:, re]

        def step(t, carry):
            s_re, s_im = carry
            r0 = pl.multiple_of(t * nb, nb)
            n_re = a_re * s_re - a_im * s_im + st[pl.ds(r0, nb), re]
            n_im = a_re * s_im + a_im * s_re + st[pl.ds(r0, nb), im]
            st[pl.ds(r0, nb), re] = n_re
            st[pl.ds(r0, nb), im] = n_im
            return n_re, n_im

        if nb == SUBLANES:
            f_re, f_im = sre[:, re], sim[:, re]
            for t in range(tt):
                r = slice(t * nb, (t + 1) * nb)
                n_re = a_re * f_re - a_im * f_im + st[r, re]
                n_im = a_re * f_im + a_im * f_re + st[r, im]
                st[r, re] = n_re
                st[r, im] = n_im
                f_re, f_im = n_re, n_im
        else:
            f_re, f_im = lax.fori_loop(0, tt, step, (sre[:, re], sim[:, re]))
        sre[:, re] = f_re
        sim[:, re] = f_im

    sre_ref[...] = sre[...]
    sim_ref[...] = sim[...]

    ys = []
    for hf in range(2):
        xr = st[:, hf * half:(hf + 1) * half].astype(BF16)
        xi = st[:, S5_LANES + hf * half:S5_LANES + (hf + 1) * half].astype(BF16)
        ys.append(jnp.dot(xr, c_ref[0, hf], preferred_element_type=F32)
                  - jnp.dot(xi, c_ref[1, hf], preferred_element_type=F32))
    y = jnp.concatenate(ys, axis=1) + d_ref[...] * utm[...]
    y = jax.nn.gelu(y)
    y = y * jax.nn.sigmoid(jnp.dot(y.astype(BF16), wglu_ref[...], preferred_element_type=F32))
    y = _rms(y, nrm_ref[...]).astype(BF16)
    if bt_major:
        yf = y.astype(F32)
        for j in range(D_S5 // LANES):
            ytm[j] = yf[:, j * LANES:(j + 1) * LANES]
        for b in range(nb):
            for j in range(D_S5 // LANES):
                y_ref[b, :, j * LANES:(j + 1) * LANES] = (
                    ytm[j, pl.ds(b, tt, stride=nb), :].astype(BF16))
    else:
        y_ref[...] = y


def _s5(u, x0re, x0im, s5p, *, nb, tt, nblk, bt_major):
    rows = nb * tt
    if bt_major:
        u_spec = pl.BlockSpec((nb, tt, D_S5), lambda i: (0, i, 0))
        y_shape = jax.ShapeDtypeStruct((nb, tt * nblk, D_S5), BF16)
    else:
        u_spec = pl.BlockSpec((rows, D_S5), lambda i: (i, 0))
        y_shape = jax.ShapeDtypeStruct((rows * nblk, D_S5), BF16)
    state_spec = _const_spec((nb, S5_LANES))
    are = jnp.broadcast_to(s5p["are"], (nb, S5_LANES))
    aim = jnp.broadcast_to(s5p["aim"], (nb, S5_LANES))
    return pl.pallas_call(
        functools.partial(_s5_kernel, nb=nb, tt=tt, bt_major=bt_major),
        grid=(nblk,),
        in_specs=[u_spec, state_spec, state_spec, state_spec, state_spec,
                  _const_spec((2, 2, D_S5 // 2, S5_LANES // 2)),
                  _const_spec((2, 2, S5_LANES // 2, D_S5 // 2)),
                  _const_spec((1, D_S5)),
                  _const_spec((D_S5, D_S5)),
                  _const_spec((1, D_S5))],
        out_specs=[u_spec, state_spec, state_spec],
        out_shape=[y_shape,
                   jax.ShapeDtypeStruct((nb, S5_LANES), F32),
                   jax.ShapeDtypeStruct((nb, S5_LANES), F32)],
        scratch_shapes=[pltpu.VMEM((rows, 2 * S5_LANES), F32),
                        pltpu.VMEM((nb, S5_LANES), F32),
                        pltpu.VMEM((nb, S5_LANES), F32),
                        pltpu.VMEM((rows, D_S5), F32),
                        pltpu.VMEM((D_S5 // LANES, rows, LANES), F32)],
        compiler_params=_cparams("arbitrary"),
        name="s5_" + ("bt" if bt_major else "tm") + str(nb),
    )(u, x0re, x0im, are, aim, s5p["b"], s5p["c"], s5p["d"], s5p["wglu"], s5p["nrm"])


def _s5_params(lam_re, lam_im, log_dt, b_re, b_im, c_re, c_im, d_skip, w_glu, nrm):
    dt = jnp.exp(log_dt)[:, None]
    mag = jnp.exp(lam_re * dt)
    abar_re, abar_im = mag * jnp.cos(lam_im * dt), mag * jnp.sin(lam_im * dt)
    num_re, num_im = abar_re - 1.0, abar_im
    den = lam_re * lam_re + lam_im * lam_im
    f_re = (num_re * lam_re + num_im * lam_im) / den
    f_im = (num_im * lam_re - num_re * lam_im) / den
    bbar_re = f_re[..., None] * b_re - f_im[..., None] * b_im
    bbar_im = f_re[..., None] * b_im + f_im[..., None] * b_re
    hg = N_S5_GROUPS // 2
    eye = jnp.eye(hg, dtype=F32)

    def bdiag(bb):
        bb = bb.reshape(2, hg, S5_STATE, S5_GROUP)
        return jnp.einsum("zgph,gk->zghkp", bb, eye).reshape(2, hg * S5_GROUP, hg * S5_STATE)

    def cdiag(cc):
        cc = cc.reshape(2, hg, S5_GROUP, S5_STATE)
        return jnp.einsum("zgnp,gk->zgpkn", cc, eye).reshape(2, hg * S5_STATE, hg * S5_GROUP)

    return {
        "are": abar_re.reshape(1, S5_LANES), "aim": abar_im.reshape(1, S5_LANES),
        "b": jnp.stack([bdiag(bbar_re), bdiag(bbar_im)]).astype(BF16),
        "c": jnp.stack([cdiag(c_re), cdiag(c_im)]).astype(BF16),
        "d": d_skip.reshape(1, D_S5), "wglu": w_glu.astype(BF16), "nrm": nrm.reshape(1, D_S5),
    }


def _head_norm_gate(o, g, gain, lo):
    inv = 1.0 / RET_HEAD_DIM

    def seg_mean(t):
        s_lo = jnp.sum(jnp.where(lo, t, 0.0), axis=1, keepdims=True)
        s_hi = jnp.sum(jnp.where(lo, 0.0, t), axis=1, keepdims=True)
        return jnp.where(lo, s_lo, s_hi) * inv

    dlt = o - seg_mean(o)
    var = seg_mean(dlt * dlt)
    return jax.nn.silu(g) * (dlt * lax.rsqrt(var + EPS) * gain)


def _ret_kernel(q_ref, k_ref, v_ref, g_ref, s0_ref, dm_ref, qd_ref, kd_ref, gc_ref, bm_ref,
                nrm_ref, y_ref, so_ref, s_acc):
    c = pl.program_id(1)

    @pl.when(c == 0)
    def _():
        s_acc[...] = s0_ref[0]

    rows = q_ref.shape[0]
    lane = lax.broadcasted_iota(jnp.int32, (rows, LANES), 1)
    lo = lane < RET_HEAD_DIM
    for j in range(N_HEAD_PAIRS):
        sl = slice(j * LANES, (j + 1) * LANES)
        q2 = q_ref[:, sl].astype(F32)
        k2 = k_ref[:, sl]
        v2 = v_ref[:, sl]
        s_pair = s_acc[j]
        cross = jnp.dot((q2 * qd_ref[:, sl]).astype(BF16), s_pair.astype(BF16),
                        preferred_element_type=F32)
        k_dec = (k2.astype(F32) * kd_ref[:, sl]).astype(BF16)
        upd = lax.dot_general(k_dec, v2, (((0,), (0,)), ((), ())), preferred_element_type=F32)
        s_acc[j] = gc_ref[j] * s_pair + bm_ref[...] * upd
        inner = []
        for hh, qh in enumerate((jnp.where(lo, q2, 0.0), jnp.where(lo, 0.0, q2))):
            sc = lax.dot_general(qh.astype(BF16), k2, (((1,), (1,)), ((), ())),
                                 preferred_element_type=F32) * dm_ref[2 * j + hh]
            inner.append(jnp.dot(sc.astype(BF16), v2, preferred_element_type=F32))
        o = jnp.where(lo, inner[0], inner[1]) + cross
        y_ref[:, sl] = _head_norm_gate(o, g_ref[:, sl], nrm_ref[:, sl], lo).astype(BF16)
    so_ref[0] = s_acc[...]


def _ret_tables(chunk):
    log_g = jnp.log1p(-jnp.exp2(-5.0 - jnp.arange(N_RET_HEADS, dtype=F32)))
    n = jnp.arange(chunk, dtype=F32)
    diff = n[:, None] - n[None, :]
    dm = jnp.where(diff[None] >= 0.0,
                   jnp.exp(log_g[:, None, None] * jnp.maximum(diff, 0.0)[None]), 0.0)
    per_lane = lambda t: jnp.repeat(t.T, RET_HEAD_DIM, axis=1)
    qd = per_lane(jnp.exp(log_g[:, None] * (n + 1.0)[None]))
    kd = per_lane(jnp.exp(log_g[:, None] * (chunk - 1.0 - n)[None]))
    gch = jnp.exp(log_g * chunk)
    blk = jnp.kron(jnp.eye(2, dtype=F32), jnp.ones((RET_HEAD_DIM, RET_HEAD_DIM), F32))
    gc = jnp.repeat(gch.reshape(N_HEAD_PAIRS, 2), RET_HEAD_DIM, axis=1)[:, :, None] * blk[None]
    return dm, qd, kd, gc, blk


def _pair_state(s):
    b = s.shape[0]
    s = s.reshape(b, N_HEAD_PAIRS, 2, RET_HEAD_DIM, RET_HEAD_DIM)
    z = jnp.zeros_like(s[:, :, 0])
    top = jnp.concatenate([s[:, :, 0], z], axis=-1)
    bot = jnp.concatenate([z, s[:, :, 1]], axis=-1)
    return jnp.concatenate([top, bot], axis=-2)


def _unpair_state(s2):
    d = RET_HEAD_DIM
    return jnp.stack([s2[:, :, :d, :d], s2[:, :, d:, d:]], axis=2).reshape(
        s2.shape[0], N_RET_HEADS, d, d)


def _retention(q, k, v, g, s0_pair, ret_norm, *, nseq, chunk, nchunk):
    dm, qd, kd, gc, blk = _ret_tables(chunk)
    row = lambda b, c: (b * nchunk + c, 0)
    st_spec = pl.BlockSpec((1, N_HEAD_PAIRS, LANES, LANES), lambda b, c: (b, 0, 0, 0))
    blk_spec = pl.BlockSpec((chunk, D_RET), row)
    return pl.pallas_call(
        _ret_kernel,
        grid=(nseq, nchunk),
        in_specs=[blk_spec, blk_spec, blk_spec, blk_spec, st_spec,
                  _const_spec((N_RET_HEADS, chunk, chunk)),
                  _const_spec((chunk, D_RET)), _const_spec((chunk, D_RET)),
                  _const_spec((N_HEAD_PAIRS, LANES, LANES)), _const_spec((LANES, LANES)),
                  _const_spec((1, D_RET))],
        out_specs=[blk_spec, st_spec],
        out_shape=[jax.ShapeDtypeStruct((nseq * nchunk * chunk, D_RET), BF16),
                   jax.ShapeDtypeStruct((nseq, N_HEAD_PAIRS, LANES, LANES), F32)],
        scratch_shapes=[pltpu.VMEM((N_HEAD_PAIRS, LANES, LANES), F32)],
        compiler_params=_cparams("arbitrary", "arbitrary"),
        name="retention_c%d" % chunk,
    )(q, k, v, g, s0_pair, dm, qd, kd, gc, blk, ret_norm)


def _ret_step_kernel(q_ref, k_ref, v_ref, g_ref, s_ref, gam_ref, nrm_ref, y_ref, so_ref):
    q = q_ref[...]
    k = k_ref[...]
    v = v_ref[...]
    gam = gam_ref[0]
    score = jnp.sum(q * k, axis=0, keepdims=True)
    q_dec = (q * gam).astype(BF16).astype(F32)
    cross = jnp.zeros(v.shape, F32)
    for d in range(RET_HEAD_DIM):
        s_d = s_ref[0, d]
        cross = cross + q_dec[d:d + 1, :] * s_d
        so_ref[0, d] = gam * s_d + k[d:d + 1, :] * v
    o = score.astype(BF16).astype(F32) * v + cross
    dlt = o - jnp.mean(o, axis=0, keepdims=True)
    var = jnp.mean(dlt * dlt, axis=0, keepdims=True)
    y_ref[...] = jax.nn.silu(g_ref[...]) * (dlt * lax.rsqrt(var + EPS) * nrm_ref[...])


def _retention_step(q, k, v, g, state, ret_norm):
    n = q.shape[0]
    gam = jnp.exp(jnp.log1p(-jnp.exp2(-5.0 - jnp.arange(N_RET_HEADS, dtype=F32))))
    gam_tab = jnp.broadcast_to(gam[:, None, None], (N_RET_HEADS, 1, n))
    seq_minor = lambda t: t.astype(F32).T
    head = lambda h: (h, 0)
    vec_spec = pl.BlockSpec((RET_HEAD_DIM, n), head)
    st_spec = pl.BlockSpec((1, RET_HEAD_DIM, RET_HEAD_DIM, n), lambda h: (h, 0, 0, 0))
    y_t, s_new = pl.pallas_call(
        _ret_step_kernel,
        grid=(N_RET_HEADS,),
        in_specs=[vec_spec, vec_spec, vec_spec, vec_spec, st_spec,
                  pl.BlockSpec((1, 1, n), lambda h: (h, 0, 0)), vec_spec],
        out_specs=[vec_spec, st_spec],
        out_shape=[jax.ShapeDtypeStruct((D_RET, n), F32),
                   jax.ShapeDtypeStruct((N_RET_HEADS, RET_HEAD_DIM, RET_HEAD_DIM, n), F32)],
        compiler_params=_cparams("parallel"),
        name="retention_step",
    )(seq_minor(q), seq_minor(k), seq_minor(v), seq_minor(g),
      jnp.transpose(state, (1, 2, 3, 0)), gam_tab,
      jnp.broadcast_to(ret_norm[:, None], (D_RET, n)))
    return y_t.T.astype(BF16), jnp.transpose(s_new, (3, 0, 1, 2))


def _first_max(vals, idxs, sentinel):
    m = jnp.max(vals, axis=0, keepdims=True)
    first = jnp.min(jnp.where(vals == m, idxs, sentinel), axis=0, keepdims=True)
    return m, idxs == first


def _route(scores, sel):
    t = sel.shape[1]
    neg = -jnp.inf
    member = lax.broadcasted_iota(jnp.int32, (GROUP_SIZE, t), 0).astype(F32)
    groups = [sel[g * GROUP_SIZE:(g + 1) * GROUP_SIZE, :] for g in range(N_EXPERT_GROUPS)]
    gscore = []
    for grp in groups:
        m1, pick = _first_max(grp, member, float(GROUP_SIZE))
        m2 = jnp.max(jnp.where(pick, neg, grp), axis=0, keepdims=True)
        gscore.append(m1 + m2)
    gs = jnp.concatenate(gscore, axis=0)
    gkeep = jnp.zeros(gs.shape, F32)
    for _ in range(TOPK_GROUPS):
        _, pick = _first_max(gs, member, float(N_EXPERT_GROUPS))
        gkeep = jnp.where(pick, 1.0, gkeep)
        gs = jnp.where(pick, neg, gs)
    cand = jnp.concatenate(
        [jnp.where(gkeep[g:g + 1, :] > 0.5, groups[g], neg) for g in range(N_EXPERT_GROUPS)],
        axis=0)
    expert = lax.broadcasted_iota(jnp.int32, (N_EXPERTS, t), 0).astype(F32)
    ids, ws = [], []
    chosen = jnp.zeros(cand.shape, F32)
    for _ in range(TOP_K):
        _, pick = _first_max(cand, expert, float(N_EXPERTS))
        ids.append(jnp.sum(jnp.where(pick, expert, 0.0), axis=0, keepdims=True))
        ws.append(jnp.sum(jnp.where(pick, scores, 0.0), axis=0, keepdims=True))
        chosen = jnp.where(pick, 1.0, chosen)
        cand = jnp.where(pick, neg, cand)
    w = jnp.concatenate(ws, axis=0)
    gates = w / jnp.sum(w, axis=0, keepdims=True) * ROUTED_SCALE
    return jnp.concatenate(ids, axis=0), gates, jnp.sum(chosen, axis=1, keepdims=True)


def _mix_kernel(x_ref, ys_ref, yr_ref, woa_ref, wob_ref, n2_ref, wrt_ref, rb_ref,
                wgs_ref, wus_ref, wds_ref, xs_ref, hp_ref, meta_ref, eid_ref, cnt_ref, *,
                n_blocks):
    @pl.when(pl.program_id(0) >= n_blocks)
    def _():
        hp_ref[...] = jnp.zeros_like(hp_ref)
        meta_ref[...] = jnp.zeros_like(meta_ref)
        cnt_ref[...] = jnp.zeros_like(cnt_ref)

    @pl.when(pl.program_id(0) < n_blocks)
    def _():
        _mix_body(x_ref, ys_ref, yr_ref, woa_ref, wob_ref, n2_ref, wrt_ref, rb_ref,
                  wgs_ref, wus_ref, wds_ref, xs_ref, hp_ref, meta_ref, eid_ref, cnt_ref)


def _mix_body(x_ref, ys_ref, yr_ref, woa_ref, wob_ref, n2_ref, wrt_ref, rb_ref,
              wgs_ref, wus_ref, wds_ref, xs_ref, hp_ref, meta_ref, eid_ref, cnt_ref):
    tm = x_ref.shape[0]
    x2 = (x_ref[...]
          + jnp.dot(ys_ref[...], woa_ref[...], preferred_element_type=F32)
          + jnp.dot(yr_ref[...], wob_ref[...], preferred_element_type=F32))
    h2f = _rms(x2, n2_ref[...])
    h2 = h2f.astype(BF16)
    h2r = h2.astype(F32)
    for c in range(PACK_ROWS):
        hp_ref[pl.ds(c, tm, stride=PACK_ROWS), :] = h2r[:, c * LANES:(c + 1) * LANES]
    h2_lo = (h2f - h2r).astype(BF16)
    nt_dims = (((1,), (1,)), ((), ()))
    logits = (lax.dot_general(wrt_ref[0], h2, nt_dims, preferred_element_type=F32)
              + lax.dot_general(wrt_ref[1], h2, nt_dims, preferred_element_type=F32)
              + lax.dot_general(wrt_ref[0], h2_lo, nt_dims, preferred_element_type=F32))
    scores = jax.nn.sigmoid(logits)
    ids, gates, counts = _route(scores, scores + rb_ref[:, :1])
    eid_ref[...] = ids.astype(jnp.int32)
    cnt_ref[0] = jnp.broadcast_to(counts, (N_EXPERTS, LANES))
    rec = jnp.concatenate([gates, ids, jnp.zeros((LANES - 2 * TOP_K, tm), F32)], axis=0).T
    meta_ref[pl.ds(0, tm, stride=META_ROWS), :] = rec
    for c in range(1, META_ROWS):
        meta_ref[pl.ds(c, tm, stride=META_ROWS), :] = jnp.zeros((tm, LANES), F32)
    a = jnp.dot(h2, wgs_ref[...], preferred_element_type=F32)
    b = jnp.dot(h2, wus_ref[...], preferred_element_type=F32)
    xs_ref[...] = x2 + jnp.dot((jax.nn.silu(a) * b).astype(BF16), wds_ref[...],
                               preferred_element_type=F32)


def _mix(x, ys5, yret, mp, tm, out_tokens):
    rows = x.shape[0]
    n_blocks = rows // tm
    assert out_tokens % tm == 0
    packed = lambda i: (i, 0)
    row = lambda i: (jnp.minimum(i, n_blocks - 1), 0)
    col = lambda i: (0, jnp.minimum(i, n_blocks - 1))
    return pl.pallas_call(
        functools.partial(_mix_kernel, n_blocks=n_blocks),
        grid=(out_tokens // tm,),
        in_specs=[pl.BlockSpec((tm, D_MODEL), row),
                  pl.BlockSpec((tm, D_S5), row), pl.BlockSpec((tm, D_RET), row),
                  _const_spec((D_S5, D_MODEL)), _const_spec((D_RET, D_MODEL)),
                  _const_spec((1, D_MODEL)),
                  _const_spec((2, N_EXPERTS, D_MODEL)), _const_spec((N_EXPERTS, LANES)),
                  _const_spec((D_MODEL, D_EXPERT)), _const_spec((D_MODEL, D_EXPERT)),
                  _const_spec((D_EXPERT, D_MODEL))],
        out_specs=[pl.BlockSpec((tm, D_MODEL), row),
                   pl.BlockSpec((tm * PACK_ROWS, LANES), packed),
                   pl.BlockSpec((tm * META_ROWS, LANES), packed),
                   pl.BlockSpec((TOP_K, tm), col),
                   pl.BlockSpec((1, N_EXPERTS, LANES), lambda i: (i, 0, 0))],
        out_shape=[jax.ShapeDtypeStruct((rows, D_MODEL), F32),
                   jax.ShapeDtypeStruct((out_tokens * PACK_ROWS, LANES), F32),
                   jax.ShapeDtypeStruct((out_tokens * META_ROWS, LANES), F32),
                   jax.ShapeDtypeStruct((TOP_K, rows), jnp.int32),
                   jax.ShapeDtypeStruct((out_tokens // tm, N_EXPERTS, LANES), F32)],
        compiler_params=_cparams("arbitrary"),
        name="out_proj_router",
    )(x, ys5, yret, mp["woa"], mp["wob"], mp["n2"], mp["wrt"], mp["rb"],
      mp["wgs"], mp["wus"], mp["wds"])


def _moe_kernel(t0_ref, nt_ref, src_ref, dst_ref, hp_ref, meta_ref, wg32_ref, wu32_ref, wd32_ref,
                acc_ref, xt_ref, xm_ref, ot_ref, xb_ref, wg_ref, wu_ref, wd_ref):
    s = pl.program_id(0)
    tm = MOE_TILE
    half = D_MODEL // 2
    all_rows = range(tm)
    nt = nt_ref[s]
    expert = (s % N_EXPERTS).astype(F32)

    @pl.when(nt > 0)
    def _():
        for src32, dst16 in ((wg32_ref, wg_ref), (wu32_ref, wu_ref), (wd32_ref, wd_ref)):
            rows = src32.shape[1]
            for i in range(0, rows, CAST_ROWS):
                dst16[0, i:i + CAST_ROWS, :] = src32[0, i:i + CAST_ROWS, :].astype(BF16)

    def fetch_rows(j, rows):
        for m in rows:
            r = pl.multiple_of(src_ref[j, 0, m], PACK_ROWS)
            xt_ref[pl.ds(m, PACK_ROWS, stride=XT_PITCH), :] = hp_ref[pl.ds(r, PACK_ROWS), :]
            xm_ref[pl.ds(m, META_ROWS, stride=XT_PITCH), :] = meta_ref[
                pl.ds(pl.multiple_of(r >> META_SHIFT, META_ROWS), META_ROWS), :]

    def add_rows(j, rows):
        for m0 in range(rows.start, rows.stop, ADD_UNROLL):
            new = []
            for m in range(m0, m0 + ADD_UNROLL):
                base = pl.multiple_of(dst_ref[j, 0, m], ACC_ROWS)
                row = ot_ref[pl.ds(m, ACC_ROWS, stride=OT_PITCH), :]
                new.append((base, acc_ref[0, pl.ds(base, ACC_ROWS), :] + row))
            for base, v in new:
                acc_ref[0, pl.ds(base, ACC_ROWS), :] = v

    @pl.when(s % N_EXPERTS == 0)
    def _():
        zero = jnp.zeros((ZERO_ROWS, LANES), F32)

        def clear(i, carry):
            acc_ref[0, pl.ds(pl.multiple_of(i * ZERO_ROWS, ZERO_ROWS), ZERO_ROWS), :] = zero
            return carry

        lax.fori_loop(0, acc_ref.shape[1] // ZERO_ROWS, clear, 0)
        ot_ref[...] = jnp.zeros_like(ot_ref)
        fetch_rows(1, all_rows)

    def stage(j, carry):
        n_slice = 4
        per = tm // n_slice
        rows = [range(i * per, (i + 1) * per) for i in range(n_slice)]
        for c in range(PACK_ROWS):
            xb_ref[:, c * LANES:(c + 1) * LANES] = (
                xt_ref[c * XT_PITCH:c * XT_PITCH + tm, :].astype(BF16))
        rec = xm_ref[0:tm, :]
        ids = pltpu.roll(rec, LANES - TOP_K, 1)
        lane = lax.broadcasted_iota(jnp.int32, rec.shape, 1)
        gate = jnp.sum(jnp.where((lane < TOP_K) & (ids == expert), rec, 0.0),
                       axis=1, keepdims=True)
        add_rows(j - 1, rows[0])
        a = jnp.dot(xb_ref[:, :half], wg_ref[0, :half], preferred_element_type=F32)
        add_rows(j - 1, rows[1])
        a = a + jnp.dot(xb_ref[:, half:], wg_ref[0, half:], preferred_element_type=F32)
        add_rows(j - 1, rows[2])
        b = jnp.dot(xb_ref[:, :half], wu_ref[0, :half], preferred_element_type=F32)
        add_rows(j - 1, rows[3])
        b = b + jnp.dot(xb_ref[:, half:], wu_ref[0, half:], preferred_element_type=F32)
        act = (jax.nn.silu(a) * b * gate).astype(BF16)
        n_col = D_MODEL // n_slice
        for piece in range(n_slice):
            fetch_rows(j + 1, rows[piece])
            out = jnp.dot(act, wd_ref[0, :, piece * n_col:(piece + 1) * n_col],
                          preferred_element_type=F32)
            for cc in range(n_col // LANES):
                c = piece * (n_col // LANES) + cc
                ot_ref[c * OT_PITCH:c * OT_PITCH + tm, :] = out[:, cc * LANES:(cc + 1) * LANES]
        return carry

    lax.fori_loop(1, nt + 1, stage, 0)

    @pl.when(s % N_EXPERTS == N_EXPERTS - 1)
    def _():
        add_rows(nt, all_rows)


def _dispatch_tables(eid_t, counts, n_tok):
    tm, ts = MOE_TILE, MOE_BLOCK_TOKENS
    ng = MOE_BLOCKS * N_EXPERTS
    assert ts < (1 << TOKEN_BITS)
    tok = jnp.arange(n_tok, dtype=jnp.int32)
    grp = (tok // ts)[None, :] * N_EXPERTS + eid_t
    gids = jnp.arange(ng, dtype=jnp.int32)
    npad = (-counts) % tm
    fill = jnp.arange(tm, dtype=jnp.int32)[None, :] < npad[:, None]
    keys = jnp.concatenate([
        ((grp << TOKEN_BITS) | (tok % ts)[None, :]).reshape(-1),
        ((jnp.where(fill, gids[:, None], ng) << TOKEN_BITS) | ts).reshape(-1)])
    toks = lax.sort(keys, dimension=0, is_stable=False) & ((1 << TOKEN_BITS) - 1)
    n_tiles = (n_tok * TOP_K + ng * tm) // tm

    ntile = (counts + npad) // tm
    tile_start = jnp.cumsum(ntile) - ntile
    src = jnp.where(toks < ts, toks, 0) * PACK_ROWS
    dst = toks * ACC_ROWS
    head = jnp.zeros((tm,), jnp.int32)
    tail = jnp.zeros((GROUP_TILES * tm,), jnp.int32)
    rows = n_tiles + 1 + GROUP_TILES
    return {"t0": tile_start.astype(jnp.int32), "nt": ntile.astype(jnp.int32),
            "src": jnp.concatenate([head, src, tail]).reshape(rows, 1, tm),
            "dst": jnp.concatenate([head, dst, tail]).reshape(rows, 1, tm)}


def _moe(hp, meta, tabs, ep):
    tm, ts = MOE_TILE, MOE_BLOCK_TOKENS
    assert GROUP_TILES >= -(-ts // tm) + 2
    window = pl.BlockSpec((pl.Element(GROUP_TILES), pl.Element(1), pl.Element(tm)),
                          lambda s, t0, nt: (t0[s], 0, 0), memory_space=pltpu.SMEM)
    wspec = lambda shape: pl.BlockSpec((1,) + shape, lambda s, t0, nt: (s % N_EXPERTS, 0, 0))
    blk = lambda s, t0, nt: (s // N_EXPERTS, 0)
    acc_rows = (ts + SUBLANES) * ACC_ROWS
    assert acc_rows % ZERO_ROWS == 0
    stage_bufs = [pltpu.VMEM((PACK_ROWS * XT_PITCH, LANES), F32),
                  pltpu.VMEM((META_ROWS * XT_PITCH, LANES), F32),
                  pltpu.VMEM((ACC_ROWS * OT_PITCH, LANES), F32),
                  pltpu.VMEM((tm, D_MODEL), BF16),
                  pltpu.VMEM((1, D_MODEL, D_EXPERT), BF16),
                  pltpu.VMEM((1, D_MODEL, D_EXPERT), BF16),
                  pltpu.VMEM((1, D_EXPERT, D_MODEL), BF16)]
    grid_spec = pltpu.PrefetchScalarGridSpec(
        num_scalar_prefetch=2,
        grid=(MOE_BLOCKS * N_EXPERTS,),
        in_specs=[window, window,
                  pl.BlockSpec((ts * PACK_ROWS, LANES), blk, pipeline_mode=pl.Buffered(1)),
                  pl.BlockSpec((ts * META_ROWS, LANES), blk, pipeline_mode=pl.Buffered(1)),
                  wspec((D_MODEL, D_EXPERT)), wspec((D_MODEL, D_EXPERT)),
                  wspec((D_EXPERT, D_MODEL))],
        out_specs=pl.BlockSpec((1, acc_rows, LANES), lambda s, t0, nt: (s // N_EXPERTS, 0, 0),
                               pipeline_mode=pl.Buffered(1)),
        scratch_shapes=stage_bufs,
    )
    return pl.pallas_call(
        _moe_kernel,
        grid_spec=grid_spec,
        out_shape=jax.ShapeDtypeStruct((MOE_BLOCKS, acc_rows, LANES), F32),
        compiler_params=_cparams("arbitrary"),
        name="moe_experts",
    )(tabs["t0"], tabs["nt"], tabs["src"], tabs["dst"], hp, meta, ep["wg"], ep["wu"], ep["wd"])


def _final_kernel(xs_ref, r_ref, fn_ref, y_ref):
    tm = xs_ref.shape[0]
    routed = jnp.concatenate(
        [r_ref[0, pl.ds(c, tm, stride=ACC_ROWS), :] for c in range(ACC_ROWS)], axis=1)
    y_ref[...] = _rms(xs_ref[...] + routed, fn_ref[...])


def _finalize(xs, routed, final_norm, tm, tok0):
    rows = xs.shape[0]
    per_block = MOE_BLOCK_TOKENS // tm
    first = tok0 // tm
    assert tok0 % tm == 0 and MOE_BLOCK_TOKENS % tm == 0
    return pl.pallas_call(
        _final_kernel,
        grid=(rows // tm,),
        in_specs=[pl.BlockSpec((tm, D_MODEL), lambda i: (i, 0)),
                  pl.BlockSpec((1, tm * ACC_ROWS, LANES),
                               lambda i: ((first + i) // per_block, (first + i) % per_block, 0)),
                  _const_spec((1, D_MODEL))],
        out_specs=pl.BlockSpec((tm, D_MODEL), lambda i: (i, 0)),
        out_shape=jax.ShapeDtypeStruct((rows, D_MODEL), F32),
        compiler_params=_cparams("parallel"),
        name="final_norm",
    )(xs, routed, final_norm)


def _rope_tables(pos):
    half = RET_HEAD_DIM // 2
    inv_freq = ROPE_BASE ** (-jnp.arange(half, dtype=F32) / half)
    ang = pos.astype(F32)[:, None] * inv_freq[None, :]
    cos, sin = jnp.cos(ang), jnp.sin(ang)
    cos_t = jnp.concatenate([cos, cos, cos, cos], axis=1)
    sin_t = jnp.concatenate([-sin, sin, -sin, sin], axis=1)
    return cos_t, sin_t


def kernel(x_prompt, x_sample, state_s5_re, state_s5_im, state_ret, meta_tokens, norm1, w_in,
           s5_lam_re, s5_lam_im, s5_log_dt, s5_b_re, s5_b_im, s5_c_re, s5_c_im, s5_d, s5_w_glu,
           s5_norm, ret_norm, w_out, norm2, w_router, router_bias, w_gate_e, w_up_e, w_down_e,
           w_gate_sh, w_up_sh, w_down_sh, final_norm):
    assert norm1.shape[0] == 1, "single-layer model"
    bp, seq, _ = x_prompt.shape
    ns = x_sample.shape[0]
    l = 0

    n1 = norm1[l].reshape(1, D_MODEL)
    w_in_b = w_in[l].astype(BF16)
    s5p = _s5_params(s5_lam_re[l], s5_lam_im[l], s5_log_dt[l], s5_b_re[l], s5_b_im[l],
                     s5_c_re[l], s5_c_im[l], s5_d[l], s5_w_glu[l], s5_norm[l])
    rnorm = ret_norm[l].reshape(1, D_RET)
    w_out_b = w_out[l].astype(BF16)
    wr_t = w_router[l].T
    wr_hi = wr_t.astype(BF16)
    wr_lo = (wr_t - wr_hi.astype(F32)).astype(BF16)
    mp = {
        "woa": w_out_b[:D_S5], "wob": w_out_b[D_S5:], "n2": norm2[l].reshape(1, D_MODEL),
        "wrt": jnp.stack([wr_hi, wr_lo]),
        "rb": jnp.broadcast_to(router_bias[l][:, None], (N_EXPERTS, LANES)),
        "wgs": w_gate_sh[l].astype(BF16), "wus": w_up_sh[l].astype(BF16),
        "wds": w_down_sh[l].astype(BF16),
    }
    ep = {"wg": w_gate_e[l], "wu": w_up_e[l],
          "wd": w_down_e[l]}
    fnorm = final_norm.reshape(1, D_MODEL)

    xp = x_prompt.reshape(bp * seq, D_MODEL)
    cos_p, sin_p = _rope_tables(N_META + jnp.arange(seq, dtype=jnp.int32))
    tm_a = 1024
    up, qp, kp, vp, gp = _project(xp, n1, w_in_b, cos_p, sin_p, tm_a, seq // tm_a)

    meta_tm = jnp.repeat(meta_tokens, SUBLANES, axis=0)
    mchunk = LANES
    meta_chunk = jnp.concatenate([jnp.zeros((mchunk - N_META, D_MODEL), F32), meta_tokens], axis=0)
    x_small = jnp.concatenate([x_sample.reshape(ns, D_MODEL), meta_tm, meta_chunk], axis=0)
    meta_pos = jnp.arange(N_META, dtype=jnp.int32)
    pos_small = jnp.concatenate([jnp.full((ns,), PAST_LEN, jnp.int32),
                                 jnp.repeat(meta_pos, SUBLANES),
                                 jnp.zeros((mchunk - N_META,), jnp.int32), meta_pos])
    cos_s, sin_s = _rope_tables(pos_small)
    n_small = x_small.shape[0]
    n_tm = ns + N_META * SUBLANES
    us, qs, ks, vs, gs = _project(x_small, n1, w_in_b, cos_s, sin_s, n_small, 1)

    zero8 = jnp.zeros((SUBLANES, S5_LANES), F32)
    _, m_re, m_im = _s5(us[ns:n_tm], zero8, zero8, s5p, nb=SUBLANES, tt=N_META, nblk=1,
                        bt_major=False)
    tt = 64
    ys5_p, p_re, p_im = _s5(up.reshape(bp, seq, D_S5), m_re, m_im, s5p, nb=bp, tt=tt,
                            nblk=seq // tt, bt_major=True)
    ys5_p = ys5_p.reshape(bp * seq, D_S5)
    ys5_s, s_re, s_im = _s5(us[:ns], state_s5_re[l].reshape(ns, S5_LANES),
                            state_s5_im[l].reshape(ns, S5_LANES), s5p, nb=ns, tt=1, nblk=1,
                            bt_major=False)

    zero_pair = jnp.zeros((1, N_HEAD_PAIRS, LANES, LANES), F32)
    _, m_pair = _retention(qs[n_tm:], ks[n_tm:], vs[n_tm:], gs[n_tm:],
                           zero_pair, rnorm, nseq=1, chunk=mchunk, nchunk=1)
    chunk = 256
    yret_p, p_pair = _retention(qp, kp, vp, gp, jnp.broadcast_to(m_pair, (bp,) + m_pair.shape[1:]),
                                rnorm, nseq=bp, chunk=chunk, nchunk=seq // chunk)
    yret_s, ret_s = _retention_step(qs[:ns], ks[:ns], vs[:ns], gs[:ns], state_ret[l], ret_norm[l])

    n_prompt = bp * seq
    assert n_prompt + ns <= MOE_BLOCKS * MOE_BLOCK_TOKENS
    n_moe = MOE_BLOCKS * MOE_BLOCK_TOKENS
    tm_mix = 512
    xs_p, hp, meta, eid_p, cnt_p = _mix(xp, ys5_p, yret_p, mp, tm_mix, n_moe)
    xs_s, hp_s, meta_s, eid_s, cnt_s = _mix(x_sample.reshape(ns, D_MODEL), ys5_s, yret_s, mp,
                                            ns, ns)
    hp = lax.dynamic_update_slice(hp, hp_s, (n_prompt * PACK_ROWS, 0))
    meta = lax.dynamic_update_slice(meta, meta_s, (n_prompt * META_ROWS, 0))
    blk_s = n_prompt // MOE_BLOCK_TOKENS
    assert MOE_BLOCK_TOKENS % tm_mix == 0 and (n_prompt + ns - 1) // MOE_BLOCK_TOKENS == blk_s
    counts = cnt_p[:, :, 0].reshape(MOE_BLOCKS, MOE_BLOCK_TOKENS // tm_mix, N_EXPERTS).sum(axis=1)
    counts = counts.at[blk_s].add(cnt_s[0, :, 0]).reshape(-1).astype(jnp.int32)
    tabs = _dispatch_tables(jnp.concatenate([eid_p, eid_s], axis=1), counts, n_prompt + ns)
    routed = _moe(hp, meta, tabs, ep)
    y_p = _finalize(xs_p, routed, fnorm, 512, 0)
    y_s = _finalize(xs_s, routed, fnorm, ns, n_prompt)

    shape5 = (1, bp, N_S5_GROUPS, S5_STATE)
    return (y_p.reshape(bp, seq, D_MODEL),
            y_s.reshape(ns, 1, D_MODEL),
            p_re.reshape(shape5), p_im.reshape(shape5),
            _unpair_state(p_pair)[None],
            s_re.reshape(1, ns, N_S5_GROUPS, S5_STATE), s_im.reshape(1, ns, N_S5_GROUPS, S5_STATE),
            ret_s[None])
```

```python
import functools

import jax
import jax.numpy as jnp
import numpy as np
from jax import lax
from jax.experimental import pallas as pl
from jax.experimental.pallas import tpu as pltpu

F32 = jnp.float32
BF16 = jnp.bfloat16

D_MODEL = 1024
N_META = 16
PAST_LEN = 16384
D_S5 = 512
S5_GROUP = 16
N_S5_GROUPS = 32
S5_STATE = 64
S5_LANES = N_S5_GROUPS * S5_STATE
D_RET = 512
N_RET_HEADS = 8
RET_HEAD_DIM = 64
N_HEAD_PAIRS = 4
ROPE_BASE = 10000.0
D_IN = D_S5 + 4 * D_RET
N_EXPERTS = 64
TOP_K = 8
N_EXPERT_GROUPS = 8
GROUP_SIZE = 8
TOPK_GROUPS = 4
D_EXPERT = 256
ROUTED_SCALE = 2.5
EPS = 1e-6

LANES = 128
SUBLANES = 8
VMEM_LIMIT = 62 * 1024 * 1024

PACK_ROWS = D_MODEL // LANES
ACC_ROWS = D_MODEL // LANES
MOE_BLOCKS = 3
MOE_BLOCK_TOKENS = 5632
MOE_TILE = 256
GROUP_TILES = MOE_BLOCK_TOKENS // MOE_TILE + 2
ADD_UNROLL = 8
OT_PITCH = MOE_TILE + SUBLANES
XT_PITCH = MOE_TILE + SUBLANES
META_ROWS = 2
META_SHIFT = 2
assert PACK_ROWS == META_ROWS << META_SHIFT
TOKEN_BITS = 13
ZERO_ROWS = 64
CAST_ROWS = 128


def _cparams(*sem):
    return pltpu.CompilerParams(dimension_semantics=sem, vmem_limit_bytes=VMEM_LIMIT)


def _const_spec(shape):
    nd = len(shape)
    return pl.BlockSpec(shape, lambda *_: (0,) * nd)


def _rms(x, gain):
    return x * lax.rsqrt(jnp.mean(x * x, axis=-1, keepdims=True) + EPS) * gain


def _proj_kernel(x_ref, n1_ref, w_ref, cos_ref, sin_ref, u_ref, q_ref, k_ref, v_ref, g_ref):
    h = _rms(x_ref[...], n1_ref[...]).astype(BF16)
    proj = jnp.dot(h, w_ref[...], preferred_element_type=F32)
    cos = cos_ref[...]
    sin = sin_ref[...]
    lane = lax.broadcasted_iota(jnp.int32, cos.shape, 1)
    first_half = (lane % RET_HEAD_DIM) < (RET_HEAD_DIM // 2)

    def rotary(t):
        partner = jnp.where(first_half,
                            pltpu.roll(t, LANES - RET_HEAD_DIM // 2, 1),
                            pltpu.roll(t, RET_HEAD_DIM // 2, 1))
        return t * cos + partner * sin

    u_ref[...] = proj[:, :D_S5]
    for j in range(N_HEAD_PAIRS):
        lo = D_S5 + j * LANES
        q_ref[:, j * LANES:(j + 1) * LANES] = rotary(proj[:, lo:lo + LANES]).astype(BF16)
        lo += D_RET
        k_ref[:, j * LANES:(j + 1) * LANES] = (
            rotary(proj[:, lo:lo + LANES]) * (RET_HEAD_DIM ** -0.5)).astype(BF16)
    v_ref[...] = proj[:, D_S5 + 2 * D_RET:D_S5 + 3 * D_RET].astype(BF16)
    g_ref[...] = proj[:, D_S5 + 3 * D_RET:]


def _project(x, norm1, w_in_b, cos_t, sin_t, tm, table_blocks):
    rows = x.shape[0]
    row = lambda i: (i, 0)
    tab = lambda i: (i % table_blocks, 0)
    return pl.pallas_call(
        _proj_kernel,
        grid=(rows // tm,),
        in_specs=[pl.BlockSpec((tm, D_MODEL), row),
                  _const_spec((1, D_MODEL)),
                  _const_spec((D_MODEL, D_IN)),
                  pl.BlockSpec((tm, LANES), tab),
                  pl.BlockSpec((tm, LANES), tab)],
        out_specs=[pl.BlockSpec((tm, D_S5), row),
                   pl.BlockSpec((tm, D_RET), row),
                   pl.BlockSpec((tm, D_RET), row),
                   pl.BlockSpec((tm, D_RET), row),
                   pl.BlockSpec((tm, D_RET), row)],
        out_shape=[jax.ShapeDtypeStruct((rows, D_S5), F32),
                   jax.ShapeDtypeStruct((rows, D_RET), BF16),
                   jax.ShapeDtypeStruct((rows, D_RET), BF16),
                   jax.ShapeDtypeStruct((rows, D_RET), BF16),
                   jax.ShapeDtypeStruct((rows, D_RET), F32)],
        compiler_params=_cparams("parallel"),
        name="in_proj",
    )(x, norm1, w_in_b, cos_t, sin_t)


def _s5_kernel(u_ref, x0re_ref, x0im_ref, are_ref, aim_ref, b_ref, c_ref, d_ref, wglu_ref,
               nrm_ref, y_ref, sre_ref, sim_ref, st, sre, sim, utm, ytm, *, nb, tt, bt_major):
    i = pl.program_id(0)
    rows = nb * tt

    @pl.when(i == 0)
    def _():
        sre[...] = x0re_ref[...]
        sim[...] = x0im_ref[...]

    if bt_major:
        for t in range(tt):
            utm[t * nb:(t + 1) * nb, :] = u_ref[:, t, :]
    else:
        utm[...] = u_ref[...]

    ub = utm[...].astype(BF16)
    half = S5_LANES // 2
    kh = D_S5 // 2
    for part in range(2):
        for hf in range(2):
            st[:, part * S5_LANES + hf * half:part * S5_LANES + (hf + 1) * half] = jnp.dot(
                ub[:, hf * kh:(hf + 1) * kh], b_ref[part, hf], preferred_element_type=F32)

    lw = 512 if nb == SUBLANES else LANES
    for lg in range(S5_LANES // lw):
        re = slice(lg * lw, (lg + 1) * lw)
        im = slice(S5_LANES + lg * lw, S5_LANES + (lg + 1) * lw)
        a_re = are_ref[:, re]
        a_im = aim_ref[:, re]

        def step(t, carry):
            s_re, s_im = carry
            r0 = pl.multiple_of(t * nb, nb)
            n_re = a_re * s_re - a_im * s_im + st[pl.ds(r0, nb), re]
            n_im = a_re * s_im + a_im * s_re + st[pl.ds(r0, nb), im]
            st[pl.ds(r0, nb), re] = n_re
            st[pl.ds(r0, nb), im] = n_im
            return n_re, n_im

        if nb == SUBLANES:
            f_re, f_im = sre[:, re], sim[:, re]
            for t in range(tt):
                r = slice(t * nb, (t + 1) * nb)
                n_re = a_re * f_re - a_im * f_im + st[r, re]
                n_im = a_re * f_im + a_im * f_re + st[r, im]
                st[r, re] = n_re
                st[r, im] = n_im
                f_re, f_im = n_re, n_im
        else:
            f_re, f_im = lax.fori_loop(0, tt, step, (sre[:, re], sim[:, re]))
        sre[:, re] = f_re
        sim[:, re] = f_im

    sre_ref[...] = sre[...]
    sim_ref[...] = sim[...]

    ys = []
    for hf in range(2):
        xr = st[:, hf * half:(hf + 1) * half].astype(BF16)
        xi = st[:, S5_LANES + hf * half:S5_LANES + (hf + 1) * half].astype(BF16)
        ys.append(jnp.dot(xr, c_ref[0, hf], preferred_element_type=F32)
                  - jnp.dot(xi, c_ref[1, hf], preferred_element_type=F32))
    y = jnp.concatenate(ys, axis=1) + d_ref[...] * utm[...]
    y = jax.nn.gelu(y)
    y = y * jax.nn.sigmoid(jnp.dot(y.astype(BF16), wglu_ref[...], preferred_element_type=F32))
    y = _rms(y, nrm_ref[...]).astype(BF16)
    if bt_major:
        yf = y.astype(F32)
        for j in range(D_S5 // LANES):
            ytm[j] = yf[:, j * LANES:(j + 1) * LANES]
        for b in range(nb):
            for j in range(D_S5 // LANES):
                y_ref[b, :, j * LANES:(j + 1) * LANES] = (
                    ytm[j, pl.ds(b, tt, stride=nb), :].astype(BF16))
    else:
        y_ref[...] = y


def _s5(u, x0re, x0im, s5p, *, nb, tt, nblk, bt_major):
    rows = nb * tt
    if bt_major:
        u_spec = pl.BlockSpec((nb, tt, D_S5), lambda i: (0, i, 0))
        y_shape = jax.ShapeDtypeStruct((nb, tt * nblk, D_S5), BF16)
    else:
        u_spec = pl.BlockSpec((rows, D_S5), lambda i: (i, 0))
        y_shape = jax.ShapeDtypeStruct((rows * nblk, D_S5), BF16)
    state_spec = _const_spec((nb, S5_LANES))
    are = jnp.broadcast_to(s5p["are"], (nb, S5_LANES))
    aim = jnp.broadcast_to(s5p["aim"], (nb, S5_LANES))
    return pl.pallas_call(
        functools.partial(_s5_kernel, nb=nb, tt=tt, bt_major=bt_major),
        grid=(nblk,),
        in_specs=[u_spec, state_spec, state_spec, state_spec, state_spec,
                  _const_spec((2, 2, D_S5 // 2, S5_LANES // 2)),
                  _const_spec((2, 2, S5_LANES // 2, D_S5 // 2)),
                  _const_spec((1, D_S5)),
                  _const_spec((D_S5, D_S5)),
                  _const_spec((1, D_S5))],
        out_specs=[u_spec, state_spec, state_spec],
        out_shape=[y_shape,
                   jax.ShapeDtypeStruct((nb, S5_LANES), F32),
                   jax.ShapeDtypeStruct((nb, S5_LANES), F32)],
        scratch_shapes=[pltpu.VMEM((rows, 2 * S5_LANES), F32),
                        pltpu.VMEM((nb, S5_LANES), F32),
                        pltpu.VMEM((nb, S5_LANES), F32),
                        pltpu.VMEM((rows, D_S5), F32),
                        pltpu.VMEM((D_S5 // LANES, rows, LANES), F32)],
        compiler_params=_cparams("arbitrary"),
        name="s5_" + ("bt" if bt_major else "tm") + str(nb),
    )(u, x0re, x0im, are, aim, s5p["b"], s5p["c"], s5p["d"], s5p["wglu"], s5p["nrm"])


def _s5_params(lam_re, lam_im, log_dt, b_re, b_im, c_re, c_im, d_skip, w_glu, nrm):
    dt = jnp.exp(log_dt)[:, None]
    mag = jnp.exp(lam_re * dt)
    abar_re, abar_im = mag * jnp.cos(lam_im * dt), mag * jnp.sin(lam_im * dt)
    num_re, num_im = abar_re - 1.0, abar_im
    den = lam_re * lam_re + lam_im * lam_im
    f_re = (num_re * lam_re + num_im * lam_im) / den
    f_im = (num_im * lam_re - num_re * lam_im) / den
    bbar_re = f_re[..., None] * b_re - f_im[..., None] * b_im
    bbar_im = f_re[..., None] * b_im + f_im[..., None] * b_re
    hg = N_S5_GROUPS // 2
    eye = jnp.eye(hg, dtype=F32)

    def bdiag(bb):
        bb = bb.reshape(2, hg, S5_STATE, S5_GROUP)
        return jnp.einsum("zgph,gk->zghkp", bb, eye).reshape(2, hg * S5_GROUP, hg * S5_STATE)

    def cdiag(cc):
        cc = cc.reshape(2, hg, S5_GROUP, S5_STATE)
        return jnp.einsum("zgnp,gk->zgpkn", cc, eye).reshape(2, hg * S5_STATE, hg * S5_GROUP)

    return {
        "are": abar_re.reshape(1, S5_LANES), "aim": abar_im.reshape(1, S5_LANES),
        "b": jnp.stack([bdiag(bbar_re), bdiag(bbar_im)]).astype(BF16),
        "c": jnp.stack([cdiag(c_re), cdiag(c_im)]).astype(BF16),
        "d": d_skip.reshape(1, D_S5), "wglu": w_glu.astype(BF16), "nrm": nrm.reshape(1, D_S5),
    }


def _head_norm_gate(o, g, gain, lo):
    inv = 1.0 / RET_HEAD_DIM

    def seg_mean(t):
        s_lo = jnp.sum(jnp.where(lo, t, 0.0), axis=1, keepdims=True)
        s_hi = jnp.sum(jnp.where(lo, 0.0, t), axis=1, keepdims=True)
        return jnp.where(lo, s_lo, s_hi) * inv

    dlt = o - seg_mean(o)
    var = seg_mean(dlt * dlt)
    return jax.nn.silu(g) * (dlt * lax.rsqrt(var + EPS) * gain)


def _ret_kernel(q_ref, k_ref, v_ref, g_ref, s0_ref, dm_ref, qd_ref, kd_ref, gc_ref, bm_ref,
                nrm_ref, y_ref, so_ref, s_acc):
    c = pl.program_id(1)

    @pl.when(c == 0)
    def _():
        s_acc[...] = s0_ref[0]

    rows = q_ref.shape[0]
    lane = lax.broadcasted_iota(jnp.int32, (rows, LANES), 1)
    lo = lane < RET_HEAD_DIM
    for j in range(N_HEAD_PAIRS):
        sl = slice(j * LANES, (j + 1) * LANES)
        q2 = q_ref[:, sl].astype(F32)
        k2 = k_ref[:, sl]
        v2 = v_ref[:, sl]
        s_pair = s_acc[j]
        cross = jnp.dot((q2 * qd_ref[:, sl]).astype(BF16), s_pair.astype(BF16),
                        preferred_element_type=F32)
        k_dec = (k2.astype(F32) * kd_ref[:, sl]).astype(BF16)
        upd = lax.dot_general(k_dec, v2, (((0,), (0,)), ((), ())), preferred_element_type=F32)
        s_acc[j] = gc_ref[j] * s_pair + bm_ref[...] * upd
        inner = []
        for hh, qh in enumerate((jnp.where(lo, q2, 0.0), jnp.where(lo, 0.0, q2))):
            sc = lax.dot_general(qh.astype(BF16), k2, (((1,), (1,)), ((), ())),
                                 preferred_element_type=F32) * dm_ref[2 * j + hh]
            inner.append(jnp.dot(sc.astype(BF16), v2, preferred_element_type=F32))
        o = jnp.where(lo, inner[0], inner[1]) + cross
        y_ref[:, sl] = _head_norm_gate(o, g_ref[:, sl], nrm_ref[:, sl], lo).astype(BF16)
    so_ref[0] = s_acc[...]


def _ret_log_decay():
    return np.log1p(-np.exp2(-5.0 - np.arange(N_RET_HEADS, dtype=np.float64)))


def _ret_tables(chunk):
    log_g = _ret_log_decay()
    n = np.arange(chunk, dtype=np.float64)
    diff = n[:, None] - n[None, :]
    dm = np.where(diff[None] >= 0.0,
                  np.exp(log_g[:, None, None] * np.maximum(diff, 0.0)[None]), 0.0)
    per_lane = lambda t: np.repeat(t.T, RET_HEAD_DIM, axis=1)
    qd = per_lane(np.exp(log_g[:, None] * (n + 1.0)[None]))
    kd = per_lane(np.exp(log_g[:, None] * (chunk - 1.0 - n)[None]))
    gch = np.exp(log_g * chunk)
    blk = np.kron(np.eye(2), np.ones((RET_HEAD_DIM, RET_HEAD_DIM)))
    gc = np.repeat(gch.reshape(N_HEAD_PAIRS, 2), RET_HEAD_DIM, axis=1)[:, :, None] * blk[None]
    return tuple(jnp.asarray(t, F32) for t in (dm, qd, kd, gc, blk))


def _pair_state(s):
    b = s.shape[0]
    s = s.reshape(b, N_HEAD_PAIRS, 2, RET_HEAD_DIM, RET_HEAD_DIM)
    z = jnp.zeros_like(s[:, :, 0])
    top = jnp.concatenate([s[:, :, 0], z], axis=-1)
    bot = jnp.concatenate([z, s[:, :, 1]], axis=-1)
    return jnp.concatenate([top, bot], axis=-2)


def _unpair_state(s2):
    d = RET_HEAD_DIM
    return jnp.stack([s2[:, :, :d, :d], s2[:, :, d:, d:]], axis=2).reshape(
        s2.shape[0], N_RET_HEADS, d, d)


def _retention(q, k, v, g, s0_pair, ret_norm, *, nseq, chunk, nchunk):
    dm, qd, kd, gc, blk = _ret_tables(chunk)
    row = lambda b, c: (b * nchunk + c, 0)
    st_spec = pl.BlockSpec((1, N_HEAD_PAIRS, LANES, LANES), lambda b, c: (b, 0, 0, 0))
    blk_spec = pl.BlockSpec((chunk, D_RET), row)
    return pl.pallas_call(
        _ret_kernel,
        grid=(nseq, nchunk),
        in_specs=[blk_spec, blk_spec, blk_spec, blk_spec, st_spec,
                  _const_spec((N_RET_HEADS, chunk, chunk)),
                  _const_spec((chunk, D_RET)), _const_spec((chunk, D_RET)),
                  _const_spec((N_HEAD_PAIRS, LANES, LANES)), _const_spec((LANES, LANES)),
                  _const_spec((1, D_RET))],
        out_specs=[blk_spec, st_spec],
        out_shape=[jax.ShapeDtypeStruct((nseq * nchunk * chunk, D_RET), BF16),
                   jax.ShapeDtypeStruct((nseq, N_HEAD_PAIRS, LANES, LANES), F32)],
        scratch_shapes=[pltpu.VMEM((N_HEAD_PAIRS, LANES, LANES), F32)],
        compiler_params=_cparams("arbitrary", "arbitrary"),
        name="retention_c%d" % chunk,
    )(q, k, v, g, s0_pair, dm, qd, kd, gc, blk, ret_norm)


def _ret_step_kernel(q_ref, k_ref, v_ref, g_ref, s_ref, gam_ref, nrm_ref, y_ref, so_ref):
    q = q_ref[...]
    k = k_ref[...]
    v = v_ref[...]
    gam = gam_ref[0]
    score = jnp.sum(q * k, axis=0, keepdims=True)
    q_dec = (q * gam).astype(BF16).astype(F32)
    cross = jnp.zeros(v.shape, F32)
    for d in range(RET_HEAD_DIM):
        s_d = s_ref[0, d]
        cross = cross + q_dec[d:d + 1, :] * s_d
        so_ref[0, d] = gam * s_d + k[d:d + 1, :] * v
    o = score.astype(BF16).astype(F32) * v + cross
    dlt = o - jnp.mean(o, axis=0, keepdims=True)
    var = jnp.mean(dlt * dlt, axis=0, keepdims=True)
    y_ref[...] = jax.nn.silu(g_ref[...]) * (dlt * lax.rsqrt(var + EPS) * nrm_ref[...])


def _retention_step(q, k, v, g, state, ret_norm):
    n = q.shape[0]
    gam = np.exp(_ret_log_decay())
    gam_tab = jnp.asarray(np.broadcast_to(gam[:, None, None], (N_RET_HEADS, 1, n)), F32)
    seq_minor = lambda t: t.astype(F32).T
    head = lambda h: (h, 0)
    vec_spec = pl.BlockSpec((RET_HEAD_DIM, n), head)
    st_spec = pl.BlockSpec((1, RET_HEAD_DIM, RET_HEAD_DIM, n), lambda h: (h, 0, 0, 0))
    y_t, s_new = pl.pallas_call(
        _ret_step_kernel,
        grid=(N_RET_HEADS,),
        in_specs=[vec_spec, vec_spec, vec_spec, vec_spec, st_spec,
                  pl.BlockSpec((1, 1, n), lambda h: (h, 0, 0)), vec_spec],
        out_specs=[vec_spec, st_spec],
        out_shape=[jax.ShapeDtypeStruct((D_RET, n), F32),
                   jax.ShapeDtypeStruct((N_RET_HEADS, RET_HEAD_DIM, RET_HEAD_DIM, n), F32)],
        compiler_params=_cparams("parallel"),
        name="retention_step",
    )(seq_minor(q), seq_minor(k), seq_minor(v), seq_minor(g),
      jnp.transpose(state, (1, 2, 3, 0)), gam_tab,
      jnp.broadcast_to(ret_norm[:, None], (D_RET, n)))
    return y_t.T.astype(BF16), jnp.transpose(s_new, (3, 0, 1, 2))


def _first_max(vals, idxs, sentinel):
    m = jnp.max(vals, axis=0, keepdims=True)
    first = jnp.min(jnp.where(vals == m, idxs, sentinel), axis=0, keepdims=True)
    return m, idxs == first


def _route(scores, sel):
    t = sel.shape[1]
    neg = -jnp.inf
    member = lax.broadcasted_iota(jnp.int32, (GROUP_SIZE, t), 0).astype(F32)
    groups = [sel[g * GROUP_SIZE:(g + 1) * GROUP_SIZE, :] for g in range(N_EXPERT_GROUPS)]
    gscore = []
    for grp in groups:
        m1, pick = _first_max(grp, member, float(GROUP_SIZE))
        m2 = jnp.max(jnp.where(pick, neg, grp), axis=0, keepdims=True)
        gscore.append(m1 + m2)
    gs = jnp.concatenate(gscore, axis=0)
    gkeep = jnp.zeros(gs.shape, F32)
    for _ in range(TOPK_GROUPS):
        _, pick = _first_max(gs, member, float(N_EXPERT_GROUPS))
        gkeep = jnp.where(pick, 1.0, gkeep)
        gs = jnp.where(pick, neg, gs)
    cand = jnp.concatenate(
        [jnp.where(gkeep[g:g + 1, :] > 0.5, groups[g], neg) for g in range(N_EXPERT_GROUPS)],
        axis=0)
    expert = lax.broadcasted_iota(jnp.int32, (N_EXPERTS, t), 0).astype(F32)
    ids, ws = [], []
    chosen = jnp.zeros(cand.shape, F32)
    for _ in range(TOP_K):
        _, pick = _first_max(cand, expert, float(N_EXPERTS))
        ids.append(jnp.sum(jnp.where(pick, expert, 0.0), axis=0, keepdims=True))
        ws.append(jnp.sum(jnp.where(pick, scores, 0.0), axis=0, keepdims=True))
        chosen = jnp.where(pick, 1.0, chosen)
        cand = jnp.where(pick, neg, cand)
    w = jnp.concatenate(ws, axis=0)
    gates = w / jnp.sum(w, axis=0, keepdims=True) * ROUTED_SCALE
    return jnp.concatenate(ids, axis=0), gates, jnp.sum(chosen, axis=1, keepdims=True)


def _mix_kernel(x_ref, ys_ref, yr_ref, woa_ref, wob_ref, n2_ref, wrt_ref, rb_ref,
                wgs_ref, wus_ref, wds_ref, xs_ref, hp_ref, meta_ref, eid_ref, cnt_ref, *,
                n_blocks):
    @pl.when(pl.program_id(0) >= n_blocks)
    def _():
        hp_ref[...] = jnp.zeros_like(hp_ref)
        meta_ref[...] = jnp.zeros_like(meta_ref)
        cnt_ref[...] = jnp.zeros_like(cnt_ref)

    @pl.when(pl.program_id(0) < n_blocks)
    def _():
        _mix_body(x_ref, ys_ref, yr_ref, woa_ref, wob_ref, n2_ref, wrt_ref, rb_ref,
                  wgs_ref, wus_ref, wds_ref, xs_ref, hp_ref, meta_ref, eid_ref, cnt_ref)


def _mix_body(x_ref, ys_ref, yr_ref, woa_ref, wob_ref, n2_ref, wrt_ref, rb_ref,
              wgs_ref, wus_ref, wds_ref, xs_ref, hp_ref, meta_ref, eid_ref, cnt_ref):
    tm = x_ref.shape[0]
    x2 = (x_ref[...]
          + jnp.dot(ys_ref[...], woa_ref[...], preferred_element_type=F32)
          + jnp.dot(yr_ref[...], wob_ref[...], preferred_element_type=F32))
    h2f = _rms(x2, n2_ref[...])
    h2 = h2f.astype(BF16)
    h2r = h2.astype(F32)
    for c in range(PACK_ROWS):
        hp_ref[pl.ds(c, tm, stride=PACK_ROWS), :] = h2r[:, c * LANES:(c + 1) * LANES]
    h2_lo = (h2f - h2r).astype(BF16)
    nt_dims = (((1,), (1,)), ((), ()))
    logits = (lax.dot_general(wrt_ref[0], h2, nt_dims, preferred_element_type=F32)
              + lax.dot_general(wrt_ref[1], h2, nt_dims, preferred_element_type=F32)
              + lax.dot_general(wrt_ref[0], h2_lo, nt_dims, preferred_element_type=F32))
    scores = jax.nn.sigmoid(logits)
    ids, gates, counts = _route(scores, scores + rb_ref[:, :1])
    eid_ref[...] = ids.astype(jnp.int32)
    cnt_ref[0] = jnp.broadcast_to(counts, (N_EXPERTS, LANES))
    rec = jnp.concatenate([gates, ids, jnp.zeros((LANES - 2 * TOP_K, tm), F32)], axis=0).T
    meta_ref[pl.ds(0, tm, stride=META_ROWS), :] = rec
    for c in range(1, META_ROWS):
        meta_ref[pl.ds(c, tm, stride=META_ROWS), :] = jnp.zeros((tm, LANES), F32)
    a = jnp.dot(h2, wgs_ref[...], preferred_element_type=F32)
    b = jnp.dot(h2, wus_ref[...], preferred_element_type=F32)
    xs_ref[...] = x2 + jnp.dot((jax.nn.silu(a) * b).astype(BF16), wds_ref[...],
                               preferred_element_type=F32)


def _mix(x, ys5, yret, mp, tm, out_tokens):
    rows = x.shape[0]
    n_blocks = rows // tm
    assert out_tokens % tm == 0
    packed = lambda i: (i, 0)
    row = lambda i: (jnp.minimum(i, n_blocks - 1), 0)
    col = lambda i: (0, jnp.minimum(i, n_blocks - 1))
    return pl.pallas_call(
        functools.partial(_mix_kernel, n_blocks=n_blocks),
        grid=(out_tokens // tm,),
        in_specs=[pl.BlockSpec((tm, D_MODEL), row),
                  pl.BlockSpec((tm, D_S5), row), pl.BlockSpec((tm, D_RET), row),
                  _const_spec((D_S5, D_MODEL)), _const_spec((D_RET, D_MODEL)),
                  _const_spec((1, D_MODEL)),
                  _const_spec((2, N_EXPERTS, D_MODEL)), _const_spec((N_EXPERTS, LANES)),
                  _const_spec((D_MODEL, D_EXPERT)), _const_spec((D_MODEL, D_EXPERT)),
                  _const_spec((D_EXPERT, D_MODEL))],
        out_specs=[pl.BlockSpec((tm, D_MODEL), row),
                   pl.BlockSpec((tm * PACK_ROWS, LANES), packed),
                   pl.BlockSpec((tm * META_ROWS, LANES), packed),
                   pl.BlockSpec((TOP_K, tm), col),
                   pl.BlockSpec((1, N_EXPERTS, LANES), lambda i: (i, 0, 0))],
        out_shape=[jax.ShapeDtypeStruct((rows, D_MODEL), F32),
                   jax.ShapeDtypeStruct((out_tokens * PACK_ROWS, LANES), F32),
                   jax.ShapeDtypeStruct((out_tokens * META_ROWS, LANES), F32),
                   jax.ShapeDtypeStruct((TOP_K, rows), jnp.int32),
                   jax.ShapeDtypeStruct((out_tokens // tm, N_EXPERTS, LANES), F32)],
        compiler_params=_cparams("arbitrary"),
        name="out_proj_router",
    )(x, ys5, yret, mp["woa"], mp["wob"], mp["n2"], mp["wrt"], mp["rb"],
      mp["wgs"], mp["wus"], mp["wds"])


def _moe_kernel(t0_ref, nt_ref, src_ref, dst_ref, hp_ref, meta_ref, wg32_ref, wu32_ref, wd32_ref,
                acc_ref, xt_ref, xm_ref, ot_ref, xb_ref, wg_ref, wu_ref, wd_ref):
    s = pl.program_id(0)
    tm = MOE_TILE
    half = D_MODEL // 2
    all_rows = range(tm)
    nt = nt_ref[s]
    expert = (s % N_EXPERTS).astype(F32)

    @pl.when(nt > 0)
    def _():
        for src32, dst16 in ((wg32_ref, wg_ref), (wu32_ref, wu_ref), (wd32_ref, wd_ref)):
            rows = src32.shape[1]
            for i in range(0, rows, CAST_ROWS):
                dst16[0, i:i + CAST_ROWS, :] = src32[0, i:i + CAST_ROWS, :].astype(BF16)

    def fetch_rows(j, rows):
        for m in rows:
            r = pl.multiple_of(src_ref[j, 0, m], PACK_ROWS)
            xt_ref[pl.ds(m, PACK_ROWS, stride=XT_PITCH), :] = hp_ref[pl.ds(r, PACK_ROWS), :]
            xm_ref[pl.ds(m, META_ROWS, stride=XT_PITCH), :] = meta_ref[
                pl.ds(pl.multiple_of(r >> META_SHIFT, META_ROWS), META_ROWS), :]

    def add_rows(j, rows):
        for m0 in range(rows.start, rows.stop, ADD_UNROLL):
            new = []
            for m in range(m0, m0 + ADD_UNROLL):
                base = pl.multiple_of(dst_ref[j, 0, m], ACC_ROWS)
                row = ot_ref[pl.ds(m, ACC_ROWS, stride=OT_PITCH), :]
                new.append((base, acc_ref[0, pl.ds(base, ACC_ROWS), :] + row))
            for base, v in new:
                acc_ref[0, pl.ds(base, ACC_ROWS), :] = v

    @pl.when(s % N_EXPERTS == 0)
    def _():
        zero = jnp.zeros((ZERO_ROWS, LANES), F32)

        def clear(i, carry):
            acc_ref[0, pl.ds(pl.multiple_of(i * ZERO_ROWS, ZERO_ROWS), ZERO_ROWS), :] = zero
            return carry

        lax.fori_loop(0, acc_ref.shape[1] // ZERO_ROWS, clear, 0)
        ot_ref[...] = jnp.zeros_like(ot_ref)
        fetch_rows(1, all_rows)

    def stage(j, carry):
        n_slice = 4
        per = tm // n_slice
        rows = [range(i * per, (i + 1) * per) for i in range(n_slice)]
        for c in range(PACK_ROWS):
            xb_ref[:, c * LANES:(c + 1) * LANES] = (
                xt_ref[c * XT_PITCH:c * XT_PITCH + tm, :].astype(BF16))
        rec = xm_ref[0:tm, :]
        ids = pltpu.roll(rec, LANES - TOP_K, 1)
        lane = lax.broadcasted_iota(jnp.int32, rec.shape, 1)
        gate = jnp.sum(jnp.where((lane < TOP_K) & (ids == expert), rec, 0.0),
                       axis=1, keepdims=True)
        add_rows(j - 1, rows[0])
        a = jnp.dot(xb_ref[:, :half], wg_ref[0, :half], preferred_element_type=F32)
        add_rows(j - 1, rows[1])
        a = a + jnp.dot(xb_ref[:, half:], wg_ref[0, half:], preferred_element_type=F32)
        add_rows(j - 1, rows[2])
        b = jnp.dot(xb_ref[:, :half], wu_ref[0, :half], preferred_element_type=F32)
        add_rows(j - 1, rows[3])
        b = b + jnp.dot(xb_ref[:, half:], wu_ref[0, half:], preferred_element_type=F32)
        act = (jax.nn.silu(a) * b * gate).astype(BF16)
        n_col = D_MODEL // n_slice
        for piece in range(n_slice):
            fetch_rows(j + 1, rows[piece])
            out = jnp.dot(act, wd_ref[0, :, piece * n_col:(piece + 1) * n_col],
                          preferred_element_type=F32)
            for cc in range(n_col // LANES):
                c = piece * (n_col // LANES) + cc
                ot_ref[c * OT_PITCH:c * OT_PITCH + tm, :] = out[:, cc * LANES:(cc + 1) * LANES]
        return carry

    lax.fori_loop(1, nt + 1, stage, 0)

    @pl.when(s % N_EXPERTS == N_EXPERTS - 1)
    def _():
        add_rows(nt, all_rows)


def _dispatch_tables(eid_t, counts, n_tok):
    tm, ts = MOE_TILE, MOE_BLOCK_TOKENS
    ng = MOE_BLOCKS * N_EXPERTS
    assert ts < (1 << TOKEN_BITS)
    tok = jnp.arange(n_tok, dtype=jnp.int32)
    grp = (tok // ts)[None, :] * N_EXPERTS + eid_t
    gids = jnp.arange(ng, dtype=jnp.int32)
    npad = (-counts) % tm
    fill = jnp.arange(tm, dtype=jnp.int32)[None, :] < npad[:, None]
    keys = jnp.concatenate([
        ((grp << TOKEN_BITS) | (tok % ts)[None, :]).reshape(-1),
        ((jnp.where(fill, gids[:, None], ng) << TOKEN_BITS) | ts).reshape(-1)])
    toks = lax.sort(keys, dimension=0, is_stable=False) & ((1 << TOKEN_BITS) - 1)
    n_tiles = (n_tok * TOP_K + ng * tm) // tm

    ntile = (counts + npad) // tm
    tile_start = jnp.cumsum(ntile) - ntile
    src = jnp.where(toks < ts, toks, 0) * PACK_ROWS
    dst = toks * ACC_ROWS
    head = jnp.zeros((tm,), jnp.int32)
    tail = jnp.zeros((GROUP_TILES * tm,), jnp.int32)
    rows = n_tiles + 1 + GROUP_TILES
    return {"t0": tile_start.astype(jnp.int32), "nt": ntile.astype(jnp.int32),
            "src": jnp.concatenate([head, src, tail]).reshape(rows, 1, tm),
            "dst": jnp.concatenate([head, dst, tail]).reshape(rows, 1, tm)}


def _moe(hp, meta, tabs, ep):
    tm, ts = MOE_TILE, MOE_BLOCK_TOKENS
    assert GROUP_TILES >= -(-ts // tm) + 2
    window = pl.BlockSpec((pl.Element(GROUP_TILES), pl.Element(1), pl.Element(tm)),
                          lambda s, t0, nt: (t0[s], 0, 0), memory_space=pltpu.SMEM)
    wspec = lambda shape: pl.BlockSpec((1,) + shape, lambda s, t0, nt: (s % N_EXPERTS, 0, 0))
    blk = lambda s, t0, nt: (s // N_EXPERTS, 0)
    acc_rows = (ts + SUBLANES) * ACC_ROWS
    assert acc_rows % ZERO_ROWS == 0
    stage_bufs = [pltpu.VMEM((PACK_ROWS * XT_PITCH, LANES), F32),
                  pltpu.VMEM((META_ROWS * XT_PITCH, LANES), F32),
                  pltpu.VMEM((ACC_ROWS * OT_PITCH, LANES), F32),
                  pltpu.VMEM((tm, D_MODEL), BF16),
                  pltpu.VMEM((1, D_MODEL, D_EXPERT), BF16),
                  pltpu.VMEM((1, D_MODEL, D_EXPERT), BF16),
                  pltpu.VMEM((1, D_EXPERT, D_MODEL), BF16)]
    grid_spec = pltpu.PrefetchScalarGridSpec(
        num_scalar_prefetch=2,
        grid=(MOE_BLOCKS * N_EXPERTS,),
        in_specs=[window, window,
                  pl.BlockSpec((ts * PACK_ROWS, LANES), blk, pipeline_mode=pl.Buffered(1)),
                  pl.BlockSpec((ts * META_ROWS, LANES), blk, pipeline_mode=pl.Buffered(1)),
                  wspec((D_MODEL, D_EXPERT)), wspec((D_MODEL, D_EXPERT)),
                  wspec((D_EXPERT, D_MODEL))],
        out_specs=pl.BlockSpec((1, acc_rows, LANES), lambda s, t0, nt: (s // N_EXPERTS, 0, 0),
                               pipeline_mode=pl.Buffered(1)),
        scratch_shapes=stage_bufs,
    )
    return pl.pallas_call(
        _moe_kernel,
        grid_spec=grid_spec,
        out_shape=jax.ShapeDtypeStruct((MOE_BLOCKS, acc_rows, LANES), F32),
        compiler_params=_cparams("arbitrary"),
        name="moe_experts",
    )(tabs["t0"], tabs["nt"], tabs["src"], tabs["dst"], hp, meta, ep["wg"], ep["wu"], ep["wd"])


def _final_kernel(xs_ref, r_ref, fn_ref, y_ref):
    tm = xs_ref.shape[0]
    routed = jnp.concatenate(
        [r_ref[0, pl.ds(c, tm, stride=ACC_ROWS), :] for c in range(ACC_ROWS)], axis=1)
    y_ref[...] = _rms(xs_ref[...] + routed, fn_ref[...])


def _finalize(xs, routed, final_norm, tm, tok0):
    rows = xs.shape[0]
    per_block = MOE_BLOCK_TOKENS // tm
    first = tok0 // tm
    assert tok0 % tm == 0 and MOE_BLOCK_TOKENS % tm == 0
    return pl.pallas_call(
        _final_kernel,
        grid=(rows // tm,),
        in_specs=[pl.BlockSpec((tm, D_MODEL), lambda i: (i, 0)),
                  pl.BlockSpec((1, tm * ACC_ROWS, LANES),
                               lambda i: ((first + i) // per_block, (first + i) % per_block, 0)),
                  _const_spec((1, D_MODEL))],
        out_specs=pl.BlockSpec((tm, D_MODEL), lambda i: (i, 0)),
        out_shape=jax.ShapeDtypeStruct((rows, D_MODEL), F32),
        compiler_params=_cparams("parallel"),
        name="final_norm",
    )(xs, routed, final_norm)


def _rope_tables(pos):
    half = RET_HEAD_DIM // 2
    inv_freq = ROPE_BASE ** (-np.arange(half, dtype=np.float64) / half)
    ang = np.asarray(pos, np.float64)[:, None] * inv_freq[None, :]
    cos, sin = np.cos(ang), np.sin(ang)
    cos_t = np.concatenate([cos, cos, cos, cos], axis=1)
    sin_t = np.concatenate([-sin, sin, -sin, sin], axis=1)
    return jnp.asarray(cos_t, F32), jnp.asarray(sin_t, F32)


def kernel(x_prompt, x_sample, state_s5_re, state_s5_im, state_ret, meta_tokens, norm1, w_in,
           s5_lam_re, s5_lam_im, s5_log_dt, s5_b_re, s5_b_im, s5_c_re, s5_c_im, s5_d, s5_w_glu,
           s5_norm, ret_norm, w_out, norm2, w_router, router_bias, w_gate_e, w_up_e, w_down_e,
           w_gate_sh, w_up_sh, w_down_sh, final_norm):
    assert norm1.shape[0] == 1, "single-layer model"
    bp, seq, _ = x_prompt.shape
    ns = x_sample.shape[0]
    l = 0

    n1 = norm1[l].reshape(1, D_MODEL)
    w_in_b = w_in[l].astype(BF16)
    s5p = _s5_params(s5_lam_re[l], s5_lam_im[l], s5_log_dt[l], s5_b_re[l], s5_b_im[l],
                     s5_c_re[l], s5_c_im[l], s5_d[l], s5_w_glu[l], s5_norm[l])
    rnorm = ret_norm[l].reshape(1, D_RET)
    w_out_b = w_out[l].astype(BF16)
    wr_t = w_router[l].T
    wr_hi = wr_t.astype(BF16)
    wr_lo = (wr_t - wr_hi.astype(F32)).astype(BF16)
    mp = {
        "woa": w_out_b[:D_S5], "wob": w_out_b[D_S5:], "n2": norm2[l].reshape(1, D_MODEL),
        "wrt": jnp.stack([wr_hi, wr_lo]),
        "rb": jnp.broadcast_to(router_bias[l][:, None], (N_EXPERTS, LANES)),
        "wgs": w_gate_sh[l].astype(BF16), "wus": w_up_sh[l].astype(BF16),
        "wds": w_down_sh[l].astype(BF16),
    }
    ep = {"wg": w_gate_e[l], "wu": w_up_e[l],
          "wd": w_down_e[l]}
    fnorm = final_norm.reshape(1, D_MODEL)

    xp = x_prompt.reshape(bp * seq, D_MODEL)
    cos_p, sin_p = _rope_tables(N_META + np.arange(seq))
    tm_a = 1024
    up, qp, kp, vp, gp = _project(xp, n1, w_in_b, cos_p, sin_p, tm_a, seq // tm_a)

    meta_tm = jnp.repeat(meta_tokens, SUBLANES, axis=0)
    mchunk = LANES
    meta_chunk = jnp.concatenate([jnp.zeros((mchunk - N_META, D_MODEL), F32), meta_tokens], axis=0)
    x_small = jnp.concatenate([x_sample.reshape(ns, D_MODEL), meta_tm, meta_chunk], axis=0)
    meta_pos = np.arange(N_META)
    pos_small = np.concatenate([np.full((ns,), PAST_LEN), np.repeat(meta_pos, SUBLANES),
                                np.zeros((mchunk - N_META,), np.int64), meta_pos])
    cos_s, sin_s = _rope_tables(pos_small)
    n_small = x_small.shape[0]
    n_tm = ns + N_META * SUBLANES
    us, qs, ks, vs, gs = _project(x_small, n1, w_in_b, cos_s, sin_s, n_small, 1)

    zero8 = jnp.zeros((SUBLANES, S5_LANES), F32)
    _, m_re, m_im = _s5(us[ns:n_tm], zero8, zero8, s5p, nb=SUBLANES, tt=N_META, nblk=1,
                        bt_major=False)
    tt = 64
    ys5_p, p_re, p_im = _s5(up.reshape(bp, seq, D_S5), m_re, m_im, s5p, nb=bp, tt=tt,
                            nblk=seq // tt, bt_major=True)
    ys5_p = ys5_p.reshape(bp * seq, D_S5)
    ys5_s, s_re, s_im = _s5(us[:ns], state_s5_re[l].reshape(ns, S5_LANES),
                            state_s5_im[l].reshape(ns, S5_LANES), s5p, nb=ns, tt=1, nblk=1,
                            bt_major=False)

    zero_pair = jnp.zeros((1, N_HEAD_PAIRS, LANES, LANES), F32)
    _, m_pair = _retention(qs[n_tm:], ks[n_tm:], vs[n_tm:], gs[n_tm:],
                           zero_pair, rnorm, nseq=1, chunk=mchunk, nchunk=1)
    chunk = 256
    yret_p, p_pair = _retention(qp, kp, vp, gp, jnp.broadcast_to(m_pair, (bp,) + m_pair.shape[1:]),
                                rnorm, nseq=bp, chunk=chunk, nchunk=seq // chunk)
    yret_s, ret_s = _retention_step(qs[:ns], ks[:ns], vs[:ns], gs[:ns], state_ret[l], ret_norm[l])

    n_prompt = bp * seq
    assert n_prompt + ns <= MOE_BLOCKS * MOE_BLOCK_TOKENS
    n_moe = MOE_BLOCKS * MOE_BLOCK_TOKENS
    tm_mix = 512
    xs_p, hp, meta, eid_p, cnt_p = _mix(xp, ys5_p, yret_p, mp, tm_mix, n_moe)
    xs_s, hp_s, meta_s, eid_s, cnt_s = _mix(x_sample.reshape(ns, D_MODEL), ys5_s, yret_s, mp,
                                            ns, ns)
    hp = lax.dynamic_update_slice(hp, hp_s, (n_prompt * PACK_ROWS, 0))
    meta = lax.dynamic_update_slice(meta, meta_s, (n_prompt * META_ROWS, 0))
    blk_s = n_prompt // MOE_BLOCK_TOKENS
    assert MOE_BLOCK_TOKENS % tm_mix == 0 and (n_prompt + ns - 1) // MOE_BLOCK_TOKENS == blk_s
    counts = cnt_p[:, :, 0].reshape(MOE_BLOCKS, MOE_BLOCK_TOKENS // tm_mix, N_EXPERTS).sum(axis=1)
    counts = counts.at[blk_s].add(cnt_s[0, :, 0]).reshape(-1).astype(jnp.int32)
    tabs = _dispatch_tables(jnp.concatenate([eid_p, eid_s], axis=1), counts, n_prompt + ns)
    routed = _moe(hp, meta, tabs, ep)
    y_p = _finalize(xs_p, routed, fnorm, 512, 0)
    y_s = _finalize(xs_s, routed, fnorm, ns, n_prompt)

    shape5 = (1, bp, N_S5_GROUPS, S5_STATE)
    return (y_p.reshape(bp, seq, D_MODEL),
            y_s.reshape(ns, 1, D_MODEL),
            p_re.reshape(shape5), p_im.reshape(shape5),
            _unpair_state(p_pair)[None],
            s_re.reshape(1, ns, N_S5_GROUPS, S5_STATE), s_im.reshape(1, ns, N_S5_GROUPS, S5_STATE),
            ret_s[None])
```

```python
import functools

import jax
import jax.numpy as jnp
import numpy as np
from jax import lax
from jax.experimental import pallas as pl
from jax.experimental.pallas import tpu as pltpu

F32 = jnp.float32
BF16 = jnp.bfloat16

D_MODEL = 1024
N_META = 16
PAST_LEN = 16384
D_S5 = 512
S5_GROUP = 16
N_S5_GROUPS = 32
S5_STATE = 64
S5_LANES = N_S5_GROUPS * S5_STATE
D_RET = 512
N_RET_HEADS = 8
RET_HEAD_DIM = 64
N_HEAD_PAIRS = 4
ROPE_BASE = 10000.0
D_IN = D_S5 + 4 * D_RET
N_EXPERTS = 64
TOP_K = 8
N_EXPERT_GROUPS = 8
GROUP_SIZE = 8
TOPK_GROUPS = 4
D_EXPERT = 256
ROUTED_SCALE = 2.5
EPS = 1e-6

LANES = 128
SUBLANES = 8
VMEM_LIMIT = 62 * 1024 * 1024

PACK_ROWS = D_MODEL // LANES
ACC_ROWS = D_MODEL // LANES
MOE_BLOCKS = 3
MOE_BLOCK_TOKENS = 5632
MOE_TILE = 256
GROUP_TILES = MOE_BLOCK_TOKENS // MOE_TILE + 2
ADD_UNROLL = 8
OT_PITCH = MOE_TILE + SUBLANES
XT_PITCH = MOE_TILE + SUBLANES
META_ROWS = 2
META_SHIFT = 2
assert PACK_ROWS == META_ROWS << META_SHIFT
TOKEN_BITS = 13
ZERO_ROWS = 64
CAST_ROWS = 128


def _cparams(*sem):
    return pltpu.CompilerParams(dimension_semantics=sem, vmem_limit_bytes=VMEM_LIMIT)


def _const_spec(shape):
    nd = len(shape)
    return pl.BlockSpec(shape, lambda *_: (0,) * nd)


def _rms(x, gain):
    return x * lax.rsqrt(jnp.mean(x * x, axis=-1, keepdims=True) + EPS) * gain


def _proj_kernel(x_ref, n1_ref, w_ref, cos_ref, sin_ref, u_ref, q_ref, k_ref, v_ref, g_ref):
    h = _rms(x_ref[...], n1_ref[...]).astype(BF16)
    proj = jnp.dot(h, w_ref[...], preferred_element_type=F32)
    cos = cos_ref[...]
    sin = sin_ref[...]
    lane = lax.broadcasted_iota(jnp.int32, cos.shape, 1)
    first_half = (lane % RET_HEAD_DIM) < (RET_HEAD_DIM // 2)

    def rotary(t):
        partner = jnp.where(first_half,
                            pltpu.roll(t, LANES - RET_HEAD_DIM // 2, 1),
                            pltpu.roll(t, RET_HEAD_DIM // 2, 1))
        return t * cos + partner * sin

    u_ref[...] = proj[:, :D_S5]
    for j in range(N_HEAD_PAIRS):
        lo = D_S5 + j * LANES
        q_ref[:, j * LANES:(j + 1) * LANES] = rotary(proj[:, lo:lo + LANES]).astype(BF16)
        lo += D_RET
        k_ref[:, j * LANES:(j + 1) * LANES] = (
            rotary(proj[:, lo:lo + LANES]) * (RET_HEAD_DIM ** -0.5)).astype(BF16)
    v_ref[...] = proj[:, D_S5 + 2 * D_RET:D_S5 + 3 * D_RET].astype(BF16)
    g_ref[...] = proj[:, D_S5 + 3 * D_RET:]


def _project(x, norm1, w_in_b, cos_t, sin_t, tm, table_blocks):
    rows = x.shape[0]
    row = lambda i: (i, 0)
    tab = lambda i: (i % table_blocks, 0)
    return pl.pallas_call(
        _proj_kernel,
        grid=(rows // tm,),
        in_specs=[pl.BlockSpec((tm, D_MODEL), row),
                  _const_spec((1, D_MODEL)),
                  _const_spec((D_MODEL, D_IN)),
                  pl.BlockSpec((tm, LANES), tab),
                  pl.BlockSpec((tm, LANES), tab)],
        out_specs=[pl.BlockSpec((tm, D_S5), row),
                   pl.BlockSpec((tm, D_RET), row),
                   pl.BlockSpec((tm, D_RET), row),
                   pl.BlockSpec((tm, D_RET), row),
                   pl.BlockSpec((tm, D_RET), row)],
        out_shape=[jax.ShapeDtypeStruct((rows, D_S5), F32),
                   jax.ShapeDtypeStruct((rows, D_RET), BF16),
                   jax.ShapeDtypeStruct((rows, D_RET), BF16),
                   jax.ShapeDtypeStruct((rows, D_RET), BF16),
                   jax.ShapeDtypeStruct((rows, D_RET), F32)],
        compiler_params=_cparams("parallel"),
        name="in_proj",
    )(x, norm1, w_in_b, cos_t, sin_t)


def _s5_kernel(u_ref, x0re_ref, x0im_ref, are_ref, aim_ref, b_ref, c_ref, d_ref, wglu_ref,
               nrm_ref, y_ref, sre_ref, sim_ref, st, sre, sim, utm, ytm, *, nb, tt, bt_major,
               state_minor):
    i = pl.program_id(0)
    rows = nb * tt

    @pl.when(i == 0)
    def _():
        sre[...] = x0re_ref[...].T if state_minor else x0re_ref[...]
        sim[...] = x0im_ref[...].T if state_minor else x0im_ref[...]

    if bt_major:
        for t in range(tt):
            utm[t * nb:(t + 1) * nb, :] = u_ref[:, t, :]
    else:
        utm[...] = u_ref[...]

    ub = utm[...].astype(BF16)
    half = S5_LANES // 2
    kh = D_S5 // 2
    for part in range(2):
        for hf in range(2):
            st[:, part * S5_LANES + hf * half:part * S5_LANES + (hf + 1) * half] = jnp.dot(
                ub[:, hf * kh:(hf + 1) * kh], b_ref[part, hf], preferred_element_type=F32)

    lw = 512 if nb == SUBLANES else LANES
    for lg in range(S5_LANES // lw):
        re = slice(lg * lw, (lg + 1) * lw)
        im = slice(S5_LANES + lg * lw, S5_LANES + (lg + 1) * lw)
        a_re = are_ref[:, re]
        a_im = aim_ref[:, re]

        def step(t, carry):
            s_re, s_im = carry
            r0 = pl.multiple_of(t * nb, nb)
            n_re = a_re * s_re - a_im * s_im + st[pl.ds(r0, nb), re]
            n_im = a_re * s_im + a_im * s_re + st[pl.ds(r0, nb), im]
            st[pl.ds(r0, nb), re] = n_re
            st[pl.ds(r0, nb), im] = n_im
            return n_re, n_im

        if nb == SUBLANES:
            f_re, f_im = sre[:, re], sim[:, re]
            for t in range(tt):
                r = slice(t * nb, (t + 1) * nb)
                n_re = a_re * f_re - a_im * f_im + st[r, re]
                n_im = a_re * f_im + a_im * f_re + st[r, im]
                st[r, re] = n_re
                st[r, im] = n_im
                f_re, f_im = n_re, n_im
        else:
            f_re, f_im = lax.fori_loop(0, tt, step, (sre[:, re], sim[:, re]))
        sre[:, re] = f_re
        sim[:, re] = f_im

    sre_ref[...] = sre[...].T if state_minor else sre[...]
    sim_ref[...] = sim[...].T if state_minor else sim[...]

    ys = []
    for hf in range(2):
        xr = st[:, hf * half:(hf + 1) * half].astype(BF16)
        xi = st[:, S5_LANES + hf * half:S5_LANES + (hf + 1) * half].astype(BF16)
        ys.append(jnp.dot(xr, c_ref[0, hf], preferred_element_type=F32)
                  - jnp.dot(xi, c_ref[1, hf], preferred_element_type=F32))
    y = jnp.concatenate(ys, axis=1) + d_ref[...] * utm[...]
    y = jax.nn.gelu(y)
    y = y * jax.nn.sigmoid(jnp.dot(y.astype(BF16), wglu_ref[...], preferred_element_type=F32))
    y = _rms(y, nrm_ref[...]).astype(BF16)
    if bt_major:
        yf = y.astype(F32)
        for j in range(D_S5 // LANES):
            ytm[j] = yf[:, j * LANES:(j + 1) * LANES]
        for b in range(nb):
            for j in range(D_S5 // LANES):
                y_ref[b, :, j * LANES:(j + 1) * LANES] = (
                    ytm[j, pl.ds(b, tt, stride=nb), :].astype(BF16))
    else:
        y_ref[...] = y


def _s5(u, x0re, x0im, s5p, *, nb, tt, nblk, bt_major, state_minor=False):
    rows = nb * tt
    io_state_shape = (S5_LANES, nb) if state_minor else (nb, S5_LANES)
    if bt_major:
        u_spec = pl.BlockSpec((nb, tt, D_S5), lambda i: (0, i, 0))
        y_shape = jax.ShapeDtypeStruct((nb, tt * nblk, D_S5), BF16)
    else:
        u_spec = pl.BlockSpec((rows, D_S5), lambda i: (i, 0))
        y_shape = jax.ShapeDtypeStruct((rows * nblk, D_S5), BF16)
    state_spec = _const_spec((nb, S5_LANES))
    are = jnp.broadcast_to(s5p["are"], (nb, S5_LANES))
    aim = jnp.broadcast_to(s5p["aim"], (nb, S5_LANES))
    io_state_spec = _const_spec(io_state_shape)
    return pl.pallas_call(
        functools.partial(_s5_kernel, nb=nb, tt=tt, bt_major=bt_major, state_minor=state_minor),
        grid=(nblk,),
        in_specs=[u_spec, io_state_spec, io_state_spec, state_spec, state_spec,
                  _const_spec((2, 2, D_S5 // 2, S5_LANES // 2)),
                  _const_spec((2, 2, S5_LANES // 2, D_S5 // 2)),
                  _const_spec((1, D_S5)),
                  _const_spec((D_S5, D_S5)),
                  _const_spec((1, D_S5))],
        out_specs=[u_spec, io_state_spec, io_state_spec],
        out_shape=[y_shape,
                   jax.ShapeDtypeStruct(io_state_shape, F32),
                   jax.ShapeDtypeStruct(io_state_shape, F32)],
        scratch_shapes=[pltpu.VMEM((rows, 2 * S5_LANES), F32),
                        pltpu.VMEM((nb, S5_LANES), F32),
                        pltpu.VMEM((nb, S5_LANES), F32),
                        pltpu.VMEM((rows, D_S5), F32),
                        pltpu.VMEM((D_S5 // LANES, rows, LANES), F32)],
        compiler_params=_cparams("arbitrary"),
        name="s5_" + ("bt" if bt_major else "tm") + str(nb),
    )(u, x0re, x0im, are, aim, s5p["b"], s5p["c"], s5p["d"], s5p["wglu"], s5p["nrm"])


def _s5_params(lam_re, lam_im, log_dt, b_re, b_im, c_re, c_im, d_skip, w_glu, nrm):
    dt = jnp.exp(log_dt)[:, None]
    mag = jnp.exp(lam_re * dt)
    abar_re, abar_im = mag * jnp.cos(lam_im * dt), mag * jnp.sin(lam_im * dt)
    num_re, num_im = abar_re - 1.0, abar_im
    den = lam_re * lam_re + lam_im * lam_im
    f_re = (num_re * lam_re + num_im * lam_im) / den
    f_im = (num_im * lam_re - num_re * lam_im) / den
    bbar_re = f_re[..., None] * b_re - f_im[..., None] * b_im
    bbar_im = f_re[..., None] * b_im + f_im[..., None] * b_re
    hg = N_S5_GROUPS // 2
    eye = jnp.eye(hg, dtype=F32)

    def bdiag(bb):
        bb = bb.reshape(2, hg, S5_STATE, S5_GROUP)
        return jnp.einsum("zgph,gk->zghkp", bb, eye).reshape(2, hg * S5_GROUP, hg * S5_STATE)

    def cdiag(cc):
        cc = cc.reshape(2, hg, S5_GROUP, S5_STATE)
        return jnp.einsum("zgnp,gk->zgpkn", cc, eye).reshape(2, hg * S5_STATE, hg * S5_GROUP)

    return {
        "are": abar_re.reshape(1, S5_LANES), "aim": abar_im.reshape(1, S5_LANES),
        "b": jnp.stack([bdiag(bbar_re), bdiag(bbar_im)]).astype(BF16),
        "c": jnp.stack([cdiag(c_re), cdiag(c_im)]).astype(BF16),
        "d": d_skip.reshape(1, D_S5), "wglu": w_glu.astype(BF16), "nrm": nrm.reshape(1, D_S5),
    }


def _head_norm_gate(o, g, gain, lo):
    inv = 1.0 / RET_HEAD_DIM

    def seg_mean(t):
        s_lo = jnp.sum(jnp.where(lo, t, 0.0), axis=1, keepdims=True)
        s_hi = jnp.sum(jnp.where(lo, 0.0, t), axis=1, keepdims=True)
        return jnp.where(lo, s_lo, s_hi) * inv

    dlt = o - seg_mean(o)
    var = seg_mean(dlt * dlt)
    return jax.nn.silu(g) * (dlt * lax.rsqrt(var + EPS) * gain)


def _ret_kernel(q_ref, k_ref, v_ref, g_ref, s0_ref, dm_ref, qd_ref, kd_ref, gc_ref, bm_ref,
                nrm_ref, y_ref, so_ref, s_acc):
    c = pl.program_id(1)

    @pl.when(c == 0)
    def _():
        s_acc[...] = s0_ref[0]

    rows = q_ref.shape[0]
    lane = lax.broadcasted_iota(jnp.int32, (rows, LANES), 1)
    lo = lane < RET_HEAD_DIM
    for j in range(N_HEAD_PAIRS):
        sl = slice(j * LANES, (j + 1) * LANES)
        q2 = q_ref[:, sl].astype(F32)
        k2 = k_ref[:, sl]
        v2 = v_ref[:, sl]
        s_pair = s_acc[j]
        cross = jnp.dot((q2 * qd_ref[:, sl]).astype(BF16), s_pair.astype(BF16),
                        preferred_element_type=F32)
        k_dec = (k2.astype(F32) * kd_ref[:, sl]).astype(BF16)
        upd = lax.dot_general(k_dec, v2, (((0,), (0,)), ((), ())), preferred_element_type=F32)
        s_acc[j] = gc_ref[j] * s_pair + bm_ref[...] * upd
        inner = []
        for hh, qh in enumerate((jnp.where(lo, q2, 0.0), jnp.where(lo, 0.0, q2))):
            sc = lax.dot_general(qh.astype(BF16), k2, (((1,), (1,)), ((), ())),
                                 preferred_element_type=F32) * dm_ref[2 * j + hh]
            inner.append(jnp.dot(sc.astype(BF16), v2, preferred_element_type=F32))
        o = jnp.where(lo, inner[0], inner[1]) + cross
        y_ref[:, sl] = _head_norm_gate(o, g_ref[:, sl], nrm_ref[:, sl], lo).astype(BF16)
    so_ref[0] = s_acc[...]


def _ret_log_decay():
    return np.log1p(-np.exp2(-5.0 - np.arange(N_RET_HEADS, dtype=np.float64)))


def _ret_tables(chunk):
    log_g = _ret_log_decay()
    n = np.arange(chunk, dtype=np.float64)
    diff = n[:, None] - n[None, :]
    dm = np.where(diff[None] >= 0.0,
                  np.exp(log_g[:, None, None] * np.maximum(diff, 0.0)[None]), 0.0)
    per_lane = lambda t: np.repeat(t.T, RET_HEAD_DIM, axis=1)
    qd = per_lane(np.exp(log_g[:, None] * (n + 1.0)[None]))
    kd = per_lane(np.exp(log_g[:, None] * (chunk - 1.0 - n)[None]))
    gch = np.exp(log_g * chunk)
    blk = np.kron(np.eye(2), np.ones((RET_HEAD_DIM, RET_HEAD_DIM)))
    gc = np.repeat(gch.reshape(N_HEAD_PAIRS, 2), RET_HEAD_DIM, axis=1)[:, :, None] * blk[None]
    return tuple(jnp.asarray(t, F32) for t in (dm, qd, kd, gc, blk))


def _pair_state(s):
    b = s.shape[0]
    s = s.reshape(b, N_HEAD_PAIRS, 2, RET_HEAD_DIM, RET_HEAD_DIM)
    z = jnp.zeros_like(s[:, :, 0])
    top = jnp.concatenate([s[:, :, 0], z], axis=-1)
    bot = jnp.concatenate([z, s[:, :, 1]], axis=-1)
    return jnp.concatenate([top, bot], axis=-2)


def _unpair_state(s2):
    d = RET_HEAD_DIM
    return jnp.stack([s2[:, :, :d, :d], s2[:, :, d:, d:]], axis=2).reshape(
        s2.shape[0], N_RET_HEADS, d, d)


def _retention(q, k, v, g, s0_pair, ret_norm, *, nseq, chunk, nchunk):
    dm, qd, kd, gc, blk = _ret_tables(chunk)
    row = lambda b, c: (b * nchunk + c, 0)
    st_spec = pl.BlockSpec((1, N_HEAD_PAIRS, LANES, LANES), lambda b, c: (b, 0, 0, 0))
    blk_spec = pl.BlockSpec((chunk, D_RET), row)
    return pl.pallas_call(
        _ret_kernel,
        grid=(nseq, nchunk),
        in_specs=[blk_spec, blk_spec, blk_spec, blk_spec, st_spec,
                  _const_spec((N_RET_HEADS, chunk, chunk)),
                  _const_spec((chunk, D_RET)), _const_spec((chunk, D_RET)),
                  _const_spec((N_HEAD_PAIRS, LANES, LANES)), _const_spec((LANES, LANES)),
                  _const_spec((1, D_RET))],
        out_specs=[blk_spec, st_spec],
        out_shape=[jax.ShapeDtypeStruct((nseq * nchunk * chunk, D_RET), BF16),
                   jax.ShapeDtypeStruct((nseq, N_HEAD_PAIRS, LANES, LANES), F32)],
        scratch_shapes=[pltpu.VMEM((N_HEAD_PAIRS, LANES, LANES), F32)],
        compiler_params=_cparams("arbitrary", "arbitrary"),
        name="retention_c%d" % chunk,
    )(q, k, v, g, s0_pair, dm, qd, kd, gc, blk, ret_norm)


def _ret_step_kernel(q_ref, k_ref, v_ref, g_ref, s_ref, gam_ref, nrm_ref, y_ref, so_ref):
    q = q_ref[...]
    k = k_ref[...]
    v = v_ref[...]
    gam = gam_ref[0]
    score = jnp.sum(q * k, axis=0, keepdims=True)
    q_dec = (q * gam).astype(BF16).astype(F32)
    cross = jnp.zeros(v.shape, F32)
    for d in range(RET_HEAD_DIM):
        s_d = s_ref[0, d]
        cross = cross + q_dec[d:d + 1, :] * s_d
        so_ref[0, d] = gam * s_d + k[d:d + 1, :] * v
    o = score.astype(BF16).astype(F32) * v + cross
    dlt = o - jnp.mean(o, axis=0, keepdims=True)
    var = jnp.mean(dlt * dlt, axis=0, keepdims=True)
    y_ref[...] = jax.nn.silu(g_ref[...]) * (dlt * lax.rsqrt(var + EPS) * nrm_ref[...])


def _retention_step(q, k, v, g, state, ret_norm):
    n = q.shape[0]
    gam = np.exp(_ret_log_decay())
    gam_tab = jnp.asarray(np.broadcast_to(gam[:, None, None], (N_RET_HEADS, 1, n)), F32)
    seq_minor = lambda t: t.astype(F32).T
    head = lambda h: (h, 0)
    vec_spec = pl.BlockSpec((RET_HEAD_DIM, n), head)
    st_spec = pl.BlockSpec((1, RET_HEAD_DIM, RET_HEAD_DIM, n), lambda h: (h, 0, 0, 0))
    y_t, s_new = pl.pallas_call(
        _ret_step_kernel,
        grid=(N_RET_HEADS,),
        in_specs=[vec_spec, vec_spec, vec_spec, vec_spec, st_spec,
                  pl.BlockSpec((1, 1, n), lambda h: (h, 0, 0)), vec_spec],
        out_specs=[vec_spec, st_spec],
        out_shape=[jax.ShapeDtypeStruct((D_RET, n), F32),
                   jax.ShapeDtypeStruct((N_RET_HEADS, RET_HEAD_DIM, RET_HEAD_DIM, n), F32)],
        compiler_params=_cparams("parallel"),
        name="retention_step",
    )(seq_minor(q), seq_minor(k), seq_minor(v), seq_minor(g),
      jnp.transpose(state, (1, 2, 3, 0)), gam_tab,
      jnp.broadcast_to(ret_norm[:, None], (D_RET, n)))
    return y_t.T.astype(BF16), jnp.transpose(s_new, (3, 0, 1, 2))


def _first_max(vals, idxs, sentinel):
    m = jnp.max(vals, axis=0, keepdims=True)
    first = jnp.min(jnp.where(vals == m, idxs, sentinel), axis=0, keepdims=True)
    return m, idxs == first


def _route(scores, sel):
    t = sel.shape[1]
    neg = -jnp.inf
    member = lax.broadcasted_iota(jnp.int32, (GROUP_SIZE, t), 0).astype(F32)
    groups = [sel[g * GROUP_SIZE:(g + 1) * GROUP_SIZE, :] for g in range(N_EXPERT_GROUPS)]
    gscore = []
    for grp in groups:
        m1, pick = _first_max(grp, member, float(GROUP_SIZE))
        m2 = jnp.max(jnp.where(pick, neg, grp), axis=0, keepdims=True)
        gscore.append(m1 + m2)
    gs = jnp.concatenate(gscore, axis=0)
    gkeep = jnp.zeros(gs.shape, F32)
    for _ in range(TOPK_GROUPS):
        _, pick = _first_max(gs, member, float(N_EXPERT_GROUPS))
        gkeep = jnp.where(pick, 1.0, gkeep)
        gs = jnp.where(pick, neg, gs)
    cand = jnp.concatenate(
        [jnp.where(gkeep[g:g + 1, :] > 0.5, groups[g], neg) for g in range(N_EXPERT_GROUPS)],
        axis=0)
    expert = lax.broadcasted_iota(jnp.int32, (N_EXPERTS, t), 0).astype(F32)
    ids, ws = [], []
    chosen = jnp.zeros(cand.shape, F32)
    for _ in range(TOP_K):
        _, pick = _first_max(cand, expert, float(N_EXPERTS))
        ids.append(jnp.sum(jnp.where(pick, expert, 0.0), axis=0, keepdims=True))
        ws.append(jnp.sum(jnp.where(pick, scores, 0.0), axis=0, keepdims=True))
        chosen = jnp.where(pick, 1.0, chosen)
        cand = jnp.where(pick, neg, cand)
    w = jnp.concatenate(ws, axis=0)
    gates = w / jnp.sum(w, axis=0, keepdims=True) * ROUTED_SCALE
    return jnp.concatenate(ids, axis=0), gates, jnp.sum(chosen, axis=1, keepdims=True)


def _mix_kernel(x_ref, ys_ref, yr_ref, woa_ref, wob_ref, n2_ref, wrt_ref, rb_ref,
                wgs_ref, wus_ref, wds_ref, xs_ref, hp_ref, meta_ref, eid_ref, cnt_ref, *,
                n_blocks):
    @pl.when(pl.program_id(0) >= n_blocks)
    def _():
        hp_ref[...] = jnp.zeros_like(hp_ref)
        meta_ref[...] = jnp.zeros_like(meta_ref)
        cnt_ref[...] = jnp.zeros_like(cnt_ref)

    @pl.when(pl.program_id(0) < n_blocks)
    def _():
        _mix_body(x_ref, ys_ref, yr_ref, woa_ref, wob_ref, n2_ref, wrt_ref, rb_ref,
                  wgs_ref, wus_ref, wds_ref, xs_ref, hp_ref, meta_ref, eid_ref, cnt_ref)


def _mix_body(x_ref, ys_ref, yr_ref, woa_ref, wob_ref, n2_ref, wrt_ref, rb_ref,
              wgs_ref, wus_ref, wds_ref, xs_ref, hp_ref, meta_ref, eid_ref, cnt_ref):
    tm = x_ref.shape[0]
    x2 = (x_ref[...]
          + jnp.dot(ys_ref[...], woa_ref[...], preferred_element_type=F32)
          + jnp.dot(yr_ref[...], wob_ref[...], preferred_element_type=F32))
    h2f = _rms(x2, n2_ref[...])
    h2 = h2f.astype(BF16)
    h2r = h2.astype(F32)
    for c in range(PACK_ROWS):
        hp_ref[pl.ds(c, tm, stride=PACK_ROWS), :] = h2r[:, c * LANES:(c + 1) * LANES]
    h2_lo = (h2f - h2r).astype(BF16)
    nt_dims = (((1,), (1,)), ((), ()))
    logits = (lax.dot_general(wrt_ref[0], h2, nt_dims, preferred_element_type=F32)
              + lax.dot_general(wrt_ref[1], h2, nt_dims, preferred_element_type=F32)
              + lax.dot_general(wrt_ref[0], h2_lo, nt_dims, preferred_element_type=F32))
    scores = jax.nn.sigmoid(logits)
    ids, gates, counts = _route(scores, scores + rb_ref[:, :1])
    eid_ref[...] = ids.astype(jnp.int32)
    cnt_ref[0] = jnp.broadcast_to(counts, (N_EXPERTS, LANES))
    rec = jnp.concatenate([gates, ids, jnp.zeros((LANES - 2 * TOP_K, tm), F32)], axis=0).T
    meta_ref[pl.ds(0, tm, stride=META_ROWS), :] = rec
    for c in range(1, META_ROWS):
        meta_ref[pl.ds(c, tm, stride=META_ROWS), :] = jnp.zeros((tm, LANES), F32)
    a = jnp.dot(h2, wgs_ref[...], preferred_element_type=F32)
    b = jnp.dot(h2, wus_ref[...], preferred_element_type=F32)
    xs_ref[...] = x2 + jnp.dot((jax.nn.silu(a) * b).astype(BF16), wds_ref[...],
                               preferred_element_type=F32)


def _mix(x, ys5, yret, mp, tm, out_tokens):
    rows = x.shape[0]
    n_blocks = rows // tm
    assert out_tokens % tm == 0
    packed = lambda i: (i, 0)
    row = lambda i: (jnp.minimum(i, n_blocks - 1), 0)
    col = lambda i: (0, jnp.minimum(i, n_blocks - 1))
    return pl.pallas_call(
        functools.partial(_mix_kernel, n_blocks=n_blocks),
        grid=(out_tokens // tm,),
        in_specs=[pl.BlockSpec((tm, D_MODEL), row),
                  pl.BlockSpec((tm, D_S5), row), pl.BlockSpec((tm, D_RET), row),
                  _const_spec((D_S5, D_MODEL)), _const_spec((D_RET, D_MODEL)),
                  _const_spec((1, D_MODEL)),
                  _const_spec((2, N_EXPERTS, D_MODEL)), _const_spec((N_EXPERTS, LANES)),
                  _const_spec((D_MODEL, D_EXPERT)), _const_spec((D_MODEL, D_EXPERT)),
                  _const_spec((D_EXPERT, D_MODEL))],
        out_specs=[pl.BlockSpec((tm, D_MODEL), row),
                   pl.BlockSpec((tm * PACK_ROWS, LANES), packed),
                   pl.BlockSpec((tm * META_ROWS, LANES), packed),
                   pl.BlockSpec((TOP_K, tm), col),
                   pl.BlockSpec((1, N_EXPERTS, LANES), lambda i: (i, 0, 0))],
        out_shape=[jax.ShapeDtypeStruct((rows, D_MODEL), F32),
                   jax.ShapeDtypeStruct((out_tokens * PACK_ROWS, LANES), F32),
                   jax.ShapeDtypeStruct((out_tokens * META_ROWS, LANES), F32),
                   jax.ShapeDtypeStruct((TOP_K, rows), jnp.int32),
                   jax.ShapeDtypeStruct((out_tokens // tm, N_EXPERTS, LANES), F32)],
        compiler_params=_cparams("arbitrary"),
        name="out_proj_router",
    )(x, ys5, yret, mp["woa"], mp["wob"], mp["n2"], mp["wrt"], mp["rb"],
      mp["wgs"], mp["wus"], mp["wds"])


def _moe_kernel(t0_ref, nt_ref, src_ref, dst_ref, hp_ref, meta_ref, wg32_ref, wu32_ref, wd32_ref,
                acc_ref, xt_ref, xm_ref, ot_ref, xb_ref, wg_ref, wu_ref, wd_ref):
    s = pl.program_id(0)
    tm = MOE_TILE
    half = D_MODEL // 2
    all_rows = range(tm)
    nt = nt_ref[s]
    expert = (s % N_EXPERTS).astype(F32)

    @pl.when(nt > 0)
    def _():
        for src32, dst16 in ((wg32_ref, wg_ref), (wu32_ref, wu_ref), (wd32_ref, wd_ref)):
            rows = src32.shape[1]
            for i in range(0, rows, CAST_ROWS):
                dst16[0, i:i + CAST_ROWS, :] = src32[0, i:i + CAST_ROWS, :].astype(BF16)

    def fetch_rows(j, rows):
        for m in rows:
            r = pl.multiple_of(src_ref[j, 0, m], PACK_ROWS)
            xt_ref[pl.ds(m, PACK_ROWS, stride=XT_PITCH), :] = hp_ref[pl.ds(r, PACK_ROWS), :]
            xm_ref[pl.ds(m, META_ROWS, stride=XT_PITCH), :] = meta_ref[
                pl.ds(pl.multiple_of(r >> META_SHIFT, META_ROWS), META_ROWS), :]

    def add_rows(j, rows):
        for m0 in range(rows.start, rows.stop, ADD_UNROLL):
            new = []
            for m in range(m0, m0 + ADD_UNROLL):
                base = pl.multiple_of(dst_ref[j, 0, m], ACC_ROWS)
                row = ot_ref[pl.ds(m, ACC_ROWS, stride=OT_PITCH), :]
                new.append((base, acc_ref[0, pl.ds(base, ACC_ROWS), :] + row))
            for base, v in new:
                acc_ref[0, pl.ds(base, ACC_ROWS), :] = v

    @pl.when(s % N_EXPERTS == 0)
    def _():
        zero = jnp.zeros((ZERO_ROWS, LANES), F32)

        def clear(i, carry):
            acc_ref[0, pl.ds(pl.multiple_of(i * ZERO_ROWS, ZERO_ROWS), ZERO_ROWS), :] = zero
            return carry

        lax.fori_loop(0, acc_ref.shape[1] // ZERO_ROWS, clear, 0)
        ot_ref[...] = jnp.zeros_like(ot_ref)
        fetch_rows(1, all_rows)

    def stage(j, carry):
        n_slice = 4
        per = tm // n_slice
        rows = [range(i * per, (i + 1) * per) for i in range(n_slice)]
        for c in range(PACK_ROWS):
            xb_ref[:, c * LANES:(c + 1) * LANES] = (
                xt_ref[c * XT_PITCH:c * XT_PITCH + tm, :].astype(BF16))
        rec = xm_ref[0:tm, :]
        ids = pltpu.roll(rec, LANES - TOP_K, 1)
        lane = lax.broadcasted_iota(jnp.int32, rec.shape, 1)
        gate = jnp.sum(jnp.where((lane < TOP_K) & (ids == expert), rec, 0.0),
                       axis=1, keepdims=True)
        add_rows(j - 1, rows[0])
        a = jnp.dot(xb_ref[:, :half], wg_ref[0, :half], preferred_element_type=F32)
        add_rows(j - 1, rows[1])
        a = a + jnp.dot(xb_ref[:, half:], wg_ref[0, half:], preferred_element_type=F32)
        add_rows(j - 1, rows[2])
        b = jnp.dot(xb_ref[:, :half], wu_ref[0, :half], preferred_element_type=F32)
        add_rows(j - 1, rows[3])
        b = b + jnp.dot(xb_ref[:, half:], wu_ref[0, half:], preferred_element_type=F32)
        act = (jax.nn.silu(a) * b * gate).astype(BF16)
        n_col = D_MODEL // n_slice
        for piece in range(n_slice):
            fetch_rows(j + 1, rows[piece])
            out = jnp.dot(act, wd_ref[0, :, piece * n_col:(piece + 1) * n_col],
                          preferred_element_type=F32)
            for cc in range(n_col // LANES):
                c = piece * (n_col // LANES) + cc
                ot_ref[c * OT_PITCH:c * OT_PITCH + tm, :] = out[:, cc * LANES:(cc + 1) * LANES]
        return carry

    lax.fori_loop(1, nt + 1, stage, 0)

    @pl.when(s % N_EXPERTS == N_EXPERTS - 1)
    def _():
        add_rows(nt, all_rows)


def _dispatch_tables(eid_t, counts, n_tok):
    tm, ts = MOE_TILE, MOE_BLOCK_TOKENS
    ng = MOE_BLOCKS * N_EXPERTS
    assert ts < (1 << TOKEN_BITS)
    tok = jnp.arange(n_tok, dtype=jnp.int32)
    grp = (tok // ts)[None, :] * N_EXPERTS + eid_t
    gids = jnp.arange(ng, dtype=jnp.int32)
    npad = (-counts) % tm
    fill = jnp.arange(tm, dtype=jnp.int32)[None, :] < npad[:, None]
    keys = jnp.concatenate([
        ((grp << TOKEN_BITS) | (tok % ts)[None, :]).reshape(-1),
        ((jnp.where(fill, gids[:, None], ng) << TOKEN_BITS) | ts).reshape(-1)])
    toks = lax.sort(keys, dimension=0, is_stable=False) & ((1 << TOKEN_BITS) - 1)
    n_tiles = (n_tok * TOP_K + ng * tm) // tm

    ntile = (counts + npad) // tm
    tile_start = jnp.cumsum(ntile) - ntile
    src = jnp.where(toks < ts, toks, 0) * PACK_ROWS
    dst = toks * ACC_ROWS
    head = jnp.zeros((tm,), jnp.int32)
    tail = jnp.zeros((GROUP_TILES * tm,), jnp.int32)
    rows = n_tiles + 1 + GROUP_TILES
    return {"t0": tile_start.astype(jnp.int32), "nt": ntile.astype(jnp.int32),
            "src": jnp.concatenate([head, src, tail]).reshape(rows, 1, tm),
            "dst": jnp.concatenate([head, dst, tail]).reshape(rows, 1, tm)}


def _moe(hp, meta, tabs, ep):
    tm, ts = MOE_TILE, MOE_BLOCK_TOKENS
    assert GROUP_TILES >= -(-ts // tm) + 2
    window = pl.BlockSpec((pl.Element(GROUP_TILES), pl.Element(1), pl.Element(tm)),
                          lambda s, t0, nt: (t0[s], 0, 0), memory_space=pltpu.SMEM)
    wspec = lambda shape: pl.BlockSpec((1,) + shape, lambda s, t0, nt: (s % N_EXPERTS, 0, 0))
    blk = lambda s, t0, nt: (s // N_EXPERTS, 0)
    acc_rows = (ts + SUBLANES) * ACC_ROWS
    assert acc_rows % ZERO_ROWS == 0
    stage_bufs = [pltpu.VMEM((PACK_ROWS * XT_PITCH, LANES), F32),
                  pltpu.VMEM((META_ROWS * XT_PITCH, LANES), F32),
                  pltpu.VMEM((ACC_ROWS * OT_PITCH, LANES), F32),
                  pltpu.VMEM((tm, D_MODEL), BF16),
                  pltpu.VMEM((1, D_MODEL, D_EXPERT), BF16),
                  pltpu.VMEM((1, D_MODEL, D_EXPERT), BF16),
                  pltpu.VMEM((1, D_EXPERT, D_MODEL), BF16)]
    grid_spec = pltpu.PrefetchScalarGridSpec(
        num_scalar_prefetch=2,
        grid=(MOE_BLOCKS * N_EXPERTS,),
        in_specs=[window, window,
                  pl.BlockSpec((ts * PACK_ROWS, LANES), blk, pipeline_mode=pl.Buffered(1)),
                  pl.BlockSpec((ts * META_ROWS, LANES), blk, pipeline_mode=pl.Buffered(1)),
                  wspec((D_MODEL, D_EXPERT)), wspec((D_MODEL, D_EXPERT)),
                  wspec((D_EXPERT, D_MODEL))],
        out_specs=pl.BlockSpec((1, acc_rows, LANES), lambda s, t0, nt: (s // N_EXPERTS, 0, 0),
                               pipeline_mode=pl.Buffered(1)),
        scratch_shapes=stage_bufs,
    )
    return pl.pallas_call(
        _moe_kernel,
        grid_spec=grid_spec,
        out_shape=jax.ShapeDtypeStruct((MOE_BLOCKS, acc_rows, LANES), F32),
        compiler_params=_cparams("arbitrary"),
        name="moe_experts",
    )(tabs["t0"], tabs["nt"], tabs["src"], tabs["dst"], hp, meta, ep["wg"], ep["wu"], ep["wd"])


def _final_kernel(xs_ref, r_ref, fn_ref, y_ref):
    tm = xs_ref.shape[0]
    routed = jnp.concatenate(
        [r_ref[0, pl.ds(c, tm, stride=ACC_ROWS), :] for c in range(ACC_ROWS)], axis=1)
    y_ref[...] = _rms(xs_ref[...] + routed, fn_ref[...])


def _finalize(xs, routed, final_norm, tm, tok0):
    rows = xs.shape[0]
    per_block = MOE_BLOCK_TOKENS // tm
    first = tok0 // tm
    assert tok0 % tm == 0 and MOE_BLOCK_TOKENS % tm == 0
    return pl.pallas_call(
        _final_kernel,
        grid=(rows // tm,),
        in_specs=[pl.BlockSpec((tm, D_MODEL), lambda i: (i, 0)),
                  pl.BlockSpec((1, tm * ACC_ROWS, LANES),
                               lambda i: ((first + i) // per_block, (first + i) % per_block, 0)),
                  _const_spec((1, D_MODEL))],
        out_specs=pl.BlockSpec((tm, D_MODEL), lambda i: (i, 0)),
        out_shape=jax.ShapeDtypeStruct((rows, D_MODEL), F32),
        compiler_params=_cparams("parallel"),
        name="final_norm",
    )(xs, routed, final_norm)


def _rope_tables(pos):
    half = RET_HEAD_DIM // 2
    inv_freq = ROPE_BASE ** (-np.arange(half, dtype=np.float64) / half)
    ang = np.asarray(pos, np.float64)[:, None] * inv_freq[None, :]
    cos, sin = np.cos(ang), np.sin(ang)
    cos_t = np.concatenate([cos, cos, cos, cos], axis=1)
    sin_t = np.concatenate([-sin, sin, -sin, sin], axis=1)
    return jnp.asarray(cos_t, F32), jnp.asarray(sin_t, F32)


def kernel(x_prompt, x_sample, state_s5_re, state_s5_im, state_ret, meta_tokens, norm1, w_in,
           s5_lam_re, s5_lam_im, s5_log_dt, s5_b_re, s5_b_im, s5_c_re, s5_c_im, s5_d, s5_w_glu,
           s5_norm, ret_norm, w_out, norm2, w_router, router_bias, w_gate_e, w_up_e, w_down_e,
           w_gate_sh, w_up_sh, w_down_sh, final_norm):
    assert norm1.shape[0] == 1, "single-layer model"
    bp, seq, _ = x_prompt.shape
    ns = x_sample.shape[0]
    l = 0

    n1 = norm1[l].reshape(1, D_MODEL)
    w_in_b = w_in[l].astype(BF16)
    s5p = _s5_params(s5_lam_re[l], s5_lam_im[l], s5_log_dt[l], s5_b_re[l], s5_b_im[l],
                     s5_c_re[l], s5_c_im[l], s5_d[l], s5_w_glu[l], s5_norm[l])
    rnorm = ret_norm[l].reshape(1, D_RET)
    w_out_b = w_out[l].astype(BF16)
    wr_t = w_router[l].T
    wr_hi = wr_t.astype(BF16)
    wr_lo = (wr_t - wr_hi.astype(F32)).astype(BF16)
    mp = {
        "woa": w_out_b[:D_S5], "wob": w_out_b[D_S5:], "n2": norm2[l].reshape(1, D_MODEL),
        "wrt": jnp.stack([wr_hi, wr_lo]),
        "rb": jnp.broadcast_to(router_bias[l][:, None], (N_EXPERTS, LANES)),
        "wgs": w_gate_sh[l].astype(BF16), "wus": w_up_sh[l].astype(BF16),
        "wds": w_down_sh[l].astype(BF16),
    }
    ep = {"wg": w_gate_e[l], "wu": w_up_e[l],
          "wd": w_down_e[l]}
    fnorm = final_norm.reshape(1, D_MODEL)

    xp = x_prompt.reshape(bp * seq, D_MODEL)
    cos_p, sin_p = _rope_tables(N_META + np.arange(seq))
    tm_a = 1024
    up, qp, kp, vp, gp = _project(xp, n1, w_in_b, cos_p, sin_p, tm_a, seq // tm_a)

    meta_tm = jnp.repeat(meta_tokens, SUBLANES, axis=0)
    mchunk = LANES
    meta_chunk = jnp.concatenate([jnp.zeros((mchunk - N_META, D_MODEL), F32), meta_tokens], axis=0)
    x_small = jnp.concatenate([x_sample.reshape(ns, D_MODEL), meta_tm, meta_chunk], axis=0)
    meta_pos = np.arange(N_META)
    pos_small = np.concatenate([np.full((ns,), PAST_LEN), np.repeat(meta_pos, SUBLANES),
                                np.zeros((mchunk - N_META,), np.int64), meta_pos])
    cos_s, sin_s = _rope_tables(pos_small)
    n_small = x_small.shape[0]
    n_tm = ns + N_META * SUBLANES
    us, qs, ks, vs, gs = _project(x_small, n1, w_in_b, cos_s, sin_s, n_small, 1)

    zero8 = jnp.zeros((SUBLANES, S5_LANES), F32)
    _, m_re, m_im = _s5(us[ns:n_tm], zero8, zero8, s5p, nb=SUBLANES, tt=N_META, nblk=1,
                        bt_major=False)
    tt = 64
    ys5_p, p_re, p_im = _s5(up.reshape(bp, seq, D_S5), m_re, m_im, s5p, nb=bp, tt=tt,
                            nblk=seq // tt, bt_major=True)
    ys5_p = ys5_p.reshape(bp * seq, D_S5)
    seq_minor = lambda s: jnp.transpose(s, (1, 2, 0)).reshape(S5_LANES, ns)
    seq_major = lambda s: jnp.transpose(s.reshape(N_S5_GROUPS, S5_STATE, ns), (2, 0, 1))[None]
    ys5_s, s_re, s_im = _s5(us[:ns], seq_minor(state_s5_re[l]), seq_minor(state_s5_im[l]), s5p,
                            nb=ns, tt=1, nblk=1, bt_major=False, state_minor=True)

    zero_pair = jnp.zeros((1, N_HEAD_PAIRS, LANES, LANES), F32)
    _, m_pair = _retention(qs[n_tm:], ks[n_tm:], vs[n_tm:], gs[n_tm:],
                           zero_pair, rnorm, nseq=1, chunk=mchunk, nchunk=1)
    chunk = 256
    yret_p, p_pair = _retention(qp, kp, vp, gp, jnp.broadcast_to(m_pair, (bp,) + m_pair.shape[1:]),
                                rnorm, nseq=bp, chunk=chunk, nchunk=seq // chunk)
    yret_s, ret_s = _retention_step(qs[:ns], ks[:ns], vs[:ns], gs[:ns], state_ret[l], ret_norm[l])

    n_prompt = bp * seq
    assert n_prompt + ns <= MOE_BLOCKS * MOE_BLOCK_TOKENS
    n_moe = MOE_BLOCKS * MOE_BLOCK_TOKENS
    tm_mix = 512
    xs_p, hp, meta, eid_p, cnt_p = _mix(xp, ys5_p, yret_p, mp, tm_mix, n_moe)
    xs_s, hp_s, meta_s, eid_s, cnt_s = _mix(x_sample.reshape(ns, D_MODEL), ys5_s, yret_s, mp,
                                            ns, ns)
    hp = lax.dynamic_update_slice(hp, hp_s, (n_prompt * PACK_ROWS, 0))
    meta = lax.dynamic_update_slice(meta, meta_s, (n_prompt * META_ROWS, 0))
    blk_s = n_prompt // MOE_BLOCK_TOKENS
    assert MOE_BLOCK_TOKENS % tm_mix == 0 and (n_prompt + ns - 1) // MOE_BLOCK_TOKENS == blk_s
    counts = cnt_p[:, :, 0].reshape(MOE_BLOCKS, MOE_BLOCK_TOKENS // tm_mix, N_EXPERTS).sum(axis=1)
    counts = counts.at[blk_s].add(cnt_s[0, :, 0]).reshape(-1).astype(jnp.int32)
    tabs = _dispatch_tables(jnp.concatenate([eid_p, eid_s], axis=1), counts, n_prompt + ns)
    routed = _moe(hp, meta, tabs, ep)
    y_p = _finalize(xs_p, routed, fnorm, 512, 0)
    y_s = _finalize(xs_s, routed, fnorm, ns, n_prompt)

    shape5 = (1, bp, N_S5_GROUPS, S5_STATE)
    return (y_p.reshape(bp, seq, D_MODEL),
            y_s.reshape(ns, 1, D_MODEL),
            p_re.reshape(shape5), p_im.reshape(shape5),
            _unpair_state(p_pair)[None],
            seq_major(s_re), seq_major(s_im),
            ret_s[None])
```

```python
import functools

import jax
import jax.numpy as jnp
import numpy as np
from jax import lax
from jax.experimental import pallas as pl
from jax.experimental.pallas import tpu as pltpu

F32 = jnp.float32
BF16 = jnp.bfloat16

D_MODEL = 1024
N_META = 16
PAST_LEN = 16384
D_S5 = 512
S5_GROUP = 16
N_S5_GROUPS = 32
S5_STATE = 64
S5_LANES = N_S5_GROUPS * S5_STATE
D_RET = 512
N_RET_HEADS = 8
RET_HEAD_DIM = 64
N_HEAD_PAIRS = 4
ROPE_BASE = 10000.0
D_IN = D_S5 + 4 * D_RET
N_EXPERTS = 64
TOP_K = 8
N_EXPERT_GROUPS = 8
GROUP_SIZE = 8
TOPK_GROUPS = 4
D_EXPERT = 256
ROUTED_SCALE = 2.5
EPS = 1e-6

LANES = 128
SUBLANES = 8
VMEM_LIMIT = 62 * 1024 * 1024

PACK_ROWS = D_MODEL // LANES
ACC_ROWS = D_MODEL // LANES
MOE_BLOCKS = 3
MOE_BLOCK_TOKENS = 5632
MOE_TILE = 256
GROUP_TILES = MOE_BLOCK_TOKENS // MOE_TILE + 2
ADD_UNROLL = 8
OT_PITCH = MOE_TILE + SUBLANES
XT_PITCH = MOE_TILE + SUBLANES
META_ROWS = 2
META_SHIFT = 2
assert PACK_ROWS == META_ROWS << META_SHIFT
TOKEN_BITS = 13
ZERO_ROWS = 64
CAST_ROWS = 128


def _cparams(*sem):
    return pltpu.CompilerParams(dimension_semantics=sem, vmem_limit_bytes=VMEM_LIMIT)


def _const_spec(shape):
    nd = len(shape)
    return pl.BlockSpec(shape, lambda *_: (0,) * nd)


def _rms(x, gain):
    return x * lax.rsqrt(jnp.mean(x * x, axis=-1, keepdims=True) + EPS) * gain


def _proj_kernel(x_ref, n1_ref, w_ref, cos_ref, sin_ref, u_ref, q_ref, k_ref, v_ref, g_ref):
    h = _rms(x_ref[...], n1_ref[...]).astype(BF16)
    proj = jnp.dot(h, w_ref[...], preferred_element_type=F32)
    cos = cos_ref[...]
    sin = sin_ref[...]
    lane = lax.broadcasted_iota(jnp.int32, cos.shape, 1)
    first_half = (lane % RET_HEAD_DIM) < (RET_HEAD_DIM // 2)

    def rotary(t):
        partner = jnp.where(first_half,
                            pltpu.roll(t, LANES - RET_HEAD_DIM // 2, 1),
                            pltpu.roll(t, RET_HEAD_DIM // 2, 1))
        return t * cos + partner * sin

    u_ref[...] = proj[:, :D_S5]
    for j in range(N_HEAD_PAIRS):
        lo = D_S5 + j * LANES
        q_ref[:, j * LANES:(j + 1) * LANES] = rotary(proj[:, lo:lo + LANES]).astype(BF16)
        lo += D_RET
        k_ref[:, j * LANES:(j + 1) * LANES] = (
            rotary(proj[:, lo:lo + LANES]) * (RET_HEAD_DIM ** -0.5)).astype(BF16)
    v_ref[...] = proj[:, D_S5 + 2 * D_RET:D_S5 + 3 * D_RET].astype(BF16)
    g_ref[...] = proj[:, D_S5 + 3 * D_RET:]


def _project(x, norm1, w_in_b, cos_t, sin_t, tm, table_blocks):
    rows = x.shape[0]
    row = lambda i: (i, 0)
    tab = lambda i: (i % table_blocks, 0)
    return pl.pallas_call(
        _proj_kernel,
        grid=(rows // tm,),
        in_specs=[pl.BlockSpec((tm, D_MODEL), row),
                  _const_spec((1, D_MODEL)),
                  _const_spec((D_MODEL, D_IN)),
                  pl.BlockSpec((tm, LANES), tab),
                  pl.BlockSpec((tm, LANES), tab)],
        out_specs=[pl.BlockSpec((tm, D_S5), row),
                   pl.BlockSpec((tm, D_RET), row),
                   pl.BlockSpec((tm, D_RET), row),
                   pl.BlockSpec((tm, D_RET), row),
                   pl.BlockSpec((tm, D_RET), row)],
        out_shape=[jax.ShapeDtypeStruct((rows, D_S5), F32),
                   jax.ShapeDtypeStruct((rows, D_RET), BF16),
                   jax.ShapeDtypeStruct((rows, D_RET), BF16),
                   jax.ShapeDtypeStruct((rows, D_RET), BF16),
                   jax.ShapeDtypeStruct((rows, D_RET), F32)],
        compiler_params=_cparams("parallel"),
        name="in_proj",
    )(x, norm1, w_in_b, cos_t, sin_t)


def _s5_kernel(u_ref, x0re_ref, x0im_ref, are_ref, aim_ref, b_ref, c_ref, d_ref, wglu_ref,
               nrm_ref, y_ref, sre_ref, sim_ref, st, sre, sim, utm, ytm, *, nb, tt, bt_major,
               state_minor):
    i = pl.program_id(0)
    rows = nb * tt

    @pl.when(i == 0)
    def _():
        sre[...] = x0re_ref[...].T if state_minor else x0re_ref[...]
        sim[...] = x0im_ref[...].T if state_minor else x0im_ref[...]

    if bt_major:
        for t in range(tt):
            utm[t * nb:(t + 1) * nb, :] = u_ref[:, t, :]
    else:
        utm[...] = u_ref[...]

    ub = utm[...].astype(BF16)
    half = S5_LANES // 2
    kh = D_S5 // 2
    for part in range(2):
        for hf in range(2):
            st[:, part * S5_LANES + hf * half:part * S5_LANES + (hf + 1) * half] = jnp.dot(
                ub[:, hf * kh:(hf + 1) * kh], b_ref[part, hf], preferred_element_type=F32)

    lw = 512 if nb == SUBLANES else LANES
    for lg in range(S5_LANES // lw):
        re = slice(lg * lw, (lg + 1) * lw)
        im = slice(S5_LANES + lg * lw, S5_LANES + (lg + 1) * lw)
        a_re = are_ref[:, re]
        a_im = aim_ref[:, re]

        def step(t, carry):
            s_re, s_im = carry
            r0 = pl.multiple_of(t * nb, nb)
            n_re = a_re * s_re - a_im * s_im + st[pl.ds(r0, nb), re]
            n_im = a_re * s_im + a_im * s_re + st[pl.ds(r0, nb), im]
            st[pl.ds(r0, nb), re] = n_re
            st[pl.ds(r0, nb), im] = n_im
            return n_re, n_im

        if nb == SUBLANES:
            f_re, f_im = sre[:, re], sim[:, re]
            for t in range(tt):
                r = slice(t * nb, (t + 1) * nb)
                n_re = a_re * f_re - a_im * f_im + st[r, re]
                n_im = a_re * f_im + a_im * f_re + st[r, im]
                st[r, re] = n_re
                st[r, im] = n_im
                f_re, f_im = n_re, n_im
        else:
            f_re, f_im = lax.fori_loop(0, tt, step, (sre[:, re], sim[:, re]))
        sre[:, re] = f_re
        sim[:, re] = f_im

    sre_ref[...] = sre[...].T if state_minor else sre[...]
    sim_ref[...] = sim[...].T if state_minor else sim[...]

    ys = []
    for hf in range(2):
        xr = st[:, hf * half:(hf + 1) * half].astype(BF16)
        xi = st[:, S5_LANES + hf * half:S5_LANES + (hf + 1) * half].astype(BF16)
        ys.append(jnp.dot(xr, c_ref[0, hf], preferred_element_type=F32)
                  - jnp.dot(xi, c_ref[1, hf], preferred_element_type=F32))
    y = jnp.concatenate(ys, axis=1) + d_ref[...] * utm[...]
    y = jax.nn.gelu(y)
    y = y * jax.nn.sigmoid(jnp.dot(y.astype(BF16), wglu_ref[...], preferred_element_type=F32))
    y = _rms(y, nrm_ref[...]).astype(BF16)
    if bt_major:
        yf = y.astype(F32)
        for j in range(D_S5 // LANES):
            ytm[j] = yf[:, j * LANES:(j + 1) * LANES]
        for b in range(nb):
            for j in range(D_S5 // LANES):
                y_ref[b, :, j * LANES:(j + 1) * LANES] = (
                    ytm[j, pl.ds(b, tt, stride=nb), :].astype(BF16))
    else:
        y_ref[...] = y


def _s5(u, x0re, x0im, s5p, *, nb, tt, nblk, bt_major, state_minor=False):
    rows = nb * tt
    io_state_shape = (S5_LANES, nb) if state_minor else (nb, S5_LANES)
    if bt_major:
        u_spec = pl.BlockSpec((nb, tt, D_S5), lambda i: (0, i, 0))
        y_shape = jax.ShapeDtypeStruct((nb, tt * nblk, D_S5), BF16)
    else:
        u_spec = pl.BlockSpec((rows, D_S5), lambda i: (i, 0))
        y_shape = jax.ShapeDtypeStruct((rows * nblk, D_S5), BF16)
    state_spec = _const_spec((nb, S5_LANES))
    are = jnp.broadcast_to(s5p["are"], (nb, S5_LANES))
    aim = jnp.broadcast_to(s5p["aim"], (nb, S5_LANES))
    io_state_spec = _const_spec(io_state_shape)
    return pl.pallas_call(
        functools.partial(_s5_kernel, nb=nb, tt=tt, bt_major=bt_major, state_minor=state_minor),
        grid=(nblk,),
        in_specs=[u_spec, io_state_spec, io_state_spec, state_spec, state_spec,
                  _const_spec((2, 2, D_S5 // 2, S5_LANES // 2)),
                  _const_spec((2, 2, S5_LANES // 2, D_S5 // 2)),
                  _const_spec((1, D_S5)),
                  _const_spec((D_S5, D_S5)),
                  _const_spec((1, D_S5))],
        out_specs=[u_spec, io_state_spec, io_state_spec],
        out_shape=[y_shape,
                   jax.ShapeDtypeStruct(io_state_shape, F32),
                   jax.ShapeDtypeStruct(io_state_shape, F32)],
        scratch_shapes=[pltpu.VMEM((rows, 2 * S5_LANES), F32),
                        pltpu.VMEM((nb, S5_LANES), F32),
                        pltpu.VMEM((nb, S5_LANES), F32),
                        pltpu.VMEM((rows, D_S5), F32),
                        pltpu.VMEM((D_S5 // LANES, rows, LANES), F32)],
        compiler_params=_cparams("arbitrary"),
        name="s5_" + ("bt" if bt_major else "tm") + str(nb),
    )(u, x0re, x0im, are, aim, s5p["b"], s5p["c"], s5p["d"], s5p["wglu"], s5p["nrm"])


def _s5_params(lam_re, lam_im, log_dt, b_re, b_im, c_re, c_im, d_skip, w_glu, nrm):
    dt = jnp.exp(log_dt)[:, None]
    mag = jnp.exp(lam_re * dt)
    abar_re, abar_im = mag * jnp.cos(lam_im * dt), mag * jnp.sin(lam_im * dt)
    num_re, num_im = abar_re - 1.0, abar_im
    den = lam_re * lam_re + lam_im * lam_im
    f_re = (num_re * lam_re + num_im * lam_im) / den
    f_im = (num_im * lam_re - num_re * lam_im) / den
    bbar_re = f_re[..., None] * b_re - f_im[..., None] * b_im
    bbar_im = f_re[..., None] * b_im + f_im[..., None] * b_re
    hg = N_S5_GROUPS // 2
    eye = jnp.eye(hg, dtype=F32)

    def bdiag(bb):
        bb = bb.reshape(2, hg, S5_STATE, S5_GROUP)
        return jnp.einsum("zgph,gk->zghkp", bb, eye).reshape(2, hg * S5_GROUP, hg * S5_STATE)

    def cdiag(cc):
        cc = cc.reshape(2, hg, S5_GROUP, S5_STATE)
        return jnp.einsum("zgnp,gk->zgpkn", cc, eye).reshape(2, hg * S5_STATE, hg * S5_GROUP)

    return {
        "are": abar_re.reshape(1, S5_LANES), "aim": abar_im.reshape(1, S5_LANES),
        "b": jnp.stack([bdiag(bbar_re), bdiag(bbar_im)]).astype(BF16),
        "c": jnp.stack([cdiag(c_re), cdiag(c_im)]).astype(BF16),
        "d": d_skip.reshape(1, D_S5), "wglu": w_glu.astype(BF16), "nrm": nrm.reshape(1, D_S5),
    }


def _head_norm_gate(o, g, gain, lo):
    inv = 1.0 / RET_HEAD_DIM

    def seg_mean(t):
        s_lo = jnp.sum(jnp.where(lo, t, 0.0), axis=1, keepdims=True)
        s_hi = jnp.sum(jnp.where(lo, 0.0, t), axis=1, keepdims=True)
        return jnp.where(lo, s_lo, s_hi) * inv

    dlt = o - seg_mean(o)
    var = seg_mean(dlt * dlt)
    return jax.nn.silu(g) * (dlt * lax.rsqrt(var + EPS) * gain)


def _ret_kernel(q_ref, k_ref, v_ref, g_ref, s0_ref, dm_ref, qd_ref, kd_ref, gc_ref, bm_ref,
                nrm_ref, y_ref, so_ref, s_acc):
    c = pl.program_id(1)

    @pl.when(c == 0)
    def _():
        s_acc[...] = s0_ref[0]

    rows = q_ref.shape[0]
    lane = lax.broadcasted_iota(jnp.int32, (rows, LANES), 1)
    lo = lane < RET_HEAD_DIM
    for j in range(N_HEAD_PAIRS):
        sl = slice(j * LANES, (j + 1) * LANES)
        q2 = q_ref[:, sl].astype(F32)
        k2 = k_ref[:, sl]
        v2 = v_ref[:, sl]
        s_pair = s_acc[j]
        cross = jnp.dot((q2 * qd_ref[:, sl]).astype(BF16), s_pair.astype(BF16),
                        preferred_element_type=F32)
        k_dec = (k2.astype(F32) * kd_ref[:, sl]).astype(BF16)
        upd = lax.dot_general(k_dec, v2, (((0,), (0,)), ((), ())), preferred_element_type=F32)
        s_acc[j] = gc_ref[j] * s_pair + bm_ref[...] * upd
        inner = []
        for hh, qh in enumerate((jnp.where(lo, q2, 0.0), jnp.where(lo, 0.0, q2))):
            sc = lax.dot_general(qh.astype(BF16), k2, (((1,), (1,)), ((), ())),
                                 preferred_element_type=F32) * dm_ref[2 * j + hh]
            inner.append(jnp.dot(sc.astype(BF16), v2, preferred_element_type=F32))
        o = jnp.where(lo, inner[0], inner[1]) + cross
        y_ref[:, sl] = _head_norm_gate(o, g_ref[:, sl], nrm_ref[:, sl], lo).astype(BF16)
    so_ref[0] = s_acc[...]


def _ret_log_decay():
    return np.log1p(-np.exp2(-5.0 - np.arange(N_RET_HEADS, dtype=np.float64)))


def _ret_tables(chunk):
    log_g = _ret_log_decay()
    n = np.arange(chunk, dtype=np.float64)
    diff = n[:, None] - n[None, :]
    dm = np.where(diff[None] >= 0.0,
                  np.exp(log_g[:, None, None] * np.maximum(diff, 0.0)[None]), 0.0)
    per_lane = lambda t: np.repeat(t.T, RET_HEAD_DIM, axis=1)
    qd = per_lane(np.exp(log_g[:, None] * (n + 1.0)[None]))
    kd = per_lane(np.exp(log_g[:, None] * (chunk - 1.0 - n)[None]))
    gch = np.exp(log_g * chunk)
    blk = np.kron(np.eye(2), np.ones((RET_HEAD_DIM, RET_HEAD_DIM)))
    gc = np.repeat(gch.reshape(N_HEAD_PAIRS, 2), RET_HEAD_DIM, axis=1)[:, :, None] * blk[None]
    return tuple(jnp.asarray(t, F32) for t in (dm, qd, kd, gc, blk))


def _pair_state(s):
    b = s.shape[0]
    s = s.reshape(b, N_HEAD_PAIRS, 2, RET_HEAD_DIM, RET_HEAD_DIM)
    z = jnp.zeros_like(s[:, :, 0])
    top = jnp.concatenate([s[:, :, 0], z], axis=-1)
    bot = jnp.concatenate([z, s[:, :, 1]], axis=-1)
    return jnp.concatenate([top, bot], axis=-2)


def _unpair_state(s2):
    d = RET_HEAD_DIM
    return jnp.stack([s2[:, :, :d, :d], s2[:, :, d:, d:]], axis=2).reshape(
        s2.shape[0], N_RET_HEADS, d, d)


def _retention(q, k, v, g, s0_pair, ret_norm, *, nseq, chunk, nchunk, chunk0=0):
    dm, qd, kd, gc, blk = _ret_tables(chunk)
    st_spec = pl.BlockSpec((1, N_HEAD_PAIRS, LANES, LANES), lambda b, c: (b, 0, 0, 0))
    blk_spec = pl.BlockSpec((chunk, D_RET), lambda b, c: (b * nchunk + c, 0))
    in_spec = pl.BlockSpec((chunk, D_RET), lambda b, c: (chunk0 + b * nchunk + c, 0))
    return pl.pallas_call(
        _ret_kernel,
        grid=(nseq, nchunk),
        in_specs=[in_spec, in_spec, in_spec, in_spec, st_spec,
                  _const_spec((N_RET_HEADS, chunk, chunk)),
                  _const_spec((chunk, D_RET)), _const_spec((chunk, D_RET)),
                  _const_spec((N_HEAD_PAIRS, LANES, LANES)), _const_spec((LANES, LANES)),
                  _const_spec((1, D_RET))],
        out_specs=[blk_spec, st_spec],
        out_shape=[jax.ShapeDtypeStruct((nseq * nchunk * chunk, D_RET), BF16),
                   jax.ShapeDtypeStruct((nseq, N_HEAD_PAIRS, LANES, LANES), F32)],
        scratch_shapes=[pltpu.VMEM((N_HEAD_PAIRS, LANES, LANES), F32)],
        compiler_params=_cparams("arbitrary", "arbitrary"),
        name="retention_c%d" % chunk,
    )(q, k, v, g, s0_pair, dm, qd, kd, gc, blk, ret_norm)


def _ret_step_kernel(q_ref, k_ref, v_ref, g_ref, s_ref, gam_ref, nrm_ref, y_ref, so_ref):
    q = q_ref[...]
    k = k_ref[...]
    v = v_ref[...]
    gam = gam_ref[0]
    score = jnp.sum(q * k, axis=0, keepdims=True)
    q_dec = (q * gam).astype(BF16).astype(F32)
    cross = jnp.zeros(v.shape, F32)
    for d in range(RET_HEAD_DIM):
        s_d = s_ref[0, d]
        cross = cross + q_dec[d:d + 1, :] * s_d
        so_ref[0, d] = gam * s_d + k[d:d + 1, :] * v
    o = score.astype(BF16).astype(F32) * v + cross
    dlt = o - jnp.mean(o, axis=0, keepdims=True)
    var = jnp.mean(dlt * dlt, axis=0, keepdims=True)
    y_ref[...] = jax.nn.silu(g_ref[...]) * (dlt * lax.rsqrt(var + EPS) * nrm_ref[...])


def _retention_step(q, k, v, g, state, ret_norm):
    n = q.shape[0]
    gam = np.exp(_ret_log_decay())
    gam_tab = jnp.asarray(np.broadcast_to(gam[:, None, None], (N_RET_HEADS, 1, n)), F32)
    seq_minor = lambda t: t.astype(F32).T
    head = lambda h: (h, 0)
    vec_spec = pl.BlockSpec((RET_HEAD_DIM, n), head)
    st_spec = pl.BlockSpec((1, RET_HEAD_DIM, RET_HEAD_DIM, n), lambda h: (h, 0, 0, 0))
    y_t, s_new = pl.pallas_call(
        _ret_step_kernel,
        grid=(N_RET_HEADS,),
        in_specs=[vec_spec, vec_spec, vec_spec, vec_spec, st_spec,
                  pl.BlockSpec((1, 1, n), lambda h: (h, 0, 0)), vec_spec],
        out_specs=[vec_spec, st_spec],
        out_shape=[jax.ShapeDtypeStruct((D_RET, n), F32),
                   jax.ShapeDtypeStruct((N_RET_HEADS, RET_HEAD_DIM, RET_HEAD_DIM, n), F32)],
        compiler_params=_cparams("parallel"),
        name="retention_step",
    )(seq_minor(q), seq_minor(k), seq_minor(v), seq_minor(g),
      jnp.transpose(state, (1, 2, 3, 0)), gam_tab,
      jnp.broadcast_to(ret_norm[:, None], (D_RET, n)))
    return y_t.T.astype(BF16), jnp.transpose(s_new, (3, 0, 1, 2))


def _first_max(vals, idxs, sentinel):
    m = jnp.max(vals, axis=0, keepdims=True)
    first = jnp.min(jnp.where(vals == m, idxs, sentinel), axis=0, keepdims=True)
    return m, idxs == first


def _route(scores, sel):
    t = sel.shape[1]
    neg = -jnp.inf
    member = lax.broadcasted_iota(jnp.int32, (GROUP_SIZE, t), 0).astype(F32)
    groups = [sel[g * GROUP_SIZE:(g + 1) * GROUP_SIZE, :] for g in range(N_EXPERT_GROUPS)]
    gscore = []
    for grp in groups:
        m1, pick = _first_max(grp, member, float(GROUP_SIZE))
        m2 = jnp.max(jnp.where(pick, neg, grp), axis=0, keepdims=True)
        gscore.append(m1 + m2)
    gs = jnp.concatenate(gscore, axis=0)
    gkeep = jnp.zeros(gs.shape, F32)
    for _ in range(TOPK_GROUPS):
        _, pick = _first_max(gs, member, float(N_EXPERT_GROUPS))
        gkeep = jnp.where(pick, 1.0, gkeep)
        gs = jnp.where(pick, neg, gs)
    cand = jnp.concatenate(
        [jnp.where(gkeep[g:g + 1, :] > 0.5, groups[g], neg) for g in range(N_EXPERT_GROUPS)],
        axis=0)
    expert = lax.broadcasted_iota(jnp.int32, (N_EXPERTS, t), 0).astype(F32)
    ids, ws = [], []
    chosen = jnp.zeros(cand.shape, F32)
    for _ in range(TOP_K):
        _, pick = _first_max(cand, expert, float(N_EXPERTS))
        ids.append(jnp.sum(jnp.where(pick, expert, 0.0), axis=0, keepdims=True))
        ws.append(jnp.sum(jnp.where(pick, scores, 0.0), axis=0, keepdims=True))
        chosen = jnp.where(pick, 1.0, chosen)
        cand = jnp.where(pick, neg, cand)
    w = jnp.concatenate(ws, axis=0)
    gates = w / jnp.sum(w, axis=0, keepdims=True) * ROUTED_SCALE
    return jnp.concatenate(ids, axis=0), gates, jnp.sum(chosen, axis=1, keepdims=True)


def _mix_kernel(x_ref, ys_ref, yr_ref, woa_ref, wob_ref, n2_ref, wrt_ref, rb_ref,
                wgs_ref, wus_ref, wds_ref, xs_ref, hp_ref, meta_ref, eid_ref, cnt_ref, *,
                n_blocks):
    @pl.when(pl.program_id(0) >= n_blocks)
    def _():
        hp_ref[...] = jnp.zeros_like(hp_ref)
        meta_ref[...] = jnp.zeros_like(meta_ref)
        cnt_ref[...] = jnp.zeros_like(cnt_ref)

    @pl.when(pl.program_id(0) < n_blocks)
    def _():
        _mix_body(x_ref, ys_ref, yr_ref, woa_ref, wob_ref, n2_ref, wrt_ref, rb_ref,
                  wgs_ref, wus_ref, wds_ref, xs_ref, hp_ref, meta_ref, eid_ref, cnt_ref)


def _mix_body(x_ref, ys_ref, yr_ref, woa_ref, wob_ref, n2_ref, wrt_ref, rb_ref,
              wgs_ref, wus_ref, wds_ref, xs_ref, hp_ref, meta_ref, eid_ref, cnt_ref):
    tm = x_ref.shape[0]
    x2 = (x_ref[...]
          + jnp.dot(ys_ref[...], woa_ref[...], preferred_element_type=F32)
          + jnp.dot(yr_ref[...], wob_ref[...], preferred_element_type=F32))
    h2f = _rms(x2, n2_ref[...])
    h2 = h2f.astype(BF16)
    h2r = h2.astype(F32)
    for c in range(PACK_ROWS):
        hp_ref[pl.ds(c, tm, stride=PACK_ROWS), :] = h2r[:, c * LANES:(c + 1) * LANES]
    h2_lo = (h2f - h2r).astype(BF16)
    nt_dims = (((1,), (1,)), ((), ()))
    logits = (lax.dot_general(wrt_ref[0], h2, nt_dims, preferred_element_type=F32)
              + lax.dot_general(wrt_ref[1], h2, nt_dims, preferred_element_type=F32)
              + lax.dot_general(wrt_ref[0], h2_lo, nt_dims, preferred_element_type=F32))
    scores = jax.nn.sigmoid(logits)
    ids, gates, counts = _route(scores, scores + rb_ref[:, :1])
    eid_ref[...] = ids.astype(jnp.int32)
    cnt_ref[0] = jnp.broadcast_to(counts, (N_EXPERTS, LANES))
    rec = jnp.concatenate([gates, ids, jnp.zeros((LANES - 2 * TOP_K, tm), F32)], axis=0).T
    meta_ref[pl.ds(0, tm, stride=META_ROWS), :] = rec
    for c in range(1, META_ROWS):
        meta_ref[pl.ds(c, tm, stride=META_ROWS), :] = jnp.zeros((tm, LANES), F32)
    a = jnp.dot(h2, wgs_ref[...], preferred_element_type=F32)
    b = jnp.dot(h2, wus_ref[...], preferred_element_type=F32)
    xs_ref[...] = x2 + jnp.dot((jax.nn.silu(a) * b).astype(BF16), wds_ref[...],
                               preferred_element_type=F32)


def _mix(x, ys5, yret, mp, tm, out_tokens):
    rows = x.shape[0]
    n_blocks = rows // tm
    assert out_tokens % tm == 0
    packed = lambda i: (i, 0)
    row = lambda i: (jnp.minimum(i, n_blocks - 1), 0)
    col = lambda i: (0, jnp.minimum(i, n_blocks - 1))
    return pl.pallas_call(
        functools.partial(_mix_kernel, n_blocks=n_blocks),
        grid=(out_tokens // tm,),
        in_specs=[pl.BlockSpec((tm, D_MODEL), row),
                  pl.BlockSpec((tm, D_S5), row), pl.BlockSpec((tm, D_RET), row),
                  _const_spec((D_S5, D_MODEL)), _const_spec((D_RET, D_MODEL)),
                  _const_spec((1, D_MODEL)),
                  _const_spec((2, N_EXPERTS, D_MODEL)), _const_spec((N_EXPERTS, LANES)),
                  _const_spec((D_MODEL, D_EXPERT)), _const_spec((D_MODEL, D_EXPERT)),
                  _const_spec((D_EXPERT, D_MODEL))],
        out_specs=[pl.BlockSpec((tm, D_MODEL), row),
                   pl.BlockSpec((tm * PACK_ROWS, LANES), packed),
                   pl.BlockSpec((tm * META_ROWS, LANES), packed),
                   pl.BlockSpec((TOP_K, tm), col),
                   pl.BlockSpec((1, N_EXPERTS, LANES), lambda i: (i, 0, 0))],
        out_shape=[jax.ShapeDtypeStruct((rows, D_MODEL), F32),
                   jax.ShapeDtypeStruct((out_tokens * PACK_ROWS, LANES), F32),
                   jax.ShapeDtypeStruct((out_tokens * META_ROWS, LANES), F32),
                   jax.ShapeDtypeStruct((TOP_K, rows), jnp.int32),
                   jax.ShapeDtypeStruct((out_tokens // tm, N_EXPERTS, LANES), F32)],
        compiler_params=_cparams("arbitrary"),
        name="out_proj_router",
    )(x, ys5, yret, mp["woa"], mp["wob"], mp["n2"], mp["wrt"], mp["rb"],
      mp["wgs"], mp["wus"], mp["wds"])


def _moe_kernel(t0_ref, nt_ref, src_ref, dst_ref, hp_ref, meta_ref, wg32_ref, wu32_ref, wd32_ref,
                acc_ref, xt_ref, xm_ref, ot_ref, xb_ref, wg_ref, wu_ref, wd_ref):
    s = pl.program_id(0)
    tm = MOE_TILE
    half = D_MODEL // 2
    all_rows = range(tm)
    nt = nt_ref[s]
    expert = (s % N_EXPERTS).astype(F32)

    @pl.when(nt > 0)
    def _():
        for src32, dst16 in ((wg32_ref, wg_ref), (wu32_ref, wu_ref), (wd32_ref, wd_ref)):
            rows = src32.shape[1]
            for i in range(0, rows, CAST_ROWS):
                dst16[0, i:i + CAST_ROWS, :] = src32[0, i:i + CAST_ROWS, :].astype(BF16)

    def fetch_rows(j, rows):
        for m in rows:
            r = pl.multiple_of(src_ref[j, 0, m], PACK_ROWS)
            xt_ref[pl.ds(m, PACK_ROWS, stride=XT_PITCH), :] = hp_ref[pl.ds(r, PACK_ROWS), :]
            xm_ref[pl.ds(m, META_ROWS, stride=XT_PITCH), :] = meta_ref[
                pl.ds(pl.multiple_of(r >> META_SHIFT, META_ROWS), META_ROWS), :]

    def add_rows(j, rows):
        for m0 in range(rows.start, rows.stop, ADD_UNROLL):
            new = []
            for m in range(m0, m0 + ADD_UNROLL):
                base = pl.multiple_of(dst_ref[j, 0, m], ACC_ROWS)
                row = ot_ref[pl.ds(m, ACC_ROWS, stride=OT_PITCH), :]
                new.append((base, acc_ref[0, pl.ds(base, ACC_ROWS), :] + row))
            for base, v in new:
                acc_ref[0, pl.ds(base, ACC_ROWS), :] = v

    @pl.when(s % N_EXPERTS == 0)
    def _():
        zero = jnp.zeros((ZERO_ROWS, LANES), F32)

        def clear(i, carry):
            acc_ref[0, pl.ds(pl.multiple_of(i * ZERO_ROWS, ZERO_ROWS), ZERO_ROWS), :] = zero
            return carry

        lax.fori_loop(0, acc_ref.shape[1] // ZERO_ROWS, clear, 0)
        ot_ref[...] = jnp.zeros_like(ot_ref)
        fetch_rows(1, all_rows)

    def stage(j, carry):
        n_slice = 4
        per = tm // n_slice
        rows = [range(i * per, (i + 1) * per) for i in range(n_slice)]
        for c in range(PACK_ROWS):
            xb_ref[:, c * LANES:(c + 1) * LANES] = (
                xt_ref[c * XT_PITCH:c * XT_PITCH + tm, :].astype(BF16))
        rec = xm_ref[0:tm, :]
        ids = pltpu.roll(rec, LANES - TOP_K, 1)
        lane = lax.broadcasted_iota(jnp.int32, rec.shape, 1)
        gate = jnp.sum(jnp.where((lane < TOP_K) & (ids == expert), rec, 0.0),
                       axis=1, keepdims=True)
        add_rows(j - 1, rows[0])
        a = jnp.dot(xb_ref[:, :half], wg_ref[0, :half], preferred_element_type=F32)
        add_rows(j - 1, rows[1])
        a = a + jnp.dot(xb_ref[:, half:], wg_ref[0, half:], preferred_element_type=F32)
        add_rows(j - 1, rows[2])
        b = jnp.dot(xb_ref[:, :half], wu_ref[0, :half], preferred_element_type=F32)
        add_rows(j - 1, rows[3])
        b = b + jnp.dot(xb_ref[:, half:], wu_ref[0, half:], preferred_element_type=F32)
        act = (jax.nn.silu(a) * b * gate).astype(BF16)
        n_col = D_MODEL // n_slice
        for piece in range(n_slice):
            fetch_rows(j + 1, rows[piece])
            out = jnp.dot(act, wd_ref[0, :, piece * n_col:(piece + 1) * n_col],
                          preferred_element_type=F32)
            for cc in range(n_col // LANES):
                c = piece * (n_col // LANES) + cc
                ot_ref[c * OT_PITCH:c * OT_PITCH + tm, :] = out[:, cc * LANES:(cc + 1) * LANES]
        return carry

    lax.fori_loop(1, nt + 1, stage, 0)

    @pl.when(s % N_EXPERTS == N_EXPERTS - 1)
    def _():
        add_rows(nt, all_rows)


def _dispatch_tables(eid_t, counts, n_tok):
    tm, ts = MOE_TILE, MOE_BLOCK_TOKENS
    ng = MOE_BLOCKS * N_EXPERTS
    assert ts < (1 << TOKEN_BITS)
    tok = jnp.arange(n_tok, dtype=jnp.int32)
    grp = (tok // ts)[None, :] * N_EXPERTS + eid_t
    gids = jnp.arange(ng, dtype=jnp.int32)
    npad = (-counts) % tm
    fill = jnp.arange(tm, dtype=jnp.int32)[None, :] < npad[:, None]
    keys = jnp.concatenate([
        ((grp << TOKEN_BITS) | (tok % ts)[None, :]).reshape(-1),
        ((jnp.where(fill, gids[:, None], ng) << TOKEN_BITS) | ts).reshape(-1)])
    toks = lax.sort(keys, dimension=0, is_stable=False) & ((1 << TOKEN_BITS) - 1)
    n_tiles = (n_tok * TOP_K + ng * tm) // tm

    ntile = (counts + npad) // tm
    tile_start = jnp.cumsum(ntile) - ntile
    src = jnp.where(toks < ts, toks, 0) * PACK_ROWS
    dst = toks * ACC_ROWS
    head = jnp.zeros((tm,), jnp.int32)
    tail = jnp.zeros((GROUP_TILES * tm,), jnp.int32)
    rows = n_tiles + 1 + GROUP_TILES
    return {"t0": tile_start.astype(jnp.int32), "nt": ntile.astype(jnp.int32),
            "src": jnp.concatenate([head, src, tail]).reshape(rows, 1, tm),
            "dst": jnp.concatenate([head, dst, tail]).reshape(rows, 1, tm)}


def _moe(hp, meta, tabs, ep):
    tm, ts = MOE_TILE, MOE_BLOCK_TOKENS
    assert GROUP_TILES >= -(-ts // tm) + 2
    window = pl.BlockSpec((pl.Element(GROUP_TILES), pl.Element(1), pl.Element(tm)),
                          lambda s, t0, nt: (t0[s], 0, 0), memory_space=pltpu.SMEM)
    wspec = lambda shape: pl.BlockSpec((1,) + shape, lambda s, t0, nt: (s % N_EXPERTS, 0, 0))
    blk = lambda s, t0, nt: (s // N_EXPERTS, 0)
    acc_rows = (ts + SUBLANES) * ACC_ROWS
    assert acc_rows % ZERO_ROWS == 0
    stage_bufs = [pltpu.VMEM((PACK_ROWS * XT_PITCH, LANES), F32),
                  pltpu.VMEM((META_ROWS * XT_PITCH, LANES), F32),
                  pltpu.VMEM((ACC_ROWS * OT_PITCH, LANES), F32),
                  pltpu.VMEM((tm, D_MODEL), BF16),
                  pltpu.VMEM((1, D_MODEL, D_EXPERT), BF16),
                  pltpu.VMEM((1, D_MODEL, D_EXPERT), BF16),
                  pltpu.VMEM((1, D_EXPERT, D_MODEL), BF16)]
    grid_spec = pltpu.PrefetchScalarGridSpec(
        num_scalar_prefetch=2,
        grid=(MOE_BLOCKS * N_EXPERTS,),
        in_specs=[window, window,
                  pl.BlockSpec((ts * PACK_ROWS, LANES), blk, pipeline_mode=pl.Buffered(1)),
                  pl.BlockSpec((ts * META_ROWS, LANES), blk, pipeline_mode=pl.Buffered(1)),
                  wspec((D_MODEL, D_EXPERT)), wspec((D_MODEL, D_EXPERT)),
                  wspec((D_EXPERT, D_MODEL))],
        out_specs=pl.BlockSpec((1, acc_rows, LANES), lambda s, t0, nt: (s // N_EXPERTS, 0, 0),
                               pipeline_mode=pl.Buffered(1)),
        scratch_shapes=stage_bufs,
    )
    return pl.pallas_call(
        _moe_kernel,
        grid_spec=grid_spec,
        out_shape=jax.ShapeDtypeStruct((MOE_BLOCKS, acc_rows, LANES), F32),
        compiler_params=_cparams("arbitrary"),
        name="moe_experts",
    )(tabs["t0"], tabs["nt"], tabs["src"], tabs["dst"], hp, meta, ep["wg"], ep["wu"], ep["wd"])


def _final_kernel(xs_ref, r_ref, fn_ref, y_ref):
    tm = xs_ref.shape[0]
    routed = jnp.concatenate(
        [r_ref[0, pl.ds(c, tm, stride=ACC_ROWS), :] for c in range(ACC_ROWS)], axis=1)
    y_ref[...] = _rms(xs_ref[...] + routed, fn_ref[...])


def _finalize(xs, routed, final_norm, tm, tok0):
    rows = xs.shape[0]
    per_block = MOE_BLOCK_TOKENS // tm
    first = tok0 // tm
    assert tok0 % tm == 0 and MOE_BLOCK_TOKENS % tm == 0
    return pl.pallas_call(
        _final_kernel,
        grid=(rows // tm,),
        in_specs=[pl.BlockSpec((tm, D_MODEL), lambda i: (i, 0)),
                  pl.BlockSpec((1, tm * ACC_ROWS, LANES),
                               lambda i: ((first + i) // per_block, (first + i) % per_block, 0)),
                  _const_spec((1, D_MODEL))],
        out_specs=pl.BlockSpec((tm, D_MODEL), lambda i: (i, 0)),
        out_shape=jax.ShapeDtypeStruct((rows, D_MODEL), F32),
        compiler_params=_cparams("parallel"),
        name="final_norm",
    )(xs, routed, final_norm)


def _rope_tables(pos):
    half = RET_HEAD_DIM // 2
    inv_freq = ROPE_BASE ** (-np.arange(half, dtype=np.float64) / half)
    ang = np.asarray(pos, np.float64)[:, None] * inv_freq[None, :]
    cos, sin = np.cos(ang), np.sin(ang)
    cos_t = np.concatenate([cos, cos, cos, cos], axis=1)
    sin_t = np.concatenate([-sin, sin, -sin, sin], axis=1)
    return jnp.asarray(cos_t, F32), jnp.asarray(sin_t, F32)


def kernel(x_prompt, x_sample, state_s5_re, state_s5_im, state_ret, meta_tokens, norm1, w_in,
           s5_lam_re, s5_lam_im, s5_log_dt, s5_b_re, s5_b_im, s5_c_re, s5_c_im, s5_d, s5_w_glu,
           s5_norm, ret_norm, w_out, norm2, w_router, router_bias, w_gate_e, w_up_e, w_down_e,
           w_gate_sh, w_up_sh, w_down_sh, final_norm):
    assert norm1.shape[0] == 1, "single-layer model"
    bp, seq, _ = x_prompt.shape
    ns = x_sample.shape[0]
    l = 0

    n1 = norm1[l].reshape(1, D_MODEL)
    w_in_b = w_in[l].astype(BF16)
    s5p = _s5_params(s5_lam_re[l], s5_lam_im[l], s5_log_dt[l], s5_b_re[l], s5_b_im[l],
                     s5_c_re[l], s5_c_im[l], s5_d[l], s5_w_glu[l], s5_norm[l])
    rnorm = ret_norm[l].reshape(1, D_RET)
    w_out_b = w_out[l].astype(BF16)
    wr_t = w_router[l].T
    wr_hi = wr_t.astype(BF16)
    wr_lo = (wr_t - wr_hi.astype(F32)).astype(BF16)
    mp = {
        "woa": w_out_b[:D_S5], "wob": w_out_b[D_S5:], "n2": norm2[l].reshape(1, D_MODEL),
        "wrt": jnp.stack([wr_hi, wr_lo]),
        "rb": jnp.broadcast_to(router_bias[l][:, None], (N_EXPERTS, LANES)),
        "wgs": w_gate_sh[l].astype(BF16), "wus": w_up_sh[l].astype(BF16),
        "wds": w_down_sh[l].astype(BF16),
    }
    ep = {"wg": w_gate_e[l], "wu": w_up_e[l],
          "wd": w_down_e[l]}
    fnorm = final_norm.reshape(1, D_MODEL)

    xp = x_prompt.reshape(bp * seq, D_MODEL)
    cos_p, sin_p = _rope_tables(N_META + np.arange(seq))
    tm_a = 1024
    up, qp, kp, vp, gp = _project(xp, n1, w_in_b, cos_p, sin_p, tm_a, seq // tm_a)

    meta_tm = jnp.repeat(meta_tokens, SUBLANES, axis=0)
    mchunk = LANES
    meta_chunk = jnp.concatenate([jnp.zeros((mchunk - N_META, D_MODEL), F32), meta_tokens], axis=0)
    x_small = jnp.concatenate([x_sample.reshape(ns, D_MODEL), meta_tm, meta_chunk], axis=0)
    meta_pos = np.arange(N_META)
    pos_small = np.concatenate([np.full((ns,), PAST_LEN), np.repeat(meta_pos, SUBLANES),
                                np.zeros((mchunk - N_META,), np.int64), meta_pos])
    cos_s, sin_s = _rope_tables(pos_small)
    n_small = x_small.shape[0]
    n_tm = ns + N_META * SUBLANES
    us, qs, ks, vs, gs = _project(x_small, n1, w_in_b, cos_s, sin_s, n_small, 1)

    zero8 = jnp.zeros((SUBLANES, S5_LANES), F32)
    _, m_re, m_im = _s5(us[ns:n_tm], zero8, zero8, s5p, nb=SUBLANES, tt=N_META, nblk=1,
                        bt_major=False)
    tt = 128
    ys5_p, p_re, p_im = _s5(up.reshape(bp, seq, D_S5), m_re, m_im, s5p, nb=bp, tt=tt,
                            nblk=seq // tt, bt_major=True)
    ys5_p = ys5_p.reshape(bp * seq, D_S5)
    seq_minor = lambda s: jnp.transpose(s, (1, 2, 0)).reshape(S5_LANES, ns)
    seq_major = lambda s: jnp.transpose(s.reshape(N_S5_GROUPS, S5_STATE, ns), (2, 0, 1))[None]
    ys5_s, s_re, s_im = _s5(us[:ns], seq_minor(state_s5_re[l]), seq_minor(state_s5_im[l]), s5p,
                            nb=ns, tt=1, nblk=1, bt_major=False, state_minor=True)

    zero_pair = jnp.zeros((1, N_HEAD_PAIRS, LANES, LANES), F32)
    assert n_tm % mchunk == 0
    _, m_pair = _retention(qs, ks, vs, gs, zero_pair, rnorm, nseq=1, chunk=mchunk, nchunk=1,
                           chunk0=n_tm // mchunk)
    chunk = 256
    yret_p, p_pair = _retention(qp, kp, vp, gp, jnp.broadcast_to(m_pair, (bp,) + m_pair.shape[1:]),
                                rnorm, nseq=bp, chunk=chunk, nchunk=seq // chunk)
    yret_s, ret_s = _retention_step(qs[:ns], ks[:ns], vs[:ns], gs[:ns], state_ret[l], ret_norm[l])

    n_prompt = bp * seq
    assert n_prompt + ns <= MOE_BLOCKS * MOE_BLOCK_TOKENS
    n_moe = MOE_BLOCKS * MOE_BLOCK_TOKENS
    tm_mix = 512
    xs_p, hp, meta, eid_p, cnt_p = _mix(xp, ys5_p, yret_p, mp, tm_mix, n_moe)
    xs_s, hp_s, meta_s, eid_s, cnt_s = _mix(x_sample.reshape(ns, D_MODEL), ys5_s, yret_s, mp,
                                            ns, ns)
    hp = lax.dynamic_update_slice(hp, hp_s, (n_prompt * PACK_ROWS, 0))
    meta = lax.dynamic_update_slice(meta, meta_s, (n_prompt * META_ROWS, 0))
    blk_s = n_prompt // MOE_BLOCK_TOKENS
    assert MOE_BLOCK_TOKENS % tm_mix == 0 and (n_prompt + ns - 1) // MOE_BLOCK_TOKENS == blk_s
    counts = cnt_p[:, :, 0].reshape(MOE_BLOCKS, MOE_BLOCK_TOKENS // tm_mix, N_EXPERTS).sum(axis=1)
    counts = counts.at[blk_s].add(cnt_s[0, :, 0]).reshape(-1).astype(jnp.int32)
    tabs = _dispatch_tables(jnp.concatenate([eid_p, eid_s], axis=1), counts, n_prompt + ns)
    routed = _moe(hp, meta, tabs, ep)
    y_p = _finalize(xs_p, routed, fnorm, 512, 0)
    y_s = _finalize(xs_s, routed, fnorm, ns, n_prompt)

    shape5 = (1, bp, N_S5_GROUPS, S5_STATE)
    return (y_p.reshape(bp, seq, D_MODEL),
            y_s.reshape(ns, 1, D_MODEL),
            p_re.reshape(shape5), p_im.reshape(shape5),
            _unpair_state(p_pair)[None],
            seq_major(s_re), seq_major(s_im),
            ret_s[None])
```
